```python
import jax, jax.numpy as jnp
from jax import lax
import numpy as np

D_MODEL = 1024
BATCH = 8
SEQ = 4096
DEPTH = 2

HEAD_DIM_A = 64
WIDTH_A = D_MODEL // 2
N_HEADS_A = WIDTH_A // HEAD_DIM_A
HEAD_DIM_B = 64
WIDTH_B = D_MODEL // 2
N_HEADS_B = WIDTH_B // HEAD_DIM_B
CONV_K = 5
CHUNK = 64
N_DIR = 2
N_GROUPS = 4
EXPERTS_PER_GROUP = 8
N_EXPERTS = N_GROUPS * EXPERTS_PER_GROUP
TOP_K = 2
D_EXPERT = D_MODEL // 4
N_MOD = 6
EPS = 1e-6
NEG_INF = -1e30

SPLIT_WIDTHS = (3 * WIDTH_A, WIDTH_A, 2 * N_DIR * N_HEADS_A,
                2 * WIDTH_B, WIDTH_B, WIDTH_B, 2 * N_DIR * N_HEADS_B, 2 * D_MODEL)
D_IN = sum(SPLIT_WIDTHS)
SPLIT_POINTS = tuple(sum(SPLIT_WIDTHS[: i + 1]) for i in range(len(SPLIT_WIDTHS) - 1))

kernel_name = "hybrid_gdn_mlstm_hiermoe_adaln"


def rmsnorm(x, g):
    xf = x.astype(jnp.float32)
    y = xf * lax.rsqrt(jnp.mean(xf * xf, axis=-1, keepdims=True) + EPS)
    return y * g.astype(jnp.float32)


def l2norm(t):
    return t * lax.rsqrt(jnp.sum(t * t, axis=-1, keepdims=True) + EPS)


def head_layernorm(t, g):
    mu = jnp.mean(t, axis=-1, keepdims=True)
    tc = t - mu
    return tc * lax.rsqrt(jnp.mean(tc * tc, axis=-1, keepdims=True) + EPS) * g


def centred_conv(x, w):
    c = x.shape[-1]
    return lax.conv_general_dilated(
        x, w[:, None, :].astype(x.dtype), window_strides=(1,),
        padding=[(CONV_K // 2, CONV_K // 2)],
        dimension_numbers=("NWC", "WIO", "NWC"), feature_group_count=c)


def chunk_heads(t):
    b, s, h, d = t.shape
    return t.reshape(b, s // CHUNK, CHUNK, h, d).transpose(0, 3, 1, 2, 4)


def chunk_scalar(t):
    b, s, h = t.shape
    return t.reshape(b, s // CHUNK, CHUNK, h).transpose(0, 3, 1, 2)


def unchunk(t):
    b, h, n, l, d = t.shape
    return t.transpose(0, 2, 3, 1, 4).reshape(b, n * l, h, d)


def flip_seq(t):
    return jnp.flip(t, axis=1)


def gated_delta_chunked(q, k, v, g, beta):
    L = q.shape[-2]
    dv = v.shape[-1]
    b, h, n, _, dk = k.shape
    tri_incl = jnp.tril(jnp.ones((L, L), dtype=bool))
    tri_strict = jnp.tril(jnp.ones((L, L), dtype=bool), -1)
    g_cum = jnp.cumsum(g, axis=-1)
    decay = jnp.exp(jnp.where(tri_incl, g_cum[..., :, None] - g_cum[..., None, :], NEG_INF))
    k_beta = k * beta[..., None]
    lower = jnp.where(tri_strict, jnp.einsum("bhnid,bhnjd->bhnij", k_beta, k) * decay, 0.0)
    lhs = lower + jnp.eye(L, dtype=lower.dtype)
    rhs = jnp.concatenate([v * beta[..., None], k_beta * jnp.exp(g_cum)[..., None]], axis=-1)
    sol = lax.linalg.triangular_solve(lhs, rhs, left_side=True, lower=True, unit_diagonal=True)
    u, w = sol[..., :dv], sol[..., dv:]
    k_to_end = k * jnp.exp(g_cum[..., -1:] - g_cum)[..., None]
    chunk_decay = jnp.exp(g_cum[..., -1])

    def step(state, xs):
        u_c, w_c, kd_c, cd_c = xs
        v_new = u_c - jnp.einsum("bhlk,bhkv->bhlv", w_c, state)
        state_next = state * cd_c[..., None, None] + jnp.einsum("bhlk,bhlv->bhkv", kd_c, v_new)
        return state_next, (state, v_new)

    s0 = jnp.zeros((b, h, dk, dv), jnp.float32)
    xs = tuple(jnp.moveaxis(t, 2, 0) for t in (u, w, k_to_end, chunk_decay))
    _, (s_prev, v_new) = lax.scan(step, s0, xs)
    s_prev = jnp.moveaxis(s_prev, 0, 2)
    v_new = jnp.moveaxis(v_new, 0, 2)
    intra = jnp.einsum("bhnid,bhnjd->bhnij", q, k) * decay
    return (jnp.einsum("bhnlk,bhnkv->bhnlv", q * jnp.exp(g_cum)[..., None], s_prev)
            + jnp.einsum("bhnij,bhnjv->bhniv", intra, v_new))


def mlstm_chunked(q, k, v, i_pre, f_pre):
    L = q.shape[-2]
    b, h, n, _, dk = k.shape
    dv = v.shape[-1]
    tri_incl = jnp.tril(jnp.ones((L, L), dtype=bool))
    bcum = jnp.cumsum(jax.nn.log_sigmoid(f_pre), axis=-1)
    b_last = bcum[..., -1]
    a_end = b_last[..., None] - bcum + i_pre

    def step(carry, xs):
        c_st, n_st, m_st = carry
        k_c, v_c, a_c, bl_c = xs
        m_next = jnp.maximum(bl_c + m_st, jnp.max(a_c, axis=-1))
        scale_prev = jnp.exp(bl_c + m_st - m_next)
        wgt = jnp.exp(a_c - m_next[..., None])
        c_next = c_st * scale_prev[..., None, None] + jnp.einsum("bhlk,bhlv->bhkv", k_c * wgt[..., None], v_c)
        n_next = n_st * scale_prev[..., None] + jnp.einsum("bhl,bhlk->bhk", wgt, k_c)
        return (c_next, n_next, m_next), (c_st, n_st, m_st)

    init = (jnp.zeros((b, h, dk, dv), jnp.float32), jnp.zeros((b, h, dk), jnp.float32),
            jnp.zeros((b, h), jnp.float32))
    xs = tuple(jnp.moveaxis(t, 2, 0) for t in (k, v, a_end, b_last))
    _, (c_prev, n_prev, m_prev) = lax.scan(step, init, xs)
    c_prev = jnp.moveaxis(c_prev, 0, 2)
    n_prev = jnp.moveaxis(n_prev, 0, 2)
    m_prev = jnp.moveaxis(m_prev, 0, 2)
    log_d = jnp.where(tri_incl, bcum[..., :, None] - bcum[..., None, :] + i_pre[..., None, :], NEG_INF)
    log_inter = bcum + m_prev[..., None]
    m_t = jnp.maximum(log_inter, jnp.max(log_d, axis=-1))
    d_mat = jnp.exp(log_d - m_t[..., None])
    inter_w = jnp.exp(log_inter - m_t)
    qk = jnp.einsum("bhntd,bhnsd->bhnts", q, k) * d_mat
    num = (jnp.einsum("bhnts,bhnsv->bhntv", qk, v)
           + inter_w[..., None] * jnp.einsum("bhntk,bhnkv->bhntv", q, c_prev))
    den = jnp.sum(qk, axis=-1) + inter_w * jnp.einsum("bhntk,bhnk->bhnt", q, n_prev)
    return num / jnp.maximum(jnp.abs(den), jnp.exp(-m_t))[..., None]


def run_gdn(q, k, v, g, beta):
    return unchunk(gated_delta_chunked(chunk_heads(q), chunk_heads(k), chunk_heads(v),
                                       chunk_scalar(g), chunk_scalar(beta)))


def run_mlstm(q, k, v, i_pre, f_pre):
    return unchunk(mlstm_chunked(chunk_heads(q), chunk_heads(k), chunk_heads(v),
                                 chunk_scalar(i_pre), chunk_scalar(f_pre)))


def hybrid_mixer(h, w_in, gdn_conv_w, gdn_a_log, gdn_dt_bias, gdn_norm_g,
                 mlstm_conv_w, mlstm_i_bias, mlstm_f_bias, mlstm_norm_g,
                 w_branch_a, w_branch_b, w_out):
    bsz, s, _ = h.shape
    proj = jnp.einsum("bsd,de->bse", h, w_in.astype(jnp.float32))
    a_qkv, a_z, a_gates, b_qk, b_v, b_o, b_gates, merge = jnp.split(proj, SPLIT_POINTS, axis=-1)

    a_qkv = jax.nn.silu(centred_conv(a_qkv, gdn_conv_w.astype(jnp.float32)))
    aq, ak, av = jnp.split(a_qkv, 3, axis=-1)
    aq = l2norm(aq.reshape(bsz, s, N_HEADS_A, HEAD_DIM_A)) * (HEAD_DIM_A ** -0.5)
    ak = l2norm(ak.reshape(bsz, s, N_HEADS_A, HEAD_DIM_A))
    av = av.reshape(bsz, s, N_HEADS_A, HEAD_DIM_A)
    a_gates = jnp.moveaxis(a_gates.reshape(bsz, s, N_DIR, 2, N_HEADS_A), 2, 0)
    g = -jnp.exp(gdn_a_log.astype(jnp.float32))[:, None, None, :] * jax.nn.softplus(
        a_gates[..., 0, :] + gdn_dt_bias.astype(jnp.float32)[:, None, None, :])
    beta = jax.nn.sigmoid(a_gates[..., 1, :])
    o_a = (run_gdn(aq, ak, av, g[0], beta[0])
           + flip_seq(run_gdn(flip_seq(aq), flip_seq(ak), flip_seq(av), flip_seq(g[1]), flip_seq(beta[1]))))
    o_a = rmsnorm(o_a, gdn_norm_g) * jax.nn.silu(a_z.reshape(bsz, s, N_HEADS_A, HEAD_DIM_A))
    o_a = o_a.reshape(bsz, s, WIDTH_A)

    b_qk = jax.nn.silu(centred_conv(b_qk, mlstm_conv_w.astype(jnp.float32)))
    bq, bk = jnp.split(b_qk, 2, axis=-1)
    bq = bq.reshape(bsz, s, N_HEADS_B, HEAD_DIM_B)
    bk = bk.reshape(bsz, s, N_HEADS_B, HEAD_DIM_B) * (HEAD_DIM_B ** -0.5)
    bv = b_v.reshape(bsz, s, N_HEADS_B, HEAD_DIM_B)
    b_gates = jnp.moveaxis(b_gates.reshape(bsz, s, N_DIR, 2, N_HEADS_B), 2, 0)
    i_pre = b_gates[..., 0, :] + mlstm_i_bias.astype(jnp.float32)[:, None, None, :]
    f_pre = b_gates[..., 1, :] + mlstm_f_bias.astype(jnp.float32)[:, None, None, :]
    h_b = (run_mlstm(bq, bk, bv, i_pre[0], f_pre[0])
           + flip_seq(run_mlstm(flip_seq(bq), flip_seq(bk), flip_seq(bv), flip_seq(i_pre[1]), flip_seq(f_pre[1]))))
    h_b = head_layernorm(h_b, mlstm_norm_g.astype(jnp.float32).reshape(N_HEADS_B, HEAD_DIM_B))
    h_b = (h_b * jax.nn.sigmoid(b_o.reshape(bsz, s, N_HEADS_B, HEAD_DIM_B))).reshape(bsz, s, WIDTH_B)

    gate_a, gate_b = jnp.split(jax.nn.sigmoid(merge), 2, axis=-1)
    y = (gate_a * jnp.einsum("bse,ed->bsd", o_a, w_branch_a.astype(jnp.float32))
         + gate_b * jnp.einsum("bse,ed->bsd", h_b, w_branch_b.astype(jnp.float32)))
    return jnp.einsum("bsd,de->bse", y, w_out.astype(jnp.float32))


def hierarchical_moe(h, router_group, router_expert, w_gate, w_up, w_down):
    bsz, s, d = h.shape
    ht = h.reshape(bsz * s, d)
    group_logits = ht @ router_group.astype(jnp.float32)
    group_probs = jax.nn.softmax(group_logits, axis=-1)
    p_group, g_sel = lax.top_k(group_probs, 1)
    expert_logits = (ht @ router_expert.astype(jnp.float32)).reshape(-1, N_GROUPS, EXPERTS_PER_GROUP)
    in_group = jnp.take_along_axis(expert_logits, g_sel[:, :, None], axis=1)[:, 0]
    top_vals, top_idx = lax.top_k(in_group, TOP_K)
    top_w = jax.nn.softmax(top_vals, axis=-1) * p_group
    expert_id = g_sel * EXPERTS_PER_GROUP + top_idx
    combine = jnp.sum(jax.nn.one_hot(expert_id, N_EXPERTS, dtype=jnp.float32) * top_w[..., None], axis=1)
    y = jnp.zeros_like(ht)
    for grp in range(N_GROUPS):
        sl = slice(grp * EXPERTS_PER_GROUP, (grp + 1) * EXPERTS_PER_GROUP)
        hg = jnp.einsum("td,edf->tef", ht, w_gate[sl].astype(jnp.float32))
        hu = jnp.einsum("td,edf->tef", ht, w_up[sl].astype(jnp.float32))
        act = jax.nn.silu(hg) * hu * combine[:, sl, None]
        y = y + jnp.einsum("tef,efd->td", act, w_down[sl].astype(jnp.float32))
    return y.reshape(bsz, s, d)


def setup_inputs(seed: int = 0) -> dict:
    key = jax.random.key(seed)
    ks = jax.random.split(key, 24)
    nrm = lambda k, shape, scale: jax.random.normal(k, shape, jnp.float32) * scale
    gain = lambda k, shape: 1.0 + 0.02 * jax.random.normal(k, shape, jnp.float32)
    dt = jnp.exp(jax.random.uniform(ks[9], (DEPTH, N_DIR, N_HEADS_A), jnp.float32,
                                    np.log(1e-3), np.log(1e-1)))
    return {
        "x": nrm(ks[0], (BATCH, SEQ, D_MODEL), 1.0),
        "c": nrm(ks[1], (BATCH, D_MODEL), 1.0),
        "ada_w": nrm(ks[2], (DEPTH, D_MODEL, N_MOD * D_MODEL), 0.5 * D_MODEL ** -0.5),
        "ada_b": nrm(ks[3], (DEPTH, N_MOD * D_MODEL), 0.01),
        "norm_mix_g": gain(ks[4], (DEPTH, D_MODEL)),
        "norm_ffn_g": gain(ks[5], (DEPTH, D_MODEL)),
        "w_in": nrm(ks[6], (DEPTH, D_MODEL, D_IN), D_MODEL ** -0.5),
        "gdn_conv_w": nrm(ks[7], (DEPTH, CONV_K, 3 * WIDTH_A), CONV_K ** -0.5),
        "gdn_a_log": jnp.log(jax.random.uniform(ks[8], (DEPTH, N_DIR, N_HEADS_A), jnp.float32, 1.0, 16.0)),
        "gdn_dt_bias": jnp.log(jnp.expm1(dt)),
        "gdn_norm_g": gain(ks[10], (DEPTH, HEAD_DIM_A)),
        "mlstm_conv_w": nrm(ks[11], (DEPTH, CONV_K, 2 * WIDTH_B), CONV_K ** -0.5),
        "mlstm_i_bias": nrm(ks[12], (DEPTH, N_DIR, N_HEADS_B), 0.1),
        "mlstm_f_bias": 3.0 + 3.0 * jax.random.uniform(ks[13], (DEPTH, N_DIR, N_HEADS_B), jnp.float32),
        "mlstm_norm_g": gain(ks[14], (DEPTH, WIDTH_B)),
        "w_branch_a": nrm(ks[15], (DEPTH, WIDTH_A, D_MODEL), WIDTH_A ** -0.5),
        "w_branch_b": nrm(ks[16], (DEPTH, WIDTH_B, D_MODEL), WIDTH_B ** -0.5),
        "w_out": nrm(ks[17], (DEPTH, D_MODEL, D_MODEL), D_MODEL ** -0.5),
        "router_group": nrm(ks[18], (DEPTH, D_MODEL, N_GROUPS), D_MODEL ** -0.5),
        "router_expert": nrm(ks[19], (DEPTH, D_MODEL, N_EXPERTS), D_MODEL ** -0.5),
        "w_gate": nrm(ks[20], (DEPTH, N_EXPERTS, D_MODEL, D_EXPERT), D_MODEL ** -0.5),
        "w_up": nrm(ks[21], (DEPTH, N_EXPERTS, D_MODEL, D_EXPERT), D_MODEL ** -0.5),
        "w_down": nrm(ks[22], (DEPTH, N_EXPERTS, D_EXPERT, D_MODEL), D_EXPERT ** -0.5),
        "final_norm_g": gain(ks[23], (D_MODEL,)),
    }


def reference(x, c, ada_w, ada_b, norm_mix_g, norm_ffn_g, w_in, gdn_conv_w, gdn_a_log,
              gdn_dt_bias, gdn_norm_g, mlstm_conv_w, mlstm_i_bias, mlstm_f_bias, mlstm_norm_g,
              w_branch_a, w_branch_b, w_out, router_group, router_expert, w_gate, w_up, w_down,
              final_norm_g):
    out_dtype = x.dtype
    h_res = x.astype(jnp.float32)
    c_act = jax.nn.silu(c.astype(jnp.float32))
    for l in range(DEPTH):
        mod = c_act @ ada_w[l].astype(jnp.float32) + ada_b[l].astype(jnp.float32)
        sh1, sc1, gt1, sh2, sc2, gt2 = [m[:, None, :] for m in jnp.split(mod, N_MOD, axis=-1)]
        hn = rmsnorm(h_res, norm_mix_g[l]) * (1.0 + sc1) + sh1
        h_res = h_res + gt1 * hybrid_mixer(
            hn, w_in[l], gdn_conv_w[l], gdn_a_log[l], gdn_dt_bias[l], gdn_norm_g[l],
            mlstm_conv_w[l], mlstm_i_bias[l], mlstm_f_bias[l], mlstm_norm_g[l],
            w_branch_a[l], w_branch_b[l], w_out[l])
        hn = rmsnorm(h_res, norm_ffn_g[l]) * (1.0 + sc2) + sh2
        h_res = h_res + gt2 * hierarchical_moe(hn, router_group[l], router_expert[l],
                                               w_gate[l], w_up[l], w_down[l])
    return rmsnorm(h_res, final_norm_g).astype(out_dtype)
```

```python
import functools

import jax
import jax.numpy as jnp
from jax import lax
from jax.experimental import pallas as pl
from jax.experimental.pallas import tpu as pltpu

D_MODEL = 1024
HEAD_DIM = 64
N_HEADS = 8
WIDTH = N_HEADS * HEAD_DIM
CONV_K = 5
CONV_W = 3 * WIDTH + 2 * WIDTH
REST_W = 3 * WIDTH + 2 * D_MODEL
N_GROUPS = 4
EXPERTS_PER_GROUP = 8
N_EXPERTS = N_GROUPS * EXPERTS_PER_GROUP
D_EXPERT = D_MODEL // 4
N_MOD = 6
EPS = 1e-6
NEG = -1e30

CHUNK = 64
HEADS_PER_GROUP = 4
GROUP_W = HEADS_PER_GROUP * HEAD_DIM
LANES = 128
BF16_ROWS = 16
VMEM_LIMIT = 56 * 1024 * 1024

F32 = jnp.float32
BF16 = jnp.bfloat16

_NN = (((1,), (0,)), ((), ()))
_NT = (((1,), (1,)), ((), ()))
_TN = (((0,), (0,)), ((), ()))


def _dot(a, b, dims=_NN):
    return lax.dot_general(a, b, dims, preferred_element_type=F32)


def _mm(a, b, dims=_NN):
    return _dot(a.astype(BF16), b.astype(BF16), dims)


def _split2(x):
    hi = x.astype(BF16)
    lo = (x - hi.astype(F32)).astype(BF16)
    return hi, lo


def _split3(x):
    hi = x.astype(BF16)
    r = x - hi.astype(F32)
    mid = r.astype(BF16)
    lo = (r - mid.astype(F32)).astype(BF16)
    return hi, mid, lo


def _mm_lhs3(x, e):
    hi, mid, lo = _split3(x)
    return _dot(hi, e) + _dot(mid, e) + _dot(lo, e)


def _mm_rhs3(e, x):
    hi, mid, lo = _split3(x)
    return _dot(e, hi) + _dot(e, mid) + _dot(e, lo)


def _mm_lhs2(x, e):
    hi, lo = _split2(x)
    return _dot(hi, e) + _dot(lo, e)


def _mm3(a, b):
    a_hi, a_lo = _split2(a)
    b_hi, b_lo = _split2(b)
    return _dot(a_hi, b_hi) + _dot(a_hi, b_lo) + _dot(a_lo, b_hi)


def _iota(shape, dim):
    return lax.broadcasted_iota(jnp.int32, shape, dim)


def _sigmoid(x):
    return 1.0 / (1.0 + jnp.exp(-x))


def _silu(x):
    return x * _sigmoid(x)


def _softplus(x):
    return jnp.maximum(x, 0.0) + jnp.log1p(jnp.exp(-jnp.abs(x)))


def _head_ones(width):
    r = _iota((width, width), 0) >> 6
    c = _iota((width, width), 1) >> 6
    return jnp.where(r == c, 1.0, 0.0).astype(BF16)


def _params(sem):
    return pltpu.CompilerParams(dimension_semantics=sem, vmem_limit_bytes=VMEM_LIMIT)


def _mod_kernel(c_ref, w_ref, b_ref, o_ref):
    o_ref[0] = _mm3(_silu(c_ref[...]), w_ref[0]) + b_ref[0]


def _modulation(c, ada_w, ada_b):
    depth, d, n = ada_w.shape
    bsz = c.shape[0]
    tn = n // 4
    return pl.pallas_call(
        _mod_kernel,
        grid=(depth, n // tn),
        in_specs=[
            pl.BlockSpec((bsz, d), lambda l, j: (0, 0)),
            pl.BlockSpec((1, d, tn), lambda l, j: (l, 0, j)),
            pl.BlockSpec((1, 1, tn), lambda l, j: (l, 0, j)),
        ],
        out_specs=pl.BlockSpec((1, bsz, tn), lambda l, j: (l, 0, j)),
        out_shape=jax.ShapeDtypeStruct((depth, bsz, n), F32),
        compiler_params=_params(("parallel", "parallel")),
        name="adaln_mod",
    )(c, ada_w, ada_b.reshape(depth, 1, n))


HALO = BF16_ROWS


def _inproj_kernel(h_ref, hprev_ref, hnext_ref, sh_ref, sc_ref, g_ref, wconv_ref, wrest_ref, wg_ref, cw_ref, gp_ref,
                   aq_ref, ak_ref, av_ref, bq_ref, bk_ref, az_ref, bv_ref, bo_ref, mg_ref, gates_ref, p_ref):
    s = pl.program_id(1)
    ns = pl.num_programs(1)
    tm = h_ref.shape[1]
    gain = g_ref[...] * (1.0 + sc_ref[0])
    shift = sh_ref[0]

    def norm(x):
        ms = jnp.mean(x * x, axis=-1, keepdims=True)
        return x * lax.rsqrt(ms + EPS) * gain + shift

    hn = norm(h_ref[0])
    hp = norm(hprev_ref[0]) * jnp.where(s > 0, 1.0, 0.0)
    hx = norm(hnext_ref[0]) * jnp.where(s < ns - 1, 1.0, 0.0)
    hn_hi = hn.astype(BF16)
    hall = jnp.concatenate([hp.astype(BF16), hn_hi, hx.astype(BF16)], axis=0)
    p_ref[...] = _dot(hall, wconv_ref[...])

    conv = jnp.zeros((tm, CONV_W), F32)
    for j in range(CONV_K):
        conv = conv + p_ref[pl.ds(HALO - CONV_K // 2 + j, tm), :] * cw_ref[j:j + 1, :]
    cv = _silu(conv)

    ones_bd = _head_ones(WIDTH)

    def l2n(t):
        ss = _mm_lhs2(t * t, ones_bd)
        return t * lax.rsqrt(ss + EPS)

    aq_ref[0] = (l2n(cv[:, 0:WIDTH]) * (HEAD_DIM ** -0.5)).astype(BF16)
    ak_ref[0] = l2n(cv[:, WIDTH:2 * WIDTH]).astype(BF16)
    av_ref[0] = cv[:, 2 * WIDTH:3 * WIDTH].astype(BF16)
    bq_ref[0] = cv[:, 3 * WIDTH:4 * WIDTH].astype(BF16)
    bk_ref[0] = (cv[:, 4 * WIDTH:5 * WIDTH] * (HEAD_DIM ** -0.5)).astype(BF16)

    rest = _dot(hn_hi, wrest_ref[...])
    az_ref[0] = rest[:, 0:WIDTH]
    bv_ref[0] = rest[:, WIDTH:2 * WIDTH].astype(BF16)
    bo_ref[0] = rest[:, 2 * WIDTH:3 * WIDTH]
    mg_ref[0] = rest[:, 3 * WIDTH:]

    hn_lo = (hn - hn_hi.astype(F32)).astype(BF16)
    wg_hi, wg_lo = _split2(wg_ref[...])
    pre = _dot(hn_hi, wg_hi) + _dot(hn_hi, wg_lo) + _dot(hn_lo, wg_hi)
    pre = pre + gp_ref[0:1, :]
    lane = _iota(pre.shape, 1)
    second = ((lane >> 3) & 1) == 1
    is_gdn = lane < 4 * N_HEADS
    is_ml = (lane >= 4 * N_HEADS) & (lane < 8 * N_HEADS)
    sp = _softplus(pre)
    gdn_val = jnp.where(second, _sigmoid(pre), -jnp.exp(gp_ref[1:2, :]) * sp)
    ml_val = jnp.where(second, -_softplus(-pre), pre)
    gates_ref[0] = jnp.where(is_gdn, gdn_val, jnp.where(is_ml, ml_val, 0.0))


def _input_projection(h, sh, sc, gain, wconv, wrest, wg, cw, gp, tm):
    bsz, seq, d = h.shape
    nt = seq // tm
    per = tm // HALO
    nhalo = seq // HALO
    tile = lambda w: pl.BlockSpec((1, tm, w), lambda b, s: (b, s, 0))
    full = lambda a: pl.BlockSpec(a.shape, lambda b, s: (0,) * a.ndim)
    outs = [(WIDTH, BF16)] * 5 + [(WIDTH, F32), (WIDTH, BF16), (WIDTH, F32), (2 * D_MODEL, F32), (LANES, F32)]
    return pl.pallas_call(
        _inproj_kernel,
        grid=(bsz, nt),
        in_specs=[
            tile(d),
            pl.BlockSpec((1, HALO, d), lambda b, s: (b, jnp.maximum(s * per - 1, 0), 0)),
            pl.BlockSpec((1, HALO, d), lambda b, s: (b, jnp.minimum((s + 1) * per, nhalo - 1), 0)),
            pl.BlockSpec((1, 1, d), lambda b, s: (b, 0, 0)),
            pl.BlockSpec((1, 1, d), lambda b, s: (b, 0, 0)),
            full(gain), full(wconv), full(wrest), full(wg), full(cw), full(gp),
        ],
        out_specs=[tile(w) for w, _ in outs],
        out_shape=[jax.ShapeDtypeStruct((bsz, seq, w), dt) for w, dt in outs],
        scratch_shapes=[pltpu.VMEM((tm + 2 * HALO, CONV_W), F32)],
        compiler_params=_params(("parallel", "parallel")),
        name="input_projection",
    )(h, h, h, sh, sc, gain, wconv, wrest, wg, cw, gp)


def _scan_masks(direction):
    c, w = CHUNK, GROUP_W
    row = _iota((c, w), 0)
    col = _iota((c, w), 1) & (c - 1)
    ri = _iota((c, c), 0)
    ci = _iota((c, c), 1)
    if direction == 0:
        return row, col, row >= col, row > col, jnp.where(ri >= ci, 1.0, 0.0).astype(BF16), c - 1
    return row, col, row <= col, row < col, jnp.where(ri <= ci, 1.0, 0.0).astype(BF16), 0


def _expander(base):
    m = _iota((LANES, GROUP_W), 0)
    h = _iota((LANES, GROUP_W), 1) >> 6
    return jnp.where(m == base + h, 1.0, 0.0).astype(BF16)


def _block_diag(x_cat, bd_bf16):
    xb = x_cat.astype(BF16)
    return jnp.concatenate([xb] * HEADS_PER_GROUP, axis=0) * bd_bf16


def _transpose_rows(x_e, dmask, ones_cc):
    return _mm_rhs3(ones_cc, jnp.where(dmask, x_e, 0.0))


def _gdn_kernel(qf_ref, kf_ref, vf_ref, gf_ref, qb_ref, kb_ref, vb_ref, gb_ref, of_ref, ob_ref, s_ref):
    n = pl.program_id(1)

    @pl.when(n == 0)
    def _():
        s_ref[...] = jnp.zeros_like(s_ref)

    c, w = CHUNK, GROUP_W
    bd = (_iota((w, w), 0) >> 6) == (_iota((w, w), 1) >> 6)
    bd_b = jnp.where(bd, 1.0, 0.0).astype(BF16)
    ones_cc = jnp.ones((c, c), BF16)
    dirs = ((qf_ref, kf_ref, vf_ref, gf_ref, of_ref), (qb_ref, kb_ref, vb_ref, gb_ref, ob_ref))
    for d, (q_ref, k_ref, v_ref, g_ref, o_ref) in enumerate(dirs):
        row, col, incl, strict, tri, last = _scan_masks(d)
        dmask = row == col
        gates = g_ref[0]
        gcum = _mm_rhs3(tri, gates)
        for hg in range(N_HEADS // HEADS_PER_GROUP):
            lo = hg * w
            base = d * 2 * N_HEADS + hg * HEADS_PER_GROUP
            q = q_ref[0, :, lo:lo + w]
            k = k_ref[0, :, lo:lo + w]
            v = v_ref[0, :, lo:lo + w].astype(F32)
            kf = k.astype(F32)
            gc = _mm_lhs3(gcum, _expander(base))
            beta = _mm_lhs3(gates, _expander(base + N_HEADS))
            gc_t = _transpose_rows(gc, dmask, ones_cc)
            g_last = gc[last:last + 1, :]
            decay = jnp.exp(jnp.where(incl, gc - gc_t, NEG))
            kbeta = kf * beta
            k_bd = jnp.concatenate([k] * HEADS_PER_GROUP, axis=0) * bd_b
            qk_kk = _dot(jnp.concatenate([q, kbeta.astype(BF16)], axis=0), k_bd, _NT)
            intra = qk_kk[:c] * decay
            a = jnp.where(strict, qk_kk[c:] * decay, 0.0)

            def level(p):
                same = (row >> (p + 1)) == (col >> (p + 1))
                r_bit = (row >> p) & 1
                c_bit = (col >> p) & 1
                if d == 0:
                    return same & (r_bit == 1) & (c_bit == 0)
                return same & (r_bit == 0) & (c_bit == 1)

            t = jnp.where(dmask, 1.0, 0.0) - jnp.where(level(0), a, 0.0)
            for p in range(1, 6):
                x = jnp.where(level(p), a, 0.0)
                y = _mm(t, _block_diag(x, bd_b))
                t = t - _mm(y, _block_diag(t, bd_b))

            e_gc = jnp.exp(gc)
            u = _mm(t, _block_diag(v * beta, bd_b))
            wmat = _mm(t, _block_diag(kbeta * e_gc, bd_b))
            idx = d * 2 + hg
            state = s_ref[idx]
            qe = q.astype(F32) * e_gc
            ws_qs = _mm(jnp.concatenate([wmat, qe], axis=0), state)
            v_new = u - ws_qs[:c]
            o_ref[0, :, lo:lo + w] = ws_qs[c:] + _mm(intra, _block_diag(v_new, bd_b))
            k_end = kf * jnp.exp(g_last - gc)
            ds = _mm(k_end, v_new, _TN)
            s_ref[idx] = state * jnp.exp(g_last) + jnp.where(bd, ds, 0.0)


def _mlstm_kernel(qf_ref, kf_ref, vf_ref, gf_ref, qb_ref, kb_ref, vb_ref, gb_ref, of_ref, ob_ref,
                  c_ref, n_ref, m_ref):
    n = pl.program_id(1)

    @pl.when(n == 0)
    def _():
        c_ref[...] = jnp.zeros_like(c_ref)
        n_ref[...] = jnp.zeros_like(n_ref)
        m_ref[...] = jnp.zeros_like(m_ref)

    c, w = CHUNK, GROUP_W
    bd = (_iota((w, w), 0) >> 6) == (_iota((w, w), 1) >> 6)
    bd_b = jnp.where(bd, 1.0, 0.0).astype(BF16)
    ones_cc = jnp.ones((c, c), BF16)
    ones_cw = jnp.ones((c, w), BF16)
    dirs = ((qf_ref, kf_ref, vf_ref, gf_ref, of_ref), (qb_ref, kb_ref, vb_ref, gb_ref, ob_ref))
    for d, (q_ref, k_ref, v_ref, g_ref, o_ref) in enumerate(dirs):
        row, col, incl, _, tri, last = _scan_masks(d)
        dmask = row == col
        gates = g_ref[0]
        gcum = _mm_rhs3(tri, gates)
        for hg in range(N_HEADS // HEADS_PER_GROUP):
            lo = hg * w
            base = 4 * N_HEADS + d * 2 * N_HEADS + hg * HEADS_PER_GROUP
            q = q_ref[0, :, lo:lo + w]
            k = k_ref[0, :, lo:lo + w]
            v = v_ref[0, :, lo:lo + w]
            i_pre = _mm_lhs3(gates, _expander(base))
            bcum = _mm_lhs3(gcum, _expander(base + N_HEADS))
            r = i_pre - bcum
            pm = r
            for p in range(6):
                sh = 1 << p
                if d == 0:
                    pm = jnp.maximum(pm, jnp.where(row >= sh, pltpu.roll(pm, sh, 0), NEG))
                else:
                    pm = jnp.maximum(pm, jnp.where(row < c - sh, pltpu.roll(pm, c - sh, 0), NEG))
            idx = d * 2 + hg
            m_prev = m_ref[idx, 0:1, :]
            mx = jnp.maximum(m_prev, pm)
            r_t = _transpose_rows(r, dmask, ones_cc)
            d_mat = jnp.exp(jnp.where(incl, r_t - mx, NEG))
            inter_w = jnp.exp(m_prev - mx)
            k_bd = jnp.concatenate([k] * HEADS_PER_GROUP, axis=0) * bd_b
            qk = _dot(q, k_bd, _NT) * d_mat
            c_state = c_ref[idx]
            n_state = n_ref[idx]
            num = _mm(qk, _block_diag(v, bd_b)) + inter_w * _mm(q, c_state)
            den = _mm_lhs2(qk, bd_b) + inter_w * _mm(q, n_state)
            o_ref[0, :, lo:lo + w] = num / jnp.maximum(jnp.abs(den), jnp.exp(-(bcum + mx)))

            b_last = bcum[last:last + 1, :]
            m_next = b_last + jnp.maximum(m_prev, pm[last:last + 1, :])
            scale_prev = jnp.exp(b_last + m_prev - m_next)
            wgt = jnp.exp(b_last + r - m_next)
            kw = (k.astype(F32) * wgt).astype(BF16)
            c_ref[idx] = c_state * scale_prev + jnp.where(bd, _dot(kw, v, _TN), 0.0)
            n_ref[idx] = n_state * scale_prev + jnp.where(bd, _dot(kw, ones_cw, _TN), 0.0)
            m_ref[idx, 0:1, :] = m_next


def _scan_call(body, name, q, k, v, gates, scratch):
    bsz, seq, width = q.shape
    nc = seq // CHUNK
    fwd = lambda wd: pl.BlockSpec((1, CHUNK, wd), lambda b, n: (b, n, 0))
    bwd = lambda wd: pl.BlockSpec((1, CHUNK, wd), lambda b, n: (b, nc - 1 - n, 0))
    return pl.pallas_call(
        body,
        grid=(bsz, nc),
        in_specs=[fwd(width), fwd(width), fwd(width), fwd(LANES), bwd(width), bwd(width), bwd(width), bwd(LANES)],
        out_specs=[fwd(width), bwd(width)],
        out_shape=[jax.ShapeDtypeStruct((bsz, seq, width), F32)] * 2,
        scratch_shapes=scratch,
        compiler_params=_params(("parallel", "arbitrary")),
        name=name,
    )(q, k, v, gates, q, k, v, gates)


def _gdn_scan(q, k, v, gates):
    return _scan_call(_gdn_kernel, "gdn_scan", q, k, v, gates, [pltpu.VMEM((4, GROUP_W, GROUP_W), F32)])


def _mlstm_scan(q, k, v, gates):
    state = pltpu.VMEM((4, GROUP_W, GROUP_W), F32)
    return _scan_call(_mlstm_kernel, "mlstm_scan", q, k, v, gates, [state, state, pltpu.VMEM((4, 8, GROUP_W), F32)])


def _outproj_kernel(h_ref, gaf_ref, gab_ref, mlf_ref, mlb_ref, az_ref, bo_ref, mg_ref, gt_ref, ng_ref, mlg_ref,
                    wa_ref, wb_ref, wo_ref, sh_ref, sc_ref, fg_ref, wr_ref, hout_ref, hn_ref, comb_ref):
    ones_bd = _head_ones(WIDTH)
    inv = 1.0 / HEAD_DIM
    oa = gaf_ref[0] + gab_ref[0]
    ms = _mm_lhs2(oa * oa, ones_bd) * inv
    oa = oa * lax.rsqrt(ms + EPS) * ng_ref[...] * _silu(az_ref[0])
    hb = mlf_ref[0] + mlb_ref[0]
    mu = _mm_lhs3(hb, ones_bd) * inv
    tc = hb - mu
    var = _mm_lhs2(tc * tc, ones_bd) * inv
    hb = tc * lax.rsqrt(var + EPS) * mlg_ref[...] * _sigmoid(bo_ref[0])
    gate = _sigmoid(mg_ref[0])
    y = gate[:, :D_MODEL] * _mm(oa, wa_ref[...]) + gate[:, D_MODEL:] * _mm(hb, wb_ref[...])
    h_new = h_ref[0] + gt_ref[0] * _mm(y, wo_ref[...])
    hout_ref[0] = h_new

    ms2 = jnp.mean(h_new * h_new, axis=-1, keepdims=True)
    hn = h_new * lax.rsqrt(ms2 + EPS) * (fg_ref[...] * (1.0 + sc_ref[0])) + sh_ref[0]
    hn_ref[0] = hn.astype(BF16)

    logits = _mm3(hn, wr_ref[...])
    lane = _iota(logits.shape, 1)
    lane_f = lane.astype(F32)
    big = 1e9
    gl = jnp.where((lane >= N_EXPERTS) & (lane < N_EXPERTS + N_GROUPS), logits, NEG)
    gmax = jnp.max(gl, axis=-1, keepdims=True)
    gidx = jnp.min(jnp.where(gl == gmax, lane_f - N_EXPERTS, big), axis=-1, keepdims=True)
    p_group = 1.0 / jnp.sum(jnp.exp(gl - gmax), axis=-1, keepdims=True)
    el = jnp.where((lane < N_EXPERTS) & ((lane >> 3).astype(F32) == gidx), logits, NEG)
    v1 = jnp.max(el, axis=-1, keepdims=True)
    i1 = jnp.min(jnp.where(el == v1, lane_f, big), axis=-1, keepdims=True)
    el2 = jnp.where(lane_f == i1, NEG, el)
    v2 = jnp.max(el2, axis=-1, keepdims=True)
    i2 = jnp.min(jnp.where(el2 == v2, lane_f, big), axis=-1, keepdims=True)
    e21 = jnp.exp(v2 - v1)
    w1 = p_group / (1.0 + e21)
    comb_ref[0] = jnp.where(lane_f == i1, w1, 0.0) + jnp.where(lane_f == i2, w1 * e21, 0.0)


def _output_projection(h, gaf, gab, mlf, mlb, az, bo, mg, gt, ng, mlg, wa, wb, wo, sh, sc, fg, wr, tm):
    bsz, seq, d = h.shape
    tile = lambda w: pl.BlockSpec((1, tm, w), lambda b, s: (b, s, 0))
    full = lambda a: pl.BlockSpec(a.shape, lambda b, s: (0,) * a.ndim)
    per_b = pl.BlockSpec((1, 1, d), lambda b, s: (b, 0, 0))
    return pl.pallas_call(
        _outproj_kernel,
        grid=(bsz, seq // tm),
        in_specs=[tile(d), tile(WIDTH), tile(WIDTH), tile(WIDTH), tile(WIDTH), tile(WIDTH), tile(WIDTH),
                  tile(2 * D_MODEL), per_b, full(ng), full(mlg), full(wa), full(wb), full(wo), per_b, per_b,
                  full(fg), full(wr)],
        out_specs=[tile(d), tile(d), tile(LANES)],
        out_shape=[jax.ShapeDtypeStruct((bsz, seq, d), F32), jax.ShapeDtypeStruct((bsz, seq, d), BF16),
                   jax.ShapeDtypeStruct((bsz, seq, LANES), F32)],
        compiler_params=_params(("parallel", "parallel")),
        name="output_projection",
    )(h, gaf, gab, mlf, mlb, az, bo, mg, gt, ng, mlg, wa, wb, wo, sh, sc, fg, wr)


def _moe_kernel(h_ref, hn_ref, comb_ref, gt_ref, wg_ref, wu_ref, wd_ref, fin_ref, o_ref, acc_ref, *, final_norm):
    e = pl.program_id(2)

    @pl.when(e == 0)
    def _():
        acc_ref[...] = jnp.zeros_like(acc_ref)

    x = hn_ref[0]
    comb = comb_ref[0]
    w_e = jnp.sum(jnp.where(_iota(comb.shape, 1) == e, comb, 0.0), axis=-1, keepdims=True)
    act = _silu(_dot(x, wg_ref[0])) * _dot(x, wu_ref[0]) * w_e
    acc_ref[...] += _mm(act, wd_ref[0])

    @pl.when(e == pl.num_programs(2) - 1)
    def _():
        out = h_ref[0] + gt_ref[0] * acc_ref[...]
        if final_norm:
            ms = jnp.mean(out * out, axis=-1, keepdims=True)
            out = out * lax.rsqrt(ms + EPS) * fin_ref[...]
        o_ref[0] = out


def _moe(h, hn, comb, gt, wg, wu, wd, fin, tm, final_norm):
    bsz, seq, d = h.shape
    tile = lambda w: pl.BlockSpec((1, tm, w), lambda b, s, e: (b, s, 0))
    return pl.pallas_call(
        functools.partial(_moe_kernel, final_norm=final_norm),
        grid=(bsz, seq // tm, N_EXPERTS),
        in_specs=[tile(d), tile(d), tile(LANES), pl.BlockSpec((1, 1, d), lambda b, s, e: (b, 0, 0)),
                  pl.BlockSpec((1, d, D_EXPERT), lambda b, s, e: (e, 0, 0)),
                  pl.BlockSpec((1, d, D_EXPERT), lambda b, s, e: (e, 0, 0)),
                  pl.BlockSpec((1, D_EXPERT, d), lambda b, s, e: (e, 0, 0)),
                  pl.BlockSpec((1, d), lambda b, s, e: (0, 0))],
        out_specs=tile(d),
        out_shape=jax.ShapeDtypeStruct((bsz, seq, d), F32),
        scratch_shapes=[pltpu.VMEM((tm, d), F32)],
        compiler_params=_params(("parallel", "parallel", "arbitrary")),
        name="expert_ffn",
    )(h, hn, comb, gt, wg, wu, wd, fin)


def _pick_tile(seq, want):
    tm = min(seq, want)
    assert seq % tm == 0 and tm % CHUNK == 0
    return tm


def kernel(x, c, ada_w, ada_b, norm_mix_g, norm_ffn_g, w_in, gdn_conv_w, gdn_a_log, gdn_dt_bias, gdn_norm_g, mlstm_conv_w, mlstm_i_bias, mlstm_f_bias, mlstm_norm_g, w_branch_a, w_branch_b, w_out, router_group, router_expert, w_gate, w_up, w_down, final_norm_g):
    bsz, seq, d = x.shape
    depth = ada_w.shape[0]
    assert d == D_MODEL and seq % CHUNK == 0
    tm_in = _pick_tile(seq, 256)
    tm_out = _pick_tile(seq, 256)
    tm_moe = _pick_tile(seq, 1024)
    row = lambda a: a.reshape(1, -1).astype(F32)

    mod = _modulation(c.astype(F32), ada_w.astype(F32), ada_b.astype(F32))
    h = x.astype(F32)
    for l in range(depth):
        sh1, sc1, gt1, sh2, sc2, gt2 = [mod[l, :, i * d:(i + 1) * d].reshape(bsz, 1, d) for i in range(N_MOD)]

        w = w_in[l].astype(F32)
        o_z = 3 * WIDTH
        o_ag = o_z + WIDTH
        o_bqk = o_ag + 4 * N_HEADS
        o_bv = o_bqk + 2 * WIDTH
        o_bo = o_bv + WIDTH
        o_bg = o_bo + WIDTH
        o_mg = o_bg + 4 * N_HEADS
        wconv = jnp.concatenate([w[:, :o_z], w[:, o_bqk:o_bv]], axis=1).astype(BF16)
        wrest = jnp.concatenate([w[:, o_z:o_ag], w[:, o_bv:o_bo], w[:, o_bo:o_bg], w[:, o_mg:]], axis=1).astype(BF16)
        wg = jnp.concatenate([w[:, o_ag:o_bqk], w[:, o_bg:o_mg], jnp.zeros((d, LANES - 8 * N_HEADS), F32)], axis=1)
        cw = jnp.concatenate([gdn_conv_w[l], mlstm_conv_w[l]], axis=1).astype(F32)
        cw = jnp.concatenate([cw, jnp.zeros((8 - CONV_K, CONV_W), F32)], axis=0)
        zero8 = jnp.zeros((N_HEADS,), F32)
        bias = jnp.concatenate([gdn_dt_bias[l, 0], zero8, gdn_dt_bias[l, 1], zero8,
                                mlstm_i_bias[l, 0], mlstm_f_bias[l, 0], mlstm_i_bias[l, 1], mlstm_f_bias[l, 1],
                                jnp.zeros((LANES - 8 * N_HEADS,), F32)]).astype(F32)
        alog = jnp.concatenate([gdn_a_log[l, 0], zero8, gdn_a_log[l, 1], zero8,
                                jnp.zeros((LANES - 4 * N_HEADS,), F32)]).astype(F32)
        gp = jnp.concatenate([bias[None], alog[None], jnp.zeros((6, LANES), F32)], axis=0)

        aq, ak, av, bq, bk, az, bv, bo, mg, gates = _input_projection(
            h, sh1, sc1, row(norm_mix_g[l]), wconv, wrest, wg, cw, gp, tm_in)
        gaf, gab = _gdn_scan(aq, ak, av, gates)
        mlf, mlb = _mlstm_scan(bq, bk, bv, gates)

        wr = jnp.concatenate([router_expert[l].astype(F32), router_group[l].astype(F32),
                              jnp.zeros((d, LANES - N_EXPERTS - N_GROUPS), F32)], axis=1)
        h, hn2, comb = _output_projection(
            h, gaf, gab, mlf, mlb, az, bo, mg, gt1, row(jnp.tile(gdn_norm_g[l], N_HEADS)), row(mlstm_norm_g[l]),
            w_branch_a[l].astype(BF16), w_branch_b[l].astype(BF16), w_out[l].astype(BF16),
            sh2, sc2, row(norm_ffn_g[l]), wr, tm_out)
        h = _moe(h, hn2, comb, gt2, w_gate[l].astype(BF16), w_up[l].astype(BF16), w_down[l].astype(BF16),
                 row(final_norm_g), tm_moe, final_norm=(l == depth - 1))
    return h.astype(x.dtype)
```

```python
import functools

import numpy as np
import jax
import jax.numpy as jnp
from jax import lax
from jax.experimental import pallas as pl
from jax.experimental.pallas import tpu as pltpu

D_MODEL = 1024
HEAD_DIM = 64
N_HEADS = 8
WIDTH = N_HEADS * HEAD_DIM
CONV_K = 5
CONV_W = 3 * WIDTH + 2 * WIDTH
REST_W = 3 * WIDTH + 2 * D_MODEL
N_GROUPS = 4
EXPERTS_PER_GROUP = 8
N_EXPERTS = N_GROUPS * EXPERTS_PER_GROUP
D_EXPERT = D_MODEL // 4
N_MOD = 6
EPS = 1e-6
NEG = -1e30

CHUNK = 64
HEADS_PER_GROUP = 4
N_HEAD_GROUPS = N_HEADS // HEADS_PER_GROUP
GROUP_W = HEADS_PER_GROUP * HEAD_DIM
LEVELS = 6
LANES = 128
SUBLANES = 8
BF16_ROWS = 16
VMEM_LIMIT = 56 * 1024 * 1024

F32 = jnp.float32
BF16 = jnp.bfloat16

_NN = (((1,), (0,)), ((), ()))
_NT = (((1,), (1,)), ((), ()))
_TN = (((0,), (0,)), ((), ()))

M_INCL, M_STRICT, M_LEVEL0, M_DIAG = 0, 1, 2, 2 + LEVELS
N_MASKS = M_DIAG + 1


def _dot(a, b, dims=_NN):
    return lax.dot_general(a, b, dims, preferred_element_type=F32)


def _mm(a, b, dims=_NN):
    return _dot(a.astype(BF16), b.astype(BF16), dims)


def _split2(x):
    hi = x.astype(BF16)
    lo = (x - hi.astype(F32)).astype(BF16)
    return hi, lo


def _split3(x):
    hi = x.astype(BF16)
    r = x - hi.astype(F32)
    mid = r.astype(BF16)
    lo = (r - mid.astype(F32)).astype(BF16)
    return hi, mid, lo


def _dot_pieces(pieces, e):
    out = _dot(pieces[0], e)
    for p in pieces[1:]:
        out = out + _dot(p, e)
    return out


def _mm_lhs3(x, e):
    return _dot_pieces(_split3(x), e)


def _mm_rhs3(e, x):
    hi, mid, lo = _split3(x)
    return _dot(e, hi) + _dot(e, mid) + _dot(e, lo)


def _mm_lhs2(x, e):
    return _dot_pieces(_split2(x), e)


def _mm3(a, b):
    a_hi, a_lo = _split2(a)
    b_hi, b_lo = _split2(b)
    return _dot(a_hi, b_hi) + _dot(a_hi, b_lo) + _dot(a_lo, b_hi)


def _iota(shape, dim):
    return lax.broadcasted_iota(jnp.int32, shape, dim)


def _sigmoid(x):
    return 1.0 / (1.0 + jnp.exp(-x))


def _silu(x):
    return x * _sigmoid(x)


def _softplus(x):
    return jnp.maximum(x, 0.0) + jnp.log1p(jnp.exp(-jnp.abs(x)))


def _head_ones(width):
    r = _iota((width, width), 0) >> 6
    c = _iota((width, width), 1) >> 6
    return jnp.where(r == c, 1.0, 0.0).astype(BF16)


def _params(sem):
    return pltpu.CompilerParams(dimension_semantics=sem, vmem_limit_bytes=VMEM_LIMIT)


def _full(a):
    return pl.BlockSpec(a.shape, lambda *_: (0,) * a.ndim)


def _mod_kernel(c_ref, w_ref, b_ref, o_ref):
    o_ref[0] = _mm3(_silu(c_ref[...]), w_ref[0]) + b_ref[0]


def _modulation(c, ada_w, ada_b):
    depth, d, n = ada_w.shape
    bsz = c.shape[0]
    tn = n // 4
    return pl.pallas_call(
        _mod_kernel,
        grid=(depth, n // tn),
        in_specs=[
            pl.BlockSpec((bsz, d), lambda l, j: (0, 0)),
            pl.BlockSpec((1, d, tn), lambda l, j: (l, 0, j)),
            pl.BlockSpec((1, 1, tn), lambda l, j: (l, 0, j)),
        ],
        out_specs=pl.BlockSpec((1, bsz, tn), lambda l, j: (l, 0, j)),
        out_shape=jax.ShapeDtypeStruct((depth, bsz, n), F32),
        compiler_params=_params(("parallel", "parallel")),
        name="adaln_mod",
    )(c, ada_w, ada_b.reshape(depth, 1, n))


HALO = BF16_ROWS


def _inproj_kernel(h_ref, hprev_ref, hnext_ref, sh_ref, sc_ref, g_ref, wconv_ref, wrest_ref, wg_ref, cw_ref, gp_ref,
                   aq_ref, ak_ref, av_ref, bq_ref, bk_ref, az_ref, bv_ref, bo_ref, mg_ref, gates_ref, cumf_ref,
                   cumb_ref, p_ref):
    s = pl.program_id(1)
    ns = pl.num_programs(1)
    tm = h_ref.shape[1]
    gain = g_ref[...] * (1.0 + sc_ref[0])
    shift = sh_ref[0]

    def norm(x):
        ms = jnp.mean(x * x, axis=-1, keepdims=True)
        return x * lax.rsqrt(ms + EPS) * gain + shift

    hn = norm(h_ref[0])
    hp = norm(hprev_ref[0]) * jnp.where(s > 0, 1.0, 0.0)
    hx = norm(hnext_ref[0]) * jnp.where(s < ns - 1, 1.0, 0.0)
    hn_hi = hn.astype(BF16)
    hall = jnp.concatenate([hp.astype(BF16), hn_hi, hx.astype(BF16)], axis=0)
    p_ref[...] = _dot(hall, wconv_ref[...])

    conv = jnp.zeros((tm, CONV_W), F32)
    for j in range(CONV_K):
        conv = conv + p_ref[pl.ds(HALO - CONV_K // 2 + j, tm), :] * cw_ref[j:j + 1, :]
    cv = _silu(conv)

    ones_bd = _head_ones(WIDTH)

    def l2n(t):
        ss = _mm_lhs2(t * t, ones_bd)
        return t * lax.rsqrt(ss + EPS)

    aq_ref[0] = (l2n(cv[:, 0:WIDTH]) * (HEAD_DIM ** -0.5)).astype(BF16)
    ak_ref[0] = l2n(cv[:, WIDTH:2 * WIDTH]).astype(BF16)
    av_ref[0] = cv[:, 2 * WIDTH:3 * WIDTH].astype(BF16)
    bq_ref[0] = cv[:, 3 * WIDTH:4 * WIDTH].astype(BF16)
    bk_ref[0] = (cv[:, 4 * WIDTH:5 * WIDTH] * (HEAD_DIM ** -0.5)).astype(BF16)

    rest = _dot(hn_hi, wrest_ref[...])
    az_ref[0] = rest[:, 0:WIDTH]
    bv_ref[0] = rest[:, WIDTH:2 * WIDTH].astype(BF16)
    bo_ref[0] = rest[:, 2 * WIDTH:3 * WIDTH]
    mg_ref[0] = rest[:, 3 * WIDTH:]

    hn_lo = (hn - hn_hi.astype(F32)).astype(BF16)
    wg_hi, wg_lo = _split2(wg_ref[...])
    pre = _dot(hn_hi, wg_hi) + _dot(hn_hi, wg_lo) + _dot(hn_lo, wg_hi)
    pre = pre + gp_ref[0:1, :]
    lane = _iota(pre.shape, 1)
    second = ((lane >> 3) & 1) == 1
    is_gdn = lane < 4 * N_HEADS
    is_ml = (lane >= 4 * N_HEADS) & (lane < 8 * N_HEADS)
    sp = _softplus(pre)
    gdn_val = jnp.where(second, _sigmoid(pre), -jnp.exp(gp_ref[1:2, :]) * sp)
    ml_val = jnp.where(second, -_softplus(-pre), pre)
    gates = jnp.where(is_gdn, gdn_val, jnp.where(is_ml, ml_val, 0.0))
    gates_ref[0] = gates

    ri = _iota((tm, tm), 0)
    ci = _iota((tm, tm), 1)
    same_chunk = (ri >> 6) == (ci >> 6)
    pieces = _split3(gates)
    for ref, tri in ((cumf_ref, same_chunk & (ri >= ci)), (cumb_ref, same_chunk & (ri <= ci))):
        tri_b = jnp.where(tri, 1.0, 0.0).astype(BF16)
        ref[0] = _dot(tri_b, pieces[0]) + _dot(tri_b, pieces[1]) + _dot(tri_b, pieces[2])


def _input_projection(h, sh, sc, gain, wconv, wrest, wg, cw, gp, tm):
    bsz, seq, d = h.shape
    nt = seq // tm
    per = tm // HALO
    nhalo = seq // HALO
    tile = lambda w: pl.BlockSpec((1, tm, w), lambda b, s: (b, s, 0))
    outs = ([(WIDTH, BF16)] * 5 + [(WIDTH, F32), (WIDTH, BF16), (WIDTH, F32), (2 * D_MODEL, F32)]
            + [(LANES, F32)] * 3)
    return pl.pallas_call(
        _inproj_kernel,
        grid=(bsz, nt),
        in_specs=[
            tile(d),
            pl.BlockSpec((1, HALO, d), lambda b, s: (b, jnp.maximum(s * per - 1, 0), 0)),
            pl.BlockSpec((1, HALO, d), lambda b, s: (b, jnp.minimum((s + 1) * per, nhalo - 1), 0)),
            pl.BlockSpec((1, 1, d), lambda b, s: (b, 0, 0)),
            pl.BlockSpec((1, 1, d), lambda b, s: (b, 0, 0)),
            _full(gain), _full(wconv), _full(wrest), _full(wg), _full(cw), _full(gp),
        ],
        out_specs=[tile(w) for w, _ in outs],
        out_shape=[jax.ShapeDtypeStruct((bsz, seq, w), dt) for w, dt in outs],
        scratch_shapes=[pltpu.VMEM((tm + 2 * HALO, CONV_W), F32)],
        compiler_params=_params(("parallel", "parallel")),
        name="input_projection",
    )(h, h, h, sh, sc, gain, wconv, wrest, wg, cw, gp)


def _scan_constants():
    c, w = CHUNK, GROUP_W
    row = np.arange(c)[:, None]
    col = (np.arange(w) % c)[None, :]
    masks = np.zeros((2, N_MASKS, c, w), np.float32)
    for d in range(2):
        masks[d, M_INCL] = (row >= col) if d == 0 else (row <= col)
        masks[d, M_STRICT] = (row > col) if d == 0 else (row < col)
        for p in range(LEVELS):
            same = (row >> (p + 1)) == (col >> (p + 1))
            r_bit = (row >> p) & 1
            c_bit = (col >> p) & 1
            masks[d, M_LEVEL0 + p] = same & ((r_bit == 1) & (c_bit == 0) if d == 0 else (r_bit == 0) & (c_bit == 1))
        masks[d, M_DIAG] = row == col
    bd = (np.arange(w)[:, None] // HEAD_DIM) == (np.arange(w)[None, :] // HEAD_DIM)
    exp = np.zeros((2, 2, N_HEAD_GROUPS, 2, LANES, w), np.float32)
    lane_head = np.arange(w) // HEAD_DIM
    for br in range(2):
        for d in range(2):
            for hg in range(N_HEAD_GROUPS):
                for kind in range(2):
                    base = br * 4 * N_HEADS + d * 2 * N_HEADS + kind * N_HEADS + hg * HEADS_PER_GROUP
                    exp[br, d, hg, kind, base + lane_head, np.arange(w)] = 1.0
    return (jnp.asarray(masks), jnp.asarray(bd.astype(np.float32)),
            jnp.asarray(bd.astype(np.float32)).astype(BF16), jnp.asarray(exp).astype(BF16))


def _block_diag(x_cat, bd_bf16):
    xb = x_cat.astype(BF16)
    return jnp.concatenate([xb] * HEADS_PER_GROUP, axis=0) * bd_bf16


def _transpose_rows(x_e, diag):
    ones = jnp.ones((SUBLANES, CHUNK), BF16)
    return _mm_rhs3(ones, x_e * diag)[0:1, :]


def _gdn_local_kernel(q_ref, k_ref, v_ref, g_ref, cumf_ref, cumb_ref, masks_ref, bdb_ref, exp_ref,
                      uf_ref, wf_ref, qef_ref, inf_ref, kef_ref, cdf_ref,
                      ub_ref, wb_ref, qeb_ref, inb_ref, keb_ref, cdb_ref):
    c, w = CHUNK, GROUP_W
    bd_b = bdb_ref[...]
    outs = ((uf_ref, wf_ref, qef_ref, inf_ref, kef_ref, cdf_ref), (ub_ref, wb_ref, qeb_ref, inb_ref, keb_ref, cdb_ref))
    chains = [(ci, d, hg) for ci in range(q_ref.shape[1] // c) for d in range(2) for hg in range(N_HEAD_GROUPS)]
    rows = lambda ci: slice(ci * c, (ci + 1) * c)
    lanes = lambda hg: slice(hg * w, (hg + 1) * w)

    g_pieces = {ci: _split2(g_ref[0, rows(ci), :]) for ci, _, _ in chains}
    cum_pieces = {(ci, d): _split3((cumf_ref if d == 0 else cumb_ref)[0, rows(ci), :]) for ci, d, _ in chains}
    gc = [_dot_pieces(cum_pieces[ci, d], exp_ref[0, d, hg, 0]) for ci, d, hg in chains]
    beta = [_dot_pieces(g_pieces[ci], exp_ref[0, d, hg, 1]) for ci, d, hg in chains]
    gc_t = [_transpose_rows(gc[i], masks_ref[d, M_DIAG]) for i, (_, d, _) in enumerate(chains)]

    qk_kk, kbeta = [], []
    for i, (ci, d, hg) in enumerate(chains):
        q = q_ref[0, rows(ci), lanes(hg)]
        k = k_ref[0, rows(ci), lanes(hg)]
        kbeta.append(k.astype(F32) * beta[i])
        k_bd = jnp.concatenate([k] * HEADS_PER_GROUP, axis=0) * bd_b
        qk_kk.append(_dot(jnp.concatenate([q, kbeta[i].astype(BF16)], axis=0), k_bd, _NT))

    a, t = [], []
    for i, (ci, d, hg) in enumerate(chains):
        decay = jnp.exp(jnp.minimum(gc[i] - gc_t[i], 0.0)) * masks_ref[d, M_INCL]
        outs[d][3][0, rows(ci), lanes(hg)] = (qk_kk[i][:c] * decay).astype(BF16)
        a.append(qk_kk[i][c:] * decay * masks_ref[d, M_STRICT])
        t.append(masks_ref[d, M_DIAG] - a[i] * masks_ref[d, M_LEVEL0])

    for p in range(1, LEVELS):
        y = [_mm(t[i], _block_diag(a[i] * masks_ref[d, M_LEVEL0 + p], bd_b)) for i, (_, d, _) in enumerate(chains)]
        t = [t[i] - _mm(y[i], _block_diag(t[i], bd_b)) for i in range(len(chains))]

    for i, (ci, d, hg) in enumerate(chains):
        u_ref, w_ref, qe_ref, _, ke_ref, cd_ref = outs[d]
        last = c - 1 if d == 0 else 0
        g_last = gc[i][last:last + 1, :]
        e_gc = jnp.exp(gc[i])
        v = v_ref[0, rows(ci), lanes(hg)].astype(F32)
        u_ref[0, rows(ci), lanes(hg)] = _mm(t[i], _block_diag(v * beta[i], bd_b))
        w_ref[0, rows(ci), lanes(hg)] = _mm(t[i], _block_diag(kbeta[i] * e_gc, bd_b)).astype(BF16)
        qe_ref[0, rows(ci), lanes(hg)] = (q_ref[0, rows(ci), lanes(hg)].astype(F32) * e_gc).astype(BF16)
        k = k_ref[0, rows(ci), lanes(hg)].astype(F32)
        ke_ref[0, rows(ci), lanes(hg)] = (k * jnp.exp(g_last - gc[i])).astype(BF16)
        cd_ref[0, ci, :, lanes(hg)] = jnp.exp(g_last)


def _gdn_local(q, k, v, gates, cumf, cumb, consts, tl):
    bsz, seq, width = q.shape
    masks, _, bd_b, exp = consts
    tile = lambda wd: pl.BlockSpec((1, tl, wd), lambda b, s: (b, s, 0))
    cd_spec = pl.BlockSpec((1, tl // CHUNK, 1, width), lambda b, s: (b, s, 0, 0))
    big = lambda dt: jax.ShapeDtypeStruct((bsz, seq, width), dt)
    cd_shape = jax.ShapeDtypeStruct((bsz, seq // CHUNK, 1, width), F32)
    per_dir_specs = [tile(width)] * 5 + [cd_spec]
    per_dir_shapes = [big(F32), big(BF16), big(BF16), big(BF16), big(BF16), cd_shape]
    return pl.pallas_call(
        _gdn_local_kernel,
        grid=(bsz, seq // tl),
        in_specs=[tile(width), tile(width), tile(width), tile(LANES), tile(LANES), tile(LANES),
                  _full(masks), _full(bd_b), _full(exp)],
        out_specs=per_dir_specs * 2,
        out_shape=per_dir_shapes * 2,
        compiler_params=_params(("parallel", "parallel")),
        name="gdn_local",
    )(q, k, v, gates, cumf, cumb, masks, bd_b, exp)


def _gdn_state_kernel(uf_ref, wf_ref, qef_ref, inf_ref, kef_ref, cdf_ref,
                      ub_ref, wb_ref, qeb_ref, inb_ref, keb_ref, cdb_ref, bdf_ref, bdb_ref, of_ref, ob_ref, s_ref):
    n = pl.program_id(1)

    @pl.when(n == 0)
    def _():
        s_ref[...] = jnp.zeros_like(s_ref)

    c, w = CHUNK, GROUP_W
    bd_b = bdb_ref[...]
    dirs = ((uf_ref, wf_ref, qef_ref, inf_ref, kef_ref, cdf_ref, of_ref),
            (ub_ref, wb_ref, qeb_ref, inb_ref, keb_ref, cdb_ref, ob_ref))
    nb = uf_ref.shape[0]
    chains = [(bi, d, hg) for bi in range(nb) for d in range(2) for hg in range(N_HEAD_GROUPS)]
    lanes = lambda hg: slice(hg * w, (hg + 1) * w)
    ws_qs = []
    for idx, (bi, d, hg) in enumerate(chains):
        w_ref, qe_ref = dirs[d][1], dirs[d][2]
        lhs = jnp.concatenate([w_ref[bi, :, lanes(hg)], qe_ref[bi, :, lanes(hg)]], axis=0)
        ws_qs.append(_dot(lhs, s_ref[idx].astype(BF16)))
    v_new = [(dirs[d][0][bi, :, lanes(hg)] - ws_qs[idx][:c]).astype(BF16) for idx, (bi, d, hg) in enumerate(chains)]
    for idx, (bi, d, hg) in enumerate(chains):
        in_ref, o_ref = dirs[d][3], dirs[d][6]
        o_ref[bi, :, lanes(hg)] = ws_qs[idx][c:] + _dot(in_ref[bi, :, lanes(hg)], _block_diag(v_new[idx], bd_b))
    for idx, (bi, d, hg) in enumerate(chains):
        ke_ref, cd_ref = dirs[d][4], dirs[d][5]
        ds = _dot(ke_ref[bi, :, lanes(hg)], v_new[idx], _TN)
        s_ref[idx] = s_ref[idx] * cd_ref[bi, 0, :, lanes(hg)] + ds * bdf_ref[...]


def _gdn_state(local_outs, consts, nb):
    uf = local_outs[0]
    bsz, seq, width = uf.shape
    nc = seq // CHUNK
    _, bd_f, bd_b, _ = consts
    fwd = pl.BlockSpec((nb, CHUNK, width), lambda b, n: (b, n, 0))
    bwd = pl.BlockSpec((nb, CHUNK, width), lambda b, n: (b, nc - 1 - n, 0))
    cd_fwd = pl.BlockSpec((nb, 1, 1, width), lambda b, n: (b, n, 0, 0))
    cd_bwd = pl.BlockSpec((nb, 1, 1, width), lambda b, n: (b, nc - 1 - n, 0, 0))
    return pl.pallas_call(
        _gdn_state_kernel,
        grid=(bsz // nb, nc),
        in_specs=[fwd] * 5 + [cd_fwd] + [bwd] * 5 + [cd_bwd] + [_full(bd_f), _full(bd_b)],
        out_specs=[fwd, bwd],
        out_shape=[jax.ShapeDtypeStruct((bsz, seq, width), F32)] * 2,
        scratch_shapes=[pltpu.VMEM((nb * 2 * N_HEAD_GROUPS, GROUP_W, GROUP_W), F32)],
        compiler_params=_params(("parallel", "arbitrary")),
        name="gdn_state",
    )(*local_outs, bd_f, bd_b)


def _mlstm_kernel(qf_ref, kf_ref, vf_ref, gf_ref, cf_ref, qb_ref, kb_ref, vb_ref, gb_ref, cb_ref,
                  masks_ref, bdf_ref, bdb_ref, exp_ref, of_ref, ob_ref, c_ref, n_ref, m_ref):
    n = pl.program_id(1)

    @pl.when(n == 0)
    def _():
        c_ref[...] = jnp.zeros_like(c_ref)
        n_ref[...] = jnp.zeros_like(n_ref)
        m_ref[...] = jnp.zeros_like(m_ref)

    c, w = CHUNK, GROUP_W
    bd_f = bdf_ref[...]
    bd_b = bdb_ref[...]
    ones_cw = jnp.ones((c, w), BF16)
    row = _iota((c, w), 0)
    dirs = ((qf_ref, kf_ref, vf_ref, gf_ref, cf_ref, of_ref), (qb_ref, kb_ref, vb_ref, gb_ref, cb_ref, ob_ref))
    nb = qf_ref.shape[0]
    chains = [(bi, d, hg) for bi in range(nb) for d in range(2) for hg in range(N_HEAD_GROUPS)]
    lanes = lambda hg: slice(hg * w, (hg + 1) * w)
    g_pieces = {(bi, d): _split3(dirs[d][3][bi]) for bi, d, _ in chains}
    cum_pieces = {(bi, d): _split3(dirs[d][4][bi]) for bi, d, _ in chains}
    bcum = [_dot_pieces(cum_pieces[bi, d], exp_ref[1, d, hg, 1]) for bi, d, hg in chains]
    r = [_dot_pieces(g_pieces[bi, d], exp_ref[1, d, hg, 0]) - bcum[i] for i, (bi, d, hg) in enumerate(chains)]
    r_t = [_transpose_rows(r[i], masks_ref[d, M_DIAG]) for i, (_, d, _) in enumerate(chains)]
    qk = []
    for bi, d, hg in chains:
        k_bd = jnp.concatenate([dirs[d][1][bi, :, lanes(hg)]] * HEADS_PER_GROUP, axis=0) * bd_b
        qk.append(_dot(dirs[d][0][bi, :, lanes(hg)], k_bd, _NT))

    pm, mx, inter_w = [], [], []
    for i, (bi, d, hg) in enumerate(chains):
        m = r[i]
        for p in range(LEVELS):
            sh = 1 << p
            if d == 0:
                m = jnp.maximum(m, jnp.where(row >= sh, pltpu.roll(m, sh, 0), NEG))
            else:
                m = jnp.maximum(m, jnp.where(row < c - sh, pltpu.roll(m, c - sh, 0), NEG))
        pm.append(m)
        m_prev = m_ref[i, 0:1, :]
        mx.append(jnp.maximum(m_prev, m))
        inter_w.append(jnp.exp(m_prev - mx[i]))
        qk[i] = qk[i] * jnp.exp(jnp.where(masks_ref[d, M_INCL] > 0.0, r_t[i] - mx[i], NEG))

    num = [_mm(qk[i], _block_diag(dirs[d][2][bi, :, lanes(hg)], bd_b)) for i, (bi, d, hg) in enumerate(chains)]
    den = [_mm_lhs2(qk[i], bd_b) for i in range(len(chains))]
    qc = [_mm(dirs[d][0][bi, :, lanes(hg)], c_ref[i]) for i, (bi, d, hg) in enumerate(chains)]
    qn = [_mm(dirs[d][0][bi, :, lanes(hg)], n_ref[i]) for i, (bi, d, hg) in enumerate(chains)]
    for i, (bi, d, hg) in enumerate(chains):
        full_num = num[i] + inter_w[i] * qc[i]
        full_den = den[i] + inter_w[i] * qn[i]
        dirs[d][5][bi, :, lanes(hg)] = full_num / jnp.maximum(jnp.abs(full_den), jnp.exp(-(bcum[i] + mx[i])))

    for i, (bi, d, hg) in enumerate(chains):
        last = c - 1 if d == 0 else 0
        m_prev = m_ref[i, 0:1, :]
        b_last = bcum[i][last:last + 1, :]
        m_next = b_last + jnp.maximum(m_prev, pm[i][last:last + 1, :])
        scale_prev = jnp.exp(b_last + m_prev - m_next)
        wgt = jnp.exp(b_last + r[i] - m_next)
        kw = (dirs[d][1][bi, :, lanes(hg)].astype(F32) * wgt).astype(BF16)
        c_ref[i] = c_ref[i] * scale_prev + _dot(kw, dirs[d][2][bi, :, lanes(hg)], _TN) * bd_f
        n_ref[i] = n_ref[i] * scale_prev + _dot(kw, ones_cw, _TN) * bd_f
        m_ref[i, 0:1, :] = m_next


def _mlstm_scan(q, k, v, gates, cumf, cumb, consts, nb):
    bsz, seq, width = q.shape
    nc = seq // CHUNK
    masks, bd_f, bd_b, exp = consts
    fwd = lambda wd: pl.BlockSpec((nb, CHUNK, wd), lambda b, n: (b, n, 0))
    bwd = lambda wd: pl.BlockSpec((nb, CHUNK, wd), lambda b, n: (b, nc - 1 - n, 0))
    n_state = nb * 2 * N_HEAD_GROUPS
    state = pltpu.VMEM((n_state, GROUP_W, GROUP_W), F32)
    return pl.pallas_call(
        _mlstm_kernel,
        grid=(bsz // nb, nc),
        in_specs=[fwd(width), fwd(width), fwd(width), fwd(LANES), fwd(LANES),
                  bwd(width), bwd(width), bwd(width), bwd(LANES), bwd(LANES),
                  _full(masks), _full(bd_f), _full(bd_b), _full(exp)],
        out_specs=[fwd(width), bwd(width)],
        out_shape=[jax.ShapeDtypeStruct((bsz, seq, width), F32)] * 2,
        scratch_shapes=[state, state, pltpu.VMEM((n_state, SUBLANES, GROUP_W), F32)],
        compiler_params=_params(("parallel", "arbitrary")),
        name="mlstm_scan",
    )(q, k, v, gates, cumf, q, k, v, gates, cumb, masks, bd_f, bd_b, exp)


def _outproj_kernel(h_ref, gaf_ref, gab_ref, mlf_ref, mlb_ref, az_ref, bo_ref, mg_ref, gt_ref, ng_ref, mlg_ref,
                    wa_ref, wb_ref, wo_ref, sh_ref, sc_ref, fg_ref, wr_ref, hout_ref, hn_ref, comb_ref):
    ones_bd = _head_ones(WIDTH)
    inv = 1.0 / HEAD_DIM
    oa = gaf_ref[0] + gab_ref[0]
    ms = _mm_lhs2(oa * oa, ones_bd) * inv
    oa = oa * lax.rsqrt(ms + EPS) * ng_ref[...] * _silu(az_ref[0])
    hb = mlf_ref[0] + mlb_ref[0]
    mu = _mm_lhs3(hb, ones_bd) * inv
    tc = hb - mu
    var = _mm_lhs2(tc * tc, ones_bd) * inv
    hb = tc * lax.rsqrt(var + EPS) * mlg_ref[...] * _sigmoid(bo_ref[0])
    gate = _sigmoid(mg_ref[0])
    y = gate[:, :D_MODEL] * _mm(oa, wa_ref[...]) + gate[:, D_MODEL:] * _mm(hb, wb_ref[...])
    h_new = h_ref[0] + gt_ref[0] * _mm(y, wo_ref[...])
    hout_ref[0] = h_new

    ms2 = jnp.mean(h_new * h_new, axis=-1, keepdims=True)
    hn = h_new * lax.rsqrt(ms2 + EPS) * (fg_ref[...] * (1.0 + sc_ref[0])) + sh_ref[0]
    hn_ref[0] = hn.astype(BF16)

    logits = _mm3(hn, wr_ref[...])
    lane = _iota(logits.shape, 1)
    lane_f = lane.astype(F32)
    big = 1e9
    gl = jnp.where((lane >= N_EXPERTS) & (lane < N_EXPERTS + N_GROUPS), logits, NEG)
    gmax = jnp.max(gl, axis=-1, keepdims=True)
    gidx = jnp.min(jnp.where(gl == gmax, lane_f - N_EXPERTS, big), axis=-1, keepdims=True)
    p_group = 1.0 / jnp.sum(jnp.exp(gl - gmax), axis=-1, keepdims=True)
    el = jnp.where((lane < N_EXPERTS) & ((lane >> 3).astype(F32) == gidx), logits, NEG)
    v1 = jnp.max(el, axis=-1, keepdims=True)
    i1 = jnp.min(jnp.where(el == v1, lane_f, big), axis=-1, keepdims=True)
    el2 = jnp.where(lane_f == i1, NEG, el)
    v2 = jnp.max(el2, axis=-1, keepdims=True)
    i2 = jnp.min(jnp.where(el2 == v2, lane_f, big), axis=-1, keepdims=True)
    e21 = jnp.exp(v2 - v1)
    w1 = p_group / (1.0 + e21)
    comb_ref[0] = jnp.where(lane_f == i1, w1, 0.0) + jnp.where(lane_f == i2, w1 * e21, 0.0)


def _output_projection(h, gaf, gab, mlf, mlb, az, bo, mg, gt, ng, mlg, wa, wb, wo, sh, sc, fg, wr, tm):
    bsz, seq, d = h.shape
    tile = lambda w: pl.BlockSpec((1, tm, w), lambda b, s: (b, s, 0))
    per_b = pl.BlockSpec((1, 1, d), lambda b, s: (b, 0, 0))
    return pl.pallas_call(
        _outproj_kernel,
        grid=(bsz, seq // tm),
        in_specs=[tile(d), tile(WIDTH), tile(WIDTH), tile(WIDTH), tile(WIDTH), tile(WIDTH), tile(WIDTH),
                  tile(2 * D_MODEL), per_b, _full(ng), _full(mlg), _full(wa), _full(wb), _full(wo), per_b, per_b,
                  _full(fg), _full(wr)],
        out_specs=[tile(d), tile(d), tile(LANES)],
        out_shape=[jax.ShapeDtypeStruct((bsz, seq, d), F32), jax.ShapeDtypeStruct((bsz, seq, d), BF16),
                   jax.ShapeDtypeStruct((bsz, seq, LANES), F32)],
        compiler_params=_params(("parallel", "parallel")),
        name="output_projection",
    )(h, gaf, gab, mlf, mlb, az, bo, mg, gt, ng, mlg, wa, wb, wo, sh, sc, fg, wr)


def _moe_kernel(h_ref, hn_ref, comb_ref, gt_ref, wg_ref, wu_ref, wd_ref, fin_ref, o_ref, acc_ref, *, final_norm):
    e = pl.program_id(2)

    @pl.when(e == 0)
    def _():
        acc_ref[...] = jnp.zeros_like(acc_ref)

    x = hn_ref[0]
    comb = comb_ref[0]
    w_e = jnp.sum(jnp.where(_iota(comb.shape, 1) == e, comb, 0.0), axis=-1, keepdims=True)
    act = _silu(_dot(x, wg_ref[0])) * _dot(x, wu_ref[0]) * w_e
    acc_ref[...] += _mm(act, wd_ref[0])

    @pl.when(e == pl.num_programs(2) - 1)
    def _():
        out = h_ref[0] + gt_ref[0] * acc_ref[...]
        if final_norm:
            ms = jnp.mean(out * out, axis=-1, keepdims=True)
            out = out * lax.rsqrt(ms + EPS) * fin_ref[...]
        o_ref[0] = out


def _moe(h, hn, comb, gt, wg, wu, wd, fin, tm, final_norm):
    bsz, seq, d = h.shape
    tile = lambda w: pl.BlockSpec((1, tm, w), lambda b, s, e: (b, s, 0))
    return pl.pallas_call(
        functools.partial(_moe_kernel, final_norm=final_norm),
        grid=(bsz, seq // tm, N_EXPERTS),
        in_specs=[tile(d), tile(d), tile(LANES), pl.BlockSpec((1, 1, d), lambda b, s, e: (b, 0, 0)),
                  pl.BlockSpec((1, d, D_EXPERT), lambda b, s, e: (e, 0, 0)),
                  pl.BlockSpec((1, d, D_EXPERT), lambda b, s, e: (e, 0, 0)),
                  pl.BlockSpec((1, D_EXPERT, d), lambda b, s, e: (e, 0, 0)),
                  pl.BlockSpec((1, d), lambda b, s, e: (0, 0))],
        out_specs=tile(d),
        out_shape=jax.ShapeDtypeStruct((bsz, seq, d), F32),
        scratch_shapes=[pltpu.VMEM((tm, d), F32)],
        compiler_params=_params(("parallel", "parallel", "arbitrary")),
        name="expert_ffn",
    )(h, hn, comb, gt, wg, wu, wd, fin)


def _pick_tile(seq, want):
    tm = min(seq, want)
    assert seq % tm == 0 and tm % CHUNK == 0
    return tm


def kernel(x, c, ada_w, ada_b, norm_mix_g, norm_ffn_g, w_in, gdn_conv_w, gdn_a_log, gdn_dt_bias, gdn_norm_g, mlstm_conv_w, mlstm_i_bias, mlstm_f_bias, mlstm_norm_g, w_branch_a, w_branch_b, w_out, router_group, router_expert, w_gate, w_up, w_down, final_norm_g):
    bsz, seq, d = x.shape
    depth = ada_w.shape[0]
    assert d == D_MODEL and seq % CHUNK == 0
    tm_in = _pick_tile(seq, 256)
    tm_out = _pick_tile(seq, 256)
    tm_moe = _pick_tile(seq, 1024)
    tl_gdn = _pick_tile(seq, 2 * CHUNK)
    nb_scan = 2 if bsz % 2 == 0 else 1
    row = lambda a: a.reshape(1, -1).astype(F32)
    consts = _scan_constants()

    mod = _modulation(c.astype(F32), ada_w.astype(F32), ada_b.astype(F32))
    h = x.astype(F32)
    for l in range(depth):
        sh1, sc1, gt1, sh2, sc2, gt2 = [mod[l, :, i * d:(i + 1) * d].reshape(bsz, 1, d) for i in range(N_MOD)]

        w = w_in[l].astype(F32)
        o_z = 3 * WIDTH
        o_ag = o_z + WIDTH
        o_bqk = o_ag + 4 * N_HEADS
        o_bv = o_bqk + 2 * WIDTH
        o_bo = o_bv + WIDTH
        o_bg = o_bo + WIDTH
        o_mg = o_bg + 4 * N_HEADS
        wconv = jnp.concatenate([w[:, :o_z], w[:, o_bqk:o_bv]], axis=1).astype(BF16)
        wrest = jnp.concatenate([w[:, o_z:o_ag], w[:, o_bv:o_bo], w[:, o_bo:o_bg], w[:, o_mg:]], axis=1).astype(BF16)
        wg = jnp.concatenate([w[:, o_ag:o_bqk], w[:, o_bg:o_mg], jnp.zeros((d, LANES - 8 * N_HEADS), F32)], axis=1)
        cw = jnp.concatenate([gdn_conv_w[l], mlstm_conv_w[l]], axis=1).astype(F32)
        cw = jnp.concatenate([cw, jnp.zeros((SUBLANES - CONV_K, CONV_W), F32)], axis=0)
        zero8 = jnp.zeros((N_HEADS,), F32)
        bias = jnp.concatenate([gdn_dt_bias[l, 0], zero8, gdn_dt_bias[l, 1], zero8,
                                mlstm_i_bias[l, 0], mlstm_f_bias[l, 0], mlstm_i_bias[l, 1], mlstm_f_bias[l, 1],
                                jnp.zeros((LANES - 8 * N_HEADS,), F32)]).astype(F32)
        alog = jnp.concatenate([gdn_a_log[l, 0], zero8, gdn_a_log[l, 1], zero8,
                                jnp.zeros((LANES - 4 * N_HEADS,), F32)]).astype(F32)
        gp = jnp.concatenate([bias[None], alog[None], jnp.zeros((SUBLANES - 2, LANES), F32)], axis=0)

        aq, ak, av, bq, bk, az, bv, bo, mg, gates, cumf, cumb = _input_projection(
            h, sh1, sc1, row(norm_mix_g[l]), wconv, wrest, wg, cw, gp, tm_in)
        gaf, gab = _gdn_state(_gdn_local(aq, ak, av, gates, cumf, cumb, consts, tl_gdn), consts, nb_scan)
        mlf, mlb = _mlstm_scan(bq, bk, bv, gates, cumf, cumb, consts, nb_scan)

        wr = jnp.concatenate([router_expert[l].astype(F32), router_group[l].astype(F32),
                              jnp.zeros((d, LANES - N_EXPERTS - N_GROUPS), F32)], axis=1)
        h, hn2, comb = _output_projection(
            h, gaf, gab, mlf, mlb, az, bo, mg, gt1, row(jnp.tile(gdn_norm_g[l], N_HEADS)), row(mlstm_norm_g[l]),
            w_branch_a[l].astype(BF16), w_branch_b[l].astype(BF16), w_out[l].astype(BF16),
            sh2, sc2, row(norm_ffn_g[l]), wr, tm_out)
        h = _moe(h, hn2, comb, gt2, w_gate[l].astype(BF16), w_up[l].astype(BF16), w_down[l].astype(BF16),
                 row(final_norm_g), tm_moe, final_norm=(l == depth - 1))
    return h.astype(x.dtype)
```

```python
import functools

import numpy as np
import jax
import jax.numpy as jnp
from jax import lax
from jax.experimental import pallas as pl
from jax.experimental.pallas import tpu as pltpu

D_MODEL = 1024
HEAD_DIM = 64
N_HEADS = 8
WIDTH = N_HEADS * HEAD_DIM
CONV_K = 5
CONV_W = 3 * WIDTH + 2 * WIDTH
REST_W = 3 * WIDTH + 2 * D_MODEL
N_GROUPS = 4
EXPERTS_PER_GROUP = 8
N_EXPERTS = N_GROUPS * EXPERTS_PER_GROUP
D_EXPERT = D_MODEL // 4
N_MOD = 6
EPS = 1e-6
NEG = -1e30

CHUNK = 64
HEADS_PER_GROUP = 2
N_HEAD_GROUPS = N_HEADS // HEADS_PER_GROUP
GROUP_W = HEADS_PER_GROUP * HEAD_DIM
LEVELS = 6
LANES = 128
SUBLANES = 8
BF16_ROWS = 16
VMEM_LIMIT = 56 * 1024 * 1024

F32 = jnp.float32
BF16 = jnp.bfloat16

_NN = (((1,), (0,)), ((), ()))
_NT = (((1,), (1,)), ((), ()))
_TN = (((0,), (0,)), ((), ()))

R_E1, R_E2, R_W1, R_W2 = 0, 1, 4, 5
MOE_CAP = 64

M_INCL, M_STRICT, M_LEVEL0, M_DIAG = 0, 1, 2, 2 + LEVELS
N_MASKS = M_DIAG + 1


def _dot(a, b, dims=_NN):
    return lax.dot_general(a, b, dims, preferred_element_type=F32)


def _mm(a, b, dims=_NN):
    return _dot(a.astype(BF16), b.astype(BF16), dims)


def _split2(x):
    hi = x.astype(BF16)
    lo = (x - hi.astype(F32)).astype(BF16)
    return hi, lo


def _split3(x):
    hi = x.astype(BF16)
    r = x - hi.astype(F32)
    mid = r.astype(BF16)
    lo = (r - mid.astype(F32)).astype(BF16)
    return hi, mid, lo


def _dot_pieces(pieces, e):
    out = _dot(pieces[0], e)
    for p in pieces[1:]:
        out = out + _dot(p, e)
    return out


def _mm_lhs3(x, e):
    return _dot_pieces(_split3(x), e)


def _mm_rhs3(e, x):
    hi, mid, lo = _split3(x)
    return _dot(e, hi) + _dot(e, mid) + _dot(e, lo)


def _mm_lhs2(x, e):
    return _dot_pieces(_split2(x), e)


def _mm3(a, b):
    a_hi, a_lo = _split2(a)
    b_hi, b_lo = _split2(b)
    return _dot(a_hi, b_hi) + _dot(a_hi, b_lo) + _dot(a_lo, b_hi)


def _iota(shape, dim):
    return lax.broadcasted_iota(jnp.int32, shape, dim)


def _sigmoid(x):
    return 1.0 / (1.0 + jnp.exp(-x))


def _silu(x):
    return x * _sigmoid(x)


def _softplus(x):
    return jnp.maximum(x, 0.0) + jnp.log1p(jnp.exp(-jnp.abs(x)))


def _head_ones(width):
    r = _iota((width, width), 0) >> 6
    c = _iota((width, width), 1) >> 6
    return jnp.where(r == c, 1.0, 0.0).astype(BF16)


def _params(sem):
    return pltpu.CompilerParams(dimension_semantics=sem, vmem_limit_bytes=VMEM_LIMIT)


def _full(a):
    return pl.BlockSpec(a.shape, lambda *_: (0,) * a.ndim)


def _mod_kernel(c_ref, w_ref, b_ref, o_ref):
    o_ref[0] = _mm3(_silu(c_ref[...]), w_ref[0]) + b_ref[0]


def _modulation(c, ada_w, ada_b):
    depth, d, n = ada_w.shape
    bsz = c.shape[0]
    tn = n // 4
    return pl.pallas_call(
        _mod_kernel,
        grid=(depth, n // tn),
        in_specs=[
            pl.BlockSpec((bsz, d), lambda l, j: (0, 0)),
            pl.BlockSpec((1, d, tn), lambda l, j: (l, 0, j)),
            pl.BlockSpec((1, 1, tn), lambda l, j: (l, 0, j)),
        ],
        out_specs=pl.BlockSpec((1, bsz, tn), lambda l, j: (l, 0, j)),
        out_shape=jax.ShapeDtypeStruct((depth, bsz, n), F32),
        compiler_params=_params(("parallel", "parallel")),
        name="adaln_mod",
    )(c, ada_w, ada_b.reshape(depth, 1, n))


HALO = BF16_ROWS


def _inproj_kernel(h_ref, hprev_ref, hnext_ref, sh_ref, sc_ref, g_ref, wconv_ref, wrest_ref, wg_ref, cw_ref, gp_ref,
                   aq_ref, ak_ref, av_ref, bq_ref, bk_ref, az_ref, bv_ref, bo_ref, mg_ref, gates_ref, cumf_ref,
                   cumb_ref, p_ref):
    s = pl.program_id(1)
    ns = pl.num_programs(1)
    tm = h_ref.shape[1]
    gain = g_ref[...] * (1.0 + sc_ref[0])
    shift = sh_ref[0]

    def norm(x):
        ms = jnp.mean(x * x, axis=-1, keepdims=True)
        return x * lax.rsqrt(ms + EPS) * gain + shift

    hn = norm(h_ref[0])
    hp = norm(hprev_ref[0]) * jnp.where(s > 0, 1.0, 0.0)
    hx = norm(hnext_ref[0]) * jnp.where(s < ns - 1, 1.0, 0.0)
    hn_hi = hn.astype(BF16)
    hall = jnp.concatenate([hp.astype(BF16), hn_hi, hx.astype(BF16)], axis=0)
    p_ref[...] = _dot(hall, wconv_ref[...])

    conv = jnp.zeros((tm, CONV_W), F32)
    for j in range(CONV_K):
        conv = conv + p_ref[pl.ds(HALO - CONV_K // 2 + j, tm), :] * cw_ref[j:j + 1, :]
    cv = _silu(conv)

    ones_bd = _head_ones(WIDTH)

    def l2n(t):
        ss = _mm_lhs2(t * t, ones_bd)
        return t * lax.rsqrt(ss + EPS)

    aq_ref[0] = (l2n(cv[:, 0:WIDTH]) * (HEAD_DIM ** -0.5)).astype(BF16)
    ak_ref[0] = l2n(cv[:, WIDTH:2 * WIDTH]).astype(BF16)
    av_ref[0] = cv[:, 2 * WIDTH:3 * WIDTH].astype(BF16)
    bq_ref[0] = cv[:, 3 * WIDTH:4 * WIDTH].astype(BF16)
    bk_ref[0] = (cv[:, 4 * WIDTH:5 * WIDTH] * (HEAD_DIM ** -0.5)).astype(BF16)

    rest = _dot(hn_hi, wrest_ref[...])
    az_ref[0] = rest[:, 0:WIDTH].astype(BF16)
    bv_ref[0] = rest[:, WIDTH:2 * WIDTH].astype(BF16)
    bo_ref[0] = rest[:, 2 * WIDTH:3 * WIDTH].astype(BF16)
    mg_ref[0] = rest[:, 3 * WIDTH:].astype(BF16)

    hn_lo = (hn - hn_hi.astype(F32)).astype(BF16)
    wg_hi, wg_lo = _split2(wg_ref[...])
    pre = _dot(hn_hi, wg_hi) + _dot(hn_hi, wg_lo) + _dot(hn_lo, wg_hi)
    pre = pre + gp_ref[0:1, :]
    lane = _iota(pre.shape, 1)
    second = ((lane >> 3) & 1) == 1
    is_gdn = lane < 4 * N_HEADS
    is_ml = (lane >= 4 * N_HEADS) & (lane < 8 * N_HEADS)
    sp = _softplus(pre)
    gdn_val = jnp.where(second, _sigmoid(pre), -jnp.exp(gp_ref[1:2, :]) * sp)
    ml_val = jnp.where(second, -_softplus(-pre), pre)
    gates = jnp.where(is_gdn, gdn_val, jnp.where(is_ml, ml_val, 0.0))
    gates_ref[0] = gates

    ri = _iota((tm, tm), 0)
    ci = _iota((tm, tm), 1)
    same_chunk = (ri >> 6) == (ci >> 6)
    pieces = _split3(gates)
    for ref, tri in ((cumf_ref, same_chunk & (ri >= ci)), (cumb_ref, same_chunk & (ri <= ci))):
        tri_b = jnp.where(tri, 1.0, 0.0).astype(BF16)
        ref[0] = _dot(tri_b, pieces[0]) + _dot(tri_b, pieces[1]) + _dot(tri_b, pieces[2])


def _input_projection(h, sh, sc, gain, wconv, wrest, wg, cw, gp, tm):
    bsz, seq, d = h.shape
    nt = seq // tm
    per = tm // HALO
    nhalo = seq // HALO
    tile = lambda w: pl.BlockSpec((1, tm, w), lambda b, s: (b, s, 0))
    outs = [(WIDTH, BF16)] * 8 + [(2 * D_MODEL, BF16)] + [(LANES, F32)] * 3
    return pl.pallas_call(
        _inproj_kernel,
        grid=(bsz, nt),
        in_specs=[
            tile(d),
            pl.BlockSpec((1, HALO, d), lambda b, s: (b, jnp.maximum(s * per - 1, 0), 0)),
            pl.BlockSpec((1, HALO, d), lambda b, s: (b, jnp.minimum((s + 1) * per, nhalo - 1), 0)),
            pl.BlockSpec((1, 1, d), lambda b, s: (b, 0, 0)),
            pl.BlockSpec((1, 1, d), lambda b, s: (b, 0, 0)),
            _full(gain), _full(wconv), _full(wrest), _full(wg), _full(cw), _full(gp),
        ],
        out_specs=[tile(w) for w, _ in outs],
        out_shape=[jax.ShapeDtypeStruct((bsz, seq, w), dt) for w, dt in outs],
        scratch_shapes=[pltpu.VMEM((tm + 2 * HALO, CONV_W), F32)],
        compiler_params=_params(("parallel", "parallel")),
        name="input_projection",
    )(h, h, h, sh, sc, gain, wconv, wrest, wg, cw, gp)


def _gate_column(branch, direction, kind):
    return branch * 4 * N_HEADS + direction * 2 * N_HEADS + kind * N_HEADS


def _expand(pieces, e):
    rows = pieces[0].shape[0]
    out = _dot(jnp.concatenate(pieces, axis=0), e)
    return functools.reduce(lambda a, b: a + b, [out[i * rows:(i + 1) * rows] for i in range(len(pieces))])


def _head_rows(x_t, column, hg, block):
    first = column + hg * HEADS_PER_GROUP
    pieces = [x_t[first + h:first + h + 1, block * CHUNK:(block + 1) * CHUNK] for h in range(HEADS_PER_GROUP)]
    return jnp.concatenate(pieces, axis=1)


def _scan_constants():
    c, w = CHUNK, GROUP_W
    row = np.arange(c)[:, None]
    col = (np.arange(w) % c)[None, :]
    masks = np.zeros((2, N_MASKS, c, w), np.float32)
    for d in range(2):
        masks[d, M_INCL] = (row >= col) if d == 0 else (row <= col)
        masks[d, M_STRICT] = (row > col) if d == 0 else (row < col)
        for p in range(LEVELS):
            same = (row >> (p + 1)) == (col >> (p + 1))
            r_bit = (row >> p) & 1
            c_bit = (col >> p) & 1
            masks[d, M_LEVEL0 + p] = same & ((r_bit == 1) & (c_bit == 0) if d == 0 else (r_bit == 0) & (c_bit == 1))
        masks[d, M_DIAG] = row == col
    bd = (np.arange(w)[:, None] // HEAD_DIM) == (np.arange(w)[None, :] // HEAD_DIM)
    exp = np.zeros((2, 2, 2, LANES, WIDTH), np.float32)
    lane_head = np.arange(WIDTH) // HEAD_DIM
    for br in range(2):
        for d in range(2):
            for kind in range(2):
                exp[br, d, kind, _gate_column(br, d, kind) + lane_head, np.arange(WIDTH)] = 1.0
    return (jnp.asarray(masks), jnp.asarray(bd.astype(np.float32)),
            jnp.asarray(bd.astype(np.float32)).astype(BF16), jnp.asarray(exp).astype(BF16))


def _block_diag(x_cat, bd_bf16):
    xb = x_cat.astype(BF16)
    return jnp.concatenate([xb] * HEADS_PER_GROUP, axis=0) * bd_bf16


def _gdn_local_kernel(q_ref, k_ref, v_ref, g_ref, cumf_ref, cumb_ref, masks_ref, bdb_ref, exp_ref,
                      uf_ref, wf_ref, qef_ref, inf_ref, kef_ref, cdf_ref,
                      ub_ref, wb_ref, qeb_ref, inb_ref, keb_ref, cdb_ref):
    c, w = CHUNK, GROUP_W
    bd_b = bdb_ref[...]
    outs = ((uf_ref, wf_ref, qef_ref, inf_ref, kef_ref, cdf_ref), (ub_ref, wb_ref, qeb_ref, inb_ref, keb_ref, cdb_ref))
    chains = [(ci, d, hg) for ci in range(q_ref.shape[1] // c) for d in range(2) for hg in range(N_HEAD_GROUPS)]
    rows = lambda ci: slice(ci * c, (ci + 1) * c)
    lanes = lambda hg: slice(hg * w, (hg + 1) * w)

    cums = (cumf_ref[0], cumb_ref[0])
    g_pieces = _split2(g_ref[0])
    gc_all = [_expand(_split3(cums[d]), exp_ref[0, d, 0]) for d in range(2)]
    beta_all = [_expand(g_pieces, exp_ref[0, d, 1]) for d in range(2)]
    cum_t = [cums[d].T for d in range(2)]
    gc = [gc_all[d][rows(ci), lanes(hg)] for ci, d, hg in chains]
    beta = [beta_all[d][rows(ci), lanes(hg)] for ci, d, hg in chains]
    gc_t = [_head_rows(cum_t[d], _gate_column(0, d, 0), hg, ci) for ci, d, hg in chains]

    qk_kk, kbeta = [], []
    for i, (ci, d, hg) in enumerate(chains):
        q = q_ref[0, rows(ci), lanes(hg)]
        k = k_ref[0, rows(ci), lanes(hg)]
        kbeta.append(k.astype(F32) * beta[i])
        k_bd = jnp.concatenate([k] * HEADS_PER_GROUP, axis=0) * bd_b
        qk_kk.append(_dot(jnp.concatenate([q, kbeta[i].astype(BF16)], axis=0), k_bd, _NT))

    a, t = [], []
    for i, (ci, d, hg) in enumerate(chains):
        decay = jnp.exp(jnp.minimum(gc[i] - gc_t[i], 0.0)) * masks_ref[d, M_INCL]
        outs[d][3][0, rows(ci), lanes(hg)] = (qk_kk[i][:c] * decay).astype(BF16)
        a.append(qk_kk[i][c:] * decay * masks_ref[d, M_STRICT])
        t.append(masks_ref[d, M_DIAG] - a[i] * masks_ref[d, M_LEVEL0])

    for p in range(1, LEVELS):
        y = [_mm(t[i], _block_diag(a[i] * masks_ref[d, M_LEVEL0 + p], bd_b)) for i, (_, d, _) in enumerate(chains)]
        t = [t[i] - _mm(y[i], _block_diag(t[i], bd_b)) for i in range(len(chains))]

    for i, (ci, d, hg) in enumerate(chains):
        u_ref, w_ref, qe_ref, _, ke_ref, cd_ref = outs[d]
        last = c - 1 if d == 0 else 0
        g_last = gc[i][last:last + 1, :]
        e_gc = jnp.exp(gc[i])
        v = v_ref[0, rows(ci), lanes(hg)].astype(F32)
        u_ref[0, rows(ci), lanes(hg)] = _mm(t[i], _block_diag(v * beta[i], bd_b))
        w_ref[0, rows(ci), lanes(hg)] = _mm(t[i], _block_diag(kbeta[i] * e_gc, bd_b)).astype(BF16)
        qe_ref[0, rows(ci), lanes(hg)] = (q_ref[0, rows(ci), lanes(hg)].astype(F32) * e_gc).astype(BF16)
        k = k_ref[0, rows(ci), lanes(hg)].astype(F32)
        ke_ref[0, rows(ci), lanes(hg)] = (k * jnp.exp(g_last - gc[i])).astype(BF16)
        cd_ref[0, ci, :, lanes(hg)] = jnp.exp(g_last)


def _gdn_local(q, k, v, gates, cumf, cumb, consts, tl):
    bsz, seq, width = q.shape
    masks, _, bd_b, exp = consts
    tile = lambda wd: pl.BlockSpec((1, tl, wd), lambda b, s: (b, s, 0))
    cd_spec = pl.BlockSpec((1, tl // CHUNK, 1, width), lambda b, s: (b, s, 0, 0))
    big = lambda dt: jax.ShapeDtypeStruct((bsz, seq, width), dt)
    cd_shape = jax.ShapeDtypeStruct((bsz, seq // CHUNK, 1, width), F32)
    per_dir_specs = [tile(width)] * 5 + [cd_spec]
    per_dir_shapes = [big(F32), big(BF16), big(BF16), big(BF16), big(BF16), cd_shape]
    return pl.pallas_call(
        _gdn_local_kernel,
        grid=(bsz, seq // tl),
        in_specs=[tile(width), tile(width), tile(width), tile(LANES), tile(LANES), tile(LANES),
                  _full(masks), _full(bd_b), _full(exp)],
        out_specs=per_dir_specs * 2,
        out_shape=per_dir_shapes * 2,
        compiler_params=_params(("parallel", "parallel")),
        name="gdn_local",
    )(q, k, v, gates, cumf, cumb, masks, bd_b, exp)


def _gdn_state_kernel(uf_ref, wf_ref, qef_ref, inf_ref, kef_ref, cdf_ref,
                      ub_ref, wb_ref, qeb_ref, inb_ref, keb_ref, cdb_ref, bdf_ref, bdb_ref, of_ref, ob_ref, s_ref):
    n = pl.program_id(1)

    @pl.when(n == 0)
    def _():
        s_ref[...] = jnp.zeros_like(s_ref)

    c, w = CHUNK, GROUP_W
    bd_b = bdb_ref[...]
    dirs = ((uf_ref, wf_ref, qef_ref, inf_ref, kef_ref, cdf_ref, of_ref),
            (ub_ref, wb_ref, qeb_ref, inb_ref, keb_ref, cdb_ref, ob_ref))
    nb = uf_ref.shape[0]
    chains = [(bi, d, hg) for bi in range(nb) for d in range(2) for hg in range(N_HEAD_GROUPS)]
    lanes = lambda hg: slice(hg * w, (hg + 1) * w)
    ws_qs = []
    for idx, (bi, d, hg) in enumerate(chains):
        w_ref, qe_ref = dirs[d][1], dirs[d][2]
        lhs = jnp.concatenate([w_ref[bi, :, lanes(hg)], qe_ref[bi, :, lanes(hg)]], axis=0)
        ws_qs.append(_dot(lhs, s_ref[idx].astype(BF16)))
    v_new = [(dirs[d][0][bi, :, lanes(hg)] - ws_qs[idx][:c]).astype(BF16) for idx, (bi, d, hg) in enumerate(chains)]
    for idx, (bi, d, hg) in enumerate(chains):
        in_ref, o_ref = dirs[d][3], dirs[d][6]
        o = ws_qs[idx][c:] + _dot(in_ref[bi, :, lanes(hg)], _block_diag(v_new[idx], bd_b))
        o_ref[bi, :, lanes(hg)] = o.astype(BF16)
    for idx, (bi, d, hg) in enumerate(chains):
        ke_ref, cd_ref = dirs[d][4], dirs[d][5]
        ds = _dot(ke_ref[bi, :, lanes(hg)], v_new[idx], _TN)
        s_ref[idx] = s_ref[idx] * cd_ref[bi, 0, :, lanes(hg)] + ds * bdf_ref[...]


def _gdn_state(local_outs, consts, nb):
    uf = local_outs[0]
    bsz, seq, width = uf.shape
    nc = seq // CHUNK
    _, bd_f, bd_b, _ = consts
    fwd = pl.BlockSpec((nb, CHUNK, width), lambda b, n: (b, n, 0))
    bwd = pl.BlockSpec((nb, CHUNK, width), lambda b, n: (b, nc - 1 - n, 0))
    cd_fwd = pl.BlockSpec((nb, 1, 1, width), lambda b, n: (b, n, 0, 0))
    cd_bwd = pl.BlockSpec((nb, 1, 1, width), lambda b, n: (b, nc - 1 - n, 0, 0))
    return pl.pallas_call(
        _gdn_state_kernel,
        grid=(bsz // nb, nc),
        in_specs=[fwd] * 5 + [cd_fwd] + [bwd] * 5 + [cd_bwd] + [_full(bd_f), _full(bd_b)],
        out_specs=[fwd, bwd],
        out_shape=[jax.ShapeDtypeStruct((bsz, seq, width), BF16)] * 2,
        scratch_shapes=[pltpu.VMEM((nb * 2 * N_HEAD_GROUPS, GROUP_W, GROUP_W), F32)],
        compiler_params=_params(("parallel", "arbitrary")),
        name="gdn_state",
    )(*local_outs, bd_f, bd_b)


def _mlstm_kernel(qf_ref, kf_ref, vf_ref, gf_ref, cf_ref, qb_ref, kb_ref, vb_ref, gb_ref, cb_ref,
                  masks_ref, bdf_ref, bdb_ref, exp_ref, of_ref, ob_ref, c_ref, n_ref, m_ref):
    n = pl.program_id(1)

    @pl.when(n == 0)
    def _():
        c_ref[...] = jnp.zeros_like(c_ref)
        n_ref[...] = jnp.zeros_like(n_ref)
        m_ref[...] = jnp.zeros_like(m_ref)

    c, w = CHUNK, GROUP_W
    bd_f = bdf_ref[...]
    bd_b = bdb_ref[...]
    ones_cw = jnp.ones((c, w), BF16)
    row = _iota((c, w), 0)
    dirs = ((qf_ref, kf_ref, vf_ref, gf_ref, cf_ref, of_ref), (qb_ref, kb_ref, vb_ref, gb_ref, cb_ref, ob_ref))
    nb = qf_ref.shape[0]
    chains = [(bi, d, hg) for bi in range(nb) for d in range(2) for hg in range(N_HEAD_GROUPS)]
    lanes = lambda hg: slice(hg * w, (hg + 1) * w)
    gates = [jnp.concatenate([dirs[d][3][bi] for bi in range(nb)], axis=0) for d in range(2)]
    cums = [jnp.concatenate([dirs[d][4][bi] for bi in range(nb)], axis=0) for d in range(2)]
    i_all = [_expand(_split3(gates[d]), exp_ref[1, d, 0]) for d in range(2)]
    bcum_all = [_expand(_split3(cums[d]), exp_ref[1, d, 1]) for d in range(2)]
    gates_t = [gates[d].T for d in range(2)]
    cums_t = [cums[d].T for d in range(2)]
    rows = lambda bi: slice(bi * c, (bi + 1) * c)
    bcum = [bcum_all[d][rows(bi), lanes(hg)] for bi, d, hg in chains]
    r = [i_all[d][rows(bi), lanes(hg)] - bcum[i] for i, (bi, d, hg) in enumerate(chains)]
    r_t = [_head_rows(gates_t[d], _gate_column(1, d, 0), hg, bi) - _head_rows(cums_t[d], _gate_column(1, d, 1), hg, bi)
           for bi, d, hg in chains]
    qk = []
    for bi, d, hg in chains:
        k_bd = jnp.concatenate([dirs[d][1][bi, :, lanes(hg)]] * HEADS_PER_GROUP, axis=0) * bd_b
        qk.append(_dot(dirs[d][0][bi, :, lanes(hg)], k_bd, _NT))

    pm, mx, inter_w = [], [], []
    for i, (bi, d, hg) in enumerate(chains):
        m = r[i]
        for p in range(LEVELS):
            sh = 1 << p
            if d == 0:
                m = jnp.maximum(m, jnp.where(row >= sh, pltpu.roll(m, sh, 0), NEG))
            else:
                m = jnp.maximum(m, jnp.where(row < c - sh, pltpu.roll(m, c - sh, 0), NEG))
        pm.append(m)
        m_prev = m_ref[i, 0:1, :]
        mx.append(jnp.maximum(m_prev, m))
        inter_w.append(jnp.exp(m_prev - mx[i]))
        qk[i] = qk[i] * jnp.exp(jnp.where(masks_ref[d, M_INCL] > 0.0, r_t[i] - mx[i], NEG))

    num = [_mm(qk[i], _block_diag(dirs[d][2][bi, :, lanes(hg)], bd_b)) for i, (bi, d, hg) in enumerate(chains)]
    den = [_mm_lhs2(qk[i], bd_b) for i in range(len(chains))]
    qc = [_mm(dirs[d][0][bi, :, lanes(hg)], c_ref[i]) for i, (bi, d, hg) in enumerate(chains)]
    qn = [_mm(dirs[d][0][bi, :, lanes(hg)], n_ref[i]) for i, (bi, d, hg) in enumerate(chains)]
    for i, (bi, d, hg) in enumerate(chains):
        full_num = num[i] + inter_w[i] * qc[i]
        full_den = den[i] + inter_w[i] * qn[i]
        hidden = full_num / jnp.maximum(jnp.abs(full_den), jnp.exp(-(bcum[i] + mx[i])))
        dirs[d][5][bi, :, lanes(hg)] = hidden.astype(BF16)

    for i, (bi, d, hg) in enumerate(chains):
        last = c - 1 if d == 0 else 0
        m_prev = m_ref[i, 0:1, :]
        b_last = bcum[i][last:last + 1, :]
        m_next = b_last + jnp.maximum(m_prev, pm[i][last:last + 1, :])
        scale_prev = jnp.exp(b_last + m_prev - m_next)
        wgt = jnp.exp(b_last + r[i] - m_next)
        kw = (dirs[d][1][bi, :, lanes(hg)].astype(F32) * wgt).astype(BF16)
        c_ref[i] = c_ref[i] * scale_prev + _dot(kw, dirs[d][2][bi, :, lanes(hg)], _TN) * bd_f
        n_ref[i] = n_ref[i] * scale_prev + _dot(kw, ones_cw, _TN) * bd_f
        m_ref[i, 0:1, :] = m_next


def _mlstm_scan(q, k, v, gates, cumf, cumb, consts, nb):
    bsz, seq, width = q.shape
    nc = seq // CHUNK
    masks, bd_f, bd_b, exp = consts
    fwd = lambda wd: pl.BlockSpec((nb, CHUNK, wd), lambda b, n: (b, n, 0))
    bwd = lambda wd: pl.BlockSpec((nb, CHUNK, wd), lambda b, n: (b, nc - 1 - n, 0))
    n_state = nb * 2 * N_HEAD_GROUPS
    state = pltpu.VMEM((n_state, GROUP_W, GROUP_W), F32)
    return pl.pallas_call(
        _mlstm_kernel,
        grid=(bsz // nb, nc),
        in_specs=[fwd(width), fwd(width), fwd(width), fwd(LANES), fwd(LANES),
                  bwd(width), bwd(width), bwd(width), bwd(LANES), bwd(LANES),
                  _full(masks), _full(bd_f), _full(bd_b), _full(exp)],
        out_specs=[fwd(width), bwd(width)],
        out_shape=[jax.ShapeDtypeStruct((bsz, seq, width), BF16)] * 2,
        scratch_shapes=[state, state, pltpu.VMEM((n_state, SUBLANES, GROUP_W), F32)],
        compiler_params=_params(("parallel", "arbitrary")),
        name="mlstm_scan",
    )(q, k, v, gates, cumf, q, k, v, gates, cumb, masks, bd_f, bd_b, exp)


def _outproj_kernel(h_ref, gaf_ref, gab_ref, mlf_ref, mlb_ref, az_ref, bo_ref, mg_ref, gt_ref, ng_ref, mlg_ref,
                    wa_ref, wb_ref, wo_ref, sh_ref, sc_ref, fg_ref, wr_ref, hout_ref, hn_ref, comb_ref):
    ones_bd = _head_ones(WIDTH)
    inv = 1.0 / HEAD_DIM
    f32 = lambda ref: ref[0].astype(F32)
    oa = f32(gaf_ref) + f32(gab_ref)
    ms = _mm_lhs2(oa * oa, ones_bd) * inv
    oa = oa * lax.rsqrt(ms + EPS) * ng_ref[...] * _silu(f32(az_ref))
    hb = f32(mlf_ref) + f32(mlb_ref)
    mu = _mm_lhs3(hb, ones_bd) * inv
    tc = hb - mu
    var = _mm_lhs2(tc * tc, ones_bd) * inv
    hb = tc * lax.rsqrt(var + EPS) * mlg_ref[...] * _sigmoid(f32(bo_ref))
    gate = _sigmoid(f32(mg_ref))
    y = gate[:, :D_MODEL] * _mm(oa, wa_ref[...]) + gate[:, D_MODEL:] * _mm(hb, wb_ref[...])
    h_new = h_ref[0] + gt_ref[0] * _mm(y, wo_ref[...])
    hout_ref[0] = h_new

    ms2 = jnp.mean(h_new * h_new, axis=-1, keepdims=True)
    hn = h_new * lax.rsqrt(ms2 + EPS) * (fg_ref[...] * (1.0 + sc_ref[0])) + sh_ref[0]
    hn_ref[0] = hn.astype(BF16)

    logits = _mm3(hn, wr_ref[...])
    lane = _iota(logits.shape, 1)
    lane_f = lane.astype(F32)
    big = 1e9
    gl = jnp.where((lane >= N_EXPERTS) & (lane < N_EXPERTS + N_GROUPS), logits, NEG)
    gmax = jnp.max(gl, axis=-1, keepdims=True)
    gidx = jnp.min(jnp.where(gl == gmax, lane_f - N_EXPERTS, big), axis=-1, keepdims=True)
    p_group = 1.0 / jnp.sum(jnp.exp(gl - gmax), axis=-1, keepdims=True)
    el = jnp.where((lane < N_EXPERTS) & ((lane >> 3).astype(F32) == gidx), logits, NEG)
    v1 = jnp.max(el, axis=-1, keepdims=True)
    i1 = jnp.min(jnp.where(el == v1, lane_f, big), axis=-1, keepdims=True)
    el2 = jnp.where(lane_f == i1, NEG, el)
    v2 = jnp.max(el2, axis=-1, keepdims=True)
    i2 = jnp.min(jnp.where(el2 == v2, lane_f, big), axis=-1, keepdims=True)
    e21 = jnp.exp(v2 - v1)
    w1 = p_group / (1.0 + e21)
    route = jnp.where(lane == R_E1, i1, 0.0) + jnp.where(lane == R_E2, i2, 0.0)
    comb_ref[0] = route + jnp.where(lane == R_W1, w1, 0.0) + jnp.where(lane == R_W2, w1 * e21, 0.0)


def _output_projection(h, gaf, gab, mlf, mlb, az, bo, mg, gt, ng, mlg, wa, wb, wo, sh, sc, fg, wr, tm):
    bsz, seq, d = h.shape
    tile = lambda w: pl.BlockSpec((1, tm, w), lambda b, s: (b, s, 0))
    per_b = pl.BlockSpec((1, 1, d), lambda b, s: (b, 0, 0))
    return pl.pallas_call(
        _outproj_kernel,
        grid=(bsz, seq // tm),
        in_specs=[tile(d), tile(WIDTH), tile(WIDTH), tile(WIDTH), tile(WIDTH), tile(WIDTH), tile(WIDTH),
                  tile(2 * D_MODEL), per_b, _full(ng), _full(mlg), _full(wa), _full(wb), _full(wo), per_b, per_b,
                  _full(fg), _full(wr)],
        out_specs=[tile(d), tile(d), tile(LANES)],
        out_shape=[jax.ShapeDtypeStruct((bsz, seq, d), F32), jax.ShapeDtypeStruct((bsz, seq, d), BF16),
                   jax.ShapeDtypeStruct((bsz, seq, LANES), F32)],
        compiler_params=_params(("parallel", "parallel")),
        name="output_projection",
    )(h, gaf, gab, mlf, mlb, az, bo, mg, gt, ng, mlg, wa, wb, wo, sh, sc, fg, wr)


def _moe_kernel(npass_ref, acc_ref, hn_ref, route_ref, gt_ref, wg_ref, wu_ref, wd_ref, fin_ref, tri_ref,
                o_ref, y_ref, *, group, final_norm):
    g = group
    tile = pl.program_id(0) * pl.num_programs(1) + pl.program_id(1)
    tm = hn_ref.shape[1]
    cap = MOE_CAP
    rows = EXPERTS_PER_GROUP * cap
    route = route_ref[0]
    lane = _iota(route.shape, 1)
    lane_f = lane.astype(F32)
    column = lambda j: jnp.sum(jnp.where(lane == j, route, 0.0), axis=-1, keepdims=True)
    first = float(g * EXPERTS_PER_GROUP)
    l1 = column(R_E1) - first
    l2 = column(R_E2) - first
    w1 = column(R_W1)
    w2 = column(R_W2)
    in1 = (l1 >= 0.0) & (l1 < EXPERTS_PER_GROUP)
    in2 = (l2 >= 0.0) & (l2 < EXPERTS_PER_GROUP)
    hit1 = in1 & (lane_f == l1)
    hit2 = in2 & (lane_f == l2)
    onehot = (jnp.where(hit1, 1.0, 0.0) + jnp.where(hit2, 1.0, 0.0)).astype(BF16)
    before = _dot(tri_ref[...], onehot)
    rank1 = jnp.sum(jnp.where(hit1, before, 0.0), axis=-1, keepdims=True)
    rank2 = jnp.sum(jnp.where(hit2, before, 0.0), axis=-1, keepdims=True)
    slot_lane = _iota((tm, rows), 1).astype(F32)
    x = hn_ref[0]
    y_ref[...] = jnp.zeros_like(y_ref)

    def one_pass(p, carry):
        lo = (p * cap).astype(F32)
        s1 = jnp.where(in1 & (rank1 >= lo) & (rank1 < lo + cap), l1 * cap + rank1 - lo, -1.0)
        s2 = jnp.where(in2 & (rank2 >= lo) & (rank2 < lo + cap), l2 * cap + rank2 - lo, -1.0)
        m1 = slot_lane == s1
        m2 = slot_lane == s2
        sel = jnp.where(m1 | m2, 1.0, 0.0).astype(BF16)
        sel_w = (jnp.where(m1, w1, 0.0) + jnp.where(m2, w2, 0.0)).astype(BF16)
        xs = _dot(sel, x, _TN).astype(BF16)
        experts = range(EXPERTS_PER_GROUP)
        xe = [xs[e * cap:(e + 1) * cap] for e in experts]
        hg = [_dot(xe[e], wg_ref[e]) for e in experts]
        hu = [_dot(xe[e], wu_ref[e]) for e in experts]
        act = [(_silu(hg[e]) * hu[e]).astype(BF16) for e in experts]
        ys = jnp.concatenate([_dot(act[e], wd_ref[e]).astype(BF16) for e in experts], axis=0)
        y_ref[...] += _dot(sel_w, ys)
        return carry

    lax.fori_loop(0, npass_ref[tile], one_pass, 0)
    out = acc_ref[0] + gt_ref[0] * y_ref[...]
    if final_norm:
        ms = jnp.mean(out * out, axis=-1, keepdims=True)
        out = out * lax.rsqrt(ms + EPS) * fin_ref[...]
    o_ref[0] = out


def _moe(h, hn, route, gt, wg, wu, wd, fin, tm, final_norm):
    bsz, seq, d = h.shape
    nt = seq // tm
    ids = route[..., (R_E1, R_E2)].astype(jnp.int32).reshape(bsz * nt, tm * 2)
    counts = jnp.sum(jax.nn.one_hot(ids, N_EXPERTS, dtype=jnp.int32), axis=1)
    most = jnp.max(counts.reshape(bsz * nt, N_GROUPS, EXPERTS_PER_GROUP), axis=-1)
    npass = ((most + MOE_CAP - 1) // MOE_CAP).T.astype(jnp.int32)
    tri = jnp.asarray(np.tril(np.ones((tm, tm), np.float32), -1)).astype(BF16)

    tile = lambda w: pl.BlockSpec((1, tm, w), lambda b, s, n: (b, s, 0))
    for g in range(N_GROUPS):
        group = lambda shape: pl.BlockSpec(shape, lambda b, s, n, g=g: (g, 0, 0))
        grid_spec = pltpu.PrefetchScalarGridSpec(
            num_scalar_prefetch=1,
            grid=(bsz, nt),
            in_specs=[tile(d), tile(d), tile(LANES), pl.BlockSpec((1, 1, d), lambda b, s, n: (b, 0, 0)),
                      group((EXPERTS_PER_GROUP, d, D_EXPERT)), group((EXPERTS_PER_GROUP, d, D_EXPERT)),
                      group((EXPERTS_PER_GROUP, D_EXPERT, d)),
                      pl.BlockSpec((1, d), lambda b, s, n: (0, 0)), pl.BlockSpec((tm, tm), lambda b, s, n: (0, 0))],
            out_specs=tile(d),
            scratch_shapes=[pltpu.VMEM((tm, d), F32)],
        )
        h = pl.pallas_call(
            functools.partial(_moe_kernel, group=g, final_norm=final_norm and g == N_GROUPS - 1),
            grid_spec=grid_spec,
            out_shape=jax.ShapeDtypeStruct((bsz, seq, d), F32),
            compiler_params=_params(("parallel", "parallel")),
            name="expert_ffn",
        )(npass[g], h, hn, route, gt, wg, wu, wd, fin, tri)
    return h


def _pick_tile(seq, want):
    tm = min(seq, want)
    assert seq % tm == 0 and tm % CHUNK == 0
    return tm


def kernel(x, c, ada_w, ada_b, norm_mix_g, norm_ffn_g, w_in, gdn_conv_w, gdn_a_log, gdn_dt_bias, gdn_norm_g, mlstm_conv_w, mlstm_i_bias, mlstm_f_bias, mlstm_norm_g, w_branch_a, w_branch_b, w_out, router_group, router_expert, w_gate, w_up, w_down, final_norm_g):
    bsz, seq, d = x.shape
    depth = ada_w.shape[0]
    assert d == D_MODEL and seq % CHUNK == 0
    tm_in = _pick_tile(seq, 512)
    tm_out = _pick_tile(seq, 512)
    tm_moe = _pick_tile(seq, 512)
    tl_gdn = _pick_tile(seq, LANES)
    nb_scan = LANES // CHUNK
    assert tl_gdn == LANES and bsz % nb_scan == 0
    row = lambda a: a.reshape(1, -1).astype(F32)
    consts = _scan_constants()

    mod = _modulation(c.astype(F32), ada_w.astype(F32), ada_b.astype(F32))
    h = x.astype(F32)
    for l in range(depth):
        sh1, sc1, gt1, sh2, sc2, gt2 = [mod[l, :, i * d:(i + 1) * d].reshape(bsz, 1, d) for i in range(N_MOD)]

        w = w_in[l].astype(F32)
        o_z = 3 * WIDTH
        o_ag = o_z + WIDTH
        o_bqk = o_ag + 4 * N_HEADS
        o_bv = o_bqk + 2 * WIDTH
        o_bo = o_bv + WIDTH
        o_bg = o_bo + WIDTH
        o_mg = o_bg + 4 * N_HEADS
        wconv = jnp.concatenate([w[:, :o_z], w[:, o_bqk:o_bv]], axis=1).astype(BF16)
        wrest = jnp.concatenate([w[:, o_z:o_ag], w[:, o_bv:o_bo], w[:, o_bo:o_bg], w[:, o_mg:]], axis=1).astype(BF16)
        wg = jnp.concatenate([w[:, o_ag:o_bqk], w[:, o_bg:o_mg], jnp.zeros((d, LANES - 8 * N_HEADS), F32)], axis=1)
        cw = jnp.concatenate([gdn_conv_w[l], mlstm_conv_w[l]], axis=1).astype(F32)
        cw = jnp.concatenate([cw, jnp.zeros((SUBLANES - CONV_K, CONV_W), F32)], axis=0)
        zero8 = jnp.zeros((N_HEADS,), F32)
        bias = jnp.concatenate([gdn_dt_bias[l, 0], zero8, gdn_dt_bias[l, 1], zero8,
                                mlstm_i_bias[l, 0], mlstm_f_bias[l, 0], mlstm_i_bias[l, 1], mlstm_f_bias[l, 1],
                                jnp.zeros((LANES - 8 * N_HEADS,), F32)]).astype(F32)
        alog = jnp.concatenate([gdn_a_log[l, 0], zero8, gdn_a_log[l, 1], zero8,
                                jnp.zeros((LANES - 4 * N_HEADS,), F32)]).astype(F32)
        gp = jnp.concatenate([bias[None], alog[None], jnp.zeros((SUBLANES - 2, LANES), F32)], axis=0)

        aq, ak, av, bq, bk, az, bv, bo, mg, gates, cumf, cumb = _input_projection(
            h, sh1, sc1, row(norm_mix_g[l]), wconv, wrest, wg, cw, gp, tm_in)
        gaf, gab = _gdn_state(_gdn_local(aq, ak, av, gates, cumf, cumb, consts, tl_gdn), consts, nb_scan)
        mlf, mlb = _mlstm_scan(bq, bk, bv, gates, cumf, cumb, consts, nb_scan)

        wr = jnp.concatenate([router_expert[l].astype(F32), router_group[l].astype(F32),
                              jnp.zeros((d, LANES - N_EXPERTS - N_GROUPS), F32)], axis=1)
        h, hn2, comb = _output_projection(
            h, gaf, gab, mlf, mlb, az, bo, mg, gt1, row(jnp.tile(gdn_norm_g[l], N_HEADS)), row(mlstm_norm_g[l]),
            w_branch_a[l].astype(BF16), w_branch_b[l].astype(BF16), w_out[l].astype(BF16),
            sh2, sc2, row(norm_ffn_g[l]), wr, tm_out)
        h = _moe(h, hn2, comb, gt2, w_gate[l].astype(BF16), w_up[l].astype(BF16), w_down[l].astype(BF16),
                 row(final_norm_g), tm_moe, final_norm=(l == depth - 1))
    return h.astype(x.dtype)
```

```python
import functools

import numpy as np
import jax
import jax.numpy as jnp
from jax import lax
from jax.experimental import pallas as pl
from jax.experimental.pallas import tpu as pltpu

D_MODEL = 1024
HEAD_DIM = 64
N_HEADS = 8
WIDTH = N_HEADS * HEAD_DIM
CONV_K = 5
CONV_W = 3 * WIDTH + 2 * WIDTH
REST_W = 3 * WIDTH + 2 * D_MODEL
N_GROUPS = 4
EXPERTS_PER_GROUP = 8
N_EXPERTS = N_GROUPS * EXPERTS_PER_GROUP
D_EXPERT = D_MODEL // 4
N_MOD = 6
EPS = 1e-6
NEG = -1e30

CHUNK = 64
HEADS_PER_GROUP = 2
N_HEAD_GROUPS = N_HEADS // HEADS_PER_GROUP
GROUP_W = HEADS_PER_GROUP * HEAD_DIM
LEVELS = 6
LANES = 128
SUBLANES = 8
BF16_ROWS = 16
VMEM_LIMIT = 56 * 1024 * 1024

F32 = jnp.float32
BF16 = jnp.bfloat16

_NN = (((1,), (0,)), ((), ()))
_NT = (((1,), (1,)), ((), ()))
_TN = (((0,), (0,)), ((), ()))

R_E1, R_E2, R_W1, R_W2 = 0, 1, 4, 5
MOE_SUB = 128
MOE_CAP = 24
MOE_SLOTS = 256

M_INCL, M_STRICT, M_LEVEL0, M_DIAG = 0, 1, 2, 2 + LEVELS
N_MASKS = M_DIAG + 1


def _dot(a, b, dims=_NN):
    return lax.dot_general(a, b, dims, preferred_element_type=F32)


def _mm(a, b, dims=_NN):
    return _dot(a.astype(BF16), b.astype(BF16), dims)


def _split2(x):
    hi = x.astype(BF16)
    lo = (x - hi.astype(F32)).astype(BF16)
    return hi, lo


def _split3(x):
    hi = x.astype(BF16)
    r = x - hi.astype(F32)
    mid = r.astype(BF16)
    lo = (r - mid.astype(F32)).astype(BF16)
    return hi, mid, lo


def _dot_pieces(pieces, e):
    out = _dot(pieces[0], e)
    for p in pieces[1:]:
        out = out + _dot(p, e)
    return out


def _mm_lhs3(x, e):
    return _dot_pieces(_split3(x), e)


def _mm_rhs3(e, x):
    hi, mid, lo = _split3(x)
    return _dot(e, hi) + _dot(e, mid) + _dot(e, lo)


def _mm_lhs2(x, e):
    return _dot_pieces(_split2(x), e)


def _mm3(a, b):
    a_hi, a_lo = _split2(a)
    b_hi, b_lo = _split2(b)
    return _dot(a_hi, b_hi) + _dot(a_hi, b_lo) + _dot(a_lo, b_hi)


def _iota(shape, dim):
    return lax.broadcasted_iota(jnp.int32, shape, dim)


def _sigmoid(x):
    return 1.0 / (1.0 + jnp.exp(-x))


def _silu(x):
    return x * _sigmoid(x)


def _softplus(x):
    return jnp.maximum(x, 0.0) + jnp.log1p(jnp.exp(-jnp.abs(x)))


def _head_ones(width):
    r = _iota((width, width), 0) >> 6
    c = _iota((width, width), 1) >> 6
    return jnp.where(r == c, 1.0, 0.0).astype(BF16)


def _params(sem):
    return pltpu.CompilerParams(dimension_semantics=sem, vmem_limit_bytes=VMEM_LIMIT)


def _full(a):
    return pl.BlockSpec(a.shape, lambda *_: (0,) * a.ndim)


def _mod_kernel(c_ref, w_ref, b_ref, o_ref):
    o_ref[0] = _mm3(_silu(c_ref[...]), w_ref[0]) + b_ref[0]


def _modulation(c, ada_w, ada_b):
    depth, d, n = ada_w.shape
    bsz = c.shape[0]
    tn = n // 4
    return pl.pallas_call(
        _mod_kernel,
        grid=(depth, n // tn),
        in_specs=[
            pl.BlockSpec((bsz, d), lambda l, j: (0, 0)),
            pl.BlockSpec((1, d, tn), lambda l, j: (l, 0, j)),
            pl.BlockSpec((1, 1, tn), lambda l, j: (l, 0, j)),
        ],
        out_specs=pl.BlockSpec((1, bsz, tn), lambda l, j: (l, 0, j)),
        out_shape=jax.ShapeDtypeStruct((depth, bsz, n), F32),
        compiler_params=_params(("parallel", "parallel")),
        name="adaln_mod",
    )(c, ada_w, ada_b.reshape(depth, 1, n))


HALO = BF16_ROWS


def _inproj_kernel(h_ref, hprev_ref, hnext_ref, sh_ref, sc_ref, g_ref, wconv_ref, wrest_ref, wg_ref, cw_ref, gp_ref,
                   aq_ref, ak_ref, av_ref, bq_ref, bk_ref, az_ref, bv_ref, bo_ref, mg_ref, gates_ref, cumf_ref,
                   cumb_ref, p_ref):
    s = pl.program_id(1)
    ns = pl.num_programs(1)
    tm = h_ref.shape[1]
    gain = g_ref[...] * (1.0 + sc_ref[0])
    shift = sh_ref[0]

    def norm(x):
        ms = jnp.mean(x * x, axis=-1, keepdims=True)
        return x * lax.rsqrt(ms + EPS) * gain + shift

    hn = norm(h_ref[0])
    hp = norm(hprev_ref[0]) * jnp.where(s > 0, 1.0, 0.0)
    hx = norm(hnext_ref[0]) * jnp.where(s < ns - 1, 1.0, 0.0)
    hn_hi = hn.astype(BF16)
    hall = jnp.concatenate([hp.astype(BF16), hn_hi, hx.astype(BF16)], axis=0)
    p_ref[...] = _dot(hall, wconv_ref[...])

    conv = jnp.zeros((tm, CONV_W), F32)
    for j in range(CONV_K):
        conv = conv + p_ref[pl.ds(HALO - CONV_K // 2 + j, tm), :] * cw_ref[j:j + 1, :]
    cv = _silu(conv)

    ones_bd = _head_ones(WIDTH)

    def l2n(t):
        ss = _mm_lhs2(t * t, ones_bd)
        return t * lax.rsqrt(ss + EPS)

    aq_ref[0] = (l2n(cv[:, 0:WIDTH]) * (HEAD_DIM ** -0.5)).astype(BF16)
    ak_ref[0] = l2n(cv[:, WIDTH:2 * WIDTH]).astype(BF16)
    av_ref[0] = cv[:, 2 * WIDTH:3 * WIDTH].astype(BF16)
    bq_ref[0] = cv[:, 3 * WIDTH:4 * WIDTH].astype(BF16)
    bk_ref[0] = (cv[:, 4 * WIDTH:5 * WIDTH] * (HEAD_DIM ** -0.5)).astype(BF16)

    rest = _dot(hn_hi, wrest_ref[...])
    az_ref[0] = rest[:, 0:WIDTH].astype(BF16)
    bv_ref[0] = rest[:, WIDTH:2 * WIDTH].astype(BF16)
    bo_ref[0] = rest[:, 2 * WIDTH:3 * WIDTH].astype(BF16)
    mg_ref[0] = rest[:, 3 * WIDTH:].astype(BF16)

    hn_lo = (hn - hn_hi.astype(F32)).astype(BF16)
    wg_hi, wg_lo = _split2(wg_ref[...])
    pre = _dot(hn_hi, wg_hi) + _dot(hn_hi, wg_lo) + _dot(hn_lo, wg_hi)
    pre = pre + gp_ref[0:1, :]
    lane = _iota(pre.shape, 1)
    second = ((lane >> 3) & 1) == 1
    is_gdn = lane < 4 * N_HEADS
    is_ml = (lane >= 4 * N_HEADS) & (lane < 8 * N_HEADS)
    sp = _softplus(pre)
    gdn_val = jnp.where(second, _sigmoid(pre), -jnp.exp(gp_ref[1:2, :]) * sp)
    ml_val = jnp.where(second, -_softplus(-pre), pre)
    gates = jnp.where(is_gdn, gdn_val, jnp.where(is_ml, ml_val, 0.0))
    gates_ref[0] = gates

    ri = _iota((tm, tm), 0)
    ci = _iota((tm, tm), 1)
    same_chunk = (ri >> 6) == (ci >> 6)
    pieces = _split3(gates)
    for ref, tri in ((cumf_ref, same_chunk & (ri >= ci)), (cumb_ref, same_chunk & (ri <= ci))):
        tri_b = jnp.where(tri, 1.0, 0.0).astype(BF16)
        ref[0] = _dot(tri_b, pieces[0]) + _dot(tri_b, pieces[1]) + _dot(tri_b, pieces[2])


def _input_projection(h, sh, sc, gain, wconv, wrest, wg, cw, gp, tm):
    bsz, seq, d = h.shape
    nt = seq // tm
    per = tm // HALO
    nhalo = seq // HALO
    tile = lambda w: pl.BlockSpec((1, tm, w), lambda b, s: (b, s, 0))
    outs = [(WIDTH, BF16)] * 8 + [(2 * D_MODEL, BF16)] + [(LANES, F32)] * 3
    return pl.pallas_call(
        _inproj_kernel,
        grid=(bsz, nt),
        in_specs=[
            tile(d),
            pl.BlockSpec((1, HALO, d), lambda b, s: (b, jnp.maximum(s * per - 1, 0), 0)),
            pl.BlockSpec((1, HALO, d), lambda b, s: (b, jnp.minimum((s + 1) * per, nhalo - 1), 0)),
            pl.BlockSpec((1, 1, d), lambda b, s: (b, 0, 0)),
            pl.BlockSpec((1, 1, d), lambda b, s: (b, 0, 0)),
            _full(gain), _full(wconv), _full(wrest), _full(wg), _full(cw), _full(gp),
        ],
        out_specs=[tile(w) for w, _ in outs],
        out_shape=[jax.ShapeDtypeStruct((bsz, seq, w), dt) for w, dt in outs],
        scratch_shapes=[pltpu.VMEM((tm + 2 * HALO, CONV_W), F32)],
        compiler_params=_params(("parallel", "parallel")),
        name="input_projection",
    )(h, h, h, sh, sc, gain, wconv, wrest, wg, cw, gp)


def _gate_column(branch, direction, kind):
    return branch * 4 * N_HEADS + direction * 2 * N_HEADS + kind * N_HEADS


def _expand(pieces, e):
    rows = pieces[0].shape[0]
    out = _dot(jnp.concatenate(pieces, axis=0), e)
    return functools.reduce(lambda a, b: a + b, [out[i * rows:(i + 1) * rows] for i in range(len(pieces))])


def _head_rows(x_t, column, hg, block):
    first = column + hg * HEADS_PER_GROUP
    pieces = [x_t[first + h:first + h + 1, block * CHUNK:(block + 1) * CHUNK] for h in range(HEADS_PER_GROUP)]
    return jnp.concatenate(pieces, axis=1)


def _scan_constants():
    c, w = CHUNK, GROUP_W
    row = np.arange(c)[:, None]
    col = (np.arange(w) % c)[None, :]
    masks = np.zeros((2, N_MASKS, c, w), np.float32)
    for d in range(2):
        masks[d, M_INCL] = (row >= col) if d == 0 else (row <= col)
        masks[d, M_STRICT] = (row > col) if d == 0 else (row < col)
        for p in range(LEVELS):
            same = (row >> (p + 1)) == (col >> (p + 1))
            r_bit = (row >> p) & 1
            c_bit = (col >> p) & 1
            masks[d, M_LEVEL0 + p] = same & ((r_bit == 1) & (c_bit == 0) if d == 0 else (r_bit == 0) & (c_bit == 1))
        masks[d, M_DIAG] = row == col
    bd = (np.arange(w)[:, None] // HEAD_DIM) == (np.arange(w)[None, :] // HEAD_DIM)
    exp = np.zeros((2, 2, 2, LANES, WIDTH), np.float32)
    lane_head = np.arange(WIDTH) // HEAD_DIM
    for br in range(2):
        for d in range(2):
            for kind in range(2):
                exp[br, d, kind, _gate_column(br, d, kind) + lane_head, np.arange(WIDTH)] = 1.0
    return (jnp.asarray(masks), jnp.asarray(bd.astype(np.float32)),
            jnp.asarray(bd.astype(np.float32)).astype(BF16), jnp.asarray(exp).astype(BF16))


def _block_diag(x_cat, bd_bf16):
    xb = x_cat.astype(BF16)
    return jnp.concatenate([xb] * HEADS_PER_GROUP, axis=0) * bd_bf16


def _gdn_local_kernel(q_ref, k_ref, v_ref, g_ref, cumf_ref, cumb_ref, masks_ref, bdb_ref, exp_ref,
                      uf_ref, wf_ref, qef_ref, inf_ref, kef_ref, cdf_ref,
                      ub_ref, wb_ref, qeb_ref, inb_ref, keb_ref, cdb_ref):
    c, w = CHUNK, GROUP_W
    bd_b = bdb_ref[...]
    outs = ((uf_ref, wf_ref, qef_ref, inf_ref, kef_ref, cdf_ref), (ub_ref, wb_ref, qeb_ref, inb_ref, keb_ref, cdb_ref))
    chains = [(ci, d, hg) for ci in range(q_ref.shape[1] // c) for d in range(2) for hg in range(N_HEAD_GROUPS)]
    rows = lambda ci: slice(ci * c, (ci + 1) * c)
    lanes = lambda hg: slice(hg * w, (hg + 1) * w)

    cums = (cumf_ref[0], cumb_ref[0])
    g_pieces = _split2(g_ref[0])
    gc_all = [_expand(_split3(cums[d]), exp_ref[0, d, 0]) for d in range(2)]
    beta_all = [_expand(g_pieces, exp_ref[0, d, 1]) for d in range(2)]
    per_block = LANES // c
    n_blocks = q_ref.shape[1] // LANES
    cum_t = [[cums[d][blk * LANES:(blk + 1) * LANES].T for blk in range(n_blocks)] for d in range(2)]
    gc = [gc_all[d][rows(ci), lanes(hg)] for ci, d, hg in chains]
    beta = [beta_all[d][rows(ci), lanes(hg)] for ci, d, hg in chains]
    gc_t = [_head_rows(cum_t[d][ci // per_block], _gate_column(0, d, 0), hg, ci % per_block) for ci, d, hg in chains]

    qk_kk = {}
    for ci, d, hg in chains:
        if d == 0:
            q = q_ref[0, rows(ci), lanes(hg)]
            k = k_ref[0, rows(ci), lanes(hg)]
            k_bd = jnp.concatenate([k] * HEADS_PER_GROUP, axis=0) * bd_b
            qk_kk[ci, hg] = _dot(jnp.concatenate([q, k], axis=0), k_bd, _NT)
    kbeta = [k_ref[0, rows(ci), lanes(hg)].astype(F32) * beta[i] for i, (ci, d, hg) in enumerate(chains)]

    a, t = [], []
    for i, (ci, d, hg) in enumerate(chains):
        decay = jnp.exp(jnp.minimum(gc[i] - gc_t[i], 0.0)) * masks_ref[d, M_INCL]
        outs[d][3][0, rows(ci), lanes(hg)] = (qk_kk[ci, hg][:c] * decay).astype(BF16)
        a.append(qk_kk[ci, hg][c:] * beta[i] * decay * masks_ref[d, M_STRICT])
        t.append(masks_ref[d, M_DIAG] - a[i] * masks_ref[d, M_LEVEL0])

    for p in range(1, LEVELS):
        y = [_mm(t[i], _block_diag(a[i] * masks_ref[d, M_LEVEL0 + p], bd_b)) for i, (_, d, _) in enumerate(chains)]
        t = [t[i] - _mm(y[i], _block_diag(t[i], bd_b)) for i in range(len(chains))]

    for i, (ci, d, hg) in enumerate(chains):
        u_ref, w_ref, qe_ref, _, ke_ref, cd_ref = outs[d]
        last = c - 1 if d == 0 else 0
        g_last = gc[i][last:last + 1, :]
        e_gc = jnp.exp(gc[i])
        v = v_ref[0, rows(ci), lanes(hg)].astype(F32)
        u_ref[0, rows(ci), lanes(hg)] = _mm(t[i], _block_diag(v * beta[i], bd_b))
        w_ref[0, rows(ci), lanes(hg)] = _mm(t[i], _block_diag(kbeta[i] * e_gc, bd_b)).astype(BF16)
        qe_ref[0, rows(ci), lanes(hg)] = (q_ref[0, rows(ci), lanes(hg)].astype(F32) * e_gc).astype(BF16)
        k = k_ref[0, rows(ci), lanes(hg)].astype(F32)
        ke_ref[0, rows(ci), lanes(hg)] = (k * jnp.exp(g_last - gc[i])).astype(BF16)
        cd_ref[0, ci, :, lanes(hg)] = jnp.exp(g_last)


def _gdn_local(q, k, v, gates, cumf, cumb, consts, tl):
    bsz, seq, width = q.shape
    masks, _, bd_b, exp = consts
    tile = lambda wd: pl.BlockSpec((1, tl, wd), lambda b, s: (b, s, 0))
    cd_spec = pl.BlockSpec((1, tl // CHUNK, 1, width), lambda b, s: (b, s, 0, 0))
    big = lambda dt: jax.ShapeDtypeStruct((bsz, seq, width), dt)
    cd_shape = jax.ShapeDtypeStruct((bsz, seq // CHUNK, 1, width), F32)
    per_dir_specs = [tile(width)] * 5 + [cd_spec]
    per_dir_shapes = [big(F32), big(BF16), big(BF16), big(BF16), big(BF16), cd_shape]
    return pl.pallas_call(
        _gdn_local_kernel,
        grid=(bsz, seq // tl),
        in_specs=[tile(width), tile(width), tile(width), tile(LANES), tile(LANES), tile(LANES),
                  _full(masks), _full(bd_b), _full(exp)],
        out_specs=per_dir_specs * 2,
        out_shape=per_dir_shapes * 2,
        compiler_params=_params(("parallel", "parallel")),
        name="gdn_local",
    )(q, k, v, gates, cumf, cumb, masks, bd_b, exp)


def _gdn_state_kernel(uf_ref, wf_ref, qef_ref, inf_ref, kef_ref, cdf_ref,
                      ub_ref, wb_ref, qeb_ref, inb_ref, keb_ref, cdb_ref, bdf_ref, bdb_ref, of_ref, ob_ref, s_ref):
    n = pl.program_id(1)

    @pl.when(n == 0)
    def _():
        s_ref[...] = jnp.zeros_like(s_ref)

    c, w = CHUNK, GROUP_W
    bd_b = bdb_ref[...]
    dirs = ((uf_ref, wf_ref, qef_ref, inf_ref, kef_ref, cdf_ref, of_ref),
            (ub_ref, wb_ref, qeb_ref, inb_ref, keb_ref, cdb_ref, ob_ref))
    nb = uf_ref.shape[0]
    chains = [(bi, d, hg) for bi in range(nb) for d in range(2) for hg in range(N_HEAD_GROUPS)]
    lanes = lambda hg: slice(hg * w, (hg + 1) * w)
    ws_qs = []
    for idx, (bi, d, hg) in enumerate(chains):
        w_ref, qe_ref = dirs[d][1], dirs[d][2]
        lhs = jnp.concatenate([w_ref[bi, :, lanes(hg)], qe_ref[bi, :, lanes(hg)]], axis=0)
        ws_qs.append(_dot(lhs, s_ref[idx].astype(BF16)))
    v_new = [(dirs[d][0][bi, :, lanes(hg)] - ws_qs[idx][:c]).astype(BF16) for idx, (bi, d, hg) in enumerate(chains)]
    for idx, (bi, d, hg) in enumerate(chains):
        in_ref, o_ref = dirs[d][3], dirs[d][6]
        o = ws_qs[idx][c:] + _dot(in_ref[bi, :, lanes(hg)], _block_diag(v_new[idx], bd_b))
        o_ref[bi, :, lanes(hg)] = o.astype(BF16)
    for idx, (bi, d, hg) in enumerate(chains):
        ke_ref, cd_ref = dirs[d][4], dirs[d][5]
        ds = _dot(ke_ref[bi, :, lanes(hg)], v_new[idx], _TN)
        s_ref[idx] = s_ref[idx] * cd_ref[bi, 0, :, lanes(hg)] + ds * bdf_ref[...]


def _gdn_state(local_outs, consts, nb):
    uf = local_outs[0]
    bsz, seq, width = uf.shape
    nc = seq // CHUNK
    _, bd_f, bd_b, _ = consts
    fwd = pl.BlockSpec((nb, CHUNK, width), lambda b, n: (b, n, 0))
    bwd = pl.BlockSpec((nb, CHUNK, width), lambda b, n: (b, nc - 1 - n, 0))
    cd_fwd = pl.BlockSpec((nb, 1, 1, width), lambda b, n: (b, n, 0, 0))
    cd_bwd = pl.BlockSpec((nb, 1, 1, width), lambda b, n: (b, nc - 1 - n, 0, 0))
    return pl.pallas_call(
        _gdn_state_kernel,
        grid=(bsz // nb, nc),
        in_specs=[fwd] * 5 + [cd_fwd] + [bwd] * 5 + [cd_bwd] + [_full(bd_f), _full(bd_b)],
        out_specs=[fwd, bwd],
        out_shape=[jax.ShapeDtypeStruct((bsz, seq, width), BF16)] * 2,
        scratch_shapes=[pltpu.VMEM((nb * 2 * N_HEAD_GROUPS, GROUP_W, GROUP_W), F32)],
        compiler_params=_params(("parallel", "arbitrary")),
        name="gdn_state",
    )(*local_outs, bd_f, bd_b)


def _mlstm_kernel(qf_ref, kf_ref, vf_ref, gf_ref, cf_ref, qb_ref, kb_ref, vb_ref, gb_ref, cb_ref,
                  masks_ref, bdf_ref, bdb_ref, exp_ref, of_ref, ob_ref, c_ref, n_ref, m_ref):
    n = pl.program_id(1)

    @pl.when(n == 0)
    def _():
        c_ref[...] = jnp.zeros_like(c_ref)
        n_ref[...] = jnp.zeros_like(n_ref)
        m_ref[...] = jnp.zeros_like(m_ref)

    c, w = CHUNK, GROUP_W
    bd_f = bdf_ref[...]
    bd_b = bdb_ref[...]
    ones_rows = jnp.ones((SUBLANES, c), BF16)
    row = _iota((c, w), 0)
    dirs = ((qf_ref, kf_ref, vf_ref, gf_ref, cf_ref, of_ref), (qb_ref, kb_ref, vb_ref, gb_ref, cb_ref, ob_ref))
    nb = qf_ref.shape[0]
    chains = [(bi, d, hg) for bi in range(nb) for d in range(2) for hg in range(N_HEAD_GROUPS)]
    lanes = lambda hg: slice(hg * w, (hg + 1) * w)
    gates = [jnp.concatenate([dirs[d][3][bi] for bi in range(nb)], axis=0) for d in range(2)]
    cums = [jnp.concatenate([dirs[d][4][bi] for bi in range(nb)], axis=0) for d in range(2)]
    i_all = [_expand(_split3(gates[d]), exp_ref[1, d, 0]) for d in range(2)]
    bcum_all = [_expand(_split3(cums[d]), exp_ref[1, d, 1]) for d in range(2)]
    gates_t = [gates[d].T for d in range(2)]
    cums_t = [cums[d].T for d in range(2)]
    rows = lambda bi: slice(bi * c, (bi + 1) * c)
    bcum = [bcum_all[d][rows(bi), lanes(hg)] for bi, d, hg in chains]
    r = [i_all[d][rows(bi), lanes(hg)] - bcum[i] for i, (bi, d, hg) in enumerate(chains)]
    r_t = [_head_rows(gates_t[d], _gate_column(1, d, 0), hg, bi) - _head_rows(cums_t[d], _gate_column(1, d, 1), hg, bi)
           for bi, d, hg in chains]
    qk = []
    for bi, d, hg in chains:
        k_bd = jnp.concatenate([dirs[d][1][bi, :, lanes(hg)]] * HEADS_PER_GROUP, axis=0) * bd_b
        qk.append(_dot(dirs[d][0][bi, :, lanes(hg)], k_bd, _NT))

    pm, mx, inter_w = [], [], []
    for i, (bi, d, hg) in enumerate(chains):
        m = r[i]
        for p in range(LEVELS):
            sh = 1 << p
            if d == 0:
                m = jnp.maximum(m, jnp.where(row >= sh, pltpu.roll(m, sh, 0), NEG))
            else:
                m = jnp.maximum(m, jnp.where(row < c - sh, pltpu.roll(m, c - sh, 0), NEG))
        pm.append(m)
        m_prev = m_ref[i, 0:1, :]
        mx.append(jnp.maximum(m_prev, m))
        inter_w.append(jnp.exp(m_prev - mx[i]))
        qk[i] = qk[i] * jnp.exp(jnp.where(masks_ref[d, M_INCL] > 0.0, r_t[i] - mx[i], NEG))

    num = [_mm(qk[i], _block_diag(dirs[d][2][bi, :, lanes(hg)], bd_b)) for i, (bi, d, hg) in enumerate(chains)]
    den = [_mm_lhs2(qk[i], bd_b) for i in range(len(chains))]
    qc = [_mm(dirs[d][0][bi, :, lanes(hg)], c_ref[i]) for i, (bi, d, hg) in enumerate(chains)]
    qn = [_mm(dirs[d][0][bi, :, lanes(hg)].astype(F32) * n_ref[i, 0:1, :], bd_b)
          for i, (bi, d, hg) in enumerate(chains)]
    for i, (bi, d, hg) in enumerate(chains):
        full_num = num[i] + inter_w[i] * qc[i]
        full_den = den[i] + inter_w[i] * qn[i]
        hidden = full_num / jnp.maximum(jnp.abs(full_den), jnp.exp(-(bcum[i] + mx[i])))
        dirs[d][5][bi, :, lanes(hg)] = hidden.astype(BF16)

    for i, (bi, d, hg) in enumerate(chains):
        last = c - 1 if d == 0 else 0
        m_prev = m_ref[i, 0:1, :]
        b_last = bcum[i][last:last + 1, :]
        m_next = b_last + jnp.maximum(m_prev, pm[i][last:last + 1, :])
        scale_prev = jnp.exp(b_last + m_prev - m_next)
        wgt = jnp.exp(b_last + r[i] - m_next)
        kw = (dirs[d][1][bi, :, lanes(hg)].astype(F32) * wgt).astype(BF16)
        c_ref[i] = c_ref[i] * scale_prev + _dot(kw, dirs[d][2][bi, :, lanes(hg)], _TN) * bd_f
        n_ref[i, 0:1, :] = n_ref[i, 0:1, :] * scale_prev + _dot(ones_rows, kw)[0:1, :]
        m_ref[i, 0:1, :] = m_next


def _mlstm_scan(q, k, v, gates, cumf, cumb, consts, nb):
    bsz, seq, width = q.shape
    nc = seq // CHUNK
    masks, bd_f, bd_b, exp = consts
    fwd = lambda wd: pl.BlockSpec((nb, CHUNK, wd), lambda b, n: (b, n, 0))
    bwd = lambda wd: pl.BlockSpec((nb, CHUNK, wd), lambda b, n: (b, nc - 1 - n, 0))
    n_state = nb * 2 * N_HEAD_GROUPS
    state = pltpu.VMEM((n_state, GROUP_W, GROUP_W), F32)
    row_state = pltpu.VMEM((n_state, SUBLANES, GROUP_W), F32)
    return pl.pallas_call(
        _mlstm_kernel,
        grid=(bsz // nb, nc),
        in_specs=[fwd(width), fwd(width), fwd(width), fwd(LANES), fwd(LANES),
                  bwd(width), bwd(width), bwd(width), bwd(LANES), bwd(LANES),
                  _full(masks), _full(bd_f), _full(bd_b), _full(exp)],
        out_specs=[fwd(width), bwd(width)],
        out_shape=[jax.ShapeDtypeStruct((bsz, seq, width), BF16)] * 2,
        scratch_shapes=[state, row_state, row_state],
        compiler_params=_params(("parallel", "arbitrary")),
        name="mlstm_scan",
    )(q, k, v, gates, cumf, q, k, v, gates, cumb, masks, bd_f, bd_b, exp)


def _outproj_kernel(h_ref, gaf_ref, gab_ref, mlf_ref, mlb_ref, az_ref, bo_ref, mg_ref, gt_ref, ng_ref, mlg_ref,
                    wa_ref, wb_ref, wo_ref, sh_ref, sc_ref, fg_ref, wr_ref, hout_ref, hn_ref, comb_ref):
    ones_bd = _head_ones(WIDTH)
    inv = 1.0 / HEAD_DIM
    f32 = lambda ref: ref[0].astype(F32)
    oa = f32(gaf_ref) + f32(gab_ref)
    ms = _mm_lhs2(oa * oa, ones_bd) * inv
    oa = oa * lax.rsqrt(ms + EPS) * ng_ref[...] * _silu(f32(az_ref))
    hb = f32(mlf_ref) + f32(mlb_ref)
    mu = _mm_lhs3(hb, ones_bd) * inv
    tc = hb - mu
    var = _mm_lhs2(tc * tc, ones_bd) * inv
    hb = tc * lax.rsqrt(var + EPS) * mlg_ref[...] * _sigmoid(f32(bo_ref))
    gate = _sigmoid(f32(mg_ref))
    y = gate[:, :D_MODEL] * _mm(oa, wa_ref[...]) + gate[:, D_MODEL:] * _mm(hb, wb_ref[...])
    h_new = h_ref[0] + gt_ref[0] * _mm(y, wo_ref[...])
    hout_ref[0] = h_new

    ms2 = jnp.mean(h_new * h_new, axis=-1, keepdims=True)
    hn = h_new * lax.rsqrt(ms2 + EPS) * (fg_ref[...] * (1.0 + sc_ref[0])) + sh_ref[0]
    hn_ref[0] = hn.astype(BF16)

    logits = _mm3(hn, wr_ref[...])
    lane = _iota(logits.shape, 1)
    lane_f = lane.astype(F32)
    big = 1e9
    gl = jnp.where((lane >= N_EXPERTS) & (lane < N_EXPERTS + N_GROUPS), logits, NEG)
    gmax = jnp.max(gl, axis=-1, keepdims=True)
    gidx = jnp.min(jnp.where(gl == gmax, lane_f - N_EXPERTS, big), axis=-1, keepdims=True)
    p_group = 1.0 / jnp.sum(jnp.exp(gl - gmax), axis=-1, keepdims=True)
    el = jnp.where((lane < N_EXPERTS) & ((lane >> 3).astype(F32) == gidx), logits, NEG)
    v1 = jnp.max(el, axis=-1, keepdims=True)
    i1 = jnp.min(jnp.where(el == v1, lane_f, big), axis=-1, keepdims=True)
    el2 = jnp.where(lane_f == i1, NEG, el)
    v2 = jnp.max(el2, axis=-1, keepdims=True)
    i2 = jnp.min(jnp.where(el2 == v2, lane_f, big), axis=-1, keepdims=True)
    e21 = jnp.exp(v2 - v1)
    w1 = p_group / (1.0 + e21)
    route = jnp.where(lane == R_E1, i1, 0.0) + jnp.where(lane == R_E2, i2, 0.0)
    comb_ref[0] = route + jnp.where(lane == R_W1, w1, 0.0) + jnp.where(lane == R_W2, w1 * e21, 0.0)


def _output_projection(h, gaf, gab, mlf, mlb, az, bo, mg, gt, ng, mlg, wa, wb, wo, sh, sc, fg, wr, tm):
    bsz, seq, d = h.shape
    tile = lambda w: pl.BlockSpec((1, tm, w), lambda b, s: (b, s, 0))
    per_b = pl.BlockSpec((1, 1, d), lambda b, s: (b, 0, 0))
    return pl.pallas_call(
        _outproj_kernel,
        grid=(bsz, seq // tm),
        in_specs=[tile(d), tile(WIDTH), tile(WIDTH), tile(WIDTH), tile(WIDTH), tile(WIDTH), tile(WIDTH),
                  tile(2 * D_MODEL), per_b, _full(ng), _full(mlg), _full(wa), _full(wb), _full(wo), per_b, per_b,
                  _full(fg), _full(wr)],
        out_specs=[tile(d), tile(d), tile(LANES)],
        out_shape=[jax.ShapeDtypeStruct((bsz, seq, d), F32), jax.ShapeDtypeStruct((bsz, seq, d), BF16),
                   jax.ShapeDtypeStruct((bsz, seq, LANES), F32)],
        compiler_params=_params(("parallel", "parallel")),
        name="output_projection",
    )(h, gaf, gab, mlf, mlb, az, bo, mg, gt, ng, mlg, wa, wb, wo, sh, sc, fg, wr)


def _moe_kernel(npass_ref, acc_ref, hn_ref, route_ref, gt_ref, wg_ref, wu_ref, wd_ref, fin_ref, tri_ref,
                o_ref, y_ref, *, group, final_norm):
    g = group
    tile = pl.program_id(0) * pl.num_programs(1) + pl.program_id(1)
    tm = hn_ref.shape[1]
    cap = MOE_CAP
    sub = MOE_SUB
    n_sub = tm // sub
    route = route_ref[0]
    lane = _iota(route.shape, 1)
    lane_f = lane.astype(F32)
    column = lambda j: jnp.sum(jnp.where(lane == j, route, 0.0), axis=-1, keepdims=True)
    first = float(g * EXPERTS_PER_GROUP)
    l1 = column(R_E1) - first
    l2 = column(R_E2) - first
    w1 = column(R_W1)
    w2 = column(R_W2)
    in1 = (l1 >= 0.0) & (l1 < EXPERTS_PER_GROUP)
    in2 = (l2 >= 0.0) & (l2 < EXPERTS_PER_GROUP)
    hit1 = in1 & (lane_f == l1)
    hit2 = in2 & (lane_f == l2)
    onehot = (jnp.where(hit1, 1.0, 0.0) + jnp.where(hit2, 1.0, 0.0)).astype(BF16)
    rows = lambda s: slice(s * sub, (s + 1) * sub)
    before = jnp.concatenate([_dot(tri_ref[...], onehot[rows(s)]) for s in range(n_sub)], axis=0)
    rank1 = jnp.sum(jnp.where(hit1, before, 0.0), axis=-1, keepdims=True)
    rank2 = jnp.sum(jnp.where(hit2, before, 0.0), axis=-1, keepdims=True)
    slot_lane = _iota((tm, MOE_SLOTS), 1).astype(F32)
    x = hn_ref[0]
    y_ref[...] = jnp.zeros_like(y_ref)
    experts = range(EXPERTS_PER_GROUP)
    subs = range(n_sub)

    def one_pass(p, carry):
        lo = (p * cap).astype(F32)
        s1 = jnp.where(in1 & (rank1 >= lo) & (rank1 < lo + cap), l1 * cap + rank1 - lo, -1.0)
        s2 = jnp.where(in2 & (rank2 >= lo) & (rank2 < lo + cap), l2 * cap + rank2 - lo, -1.0)
        m1 = slot_lane == s1
        m2 = slot_lane == s2
        sel = jnp.where(m1 | m2, 1.0, 0.0).astype(BF16)
        sel_w = (jnp.where(m1, w1, 0.0) + jnp.where(m2, w2, 0.0)).astype(BF16)
        xs = [_dot(sel[rows(s)], x[rows(s)], _TN) for s in subs]
        xe = [jnp.concatenate([xs[s][e * cap:(e + 1) * cap] for s in subs], axis=0).astype(BF16) for e in experts]
        hg = [_dot(xe[e], wg_ref[e]) for e in experts]
        hu = [_dot(xe[e], wu_ref[e]) for e in experts]
        act = [(_silu(hg[e]) * hu[e]).astype(BF16) for e in experts]
        ye = [_dot(act[e], wd_ref[e]) for e in experts]
        pad = jnp.zeros((MOE_SLOTS - EXPERTS_PER_GROUP * cap, x.shape[1]), F32)
        for s in subs:
            ys = jnp.concatenate([ye[e][s * cap:(s + 1) * cap] for e in experts] + [pad], axis=0).astype(BF16)
            y_ref[rows(s), :] += _dot(sel_w[rows(s)], ys)
        return carry

    lax.fori_loop(0, npass_ref[tile], one_pass, 0)
    out = acc_ref[0] + gt_ref[0] * y_ref[...]
    if final_norm:
        ms = jnp.mean(out * out, axis=-1, keepdims=True)
        out = out * lax.rsqrt(ms + EPS) * fin_ref[...]
    o_ref[0] = out


def _moe(h, hn, route, gt, wg, wu, wd, fin, tm, final_norm):
    bsz, seq, d = h.shape
    nt = seq // tm
    n_sub = tm // MOE_SUB
    ids = route[..., (R_E1, R_E2)].astype(jnp.int32).reshape(bsz * nt, n_sub, MOE_SUB * 2)
    counts = jnp.sum(jax.nn.one_hot(ids, N_EXPERTS, dtype=jnp.int32), axis=2)
    most = jnp.max(counts.reshape(bsz * nt, n_sub, N_GROUPS, EXPERTS_PER_GROUP), axis=(1, 3))
    npass = ((most + MOE_CAP - 1) // MOE_CAP).T.astype(jnp.int32)
    tri = jnp.asarray(np.tril(np.ones((MOE_SUB, MOE_SUB), np.float32), -1)).astype(BF16)

    tile = lambda w: pl.BlockSpec((1, tm, w), lambda b, s, n: (b, s, 0))
    for g in range(N_GROUPS):
        group = lambda shape: pl.BlockSpec(shape, lambda b, s, n, g=g: (g, 0, 0))
        grid_spec = pltpu.PrefetchScalarGridSpec(
            num_scalar_prefetch=1,
            grid=(bsz, nt),
            in_specs=[tile(d), tile(d), tile(LANES), pl.BlockSpec((1, 1, d), lambda b, s, n: (b, 0, 0)),
                      group((EXPERTS_PER_GROUP, d, D_EXPERT)), group((EXPERTS_PER_GROUP, d, D_EXPERT)),
                      group((EXPERTS_PER_GROUP, D_EXPERT, d)),
                      pl.BlockSpec((1, d), lambda b, s, n: (0, 0)), pl.BlockSpec(tri.shape, lambda b, s, n: (0, 0))],
            out_specs=tile(d),
            scratch_shapes=[pltpu.VMEM((tm, d), F32)],
        )
        h = pl.pallas_call(
            functools.partial(_moe_kernel, group=g, final_norm=final_norm and g == N_GROUPS - 1),
            grid_spec=grid_spec,
            out_shape=jax.ShapeDtypeStruct((bsz, seq, d), F32),
            compiler_params=_params(("parallel", "parallel")),
            name="expert_ffn",
        )(npass[g], h, hn, route, gt, wg, wu, wd, fin, tri)
    return h


def _pick_tile(seq, want):
    tm = min(seq, want)
    assert seq % tm == 0 and tm % CHUNK == 0
    return tm


def kernel(x, c, ada_w, ada_b, norm_mix_g, norm_ffn_g, w_in, gdn_conv_w, gdn_a_log, gdn_dt_bias, gdn_norm_g, mlstm_conv_w, mlstm_i_bias, mlstm_f_bias, mlstm_norm_g, w_branch_a, w_branch_b, w_out, router_group, router_expert, w_gate, w_up, w_down, final_norm_g):
    bsz, seq, d = x.shape
    depth = ada_w.shape[0]
    assert d == D_MODEL and seq % CHUNK == 0
    tm_in = _pick_tile(seq, 512)
    tm_out = _pick_tile(seq, 512)
    tm_moe = _pick_tile(seq, 512)
    tl_gdn = _pick_tile(seq, LANES)
    nb_scan = LANES // CHUNK
    assert tl_gdn % LANES == 0 and bsz % nb_scan == 0
    row = lambda a: a.reshape(1, -1).astype(F32)
    consts = _scan_constants()

    mod = _modulation(c.astype(F32), ada_w.astype(F32), ada_b.astype(F32))
    h = x.astype(F32)
    for l in range(depth):
        sh1, sc1, gt1, sh2, sc2, gt2 = [mod[l, :, i * d:(i + 1) * d].reshape(bsz, 1, d) for i in range(N_MOD)]

        w = w_in[l].astype(F32)
        o_z = 3 * WIDTH
        o_ag = o_z + WIDTH
        o_bqk = o_ag + 4 * N_HEADS
        o_bv = o_bqk + 2 * WIDTH
        o_bo = o_bv + WIDTH
        o_bg = o_bo + WIDTH
        o_mg = o_bg + 4 * N_HEADS
        wconv = jnp.concatenate([w[:, :o_z], w[:, o_bqk:o_bv]], axis=1).astype(BF16)
        wrest = jnp.concatenate([w[:, o_z:o_ag], w[:, o_bv:o_bo], w[:, o_bo:o_bg], w[:, o_mg:]], axis=1).astype(BF16)
        wg = jnp.concatenate([w[:, o_ag:o_bqk], w[:, o_bg:o_mg], jnp.zeros((d, LANES - 8 * N_HEADS), F32)], axis=1)
        cw = jnp.concatenate([gdn_conv_w[l], mlstm_conv_w[l]], axis=1).astype(F32)
        cw = jnp.concatenate([cw, jnp.zeros((SUBLANES - CONV_K, CONV_W), F32)], axis=0)
        zero8 = jnp.zeros((N_HEADS,), F32)
        bias = jnp.concatenate([gdn_dt_bias[l, 0], zero8, gdn_dt_bias[l, 1], zero8,
                                mlstm_i_bias[l, 0], mlstm_f_bias[l, 0], mlstm_i_bias[l, 1], mlstm_f_bias[l, 1],
                                jnp.zeros((LANES - 8 * N_HEADS,), F32)]).astype(F32)
        alog = jnp.concatenate([gdn_a_log[l, 0], zero8, gdn_a_log[l, 1], zero8,
                                jnp.zeros((LANES - 4 * N_HEADS,), F32)]).astype(F32)
        gp = jnp.concatenate([bias[None], alog[None], jnp.zeros((SUBLANES - 2, LANES), F32)], axis=0)

        aq, ak, av, bq, bk, az, bv, bo, mg, gates, cumf, cumb = _input_projection(
            h, sh1, sc1, row(norm_mix_g[l]), wconv, wrest, wg, cw, gp, tm_in)
        gaf, gab = _gdn_state(_gdn_local(aq, ak, av, gates, cumf, cumb, consts, tl_gdn), consts, nb_scan)
        mlf, mlb = _mlstm_scan(bq, bk, bv, gates, cumf, cumb, consts, nb_scan)

        wr = jnp.concatenate([router_expert[l].astype(F32), router_group[l].astype(F32),
                              jnp.zeros((d, LANES - N_EXPERTS - N_GROUPS), F32)], axis=1)
        h, hn2, comb = _output_projection(
            h, gaf, gab, mlf, mlb, az, bo, mg, gt1, row(jnp.tile(gdn_norm_g[l], N_HEADS)), row(mlstm_norm_g[l]),
            w_branch_a[l].astype(BF16), w_branch_b[l].astype(BF16), w_out[l].astype(BF16),
            sh2, sc2, row(norm_ffn_g[l]), wr, tm_out)
        h = _moe(h, hn2, comb, gt2, w_gate[l].astype(BF16), w_up[l].astype(BF16), w_down[l].astype(BF16),
                 row(final_norm_g), tm_moe, final_norm=(l == depth - 1))
    return h.astype(x.dtype)
```

```python
import functools

import numpy as np
import jax
import jax.numpy as jnp
from jax import lax
from jax.experimental import pallas as pl
from jax.experimental.pallas import tpu as pltpu

D_MODEL = 1024
HEAD_DIM = 64
N_HEADS = 8
WIDTH = N_HEADS * HEAD_DIM
CONV_K = 5
CONV_W = 3 * WIDTH + 2 * WIDTH
REST_W = 3 * WIDTH + 2 * D_MODEL
N_GROUPS = 4
EXPERTS_PER_GROUP = 8
N_EXPERTS = N_GROUPS * EXPERTS_PER_GROUP
D_EXPERT = D_MODEL // 4
N_MOD = 6
EPS = 1e-6
NEG = -1e30

CHUNK = 64
HEADS_PER_GROUP = 2
N_HEAD_GROUPS = N_HEADS // HEADS_PER_GROUP
GROUP_W = HEADS_PER_GROUP * HEAD_DIM
LEVELS = 6
LANES = 128
SUBLANES = 8
BF16_ROWS = 16
VMEM_LIMIT = 56 * 1024 * 1024

F32 = jnp.float32
BF16 = jnp.bfloat16

_NN = (((1,), (0,)), ((), ()))
_NT = (((1,), (1,)), ((), ()))
_TN = (((0,), (0,)), ((), ()))

R_E1, R_E2, R_W1, R_W2 = 0, 1, 4, 5
MOE_SUB = 128
MOE_CAP = 24
MOE_SLOTS = 256

M_INCL, M_STRICT, M_LEVEL0, M_DIAG = 0, 1, 2, 2 + LEVELS
N_MASKS = M_DIAG + 1


def _dot(a, b, dims=_NN):
    return lax.dot_general(a, b, dims, preferred_element_type=F32)


def _mm(a, b, dims=_NN):
    return _dot(a.astype(BF16), b.astype(BF16), dims)


def _split2(x):
    hi = x.astype(BF16)
    lo = (x - hi.astype(F32)).astype(BF16)
    return hi, lo


def _split3(x):
    hi = x.astype(BF16)
    r = x - hi.astype(F32)
    mid = r.astype(BF16)
    lo = (r - mid.astype(F32)).astype(BF16)
    return hi, mid, lo


def _dot_pieces(pieces, e):
    out = _dot(pieces[0], e)
    for p in pieces[1:]:
        out = out + _dot(p, e)
    return out


def _mm_lhs3(x, e):
    return _dot_pieces(_split3(x), e)


def _mm_rhs3(e, x):
    hi, mid, lo = _split3(x)
    return _dot(e, hi) + _dot(e, mid) + _dot(e, lo)


def _mm_lhs2(x, e):
    return _dot_pieces(_split2(x), e)


def _mm3(a, b):
    a_hi, a_lo = _split2(a)
    b_hi, b_lo = _split2(b)
    return _dot(a_hi, b_hi) + _dot(a_hi, b_lo) + _dot(a_lo, b_hi)


def _iota(shape, dim):
    return lax.broadcasted_iota(jnp.int32, shape, dim)


def _sigmoid(x):
    return 0.5 * jnp.tanh(0.5 * x) + 0.5


def _silu(x):
    return x * _sigmoid(x)


def _softplus(x):
    return jnp.maximum(x, 0.0) + jnp.log1p(jnp.exp(-jnp.abs(x)))


def _head_ones(width):
    r = _iota((width, width), 0) >> 6
    c = _iota((width, width), 1) >> 6
    return jnp.where(r == c, 1.0, 0.0).astype(BF16)


def _params(sem):
    return pltpu.CompilerParams(dimension_semantics=sem, vmem_limit_bytes=VMEM_LIMIT)


def _full(a):
    return pl.BlockSpec(a.shape, lambda *_: (0,) * a.ndim)


def _mod_kernel(c_ref, w_ref, b_ref, o_ref):
    o_ref[0] = _mm3(_silu(c_ref[...]), w_ref[0]) + b_ref[0]


def _modulation(c, ada_w, ada_b):
    depth, d, n = ada_w.shape
    bsz = c.shape[0]
    tn = n // 4
    return pl.pallas_call(
        _mod_kernel,
        grid=(depth, n // tn),
        in_specs=[
            pl.BlockSpec((bsz, d), lambda l, j: (0, 0)),
            pl.BlockSpec((1, d, tn), lambda l, j: (l, 0, j)),
            pl.BlockSpec((1, 1, tn), lambda l, j: (l, 0, j)),
        ],
        out_specs=pl.BlockSpec((1, bsz, tn), lambda l, j: (l, 0, j)),
        out_shape=jax.ShapeDtypeStruct((depth, bsz, n), F32),
        compiler_params=_params(("parallel", "parallel")),
        name="adaln_mod",
    )(c, ada_w, ada_b.reshape(depth, 1, n))


HALO = BF16_ROWS


def _inproj_kernel(h_ref, hprev_ref, hnext_ref, sh_ref, sc_ref, g_ref, wconv_ref, wrest_ref, wg_ref, cw_ref, gp_ref,
                   perm_ref, permt_ref, aq_ref, ak_ref, av_ref, bq_ref, bk_ref, az_ref, bv_ref, bo_ref, mg_ref, gates_ref, cumf_ref,
                   cumb_ref, p_ref):
    s = pl.program_id(1)
    ns = pl.num_programs(1)
    tm = h_ref.shape[1]
    gain = g_ref[...] * (1.0 + sc_ref[0])
    shift = sh_ref[0]

    def norm(x):
        ms = jnp.mean(x * x, axis=-1, keepdims=True)
        return x * lax.rsqrt(ms + EPS) * gain + shift

    hn = norm(h_ref[0])
    hp = norm(hprev_ref[0]) * jnp.where(s > 0, 1.0, 0.0)
    hx = norm(hnext_ref[0]) * jnp.where(s < ns - 1, 1.0, 0.0)
    hn_hi = hn.astype(BF16)

    nv = tm // SUBLANES
    edge = CONV_K // 2
    hn_perm = _dot(perm_ref[...], hn_hi).astype(BF16)
    pe = _dot(jnp.concatenate([hn_perm, hp.astype(BF16), hx.astype(BF16)], axis=0), wconv_ref[...])
    rest = _dot(hn_hi, wrest_ref[...])
    hn_lo = (hn - hn_hi.astype(F32)).astype(BF16)
    wg_hi, wg_lo = _split2(wg_ref[...])
    pre = _dot(hn_hi, wg_hi) + _dot(hn_hi, wg_lo) + _dot(hn_lo, wg_hi)
    prev = pe[tm:tm + HALO]
    nxt = pe[tm + HALO:]
    sub = _iota((SUBLANES, CONV_W), 0)
    block = lambda v: pe[v * SUBLANES:(v + 1) * SUBLANES]
    for i in range(edge):
        before = jnp.where(sub == 0, prev[HALO - edge + i:HALO - edge + i + 1], pltpu.roll(block(nv - edge + i), 1, 0))
        p_ref[i * SUBLANES:(i + 1) * SUBLANES, :] = before
        after = jnp.where(sub == SUBLANES - 1, nxt[i:i + 1], pltpu.roll(block(i), SUBLANES - 1, 0))
        p_ref[(edge + nv + i) * SUBLANES:(edge + nv + i + 1) * SUBLANES, :] = after
    p_ref[edge * SUBLANES:(edge + nv) * SUBLANES, :] = pe[:tm]

    conv = jnp.zeros((tm, CONV_W), F32)
    for j in range(CONV_K):
        conv = conv + p_ref[j * SUBLANES:j * SUBLANES + tm, :] * cw_ref[j:j + 1, :]
    cv = _silu(conv)

    ones_bd = _head_ones(WIDTH)

    def l2n(t):
        ss = _mm_lhs2(t * t, ones_bd)
        return t * lax.rsqrt(ss + EPS)

    permuted = jnp.concatenate([
        (l2n(cv[:, 0:WIDTH]) * (HEAD_DIM ** -0.5)).astype(BF16),
        l2n(cv[:, WIDTH:2 * WIDTH]).astype(BF16),
        cv[:, 2 * WIDTH:4 * WIDTH].astype(BF16),
        (cv[:, 4 * WIDTH:5 * WIDTH] * (HEAD_DIM ** -0.5)).astype(BF16)], axis=1)
    natural = _dot(permt_ref[...], permuted).astype(BF16)
    for i, ref in enumerate((aq_ref, ak_ref, av_ref, bq_ref, bk_ref)):
        ref[0] = natural[:, i * WIDTH:(i + 1) * WIDTH]

    az_ref[0] = rest[:, 0:WIDTH].astype(BF16)
    bv_ref[0] = rest[:, WIDTH:2 * WIDTH].astype(BF16)
    bo_ref[0] = rest[:, 2 * WIDTH:3 * WIDTH].astype(BF16)
    mg_ref[0] = rest[:, 3 * WIDTH:].astype(BF16)

    pre = pre + gp_ref[0:1, :]
    lane = _iota(pre.shape, 1)
    second = ((lane >> 3) & 1) == 1
    is_gdn = lane < 4 * N_HEADS
    is_ml = (lane >= 4 * N_HEADS) & (lane < 8 * N_HEADS)
    sp = _softplus(pre)
    gdn_val = jnp.where(second, _sigmoid(pre), -jnp.exp(gp_ref[1:2, :]) * sp)
    ml_val = jnp.where(second, -_softplus(-pre), pre)
    gates = jnp.where(is_gdn, gdn_val, jnp.where(is_ml, ml_val, 0.0))
    gates_ref[0] = gates

    ri = _iota((tm, tm), 0)
    ci = _iota((tm, tm), 1)
    same_chunk = (ri >> 6) == (ci >> 6)
    pieces = _split3(gates)
    for ref, tri in ((cumf_ref, same_chunk & (ri >= ci)), (cumb_ref, same_chunk & (ri <= ci))):
        tri_b = jnp.where(tri, 1.0, 0.0).astype(BF16)
        ref[0] = _dot(tri_b, pieces[0]) + _dot(tri_b, pieces[1]) + _dot(tri_b, pieces[2])


def _input_projection(h, sh, sc, gain, wconv, wrest, wg, cw, gp, tm):
    bsz, seq, d = h.shape
    nt = seq // tm
    per = tm // HALO
    nhalo = seq // HALO
    tile = lambda w: pl.BlockSpec((1, tm, w), lambda b, s: (b, s, 0))
    outs = [(WIDTH, BF16)] * 8 + [(2 * D_MODEL, BF16)] + [(LANES, F32)] * 3
    r = np.arange(tm)
    perm_np = np.zeros((tm, tm), np.float32)
    perm_np[r, (r % SUBLANES) * (tm // SUBLANES) + r // SUBLANES] = 1.0
    perm = jnp.asarray(perm_np).astype(BF16)
    perm_t = jnp.asarray(perm_np.T).astype(BF16)
    return pl.pallas_call(
        _inproj_kernel,
        grid=(bsz, nt),
        in_specs=[
            tile(d),
            pl.BlockSpec((1, HALO, d), lambda b, s: (b, jnp.maximum(s * per - 1, 0), 0)),
            pl.BlockSpec((1, HALO, d), lambda b, s: (b, jnp.minimum((s + 1) * per, nhalo - 1), 0)),
            pl.BlockSpec((1, 1, d), lambda b, s: (b, 0, 0)),
            pl.BlockSpec((1, 1, d), lambda b, s: (b, 0, 0)),
            _full(gain), _full(wconv), _full(wrest), _full(wg), _full(cw), _full(gp), _full(perm), _full(perm_t),
        ],
        out_specs=[tile(w) for w, _ in outs],
        out_shape=[jax.ShapeDtypeStruct((bsz, seq, w), dt) for w, dt in outs],
        scratch_shapes=[pltpu.VMEM((tm + 2 * HALO, CONV_W), F32)],
        compiler_params=_params(("parallel", "parallel")),
        name="input_projection",
    )(h, h, h, sh, sc, gain, wconv, wrest, wg, cw, gp, perm, perm_t)


def _gate_column(branch, direction, kind):
    return branch * 4 * N_HEADS + direction * 2 * N_HEADS + kind * N_HEADS


def _expand(pieces, e):
    rows = pieces[0].shape[0]
    out = _dot(jnp.concatenate(pieces, axis=0), e)
    return functools.reduce(lambda a, b: a + b, [out[i * rows:(i + 1) * rows] for i in range(len(pieces))])


def _head_rows(x_t, column, hg, block):
    first = column + hg * HEADS_PER_GROUP
    pieces = [x_t[first + h:first + h + 1, block * CHUNK:(block + 1) * CHUNK] for h in range(HEADS_PER_GROUP)]
    return jnp.concatenate(pieces, axis=1)


def _scan_constants():
    c, w = CHUNK, GROUP_W
    row = np.arange(c)[:, None]
    col = (np.arange(w) % c)[None, :]
    masks = np.zeros((2, N_MASKS, c, w), np.float32)
    for d in range(2):
        masks[d, M_INCL] = (row >= col) if d == 0 else (row <= col)
        masks[d, M_STRICT] = (row > col) if d == 0 else (row < col)
        for p in range(LEVELS):
            same = (row >> (p + 1)) == (col >> (p + 1))
            r_bit = (row >> p) & 1
            c_bit = (col >> p) & 1
            masks[d, M_LEVEL0 + p] = same & ((r_bit == 1) & (c_bit == 0) if d == 0 else (r_bit == 0) & (c_bit == 1))
        masks[d, M_DIAG] = row == col
    bd = (np.arange(w)[:, None] // HEAD_DIM) == (np.arange(w)[None, :] // HEAD_DIM)
    exp = np.zeros((2, 2, 2, LANES, WIDTH), np.float32)
    lane_head = np.arange(WIDTH) // HEAD_DIM
    for br in range(2):
        for d in range(2):
            for kind in range(2):
                exp[br, d, kind, _gate_column(br, d, kind) + lane_head, np.arange(WIDTH)] = 1.0
    return (jnp.asarray(masks), jnp.asarray(bd.astype(np.float32)),
            jnp.asarray(bd.astype(np.float32)).astype(BF16), jnp.asarray(exp).astype(BF16))


def _block_diag(x_cat, bd_bf16):
    xb = x_cat.astype(BF16)
    return jnp.concatenate([xb] * HEADS_PER_GROUP, axis=0) * bd_bf16


def _gdn_local_kernel(q_ref, k_ref, v_ref, g_ref, cumf_ref, cumb_ref, masks_ref, bdb_ref, exp_ref,
                      uf_ref, wf_ref, qef_ref, inf_ref, kef_ref, cdf_ref,
                      ub_ref, wb_ref, qeb_ref, inb_ref, keb_ref, cdb_ref):
    c, w = CHUNK, GROUP_W
    bd_b = bdb_ref[...]
    outs = ((uf_ref, wf_ref, qef_ref, inf_ref, kef_ref, cdf_ref), (ub_ref, wb_ref, qeb_ref, inb_ref, keb_ref, cdb_ref))
    chains = [(ci, d, hg) for ci in range(q_ref.shape[1] // c) for d in range(2) for hg in range(N_HEAD_GROUPS)]
    rows = lambda ci: slice(ci * c, (ci + 1) * c)
    lanes = lambda hg: slice(hg * w, (hg + 1) * w)

    cums = (cumf_ref[0], cumb_ref[0])
    g_pieces = _split2(g_ref[0])
    gc_all = [_expand(_split3(cums[d]), exp_ref[0, d, 0]) for d in range(2)]
    beta_all = [_expand(g_pieces, exp_ref[0, d, 1]) for d in range(2)]
    per_block = LANES // c
    n_blocks = q_ref.shape[1] // LANES
    cum_t = [[cums[d][blk * LANES:(blk + 1) * LANES].T for blk in range(n_blocks)] for d in range(2)]
    gc = [gc_all[d][rows(ci), lanes(hg)] for ci, d, hg in chains]
    beta = [beta_all[d][rows(ci), lanes(hg)] for ci, d, hg in chains]
    gc_t = [_head_rows(cum_t[d][ci // per_block], _gate_column(0, d, 0), hg, ci % per_block) for ci, d, hg in chains]

    qk_kk = {}
    for ci, d, hg in chains:
        if d == 0:
            q = q_ref[0, rows(ci), lanes(hg)]
            k = k_ref[0, rows(ci), lanes(hg)]
            k_bd = jnp.concatenate([k] * HEADS_PER_GROUP, axis=0) * bd_b
            qk_kk[ci, hg] = _dot(jnp.concatenate([q, k], axis=0), k_bd, _NT)
    kbeta = [k_ref[0, rows(ci), lanes(hg)].astype(F32) * beta[i] for i, (ci, d, hg) in enumerate(chains)]

    a, t = [], []
    for i, (ci, d, hg) in enumerate(chains):
        decay = jnp.exp(jnp.minimum(gc[i] - gc_t[i], 0.0)) * masks_ref[d, M_INCL]
        outs[d][3][0, rows(ci), lanes(hg)] = (qk_kk[ci, hg][:c] * decay).astype(BF16)
        a.append(qk_kk[ci, hg][c:] * beta[i] * decay * masks_ref[d, M_STRICT])
        t.append(masks_ref[d, M_DIAG] - a[i] * masks_ref[d, M_LEVEL0])

    for p in range(1, LEVELS):
        y = [_mm(t[i], _block_diag(a[i] * masks_ref[d, M_LEVEL0 + p], bd_b)) for i, (_, d, _) in enumerate(chains)]
        t = [t[i] - _mm(y[i], _block_diag(t[i], bd_b)) for i in range(len(chains))]

    for i, (ci, d, hg) in enumerate(chains):
        u_ref, w_ref, qe_ref, _, ke_ref, cd_ref = outs[d]
        last = c - 1 if d == 0 else 0
        g_last = gc[i][last:last + 1, :]
        e_gc = jnp.exp(gc[i])
        v = v_ref[0, rows(ci), lanes(hg)].astype(F32)
        u_ref[0, rows(ci), lanes(hg)] = _mm(t[i], _block_diag(v * beta[i], bd_b))
        w_ref[0, rows(ci), lanes(hg)] = _mm(t[i], _block_diag(kbeta[i] * e_gc, bd_b)).astype(BF16)
        qe_ref[0, rows(ci), lanes(hg)] = (q_ref[0, rows(ci), lanes(hg)].astype(F32) * e_gc).astype(BF16)
        k = k_ref[0, rows(ci), lanes(hg)].astype(F32)
        ke_ref[0, rows(ci), lanes(hg)] = (k * jnp.exp(g_last - gc[i])).astype(BF16)
        cd_ref[0, ci, :, lanes(hg)] = jnp.exp(g_last)


def _gdn_local(q, k, v, gates, cumf, cumb, consts, tl):
    bsz, seq, width = q.shape
    masks, _, bd_b, exp = consts
    tile = lambda wd: pl.BlockSpec((1, tl, wd), lambda b, s: (b, s, 0))
    cd_spec = pl.BlockSpec((1, tl // CHUNK, 1, width), lambda b, s: (b, s, 0, 0))
    big = lambda dt: jax.ShapeDtypeStruct((bsz, seq, width), dt)
    cd_shape = jax.ShapeDtypeStruct((bsz, seq // CHUNK, 1, width), F32)
    per_dir_specs = [tile(width)] * 5 + [cd_spec]
    per_dir_shapes = [big(F32), big(BF16), big(BF16), big(BF16), big(BF16), cd_shape]
    return pl.pallas_call(
        _gdn_local_kernel,
        grid=(bsz, seq // tl),
        in_specs=[tile(width), tile(width), tile(width), tile(LANES), tile(LANES), tile(LANES),
                  _full(masks), _full(bd_b), _full(exp)],
        out_specs=per_dir_specs * 2,
        out_shape=per_dir_shapes * 2,
        compiler_params=_params(("parallel", "parallel")),
        name="gdn_local",
    )(q, k, v, gates, cumf, cumb, masks, bd_b, exp)


def _gdn_state_kernel(uf_ref, wf_ref, qef_ref, inf_ref, kef_ref, cdf_ref,
                      ub_ref, wb_ref, qeb_ref, inb_ref, keb_ref, cdb_ref, bdf_ref, bdb_ref, of_ref, ob_ref, s_ref):
    n = pl.program_id(1)

    @pl.when(n == 0)
    def _():
        s_ref[...] = jnp.zeros_like(s_ref)

    c, w = CHUNK, GROUP_W
    bd_b = bdb_ref[...]
    dirs = ((uf_ref, wf_ref, qef_ref, inf_ref, kef_ref, cdf_ref, of_ref),
            (ub_ref, wb_ref, qeb_ref, inb_ref, keb_ref, cdb_ref, ob_ref))
    nb = uf_ref.shape[0]
    chains = [(bi, d, hg) for bi in range(nb) for d in range(2) for hg in range(N_HEAD_GROUPS)]
    lanes = lambda hg: slice(hg * w, (hg + 1) * w)
    ws_qs = []
    for idx, (bi, d, hg) in enumerate(chains):
        w_ref, qe_ref = dirs[d][1], dirs[d][2]
        lhs = jnp.concatenate([w_ref[bi, :, lanes(hg)], qe_ref[bi, :, lanes(hg)]], axis=0)
        ws_qs.append(_dot(lhs, s_ref[idx].astype(BF16)))
    v_new = [(dirs[d][0][bi, :, lanes(hg)] - ws_qs[idx][:c]).astype(BF16) for idx, (bi, d, hg) in enumerate(chains)]
    for idx, (bi, d, hg) in enumerate(chains):
        in_ref, o_ref = dirs[d][3], dirs[d][6]
        o = ws_qs[idx][c:] + _dot(in_ref[bi, :, lanes(hg)], _block_diag(v_new[idx], bd_b))
        o_ref[bi, :, lanes(hg)] = o.astype(BF16)
    for idx, (bi, d, hg) in enumerate(chains):
        ke_ref, cd_ref = dirs[d][4], dirs[d][5]
        ds = _dot(ke_ref[bi, :, lanes(hg)], v_new[idx], _TN)
        s_ref[idx] = s_ref[idx] * cd_ref[bi, 0, :, lanes(hg)] + ds * bdf_ref[...]


def _gdn_state(local_outs, consts, nb):
    uf = local_outs[0]
    bsz, seq, width = uf.shape
    nc = seq // CHUNK
    _, bd_f, bd_b, _ = consts
    fwd = pl.BlockSpec((nb, CHUNK, width), lambda b, n: (b, n, 0))
    bwd = pl.BlockSpec((nb, CHUNK, width), lambda b, n: (b, nc - 1 - n, 0))
    cd_fwd = pl.BlockSpec((nb, 1, 1, width), lambda b, n: (b, n, 0, 0))
    cd_bwd = pl.BlockSpec((nb, 1, 1, width), lambda b, n: (b, nc - 1 - n, 0, 0))
    return pl.pallas_call(
        _gdn_state_kernel,
        grid=(bsz // nb, nc),
        in_specs=[fwd] * 5 + [cd_fwd] + [bwd] * 5 + [cd_bwd] + [_full(bd_f), _full(bd_b)],
        out_specs=[fwd, bwd],
        out_shape=[jax.ShapeDtypeStruct((bsz, seq, width), BF16)] * 2,
        scratch_shapes=[pltpu.VMEM((nb * 2 * N_HEAD_GROUPS, GROUP_W, GROUP_W), F32)],
        compiler_params=_params(("parallel", "arbitrary")),
        name="gdn_state",
    )(*local_outs, bd_f, bd_b)


def _mlstm_kernel(qf_ref, kf_ref, vf_ref, gf_ref, cf_ref, qb_ref, kb_ref, vb_ref, gb_ref, cb_ref,
                  masks_ref, bdf_ref, bdb_ref, exp_ref, of_ref, ob_ref, c_ref, n_ref, m_ref):
    n = pl.program_id(1)

    @pl.when(n == 0)
    def _():
        c_ref[...] = jnp.zeros_like(c_ref)
        n_ref[...] = jnp.zeros_like(n_ref)
        m_ref[...] = jnp.zeros_like(m_ref)

    c, w = CHUNK, GROUP_W
    bd_f = bdf_ref[...]
    bd_b = bdb_ref[...]
    ones_rows = jnp.ones((SUBLANES, c), BF16)
    row = _iota((c, w), 0)
    dirs = ((qf_ref, kf_ref, vf_ref, gf_ref, cf_ref, of_ref), (qb_ref, kb_ref, vb_ref, gb_ref, cb_ref, ob_ref))
    nb = qf_ref.shape[0]
    chains = [(bi, d, hg) for bi in range(nb) for d in range(2) for hg in range(N_HEAD_GROUPS)]
    lanes = lambda hg: slice(hg * w, (hg + 1) * w)
    gates = [jnp.concatenate([dirs[d][3][bi] for bi in range(nb)], axis=0) for d in range(2)]
    cums = [jnp.concatenate([dirs[d][4][bi] for bi in range(nb)], axis=0) for d in range(2)]
    i_all = [_expand(_split3(gates[d]), exp_ref[1, d, 0]) for d in range(2)]
    bcum_all = [_expand(_split3(cums[d]), exp_ref[1, d, 1]) for d in range(2)]
    gates_t = [gates[d].T for d in range(2)]
    cums_t = [cums[d].T for d in range(2)]
    rows = lambda bi: slice(bi * c, (bi + 1) * c)
    bcum = [bcum_all[d][rows(bi), lanes(hg)] for bi, d, hg in chains]
    r = [i_all[d][rows(bi), lanes(hg)] - bcum[i] for i, (bi, d, hg) in enumerate(chains)]
    r_t = [_head_rows(gates_t[d], _gate_column(1, d, 0), hg, bi) - _head_rows(cums_t[d], _gate_column(1, d, 1), hg, bi)
           for bi, d, hg in chains]
    qk = []
    for bi, d, hg in chains:
        k_bd = jnp.concatenate([dirs[d][1][bi, :, lanes(hg)]] * HEADS_PER_GROUP, axis=0) * bd_b
        qk.append(_dot(dirs[d][0][bi, :, lanes(hg)], k_bd, _NT))

    pm, mx, inter_w = [], [], []
    for i, (bi, d, hg) in enumerate(chains):
        m = r[i]
        for p in range(LEVELS):
            sh = 1 << p
            if d == 0:
                m = jnp.maximum(m, jnp.where(row >= sh, pltpu.roll(m, sh, 0), NEG))
            else:
                m = jnp.maximum(m, jnp.where(row < c - sh, pltpu.roll(m, c - sh, 0), NEG))
        pm.append(m)
        m_prev = m_ref[i, 0:1, :]
        mx.append(jnp.maximum(m_prev, m))
        inter_w.append(jnp.exp(m_prev - mx[i]))
        qk[i] = qk[i] * jnp.exp(jnp.where(masks_ref[d, M_INCL] > 0.0, r_t[i] - mx[i], NEG))

    num = [_mm(qk[i], _block_diag(dirs[d][2][bi, :, lanes(hg)], bd_b)) for i, (bi, d, hg) in enumerate(chains)]
    den = [_mm_lhs2(qk[i], bd_b) for i in range(len(chains))]
    qc = [_mm(dirs[d][0][bi, :, lanes(hg)], c_ref[i]) for i, (bi, d, hg) in enumerate(chains)]
    qn = [_mm(dirs[d][0][bi, :, lanes(hg)].astype(F32) * n_ref[i, 0:1, :], bd_b)
          for i, (bi, d, hg) in enumerate(chains)]
    for i, (bi, d, hg) in enumerate(chains):
        full_num = num[i] + inter_w[i] * qc[i]
        full_den = den[i] + inter_w[i] * qn[i]
        hidden = full_num / jnp.maximum(jnp.abs(full_den), jnp.exp(-(bcum[i] + mx[i])))
        dirs[d][5][bi, :, lanes(hg)] = hidden.astype(BF16)

    for i, (bi, d, hg) in enumerate(chains):
        last = c - 1 if d == 0 else 0
        m_prev = m_ref[i, 0:1, :]
        b_last = bcum[i][last:last + 1, :]
        m_next = b_last + jnp.maximum(m_prev, pm[i][last:last + 1, :])
        scale_prev = jnp.exp(b_last + m_prev - m_next)
        wgt = jnp.exp(b_last + r[i] - m_next)
        kw = (dirs[d][1][bi, :, lanes(hg)].astype(F32) * wgt).astype(BF16)
        c_ref[i] = c_ref[i] * scale_prev + _dot(kw, dirs[d][2][bi, :, lanes(hg)], _TN) * bd_f
        n_ref[i, 0:1, :] = n_ref[i, 0:1, :] * scale_prev + _dot(ones_rows, kw)[0:1, :]
        m_ref[i, 0:1, :] = m_next


def _mlstm_scan(q, k, v, gates, cumf, cumb, consts, nb):
    bsz, seq, width = q.shape
    nc = seq // CHUNK
    masks, bd_f, bd_b, exp = consts
    fwd = lambda wd: pl.BlockSpec((nb, CHUNK, wd), lambda b, n: (b, n, 0))
    bwd = lambda wd: pl.BlockSpec((nb, CHUNK, wd), lambda b, n: (b, nc - 1 - n, 0))
    n_state = nb * 2 * N_HEAD_GROUPS
    state = pltpu.VMEM((n_state, GROUP_W, GROUP_W), F32)
    row_state = pltpu.VMEM((n_state, SUBLANES, GROUP_W), F32)
    return pl.pallas_call(
        _mlstm_kernel,
        grid=(bsz // nb, nc),
        in_specs=[fwd(width), fwd(width), fwd(width), fwd(LANES), fwd(LANES),
                  bwd(width), bwd(width), bwd(width), bwd(LANES), bwd(LANES),
                  _full(masks), _full(bd_f), _full(bd_b), _full(exp)],
        out_specs=[fwd(width), bwd(width)],
        out_shape=[jax.ShapeDtypeStruct((bsz, seq, width), BF16)] * 2,
        scratch_shapes=[state, row_state, row_state],
        compiler_params=_params(("parallel", "arbitrary")),
        name="mlstm_scan",
    )(q, k, v, gates, cumf, q, k, v, gates, cumb, masks, bd_f, bd_b, exp)


def _outproj_kernel(h_ref, gaf_ref, gab_ref, mlf_ref, mlb_ref, az_ref, bo_ref, mg_ref, gt_ref, ng_ref, mlg_ref,
                    wa_ref, wb_ref, wo_ref, sh_ref, sc_ref, fg_ref, wr_ref, hout_ref, hn_ref, comb_ref):
    ones_bd = _head_ones(WIDTH)
    inv = 1.0 / HEAD_DIM
    tm = h_ref.shape[1]
    n_parts = 2 if tm % (2 * BF16_ROWS) == 0 else 1
    parts = [slice(i * tm // n_parts, (i + 1) * tm // n_parts) for i in range(n_parts)]
    f32 = lambda ref, rs: ref[0, rs, :].astype(F32)

    def branch_inputs(rs):
        oa = f32(gaf_ref, rs) + f32(gab_ref, rs)
        ms = _mm_lhs2(oa * oa, ones_bd) * inv
        oa = oa * lax.rsqrt(ms + EPS) * ng_ref[...] * _silu(f32(az_ref, rs))
        hb = f32(mlf_ref, rs) + f32(mlb_ref, rs)
        mu = _mm_lhs3(hb, ones_bd) * inv
        tc = hb - mu
        var = _mm_lhs2(tc * tc, ones_bd) * inv
        hb = tc * lax.rsqrt(var + EPS) * mlg_ref[...] * _sigmoid(f32(bo_ref, rs))
        return oa.astype(BF16), hb.astype(BF16)

    def merged(rs, oa, hb):
        gate = _sigmoid(f32(mg_ref, rs))
        return gate[:, :D_MODEL] * _dot(oa, wa_ref[...]) + gate[:, D_MODEL:] * _dot(hb, wb_ref[...])

    def residual(rs, y):
        h_new = h_ref[0, rs, :] + gt_ref[0] * _mm(y, wo_ref[...])
        hout_ref[0, rs, :] = h_new
        ms2 = jnp.mean(h_new * h_new, axis=-1, keepdims=True)
        hn = h_new * lax.rsqrt(ms2 + EPS) * (fg_ref[...] * (1.0 + sc_ref[0])) + sh_ref[0]
        hn_ref[0, rs, :] = hn.astype(BF16)
        return hn

    def routing(rs, logits):
        lane = _iota(logits.shape, 1)
        lane_f = lane.astype(F32)
        big = 1e9
        gl = jnp.where((lane >= N_EXPERTS) & (lane < N_EXPERTS + N_GROUPS), logits, NEG)
        gmax = jnp.max(gl, axis=-1, keepdims=True)
        gidx = jnp.min(jnp.where(gl == gmax, lane_f - N_EXPERTS, big), axis=-1, keepdims=True)
        p_group = 1.0 / jnp.sum(jnp.exp(gl - gmax), axis=-1, keepdims=True)
        el = jnp.where((lane < N_EXPERTS) & ((lane >> 3).astype(F32) == gidx), logits, NEG)
        v1 = jnp.max(el, axis=-1, keepdims=True)
        i1 = jnp.min(jnp.where(el == v1, lane_f, big), axis=-1, keepdims=True)
        el2 = jnp.where(lane_f == i1, NEG, el)
        v2 = jnp.max(el2, axis=-1, keepdims=True)
        i2 = jnp.min(jnp.where(el2 == v2, lane_f, big), axis=-1, keepdims=True)
        e21 = jnp.exp(v2 - v1)
        w1 = p_group / (1.0 + e21)
        route = jnp.where(lane == R_E1, i1, 0.0) + jnp.where(lane == R_E2, i2, 0.0)
        comb_ref[0, rs, :] = route + jnp.where(lane == R_W1, w1, 0.0) + jnp.where(lane == R_W2, w1 * e21, 0.0)

    branches = [branch_inputs(rs) for rs in parts]
    ys = [merged(rs, *branches[i]) for i, rs in enumerate(parts)]
    hns = [residual(rs, ys[i]) for i, rs in enumerate(parts)]
    logits = [_mm3(hn, wr_ref[...]) for hn in hns]
    for i, rs in enumerate(parts):
        routing(rs, logits[i])


def _output_projection(h, gaf, gab, mlf, mlb, az, bo, mg, gt, ng, mlg, wa, wb, wo, sh, sc, fg, wr, tm):
    bsz, seq, d = h.shape
    tile = lambda w: pl.BlockSpec((1, tm, w), lambda b, s: (b, s, 0))
    per_b = pl.BlockSpec((1, 1, d), lambda b, s: (b, 0, 0))
    return pl.pallas_call(
        _outproj_kernel,
        grid=(bsz, seq // tm),
        in_specs=[tile(d), tile(WIDTH), tile(WIDTH), tile(WIDTH), tile(WIDTH), tile(WIDTH), tile(WIDTH),
                  tile(2 * D_MODEL), per_b, _full(ng), _full(mlg), _full(wa), _full(wb), _full(wo), per_b, per_b,
                  _full(fg), _full(wr)],
        out_specs=[tile(d), tile(d), tile(LANES)],
        out_shape=[jax.ShapeDtypeStruct((bsz, seq, d), F32), jax.ShapeDtypeStruct((bsz, seq, d), BF16),
                   jax.ShapeDtypeStruct((bsz, seq, LANES), F32)],
        compiler_params=_params(("parallel", "parallel")),
        name="output_projection",
    )(h, gaf, gab, mlf, mlb, az, bo, mg, gt, ng, mlg, wa, wb, wo, sh, sc, fg, wr)


def _moe_kernel(npass_ref, acc_ref, hn_ref, route_ref, gt_ref, wg_ref, wu_ref, wd_ref, fin_ref, tri_ref,
                o_ref, y_ref, *, group, final_norm):
    g = group
    tile = pl.program_id(0) * pl.num_programs(1) + pl.program_id(1)
    tm = hn_ref.shape[1]
    cap = MOE_CAP
    sub = MOE_SUB
    n_sub = tm // sub
    route = route_ref[0]
    lane = _iota(route.shape, 1)
    lane_f = lane.astype(F32)
    column = lambda j: jnp.sum(jnp.where(lane == j, route, 0.0), axis=-1, keepdims=True)
    first = float(g * EXPERTS_PER_GROUP)
    l1 = column(R_E1) - first
    l2 = column(R_E2) - first
    w1 = column(R_W1)
    w2 = column(R_W2)
    in1 = (l1 >= 0.0) & (l1 < EXPERTS_PER_GROUP)
    in2 = (l2 >= 0.0) & (l2 < EXPERTS_PER_GROUP)
    hit1 = in1 & (lane_f == l1)
    hit2 = in2 & (lane_f == l2)
    onehot = (jnp.where(hit1, 1.0, 0.0) + jnp.where(hit2, 1.0, 0.0)).astype(BF16)
    rows = lambda s: slice(s * sub, (s + 1) * sub)
    before = jnp.concatenate([_dot(tri_ref[...], onehot[rows(s)]) for s in range(n_sub)], axis=0)
    rank1 = jnp.sum(jnp.where(hit1, before, 0.0), axis=-1, keepdims=True)
    rank2 = jnp.sum(jnp.where(hit2, before, 0.0), axis=-1, keepdims=True)
    slot_lane = _iota((tm, MOE_SLOTS), 1).astype(F32)
    x = hn_ref[0]
    y_ref[...] = jnp.zeros_like(y_ref)
    experts = range(EXPERTS_PER_GROUP)
    subs = range(n_sub)

    def one_pass(p, carry):
        lo = (p * cap).astype(F32)
        s1 = jnp.where(in1 & (rank1 >= lo) & (rank1 < lo + cap), l1 * cap + rank1 - lo, -1.0)
        s2 = jnp.where(in2 & (rank2 >= lo) & (rank2 < lo + cap), l2 * cap + rank2 - lo, -1.0)
        m1 = slot_lane == s1
        m2 = slot_lane == s2
        sel = jnp.where(m1 | m2, 1.0, 0.0).astype(BF16)
        sel_w = (jnp.where(m1, w1, 0.0) + jnp.where(m2, w2, 0.0)).astype(BF16)
        xs = [_dot(sel[rows(s)], x[rows(s)], _TN) for s in subs]
        xe = [jnp.concatenate([xs[s][e * cap:(e + 1) * cap] for s in subs], axis=0).astype(BF16) for e in experts]
        hg = [_dot(xe[e], wg_ref[e]) for e in experts]
        hu = [_dot(xe[e], wu_ref[e]) for e in experts]
        act = [(_silu(hg[e]) * hu[e]).astype(BF16) for e in experts]
        ye = [_dot(act[e], wd_ref[e]) for e in experts]
        pad = jnp.zeros((MOE_SLOTS - EXPERTS_PER_GROUP * cap, x.shape[1]), F32)
        for s in subs:
            ys = jnp.concatenate([ye[e][s * cap:(s + 1) * cap] for e in experts] + [pad], axis=0).astype(BF16)
            y_ref[rows(s), :] += _dot(sel_w[rows(s)], ys)
        return carry

    lax.fori_loop(0, npass_ref[tile], one_pass, 0)
    out = acc_ref[0] + gt_ref[0] * y_ref[...]
    if final_norm:
        ms = jnp.mean(out * out, axis=-1, keepdims=True)
        out = out * lax.rsqrt(ms + EPS) * fin_ref[...]
    o_ref[0] = out


def _moe(h, hn, route, gt, wg, wu, wd, fin, tm, final_norm):
    bsz, seq, d = h.shape
    nt = seq // tm
    n_sub = tm // MOE_SUB
    ids = route[..., (R_E1, R_E2)].astype(jnp.int32).reshape(bsz * nt, n_sub, MOE_SUB * 2)
    counts = jnp.sum(jax.nn.one_hot(ids, N_EXPERTS, dtype=jnp.int32), axis=2)
    most = jnp.max(counts.reshape(bsz * nt, n_sub, N_GROUPS, EXPERTS_PER_GROUP), axis=(1, 3))
    npass = ((most + MOE_CAP - 1) // MOE_CAP).T.astype(jnp.int32)
    tri = jnp.asarray(np.tril(np.ones((MOE_SUB, MOE_SUB), np.float32), -1)).astype(BF16)

    tile = lambda w: pl.BlockSpec((1, tm, w), lambda b, s, n: (b, s, 0))
    for g in range(N_GROUPS):
        group = lambda shape: pl.BlockSpec(shape, lambda b, s, n, g=g: (g, 0, 0))
        grid_spec = pltpu.PrefetchScalarGridSpec(
            num_scalar_prefetch=1,
            grid=(bsz, nt),
            in_specs=[tile(d), tile(d), tile(LANES), pl.BlockSpec((1, 1, d), lambda b, s, n: (b, 0, 0)),
                      group((EXPERTS_PER_GROUP, d, D_EXPERT)), group((EXPERTS_PER_GROUP, d, D_EXPERT)),
                      group((EXPERTS_PER_GROUP, D_EXPERT, d)),
                      pl.BlockSpec((1, d), lambda b, s, n: (0, 0)), pl.BlockSpec(tri.shape, lambda b, s, n: (0, 0))],
            out_specs=tile(d),
            scratch_shapes=[pltpu.VMEM((tm, d), F32)],
        )
        h = pl.pallas_call(
            functools.partial(_moe_kernel, group=g, final_norm=final_norm and g == N_GROUPS - 1),
            grid_spec=grid_spec,
            out_shape=jax.ShapeDtypeStruct((bsz, seq, d), F32),
            compiler_params=_params(("parallel", "parallel")),
            name="expert_ffn",
        )(npass[g], h, hn, route, gt, wg, wu, wd, fin, tri)
    return h


def _pick_tile(seq, want):
    tm = min(seq, want)
    assert seq % tm == 0 and tm % CHUNK == 0
    return tm


def kernel(x, c, ada_w, ada_b, norm_mix_g, norm_ffn_g, w_in, gdn_conv_w, gdn_a_log, gdn_dt_bias, gdn_norm_g, mlstm_conv_w, mlstm_i_bias, mlstm_f_bias, mlstm_norm_g, w_branch_a, w_branch_b, w_out, router_group, router_expert, w_gate, w_up, w_down, final_norm_g):
    bsz, seq, d = x.shape
    depth = ada_w.shape[0]
    assert d == D_MODEL and seq % CHUNK == 0
    tm_in = _pick_tile(seq, 512)
    tm_out = _pick_tile(seq, 512)
    tm_moe = _pick_tile(seq, 512)
    tl_gdn = _pick_tile(seq, LANES)
    nb_scan = LANES // CHUNK
    assert tl_gdn % LANES == 0 and bsz % nb_scan == 0
    row = lambda a: a.reshape(1, -1).astype(F32)
    consts = _scan_constants()

    mod = _modulation(c.astype(F32), ada_w.astype(F32), ada_b.astype(F32))
    h = x.astype(F32)
    for l in range(depth):
        sh1, sc1, gt1, sh2, sc2, gt2 = [mod[l, :, i * d:(i + 1) * d].reshape(bsz, 1, d) for i in range(N_MOD)]

        w = w_in[l].astype(F32)
        o_z = 3 * WIDTH
        o_ag = o_z + WIDTH
        o_bqk = o_ag + 4 * N_HEADS
        o_bv = o_bqk + 2 * WIDTH
        o_bo = o_bv + WIDTH
        o_bg = o_bo + WIDTH
        o_mg = o_bg + 4 * N_HEADS
        wconv = jnp.concatenate([w[:, :o_z], w[:, o_bqk:o_bv]], axis=1).astype(BF16)
        wrest = jnp.concatenate([w[:, o_z:o_ag], w[:, o_bv:o_bo], w[:, o_bo:o_bg], w[:, o_mg:]], axis=1).astype(BF16)
        wg = jnp.concatenate([w[:, o_ag:o_bqk], w[:, o_bg:o_mg], jnp.zeros((d, LANES - 8 * N_HEADS), F32)], axis=1)
        cw = jnp.concatenate([gdn_conv_w[l], mlstm_conv_w[l]], axis=1).astype(F32)
        cw = jnp.concatenate([cw, jnp.zeros((SUBLANES - CONV_K, CONV_W), F32)], axis=0)
        zero8 = jnp.zeros((N_HEADS,), F32)
        bias = jnp.concatenate([gdn_dt_bias[l, 0], zero8, gdn_dt_bias[l, 1], zero8,
                                mlstm_i_bias[l, 0], mlstm_f_bias[l, 0], mlstm_i_bias[l, 1], mlstm_f_bias[l, 1],
                                jnp.zeros((LANES - 8 * N_HEADS,), F32)]).astype(F32)
        alog = jnp.concatenate([gdn_a_log[l, 0], zero8, gdn_a_log[l, 1], zero8,
                                jnp.zeros((LANES - 4 * N_HEADS,), F32)]).astype(F32)
        gp = jnp.concatenate([bias[None], alog[None], jnp.zeros((SUBLANES - 2, LANES), F32)], axis=0)

        aq, ak, av, bq, bk, az, bv, bo, mg, gates, cumf, cumb = _input_projection(
            h, sh1, sc1, row(norm_mix_g[l]), wconv, wrest, wg, cw, gp, tm_in)
        nb_state = 4 if bsz % 4 == 0 else nb_scan
        gaf, gab = _gdn_state(_gdn_local(aq, ak, av, gates, cumf, cumb, consts, tl_gdn), consts, nb_state)
        mlf, mlb = _mlstm_scan(bq, bk, bv, gates, cumf, cumb, consts, nb_scan)

        wr = jnp.concatenate([router_expert[l].astype(F32), router_group[l].astype(F32),
                              jnp.zeros((d, LANES - N_EXPERTS - N_GROUPS), F32)], axis=1)
        h, hn2, comb = _output_projection(
            h, gaf, gab, mlf, mlb, az, bo, mg, gt1, row(jnp.tile(gdn_norm_g[l], N_HEADS)), row(mlstm_norm_g[l]),
            w_branch_a[l].astype(BF16), w_branch_b[l].astype(BF16), w_out[l].astype(BF16),
            sh2, sc2, row(norm_ffn_g[l]), wr, tm_out)
        h = _moe(h, hn2, comb, gt2, w_gate[l].astype(BF16), w_up[l].astype(BF16), w_down[l].astype(BF16),
                 row(final_norm_g), tm_moe, final_norm=(l == depth - 1))
    return h.astype(x.dtype)
```

```python
import functools

import numpy as np
import jax
import jax.numpy as jnp
from jax import lax
from jax.experimental import pallas as pl
from jax.experimental.pallas import tpu as pltpu

D_MODEL = 1024
HEAD_DIM = 64
N_HEADS = 8
WIDTH = N_HEADS * HEAD_DIM
CONV_K = 5
CONV_W = 3 * WIDTH + 2 * WIDTH
REST_W = 3 * WIDTH + 2 * D_MODEL
N_GROUPS = 4
EXPERTS_PER_GROUP = 8
N_EXPERTS = N_GROUPS * EXPERTS_PER_GROUP
D_EXPERT = D_MODEL // 4
N_MOD = 6
EPS = 1e-6
NEG = -1e30

CHUNK = 64
HEADS_PER_GROUP = 2
N_HEAD_GROUPS = N_HEADS // HEADS_PER_GROUP
GROUP_W = HEADS_PER_GROUP * HEAD_DIM
LEVELS = 6
LANES = 128
SUBLANES = 8
BF16_ROWS = 16
VMEM_LIMIT = 56 * 1024 * 1024

F32 = jnp.float32
BF16 = jnp.bfloat16

_NN = (((1,), (0,)), ((), ()))
_NT = (((1,), (1,)), ((), ()))
_TN = (((0,), (0,)), ((), ()))

R_E1, R_E2, R_W1, R_W2 = 0, 1, 4, 5
MOE_SUB = 128
MOE_CAP = 24
MOE_SLOTS = 256

M_INCL, M_STRICT, M_LEVEL0, M_DIAG = 0, 1, 2, 2 + LEVELS
N_MASKS = M_DIAG + 1


def _dot(a, b, dims=_NN):
    return lax.dot_general(a, b, dims, preferred_element_type=F32)


def _mm(a, b, dims=_NN):
    return _dot(a.astype(BF16), b.astype(BF16), dims)


def _split2(x):
    hi = x.astype(BF16)
    lo = (x - hi.astype(F32)).astype(BF16)
    return hi, lo


def _split3(x):
    hi = x.astype(BF16)
    r = x - hi.astype(F32)
    mid = r.astype(BF16)
    lo = (r - mid.astype(F32)).astype(BF16)
    return hi, mid, lo


def _dot_pieces(pieces, e):
    out = _dot(pieces[0], e)
    for p in pieces[1:]:
        out = out + _dot(p, e)
    return out


def _mm_lhs3(x, e):
    return _dot_pieces(_split3(x), e)


def _mm_rhs3(e, x):
    hi, mid, lo = _split3(x)
    return _dot(e, hi) + _dot(e, mid) + _dot(e, lo)


def _mm_lhs2(x, e):
    return _dot_pieces(_split2(x), e)


def _mm3(a, b):
    a_hi, a_lo = _split2(a)
    b_hi, b_lo = _split2(b)
    return _dot(a_hi, b_hi) + _dot(a_hi, b_lo) + _dot(a_lo, b_hi)


def _iota(shape, dim):
    return lax.broadcasted_iota(jnp.int32, shape, dim)


def _sigmoid(x):
    return 0.5 * jnp.tanh(0.5 * x) + 0.5


def _silu(x):
    return x * _sigmoid(x)


def _softplus(x):
    return jnp.maximum(x, 0.0) + jnp.log1p(jnp.exp(-jnp.abs(x)))


def _head_ones(width):
    r = _iota((width, width), 0) >> 6
    c = _iota((width, width), 1) >> 6
    return jnp.where(r == c, 1.0, 0.0).astype(BF16)


def _params(sem):
    return pltpu.CompilerParams(dimension_semantics=sem, vmem_limit_bytes=VMEM_LIMIT)


def _full(a):
    return pl.BlockSpec(a.shape, lambda *_: (0,) * a.ndim)


def _mod_kernel(c_ref, w_ref, b_ref, o_ref):
    o_ref[0] = _mm3(_silu(c_ref[...]), w_ref[0]) + b_ref[0]


def _modulation(c, ada_w, ada_b):
    depth, d, n = ada_w.shape
    bsz = c.shape[0]
    tn = n // 4
    return pl.pallas_call(
        _mod_kernel,
        grid=(depth, n // tn),
        in_specs=[
            pl.BlockSpec((bsz, d), lambda l, j: (0, 0)),
            pl.BlockSpec((1, d, tn), lambda l, j: (l, 0, j)),
            pl.BlockSpec((1, 1, tn), lambda l, j: (l, 0, j)),
        ],
        out_specs=pl.BlockSpec((1, bsz, tn), lambda l, j: (l, 0, j)),
        out_shape=jax.ShapeDtypeStruct((depth, bsz, n), F32),
        compiler_params=_params(("parallel", "parallel")),
        name="adaln_mod",
    )(c, ada_w, ada_b.reshape(depth, 1, n))


HALO = BF16_ROWS


def _inproj_kernel(h_ref, hprev_ref, hnext_ref, sh_ref, sc_ref, g_ref, wconv_ref, wrest_ref, wg_ref, cw_ref, gp_ref,
                   perm_ref, permt_ref, aq_ref, ak_ref, av_ref, bq_ref, bk_ref, az_ref, bv_ref, bo_ref, mg_ref, gates_ref, cumf_ref,
                   cumb_ref, p_ref):
    s = pl.program_id(1)
    ns = pl.num_programs(1)
    tm = h_ref.shape[1]
    gain = g_ref[...] * (1.0 + sc_ref[0])
    shift = sh_ref[0]

    def norm(x):
        ms = jnp.mean(x * x, axis=-1, keepdims=True)
        return x * lax.rsqrt(ms + EPS) * gain + shift

    hn = norm(h_ref[0])
    hp = norm(hprev_ref[0]) * jnp.where(s > 0, 1.0, 0.0)
    hx = norm(hnext_ref[0]) * jnp.where(s < ns - 1, 1.0, 0.0)
    hn_hi = hn.astype(BF16)

    nv = tm // SUBLANES
    edge = CONV_K // 2
    hn_perm = _dot(perm_ref[...], hn_hi).astype(BF16)
    pe = _dot(jnp.concatenate([hn_perm, hp.astype(BF16), hx.astype(BF16)], axis=0), wconv_ref[...])
    rest = _dot(hn_hi, wrest_ref[...])
    hn_lo = (hn - hn_hi.astype(F32)).astype(BF16)
    wg_hi, wg_lo = _split2(wg_ref[...])
    pre = _dot(hn_hi, wg_hi) + _dot(hn_hi, wg_lo) + _dot(hn_lo, wg_hi)
    prev = pe[tm:tm + HALO]
    nxt = pe[tm + HALO:]
    sub = _iota((SUBLANES, CONV_W), 0)
    block = lambda v: pe[v * SUBLANES:(v + 1) * SUBLANES]
    for i in range(edge):
        before = jnp.where(sub == 0, prev[HALO - edge + i:HALO - edge + i + 1], pltpu.roll(block(nv - edge + i), 1, 0))
        p_ref[i * SUBLANES:(i + 1) * SUBLANES, :] = before
        after = jnp.where(sub == SUBLANES - 1, nxt[i:i + 1], pltpu.roll(block(i), SUBLANES - 1, 0))
        p_ref[(edge + nv + i) * SUBLANES:(edge + nv + i + 1) * SUBLANES, :] = after
    p_ref[edge * SUBLANES:(edge + nv) * SUBLANES, :] = pe[:tm]

    conv = jnp.zeros((tm, CONV_W), F32)
    for j in range(CONV_K):
        conv = conv + p_ref[j * SUBLANES:j * SUBLANES + tm, :] * cw_ref[j:j + 1, :]
    cv = _silu(conv)

    ones_bd = _head_ones(WIDTH)

    def l2n(t):
        ss = _mm_lhs2(t * t, ones_bd)
        return t * lax.rsqrt(ss + EPS)

    permuted = jnp.concatenate([
        (l2n(cv[:, 0:WIDTH]) * (HEAD_DIM ** -0.5)).astype(BF16),
        l2n(cv[:, WIDTH:2 * WIDTH]).astype(BF16),
        cv[:, 2 * WIDTH:4 * WIDTH].astype(BF16),
        (cv[:, 4 * WIDTH:5 * WIDTH] * (HEAD_DIM ** -0.5)).astype(BF16)], axis=1)
    natural = _dot(permt_ref[...], permuted).astype(BF16)
    for i, ref in enumerate((aq_ref, ak_ref, av_ref, bq_ref, bk_ref)):
        ref[0] = natural[:, i * WIDTH:(i + 1) * WIDTH]

    az_ref[0] = rest[:, 0:WIDTH].astype(BF16)
    bv_ref[0] = rest[:, WIDTH:2 * WIDTH].astype(BF16)
    bo_ref[0] = rest[:, 2 * WIDTH:3 * WIDTH].astype(BF16)
    mg_ref[0] = rest[:, 3 * WIDTH:].astype(BF16)

    pre = pre + gp_ref[0:1, :]
    lane = _iota(pre.shape, 1)
    second = ((lane >> 3) & 1) == 1
    is_gdn = lane < 4 * N_HEADS
    is_ml = (lane >= 4 * N_HEADS) & (lane < 8 * N_HEADS)
    sp = _softplus(pre)
    gdn_val = jnp.where(second, _sigmoid(pre), -jnp.exp(gp_ref[1:2, :]) * sp)
    ml_val = jnp.where(second, -_softplus(-pre), pre)
    gates = jnp.where(is_gdn, gdn_val, jnp.where(is_ml, ml_val, 0.0))
    gates_ref[0] = gates

    ri = _iota((tm, tm), 0)
    ci = _iota((tm, tm), 1)
    same_chunk = (ri >> 6) == (ci >> 6)
    pieces = _split3(gates)
    for ref, tri in ((cumf_ref, same_chunk & (ri >= ci)), (cumb_ref, same_chunk & (ri <= ci))):
        tri_b = jnp.where(tri, 1.0, 0.0).astype(BF16)
        ref[0] = _dot(tri_b, pieces[0]) + _dot(tri_b, pieces[1]) + _dot(tri_b, pieces[2])


def _input_projection(h, sh, sc, gain, wconv, wrest, wg, cw, gp, tm):
    bsz, seq, d = h.shape
    nt = seq // tm
    per = tm // HALO
    nhalo = seq // HALO
    tile = lambda w: pl.BlockSpec((1, tm, w), lambda b, s: (b, s, 0))
    outs = [(WIDTH, BF16)] * 8 + [(2 * D_MODEL, BF16)] + [(LANES, F32)] * 3
    r = np.arange(tm)
    perm_np = np.zeros((tm, tm), np.float32)
    perm_np[r, (r % SUBLANES) * (tm // SUBLANES) + r // SUBLANES] = 1.0
    perm = jnp.asarray(perm_np).astype(BF16)
    perm_t = jnp.asarray(perm_np.T).astype(BF16)
    return pl.pallas_call(
        _inproj_kernel,
        grid=(bsz, nt),
        in_specs=[
            tile(d),
            pl.BlockSpec((1, HALO, d), lambda b, s: (b, jnp.maximum(s * per - 1, 0), 0)),
            pl.BlockSpec((1, HALO, d), lambda b, s: (b, jnp.minimum((s + 1) * per, nhalo - 1), 0)),
            pl.BlockSpec((1, 1, d), lambda b, s: (b, 0, 0)),
            pl.BlockSpec((1, 1, d), lambda b, s: (b, 0, 0)),
            _full(gain), _full(wconv), _full(wrest), _full(wg), _full(cw), _full(gp), _full(perm), _full(perm_t),
        ],
        out_specs=[tile(w) for w, _ in outs],
        out_shape=[jax.ShapeDtypeStruct((bsz, seq, w), dt) for w, dt in outs],
        scratch_shapes=[pltpu.VMEM((tm + 2 * HALO, CONV_W), F32)],
        compiler_params=_params(("parallel", "parallel")),
        name="input_projection",
    )(h, h, h, sh, sc, gain, wconv, wrest, wg, cw, gp, perm, perm_t)


def _gate_column(branch, direction, kind):
    return branch * 4 * N_HEADS + direction * 2 * N_HEADS + kind * N_HEADS


def _expand(pieces, e):
    rows = pieces[0].shape[0]
    out = _dot(jnp.concatenate(pieces, axis=0), e)
    return functools.reduce(lambda a, b: a + b, [out[i * rows:(i + 1) * rows] for i in range(len(pieces))])


def _head_rows(x_t, column, hg, block):
    first = column + hg * HEADS_PER_GROUP
    pieces = [x_t[first + h:first + h + 1, block * CHUNK:(block + 1) * CHUNK] for h in range(HEADS_PER_GROUP)]
    return jnp.concatenate(pieces, axis=1)


def _scan_constants():
    c, w = CHUNK, GROUP_W
    row = np.arange(c)[:, None]
    col = (np.arange(w) % c)[None, :]
    masks = np.zeros((2, N_MASKS, c, w), np.float32)
    for d in range(2):
        masks[d, M_INCL] = (row >= col) if d == 0 else (row <= col)
        masks[d, M_STRICT] = (row > col) if d == 0 else (row < col)
        for p in range(LEVELS):
            same = (row >> (p + 1)) == (col >> (p + 1))
            r_bit = (row >> p) & 1
            c_bit = (col >> p) & 1
            masks[d, M_LEVEL0 + p] = same & ((r_bit == 1) & (c_bit == 0) if d == 0 else (r_bit == 0) & (c_bit == 1))
        masks[d, M_DIAG] = row == col
    bd = (np.arange(w)[:, None] // HEAD_DIM) == (np.arange(w)[None, :] // HEAD_DIM)
    exp = np.zeros((2, 2, 2, LANES, WIDTH), np.float32)
    lane_head = np.arange(WIDTH) // HEAD_DIM
    for br in range(2):
        for d in range(2):
            for kind in range(2):
                exp[br, d, kind, _gate_column(br, d, kind) + lane_head, np.arange(WIDTH)] = 1.0
    return (jnp.asarray(masks), jnp.asarray(bd.astype(np.float32)),
            jnp.asarray(bd.astype(np.float32)).astype(BF16), jnp.asarray(exp).astype(BF16))


def _block_diag(x_cat, bd_bf16):
    xb = x_cat.astype(BF16)
    return jnp.concatenate([xb] * HEADS_PER_GROUP, axis=0) * bd_bf16


def _gdn_local_kernel(q_ref, k_ref, v_ref, g_ref, cumf_ref, cumb_ref, masks_ref, bdb_ref, exp_ref,
                      uf_ref, wf_ref, qef_ref, inf_ref, kef_ref, cdf_ref,
                      ub_ref, wb_ref, qeb_ref, inb_ref, keb_ref, cdb_ref):
    c, w = CHUNK, GROUP_W
    bd_b = bdb_ref[...]
    outs = ((uf_ref, wf_ref, qef_ref, inf_ref, kef_ref, cdf_ref), (ub_ref, wb_ref, qeb_ref, inb_ref, keb_ref, cdb_ref))
    chains = [(ci, d, hg) for ci in range(q_ref.shape[1] // c) for d in range(2) for hg in range(N_HEAD_GROUPS)]
    rows = lambda ci: slice(ci * c, (ci + 1) * c)
    lanes = lambda hg: slice(hg * w, (hg + 1) * w)

    cums = (cumf_ref[0], cumb_ref[0])
    g_pieces = _split2(g_ref[0])
    gc_all = [_expand(_split3(cums[d]), exp_ref[0, d, 0]) for d in range(2)]
    beta_all = [_expand(g_pieces, exp_ref[0, d, 1]) for d in range(2)]
    per_block = LANES // c
    n_blocks = q_ref.shape[1] // LANES
    cum_t = [[cums[d][blk * LANES:(blk + 1) * LANES].T for blk in range(n_blocks)] for d in range(2)]
    gc = [gc_all[d][rows(ci), lanes(hg)] for ci, d, hg in chains]
    beta = [beta_all[d][rows(ci), lanes(hg)] for ci, d, hg in chains]
    gc_t = [_head_rows(cum_t[d][ci // per_block], _gate_column(0, d, 0), hg, ci % per_block) for ci, d, hg in chains]

    qk_kk = {}
    for ci, d, hg in chains:
        if d == 0:
            q = q_ref[0, rows(ci), lanes(hg)]
            k = k_ref[0, rows(ci), lanes(hg)]
            k_bd = jnp.concatenate([k] * HEADS_PER_GROUP, axis=0) * bd_b
            qk_kk[ci, hg] = _dot(jnp.concatenate([q, k], axis=0), k_bd, _NT)
    kbeta = [k_ref[0, rows(ci), lanes(hg)].astype(F32) * beta[i] for i, (ci, d, hg) in enumerate(chains)]

    a, t = [], []
    for i, (ci, d, hg) in enumerate(chains):
        decay = jnp.exp(jnp.minimum(gc[i] - gc_t[i], 0.0)) * masks_ref[d, M_INCL]
        outs[d][3][0, rows(ci), lanes(hg)] = (qk_kk[ci, hg][:c] * decay).astype(BF16)
        a.append(qk_kk[ci, hg][c:] * beta[i] * decay * masks_ref[d, M_STRICT])
        t.append(masks_ref[d, M_DIAG] - a[i] * masks_ref[d, M_LEVEL0])

    for p in range(1, LEVELS):
        y = [_mm(t[i], _block_diag(a[i] * masks_ref[d, M_LEVEL0 + p], bd_b)) for i, (_, d, _) in enumerate(chains)]
        t = [t[i] - _mm(y[i], _block_diag(t[i], bd_b)) for i in range(len(chains))]

    for i, (ci, d, hg) in enumerate(chains):
        u_ref, w_ref, qe_ref, _, ke_ref, cd_ref = outs[d]
        last = c - 1 if d == 0 else 0
        g_last = gc[i][last:last + 1, :]
        e_gc = jnp.exp(gc[i])
        v = v_ref[0, rows(ci), lanes(hg)].astype(F32)
        u_ref[0, rows(ci), lanes(hg)] = _mm(t[i], _block_diag(v * beta[i], bd_b))
        w_ref[0, rows(ci), lanes(hg)] = _mm(t[i], _block_diag(kbeta[i] * e_gc, bd_b)).astype(BF16)
        qe_ref[0, rows(ci), lanes(hg)] = (q_ref[0, rows(ci), lanes(hg)].astype(F32) * e_gc).astype(BF16)
        k = k_ref[0, rows(ci), lanes(hg)].astype(F32)
        ke_ref[0, rows(ci), lanes(hg)] = (k * jnp.exp(g_last - gc[i])).astype(BF16)
        cd_ref[0, ci, :, lanes(hg)] = jnp.exp(g_last)


def _gdn_local(q, k, v, gates, cumf, cumb, consts, tl):
    bsz, seq, width = q.shape
    masks, _, bd_b, exp = consts
    tile = lambda wd: pl.BlockSpec((1, tl, wd), lambda b, s: (b, s, 0))
    cd_spec = pl.BlockSpec((1, tl // CHUNK, 1, width), lambda b, s: (b, s, 0, 0))
    big = lambda dt: jax.ShapeDtypeStruct((bsz, seq, width), dt)
    cd_shape = jax.ShapeDtypeStruct((bsz, seq // CHUNK, 1, width), F32)
    per_dir_specs = [tile(width)] * 5 + [cd_spec]
    per_dir_shapes = [big(F32), big(BF16), big(BF16), big(BF16), big(BF16), cd_shape]
    return pl.pallas_call(
        _gdn_local_kernel,
        grid=(bsz, seq // tl),
        in_specs=[tile(width), tile(width), tile(width), tile(LANES), tile(LANES), tile(LANES),
                  _full(masks), _full(bd_b), _full(exp)],
        out_specs=per_dir_specs * 2,
        out_shape=per_dir_shapes * 2,
        compiler_params=_params(("parallel", "parallel")),
        name="gdn_local",
    )(q, k, v, gates, cumf, cumb, masks, bd_b, exp)


def _gdn_state_kernel(uf_ref, wf_ref, qef_ref, inf_ref, kef_ref, cdf_ref,
                      ub_ref, wb_ref, qeb_ref, inb_ref, keb_ref, cdb_ref, bdf_ref, bdb_ref, of_ref, ob_ref, s_ref):
    n = pl.program_id(1)

    @pl.when(n == 0)
    def _():
        s_ref[...] = jnp.zeros_like(s_ref)

    c, w = CHUNK, GROUP_W
    bd_b = bdb_ref[...]
    dirs = ((uf_ref, wf_ref, qef_ref, inf_ref, kef_ref, cdf_ref, of_ref),
            (ub_ref, wb_ref, qeb_ref, inb_ref, keb_ref, cdb_ref, ob_ref))
    nb = uf_ref.shape[0]
    chains = [(bi, d, hg) for bi in range(nb) for d in range(2) for hg in range(N_HEAD_GROUPS)]
    lanes = lambda hg: slice(hg * w, (hg + 1) * w)
    ws_qs = []
    for idx, (bi, d, hg) in enumerate(chains):
        w_ref, qe_ref = dirs[d][1], dirs[d][2]
        lhs = jnp.concatenate([w_ref[bi, :, lanes(hg)], qe_ref[bi, :, lanes(hg)]], axis=0)
        ws_qs.append(_dot(lhs, s_ref[idx].astype(BF16)))
    v_new = [(dirs[d][0][bi, :, lanes(hg)] - ws_qs[idx][:c]).astype(BF16) for idx, (bi, d, hg) in enumerate(chains)]
    for idx, (bi, d, hg) in enumerate(chains):
        in_ref, o_ref = dirs[d][3], dirs[d][6]
        o = ws_qs[idx][c:] + _dot(in_ref[bi, :, lanes(hg)], _block_diag(v_new[idx], bd_b))
        o_ref[bi, :, lanes(hg)] = o.astype(BF16)
    for idx, (bi, d, hg) in enumerate(chains):
        ke_ref, cd_ref = dirs[d][4], dirs[d][5]
        ds = _dot(ke_ref[bi, :, lanes(hg)], v_new[idx], _TN)
        s_ref[idx] = s_ref[idx] * cd_ref[bi, 0, :, lanes(hg)] + ds * bdf_ref[...]


def _gdn_state(local_outs, consts, nb):
    uf = local_outs[0]
    bsz, seq, width = uf.shape
    nc = seq // CHUNK
    _, bd_f, bd_b, _ = consts
    fwd = pl.BlockSpec((nb, CHUNK, width), lambda b, n: (b, n, 0))
    bwd = pl.BlockSpec((nb, CHUNK, width), lambda b, n: (b, nc - 1 - n, 0))
    cd_fwd = pl.BlockSpec((nb, 1, 1, width), lambda b, n: (b, n, 0, 0))
    cd_bwd = pl.BlockSpec((nb, 1, 1, width), lambda b, n: (b, nc - 1 - n, 0, 0))
    return pl.pallas_call(
        _gdn_state_kernel,
        grid=(bsz // nb, nc),
        in_specs=[fwd] * 5 + [cd_fwd] + [bwd] * 5 + [cd_bwd] + [_full(bd_f), _full(bd_b)],
        out_specs=[fwd, bwd],
        out_shape=[jax.ShapeDtypeStruct((bsz, seq, width), BF16)] * 2,
        scratch_shapes=[pltpu.VMEM((nb * 2 * N_HEAD_GROUPS, GROUP_W, GROUP_W), F32)],
        compiler_params=_params(("parallel", "arbitrary")),
        name="gdn_state",
    )(*local_outs, bd_f, bd_b)


def _mlstm_kernel(qf_ref, kf_ref, vf_ref, gf_ref, cf_ref, qb_ref, kb_ref, vb_ref, gb_ref, cb_ref,
                  masks_ref, bdf_ref, bdb_ref, exp_ref, of_ref, ob_ref, c_ref, n_ref, m_ref):
    n = pl.program_id(1)

    @pl.when(n == 0)
    def _():
        c_ref[...] = jnp.zeros_like(c_ref)
        n_ref[...] = jnp.zeros_like(n_ref)
        m_ref[...] = jnp.zeros_like(m_ref)

    c, w = CHUNK, GROUP_W
    bd_f = bdf_ref[...]
    bd_b = bdb_ref[...]
    ones_rows = jnp.ones((SUBLANES, c), BF16)
    row = _iota((c, w), 0)
    dirs = ((qf_ref, kf_ref, vf_ref, gf_ref, cf_ref, of_ref), (qb_ref, kb_ref, vb_ref, gb_ref, cb_ref, ob_ref))
    nb = qf_ref.shape[0]
    chains = [(bi, d, hg) for bi in range(nb) for d in range(2) for hg in range(N_HEAD_GROUPS)]
    lanes = lambda hg: slice(hg * w, (hg + 1) * w)
    gates = [jnp.concatenate([dirs[d][3][bi] for bi in range(nb)], axis=0) for d in range(2)]
    cums = [jnp.concatenate([dirs[d][4][bi] for bi in range(nb)], axis=0) for d in range(2)]
    i_all = [_expand(_split3(gates[d]), exp_ref[1, d, 0]) for d in range(2)]
    bcum_all = [_expand(_split3(cums[d]), exp_ref[1, d, 1]) for d in range(2)]
    gates_t = [gates[d].T for d in range(2)]
    cums_t = [cums[d].T for d in range(2)]
    rows = lambda bi: slice(bi * c, (bi + 1) * c)
    bcum = [bcum_all[d][rows(bi), lanes(hg)] for bi, d, hg in chains]
    r = [i_all[d][rows(bi), lanes(hg)] - bcum[i] for i, (bi, d, hg) in enumerate(chains)]
    r_t = [_head_rows(gates_t[d], _gate_column(1, d, 0), hg, bi) - _head_rows(cums_t[d], _gate_column(1, d, 1), hg, bi)
           for bi, d, hg in chains]
    qk = []
    for bi, d, hg in chains:
        k_bd = jnp.concatenate([dirs[d][1][bi, :, lanes(hg)]] * HEADS_PER_GROUP, axis=0) * bd_b
        qk.append(_dot(dirs[d][0][bi, :, lanes(hg)], k_bd, _NT))

    pm, mx, inter_w = [], [], []
    for i, (bi, d, hg) in enumerate(chains):
        m = r[i]
        for p in range(LEVELS):
            sh = 1 << p
            if d == 0:
                m = jnp.maximum(m, jnp.where(row >= sh, pltpu.roll(m, sh, 0), NEG))
            else:
                m = jnp.maximum(m, jnp.where(row < c - sh, pltpu.roll(m, c - sh, 0), NEG))
        pm.append(m)
        m_prev = m_ref[i, 0:1, :]
        mx.append(jnp.maximum(m_prev, m))
        inter_w.append(jnp.exp(m_prev - mx[i]))
        qk[i] = qk[i] * jnp.exp(jnp.where(masks_ref[d, M_INCL] > 0.0, r_t[i] - mx[i], NEG))

    num = [_mm(qk[i], _block_diag(dirs[d][2][bi, :, lanes(hg)], bd_b)) for i, (bi, d, hg) in enumerate(chains)]
    den = [_mm_lhs2(qk[i], bd_b) for i in range(len(chains))]
    qc = [_mm(dirs[d][0][bi, :, lanes(hg)], c_ref[i]) for i, (bi, d, hg) in enumerate(chains)]
    qn = [_mm(dirs[d][0][bi, :, lanes(hg)].astype(F32) * n_ref[i, 0:1, :], bd_b)
          for i, (bi, d, hg) in enumerate(chains)]
    for i, (bi, d, hg) in enumerate(chains):
        full_num = num[i] + inter_w[i] * qc[i]
        full_den = den[i] + inter_w[i] * qn[i]
        hidden = full_num / jnp.maximum(jnp.abs(full_den), jnp.exp(-(bcum[i] + mx[i])))
        dirs[d][5][bi, :, lanes(hg)] = hidden.astype(BF16)

    for i, (bi, d, hg) in enumerate(chains):
        last = c - 1 if d == 0 else 0
        m_prev = m_ref[i, 0:1, :]
        b_last = bcum[i][last:last + 1, :]
        m_next = b_last + jnp.maximum(m_prev, pm[i][last:last + 1, :])
        scale_prev = jnp.exp(b_last + m_prev - m_next)
        wgt = jnp.exp(b_last + r[i] - m_next)
        kw = (dirs[d][1][bi, :, lanes(hg)].astype(F32) * wgt).astype(BF16)
        c_ref[i] = c_ref[i] * scale_prev + _dot(kw, dirs[d][2][bi, :, lanes(hg)], _TN) * bd_f
        n_ref[i, 0:1, :] = n_ref[i, 0:1, :] * scale_prev + _dot(ones_rows, kw)[0:1, :]
        m_ref[i, 0:1, :] = m_next


def _mlstm_scan(q, k, v, gates, cumf, cumb, consts, nb):
    bsz, seq, width = q.shape
    nc = seq // CHUNK
    masks, bd_f, bd_b, exp = consts
    fwd = lambda wd: pl.BlockSpec((nb, CHUNK, wd), lambda b, n: (b, n, 0))
    bwd = lambda wd: pl.BlockSpec((nb, CHUNK, wd), lambda b, n: (b, nc - 1 - n, 0))
    n_state = nb * 2 * N_HEAD_GROUPS
    state = pltpu.VMEM((n_state, GROUP_W, GROUP_W), F32)
    row_state = pltpu.VMEM((n_state, SUBLANES, GROUP_W), F32)
    return pl.pallas_call(
        _mlstm_kernel,
        grid=(bsz // nb, nc),
        in_specs=[fwd(width), fwd(width), fwd(width), fwd(LANES), fwd(LANES),
                  bwd(width), bwd(width), bwd(width), bwd(LANES), bwd(LANES),
                  _full(masks), _full(bd_f), _full(bd_b), _full(exp)],
        out_specs=[fwd(width), bwd(width)],
        out_shape=[jax.ShapeDtypeStruct((bsz, seq, width), BF16)] * 2,
        scratch_shapes=[state, row_state, row_state],
        compiler_params=_params(("parallel", "arbitrary")),
        name="mlstm_scan",
    )(q, k, v, gates, cumf, q, k, v, gates, cumb, masks, bd_f, bd_b, exp)


def _outproj_kernel(h_ref, gaf_ref, gab_ref, mlf_ref, mlb_ref, az_ref, bo_ref, mg_ref, gt_ref, ng_ref, mlg_ref,
                    wa_ref, wb_ref, wo_ref, sh_ref, sc_ref, fg_ref, wr_ref, hout_ref, hn_ref, comb_ref):
    ones_bd = _head_ones(WIDTH)
    inv = 1.0 / HEAD_DIM
    tm = h_ref.shape[1]
    n_parts = 2 if tm % (2 * BF16_ROWS) == 0 else 1
    parts = [slice(i * tm // n_parts, (i + 1) * tm // n_parts) for i in range(n_parts)]
    f32 = lambda ref, rs: ref[0, rs, :].astype(F32)

    def branch_inputs(rs):
        oa = f32(gaf_ref, rs) + f32(gab_ref, rs)
        ms = _mm_lhs2(oa * oa, ones_bd) * inv
        oa = oa * lax.rsqrt(ms + EPS) * ng_ref[...] * _silu(f32(az_ref, rs))
        hb = f32(mlf_ref, rs) + f32(mlb_ref, rs)
        mu = _mm_lhs3(hb, ones_bd) * inv
        tc = hb - mu
        var = _mm_lhs2(tc * tc, ones_bd) * inv
        hb = tc * lax.rsqrt(var + EPS) * mlg_ref[...] * _sigmoid(f32(bo_ref, rs))
        return oa.astype(BF16), hb.astype(BF16)

    def merged(rs, oa, hb):
        gate = _sigmoid(f32(mg_ref, rs))
        return gate[:, :D_MODEL] * _dot(oa, wa_ref[...]) + gate[:, D_MODEL:] * _dot(hb, wb_ref[...])

    def residual(rs, y):
        h_new = h_ref[0, rs, :] + gt_ref[0] * _mm(y, wo_ref[...])
        hout_ref[0, rs, :] = h_new
        ms2 = jnp.mean(h_new * h_new, axis=-1, keepdims=True)
        hn = h_new * lax.rsqrt(ms2 + EPS) * (fg_ref[...] * (1.0 + sc_ref[0])) + sh_ref[0]
        hn_ref[0, rs, :] = hn.astype(BF16)
        return hn

    def routing(rs, logits):
        lane = _iota(logits.shape, 1)
        lane_f = lane.astype(F32)
        big = 1e9
        gl = jnp.where((lane >= N_EXPERTS) & (lane < N_EXPERTS + N_GROUPS), logits, NEG)
        gmax = jnp.max(gl, axis=-1, keepdims=True)
        gidx = jnp.min(jnp.where(gl == gmax, lane_f - N_EXPERTS, big), axis=-1, keepdims=True)
        p_group = 1.0 / jnp.sum(jnp.exp(gl - gmax), axis=-1, keepdims=True)
        el = jnp.where((lane < N_EXPERTS) & ((lane >> 3).astype(F32) == gidx), logits, NEG)
        v1 = jnp.max(el, axis=-1, keepdims=True)
        i1 = jnp.min(jnp.where(el == v1, lane_f, big), axis=-1, keepdims=True)
        el2 = jnp.where(lane_f == i1, NEG, el)
        v2 = jnp.max(el2, axis=-1, keepdims=True)
        i2 = jnp.min(jnp.where(el2 == v2, lane_f, big), axis=-1, keepdims=True)
        e21 = jnp.exp(v2 - v1)
        w1 = p_group / (1.0 + e21)
        route = jnp.where(lane == R_E1, i1, 0.0) + jnp.where(lane == R_E2, i2, 0.0)
        comb_ref[0, rs, :] = route + jnp.where(lane == R_W1, w1, 0.0) + jnp.where(lane == R_W2, w1 * e21, 0.0)

    branches = [branch_inputs(rs) for rs in parts]
    ys = [merged(rs, *branches[i]) for i, rs in enumerate(parts)]
    hns = [residual(rs, ys[i]) for i, rs in enumerate(parts)]
    logits = [_mm3(hn, wr_ref[...]) for hn in hns]
    for i, rs in enumerate(parts):
        routing(rs, logits[i])


def _output_projection(h, gaf, gab, mlf, mlb, az, bo, mg, gt, ng, mlg, wa, wb, wo, sh, sc, fg, wr, tm):
    bsz, seq, d = h.shape
    tile = lambda w: pl.BlockSpec((1, tm, w), lambda b, s: (b, s, 0))
    per_b = pl.BlockSpec((1, 1, d), lambda b, s: (b, 0, 0))
    return pl.pallas_call(
        _outproj_kernel,
        grid=(bsz, seq // tm),
        in_specs=[tile(d), tile(WIDTH), tile(WIDTH), tile(WIDTH), tile(WIDTH), tile(WIDTH), tile(WIDTH),
                  tile(2 * D_MODEL), per_b, _full(ng), _full(mlg), _full(wa), _full(wb), _full(wo), per_b, per_b,
                  _full(fg), _full(wr)],
        out_specs=[tile(d), tile(d), tile(LANES)],
        out_shape=[jax.ShapeDtypeStruct((bsz, seq, d), F32), jax.ShapeDtypeStruct((bsz, seq, d), BF16),
                   jax.ShapeDtypeStruct((bsz, seq, LANES), F32)],
        compiler_params=_params(("parallel", "parallel")),
        name="output_projection",
    )(h, gaf, gab, mlf, mlb, az, bo, mg, gt, ng, mlg, wa, wb, wo, sh, sc, fg, wr)


def _moe_kernel(npass_ref, acc_ref, hn_ref, route_ref, gt_ref, wg_ref, wu_ref, wd_ref, fin_ref, tri_ref,
                o_ref, y_ref, *, group, final_norm):
    g = group
    tile = pl.program_id(0) * pl.num_programs(1) + pl.program_id(1)
    tm = hn_ref.shape[1]
    cap = MOE_CAP
    sub = MOE_SUB
    n_sub = tm // sub
    route = route_ref[0]
    lane = _iota(route.shape, 1)
    lane_f = lane.astype(F32)
    column = lambda j: jnp.sum(jnp.where(lane == j, route, 0.0), axis=-1, keepdims=True)
    first = float(g * EXPERTS_PER_GROUP)
    l1 = column(R_E1) - first
    l2 = column(R_E2) - first
    w1 = column(R_W1)
    w2 = column(R_W2)
    in1 = (l1 >= 0.0) & (l1 < EXPERTS_PER_GROUP)
    in2 = (l2 >= 0.0) & (l2 < EXPERTS_PER_GROUP)
    hit1 = in1 & (lane_f == l1)
    hit2 = in2 & (lane_f == l2)
    onehot = (jnp.where(hit1, 1.0, 0.0) + jnp.where(hit2, 1.0, 0.0)).astype(BF16)
    rows = lambda s: slice(s * sub, (s + 1) * sub)
    before = jnp.concatenate([_dot(tri_ref[...], onehot[rows(s)]) for s in range(n_sub)], axis=0)
    rank1 = jnp.sum(jnp.where(hit1, before, 0.0), axis=-1, keepdims=True)
    rank2 = jnp.sum(jnp.where(hit2, before, 0.0), axis=-1, keepdims=True)
    slot_lane = _iota((tm, MOE_SLOTS), 1).astype(F32)
    x = hn_ref[0]
    experts = range(EXPERTS_PER_GROUP)
    subs = range(n_sub)

    def one_pass(p, first):
        lo = p * float(cap) if first else (p * cap).astype(F32)
        s1 = jnp.where(in1 & (rank1 >= lo) & (rank1 < lo + cap), l1 * cap + rank1 - lo, -1.0)
        s2 = jnp.where(in2 & (rank2 >= lo) & (rank2 < lo + cap), l2 * cap + rank2 - lo, -1.0)
        m1 = slot_lane == s1
        m2 = slot_lane == s2
        sel = jnp.where(m1 | m2, 1.0, 0.0).astype(BF16)
        sel_w = (jnp.where(m1, w1, 0.0) + jnp.where(m2, w2, 0.0)).astype(BF16)
        xs = [_dot(sel[rows(s)], x[rows(s)], _TN) for s in subs]
        xe = [jnp.concatenate([xs[s][e * cap:(e + 1) * cap] for s in subs], axis=0).astype(BF16) for e in experts]
        hg = [_dot(xe[e], wg_ref[e]) for e in experts]
        hu = [_dot(xe[e], wu_ref[e]) for e in experts]
        act = [(_silu(hg[e]) * hu[e]).astype(BF16) for e in experts]
        ye = [_dot(act[e], wd_ref[e]) for e in experts]
        pad = jnp.zeros((MOE_SLOTS - EXPERTS_PER_GROUP * cap, x.shape[1]), F32)
        for s in subs:
            ys = jnp.concatenate([ye[e][s * cap:(s + 1) * cap] for e in experts] + [pad], axis=0).astype(BF16)
            if first:
                y_ref[rows(s), :] = _dot(sel_w[rows(s)], ys)
            else:
                y_ref[rows(s), :] += _dot(sel_w[rows(s)], ys)

    one_pass(0, True)

    def extra_pass(p, carry):
        one_pass(p, False)
        return carry

    lax.fori_loop(1, npass_ref[tile], extra_pass, 0)
    out = acc_ref[0] + gt_ref[0] * y_ref[...]
    if final_norm:
        ms = jnp.mean(out * out, axis=-1, keepdims=True)
        out = out * lax.rsqrt(ms + EPS) * fin_ref[...]
    o_ref[0] = out


def _moe(h, hn, route, gt, wg, wu, wd, fin, tm, final_norm):
    bsz, seq, d = h.shape
    nt = seq // tm
    n_sub = tm // MOE_SUB
    ids = route[..., (R_E1, R_E2)].astype(jnp.int32).reshape(bsz * nt, n_sub, MOE_SUB * 2)
    counts = jnp.sum(jax.nn.one_hot(ids, N_EXPERTS, dtype=jnp.int32), axis=2)
    most = jnp.max(counts.reshape(bsz * nt, n_sub, N_GROUPS, EXPERTS_PER_GROUP), axis=(1, 3))
    npass = ((most + MOE_CAP - 1) // MOE_CAP).T.astype(jnp.int32)
    tri = jnp.asarray(np.tril(np.ones((MOE_SUB, MOE_SUB), np.float32), -1)).astype(BF16)

    tile = lambda w: pl.BlockSpec((1, tm, w), lambda b, s, n: (b, s, 0))
    for g in range(N_GROUPS):
        group = lambda shape: pl.BlockSpec(shape, lambda b, s, n, g=g: (g, 0, 0))
        grid_spec = pltpu.PrefetchScalarGridSpec(
            num_scalar_prefetch=1,
            grid=(bsz, nt),
            in_specs=[tile(d), tile(d), tile(LANES), pl.BlockSpec((1, 1, d), lambda b, s, n: (b, 0, 0)),
                      group((EXPERTS_PER_GROUP, d, D_EXPERT)), group((EXPERTS_PER_GROUP, d, D_EXPERT)),
                      group((EXPERTS_PER_GROUP, D_EXPERT, d)),
                      pl.BlockSpec((1, d), lambda b, s, n: (0, 0)), pl.BlockSpec(tri.shape, lambda b, s, n: (0, 0))],
            out_specs=tile(d),
            scratch_shapes=[pltpu.VMEM((tm, d), F32)],
        )
        h = pl.pallas_call(
            functools.partial(_moe_kernel, group=g, final_norm=final_norm and g == N_GROUPS - 1),
            grid_spec=grid_spec,
            out_shape=jax.ShapeDtypeStruct((bsz, seq, d), F32),
            compiler_params=_params(("parallel", "parallel")),
            name="expert_ffn",
        )(npass[g], h, hn, route, gt, wg, wu, wd, fin, tri)
    return h


def _pick_tile(seq, want):
    tm = min(seq, want)
    assert seq % tm == 0 and tm % CHUNK == 0
    return tm


def kernel(x, c, ada_w, ada_b, norm_mix_g, norm_ffn_g, w_in, gdn_conv_w, gdn_a_log, gdn_dt_bias, gdn_norm_g, mlstm_conv_w, mlstm_i_bias, mlstm_f_bias, mlstm_norm_g, w_branch_a, w_branch_b, w_out, router_group, router_expert, w_gate, w_up, w_down, final_norm_g):
    bsz, seq, d = x.shape
    depth = ada_w.shape[0]
    assert d == D_MODEL and seq % CHUNK == 0
    tm_in = _pick_tile(seq, 512)
    tm_out = _pick_tile(seq, 512)
    tm_moe = _pick_tile(seq, 512)
    tl_gdn = _pick_tile(seq, LANES)
    nb_scan = LANES // CHUNK
    assert tl_gdn % LANES == 0 and bsz % nb_scan == 0
    row = lambda a: a.reshape(1, -1).astype(F32)
    consts = _scan_constants()

    mod = _modulation(c.astype(F32), ada_w.astype(F32), ada_b.astype(F32))
    h = x.astype(F32)
    for l in range(depth):
        sh1, sc1, gt1, sh2, sc2, gt2 = [mod[l, :, i * d:(i + 1) * d].reshape(bsz, 1, d) for i in range(N_MOD)]

        w = w_in[l].astype(F32)
        o_z = 3 * WIDTH
        o_ag = o_z + WIDTH
        o_bqk = o_ag + 4 * N_HEADS
        o_bv = o_bqk + 2 * WIDTH
        o_bo = o_bv + WIDTH
        o_bg = o_bo + WIDTH
        o_mg = o_bg + 4 * N_HEADS
        wconv = jnp.concatenate([w[:, :o_z], w[:, o_bqk:o_bv]], axis=1).astype(BF16)
        wrest = jnp.concatenate([w[:, o_z:o_ag], w[:, o_bv:o_bo], w[:, o_bo:o_bg], w[:, o_mg:]], axis=1).astype(BF16)
        wg = jnp.concatenate([w[:, o_ag:o_bqk], w[:, o_bg:o_mg], jnp.zeros((d, LANES - 8 * N_HEADS), F32)], axis=1)
        cw = jnp.concatenate([gdn_conv_w[l], mlstm_conv_w[l]], axis=1).astype(F32)
        cw = jnp.concatenate([cw, jnp.zeros((SUBLANES - CONV_K, CONV_W), F32)], axis=0)
        zero8 = jnp.zeros((N_HEADS,), F32)
        bias = jnp.concatenate([gdn_dt_bias[l, 0], zero8, gdn_dt_bias[l, 1], zero8,
                                mlstm_i_bias[l, 0], mlstm_f_bias[l, 0], mlstm_i_bias[l, 1], mlstm_f_bias[l, 1],
                                jnp.zeros((LANES - 8 * N_HEADS,), F32)]).astype(F32)
        alog = jnp.concatenate([gdn_a_log[l, 0], zero8, gdn_a_log[l, 1], zero8,
                                jnp.zeros((LANES - 4 * N_HEADS,), F32)]).astype(F32)
        gp = jnp.concatenate([bias[None], alog[None], jnp.zeros((SUBLANES - 2, LANES), F32)], axis=0)

        aq, ak, av, bq, bk, az, bv, bo, mg, gates, cumf, cumb = _input_projection(
            h, sh1, sc1, row(norm_mix_g[l]), wconv, wrest, wg, cw, gp, tm_in)
        nb_state = 4 if bsz % 4 == 0 else nb_scan
        gaf, gab = _gdn_state(_gdn_local(aq, ak, av, gates, cumf, cumb, consts, tl_gdn), consts, nb_state)
        mlf, mlb = _mlstm_scan(bq, bk, bv, gates, cumf, cumb, consts, nb_scan)

        wr = jnp.concatenate([router_expert[l].astype(F32), router_group[l].astype(F32),
                              jnp.zeros((d, LANES - N_EXPERTS - N_GROUPS), F32)], axis=1)
        h, hn2, comb = _output_projection(
            h, gaf, gab, mlf, mlb, az, bo, mg, gt1, row(jnp.tile(gdn_norm_g[l], N_HEADS)), row(mlstm_norm_g[l]),
            w_branch_a[l].astype(BF16), w_branch_b[l].astype(BF16), w_out[l].astype(BF16),
            sh2, sc2, row(norm_ffn_g[l]), wr, tm_out)
        h = _moe(h, hn2, comb, gt2, w_gate[l].astype(BF16), w_up[l].astype(BF16), w_down[l].astype(BF16),
                 row(final_norm_g), tm_moe, final_norm=(l == depth - 1))
    return h.astype(x.dtype)
```

```python
import functools

import numpy as np
import jax
import jax.numpy as jnp
from jax import lax
from jax.experimental import pallas as pl
from jax.experimental.pallas import tpu as pltpu

D_MODEL = 1024
HEAD_DIM = 64
N_HEADS = 8
WIDTH = N_HEADS * HEAD_DIM
CONV_K = 5
CONV_W = 3 * WIDTH + 2 * WIDTH
REST_W = 3 * WIDTH + 2 * D_MODEL
N_GROUPS = 4
EXPERTS_PER_GROUP = 8
N_EXPERTS = N_GROUPS * EXPERTS_PER_GROUP
D_EXPERT = D_MODEL // 4
N_MOD = 6
EPS = 1e-6
NEG = -1e30

CHUNK = 64
HEADS_PER_GROUP = 2
N_HEAD_GROUPS = N_HEADS // HEADS_PER_GROUP
GROUP_W = HEADS_PER_GROUP * HEAD_DIM
LEVELS = 6
LANES = 128
SUBLANES = 8
BF16_ROWS = 16
VMEM_LIMIT = 56 * 1024 * 1024

F32 = jnp.float32
BF16 = jnp.bfloat16

_NN = (((1,), (0,)), ((), ()))
_NT = (((1,), (1,)), ((), ()))
_TN = (((0,), (0,)), ((), ()))

R_E1, R_E2, R_W1, R_W2 = 0, 1, 4, 5
MOE_SUB = 128
MOE_CAP = 24
MOE_SLOTS = 256

M_INCL, M_STRICT, M_LEVEL0, M_DIAG = 0, 1, 2, 2 + LEVELS
N_MASKS = M_DIAG + 1


def _dot(a, b, dims=_NN):
    return lax.dot_general(a, b, dims, preferred_element_type=F32)


def _mm(a, b, dims=_NN):
    return _dot(a.astype(BF16), b.astype(BF16), dims)


def _split2(x):
    hi = x.astype(BF16)
    lo = (x - hi.astype(F32)).astype(BF16)
    return hi, lo


def _split3(x):
    hi = x.astype(BF16)
    r = x - hi.astype(F32)
    mid = r.astype(BF16)
    lo = (r - mid.astype(F32)).astype(BF16)
    return hi, mid, lo


def _dot_pieces(pieces, e):
    out = _dot(pieces[0], e)
    for p in pieces[1:]:
        out = out + _dot(p, e)
    return out


def _mm_lhs3(x, e):
    return _dot_pieces(_split3(x), e)


def _mm_rhs3(e, x):
    hi, mid, lo = _split3(x)
    return _dot(e, hi) + _dot(e, mid) + _dot(e, lo)


def _mm_lhs2(x, e):
    return _dot_pieces(_split2(x), e)


def _mm3(a, b):
    a_hi, a_lo = _split2(a)
    b_hi, b_lo = _split2(b)
    return _dot(a_hi, b_hi) + _dot(a_hi, b_lo) + _dot(a_lo, b_hi)


def _iota(shape, dim):
    return lax.broadcasted_iota(jnp.int32, shape, dim)


def _sigmoid(x):
    return 0.5 * jnp.tanh(0.5 * x) + 0.5


def _silu(x):
    return x * _sigmoid(x)


def _softplus(x):
    return jnp.maximum(x, 0.0) + jnp.log1p(jnp.exp(-jnp.abs(x)))


def _head_ones(width):
    r = _iota((width, width), 0) >> 6
    c = _iota((width, width), 1) >> 6
    return jnp.where(r == c, 1.0, 0.0).astype(BF16)


def _params(sem):
    return pltpu.CompilerParams(dimension_semantics=sem, vmem_limit_bytes=VMEM_LIMIT)


def _full(a):
    return pl.BlockSpec(a.shape, lambda *_: (0,) * a.ndim)


def _mod_kernel(c_ref, w_ref, b_ref, o_ref):
    o_ref[0] = _mm3(_silu(c_ref[...]), w_ref[0]) + b_ref[0]


def _modulation(c, ada_w, ada_b):
    depth, d, n = ada_w.shape
    bsz = c.shape[0]
    tn = n // 4
    return pl.pallas_call(
        _mod_kernel,
        grid=(depth, n // tn),
        in_specs=[
            pl.BlockSpec((bsz, d), lambda l, j: (0, 0)),
            pl.BlockSpec((1, d, tn), lambda l, j: (l, 0, j)),
            pl.BlockSpec((1, 1, tn), lambda l, j: (l, 0, j)),
        ],
        out_specs=pl.BlockSpec((1, bsz, tn), lambda l, j: (l, 0, j)),
        out_shape=jax.ShapeDtypeStruct((depth, bsz, n), F32),
        compiler_params=_params(("parallel", "parallel")),
        name="adaln_mod",
    )(c, ada_w, ada_b.reshape(depth, 1, n))


HALO = BF16_ROWS


def _inproj_kernel(h_ref, hprev_ref, hnext_ref, sh_ref, sc_ref, g_ref, wconv_ref, wrest_ref, cw_ref, gp_ref,
                   perm_ref, permt_ref, aq_ref, ak_ref, av_ref, bq_ref, bk_ref, az_ref, bv_ref, bo_ref, mg_ref, gates_ref, cumf_ref,
                   cumb_ref):
    s = pl.program_id(1)
    ns = pl.num_programs(1)
    tm = h_ref.shape[1]
    gain = g_ref[...] * (1.0 + sc_ref[0])
    shift = sh_ref[0]

    def norm(x):
        ms = jnp.mean(x * x, axis=-1, keepdims=True)
        return x * lax.rsqrt(ms + EPS) * gain + shift

    hn = norm(h_ref[0])
    hp = norm(hprev_ref[0]) * jnp.where(s > 0, 1.0, 0.0)
    hx = norm(hnext_ref[0]) * jnp.where(s < ns - 1, 1.0, 0.0)
    hn_hi = hn.astype(BF16)

    nv = tm // SUBLANES
    edge = CONV_K // 2
    hn_perm = _dot(perm_ref[...], hn_hi).astype(BF16)
    hall = jnp.concatenate([hn_perm, hp.astype(BF16), hx.astype(BF16)], axis=0)
    groups = range(CONV_W // WIDTH)
    cols = lambda g: slice(g * WIDTH, (g + 1) * WIDTH)
    pe = [_dot(hall, wconv_ref[:, cols(g)]) for g in groups]
    rest = _dot(hn_hi, wrest_ref[...])
    hn_lo = (hn - hn_hi.astype(F32)).astype(BF16)
    pre = (rest[:, REST_W:REST_W + LANES] + rest[:, REST_W + LANES:]
           + _dot(hn_lo, wrest_ref[:, REST_W:REST_W + LANES]))

    sub = _iota((SUBLANES, WIDTH), 0)
    ones_bd = _head_ones(WIDTH)
    scale = HEAD_DIM ** -0.5
    out_refs = (aq_ref, ak_ref, av_ref, bq_ref, bk_ref)
    permuted = []
    for g in groups:
        prev = pe[g][tm:tm + HALO]
        nxt = pe[g][tm + HALO:]
        block = lambda v: pe[g][v * SUBLANES:(v + 1) * SUBLANES]
        before = [jnp.where(sub == 0, prev[HALO - edge + i:HALO - edge + i + 1], pltpu.roll(block(nv - edge + i), 1, 0))
                  for i in range(edge)]
        after = [jnp.where(sub == SUBLANES - 1, nxt[i:i + 1], pltpu.roll(block(i), SUBLANES - 1, 0))
                 for i in range(edge)]
        ext = jnp.concatenate(before + [pe[g][:tm]] + after, axis=0)

        conv = jnp.zeros((tm, WIDTH), F32)
        for j in range(CONV_K):
            conv = conv + ext[j * SUBLANES:j * SUBLANES + tm] * cw_ref[j:j + 1, cols(g)]
        cv = _silu(conv)
        if g in (0, 1):
            cv = cv * lax.rsqrt(_mm(cv * cv, ones_bd) + EPS)
        if g in (0, 4):
            cv = cv * scale
        permuted.append(cv.astype(BF16))
    for g in groups:
        out_refs[g][0] = _dot(permt_ref[...], permuted[g]).astype(BF16)

    az_ref[0] = rest[:, 0:WIDTH].astype(BF16)
    bv_ref[0] = rest[:, WIDTH:2 * WIDTH].astype(BF16)
    bo_ref[0] = rest[:, 2 * WIDTH:3 * WIDTH].astype(BF16)
    mg_ref[0] = rest[:, 3 * WIDTH:REST_W].astype(BF16)

    pre = pre + gp_ref[0:1, :]
    lane = _iota(pre.shape, 1)
    second = ((lane >> 3) & 1) == 1
    is_gdn = lane < 4 * N_HEADS
    is_ml = (lane >= 4 * N_HEADS) & (lane < 8 * N_HEADS)
    sp = _softplus(pre)
    gdn_val = jnp.where(second, _sigmoid(pre), -jnp.exp(gp_ref[1:2, :]) * sp)
    ml_val = jnp.where(second, -_softplus(-pre), pre)
    gates = jnp.where(is_gdn, gdn_val, jnp.where(is_ml, ml_val, 0.0))
    gates_ref[0] = gates

    ri = _iota((LANES, LANES), 0)
    ci = _iota((LANES, LANES), 1)
    same_chunk = (ri >> 6) == (ci >> 6)
    pieces = _split3(gates)
    for ref, tri in ((cumf_ref, same_chunk & (ri >= ci)), (cumb_ref, same_chunk & (ri <= ci))):
        tri_b = jnp.where(tri, 1.0, 0.0).astype(BF16)
        for blk in range(tm // LANES):
            rs = slice(blk * LANES, (blk + 1) * LANES)
            ref[0, rs, :] = _dot(tri_b, pieces[0][rs]) + _dot(tri_b, pieces[1][rs]) + _dot(tri_b, pieces[2][rs])


def _input_projection(h, sh, sc, gain, wconv, wrest, cw, gp, tm):
    bsz, seq, d = h.shape
    nt = seq // tm
    per = tm // HALO
    nhalo = seq // HALO
    tile = lambda w: pl.BlockSpec((1, tm, w), lambda b, s: (b, s, 0))
    outs = [(WIDTH, BF16)] * 8 + [(2 * D_MODEL, BF16)] + [(LANES, F32)] * 3
    r = np.arange(tm)
    perm_np = np.zeros((tm, tm), np.float32)
    perm_np[r, (r % SUBLANES) * (tm // SUBLANES) + r // SUBLANES] = 1.0
    perm = jnp.asarray(perm_np).astype(BF16)
    perm_t = jnp.asarray(perm_np.T).astype(BF16)
    return pl.pallas_call(
        _inproj_kernel,
        grid=(bsz, nt),
        in_specs=[
            tile(d),
            pl.BlockSpec((1, HALO, d), lambda b, s: (b, jnp.maximum(s * per - 1, 0), 0)),
            pl.BlockSpec((1, HALO, d), lambda b, s: (b, jnp.minimum((s + 1) * per, nhalo - 1), 0)),
            pl.BlockSpec((1, 1, d), lambda b, s: (b, 0, 0)),
            pl.BlockSpec((1, 1, d), lambda b, s: (b, 0, 0)),
            _full(gain), _full(wconv), _full(wrest), _full(cw), _full(gp), _full(perm), _full(perm_t),
        ],
        out_specs=[tile(w) for w, _ in outs],
        out_shape=[jax.ShapeDtypeStruct((bsz, seq, w), dt) for w, dt in outs],
        compiler_params=_params(("parallel", "parallel")),
        name="input_projection",
    )(h, h, h, sh, sc, gain, wconv, wrest, cw, gp, perm, perm_t)


def _gate_column(branch, direction, kind):
    return branch * 4 * N_HEADS + direction * 2 * N_HEADS + kind * N_HEADS


def _expand(pieces, e):
    rows = pieces[0].shape[0]
    out = _dot(jnp.concatenate(pieces, axis=0), e)
    return functools.reduce(lambda a, b: a + b, [out[i * rows:(i + 1) * rows] for i in range(len(pieces))])


def _head_rows(x_t, column, hg, block):
    first = column + hg * HEADS_PER_GROUP
    pieces = [x_t[first + h:first + h + 1, block * CHUNK:(block + 1) * CHUNK] for h in range(HEADS_PER_GROUP)]
    return jnp.concatenate(pieces, axis=1)


def _scan_constants():
    c, w = CHUNK, GROUP_W
    row = np.arange(c)[:, None]
    col = (np.arange(w) % c)[None, :]
    masks = np.zeros((2, N_MASKS, c, w), np.float32)
    for d in range(2):
        masks[d, M_INCL] = (row >= col) if d == 0 else (row <= col)
        masks[d, M_STRICT] = (row > col) if d == 0 else (row < col)
        for p in range(LEVELS):
            same = (row >> (p + 1)) == (col >> (p + 1))
            r_bit = (row >> p) & 1
            c_bit = (col >> p) & 1
            masks[d, M_LEVEL0 + p] = same & ((r_bit == 1) & (c_bit == 0) if d == 0 else (r_bit == 0) & (c_bit == 1))
        masks[d, M_DIAG] = row == col
    bd = (np.arange(w)[:, None] // HEAD_DIM) == (np.arange(w)[None, :] // HEAD_DIM)
    exp = np.zeros((2, 2, 2, LANES, WIDTH), np.float32)
    lane_head = np.arange(WIDTH) // HEAD_DIM
    for br in range(2):
        for d in range(2):
            for kind in range(2):
                exp[br, d, kind, _gate_column(br, d, kind) + lane_head, np.arange(WIDTH)] = 1.0
    return (jnp.asarray(masks), jnp.asarray(bd.astype(np.float32)),
            jnp.asarray(bd.astype(np.float32)).astype(BF16), jnp.asarray(exp).astype(BF16))


def _block_diag(x_cat, bd_bf16):
    xb = x_cat.astype(BF16)
    return jnp.concatenate([xb] * HEADS_PER_GROUP, axis=0) * bd_bf16


def _gdn_local_kernel(q_ref, k_ref, v_ref, g_ref, cumf_ref, cumb_ref, masks_ref, bdb_ref, exp_ref,
                      uf_ref, wf_ref, qef_ref, inf_ref, kef_ref, cdf_ref,
                      ub_ref, wb_ref, qeb_ref, inb_ref, keb_ref, cdb_ref):
    c, w = CHUNK, GROUP_W
    bd_b = bdb_ref[...]
    outs = ((uf_ref, wf_ref, qef_ref, inf_ref, kef_ref, cdf_ref), (ub_ref, wb_ref, qeb_ref, inb_ref, keb_ref, cdb_ref))
    chains = [(ci, d, hg) for ci in range(q_ref.shape[1] // c) for d in range(2) for hg in range(N_HEAD_GROUPS)]
    rows = lambda ci: slice(ci * c, (ci + 1) * c)
    lanes = lambda hg: slice(hg * w, (hg + 1) * w)

    cums = (cumf_ref[0], cumb_ref[0])
    g_pieces = _split2(g_ref[0])
    gc_all = [_expand(_split3(cums[d]), exp_ref[0, d, 0]) for d in range(2)]
    beta_all = [_expand(g_pieces, exp_ref[0, d, 1]) for d in range(2)]
    per_block = LANES // c
    n_blocks = q_ref.shape[1] // LANES
    cum_t = [[cums[d][blk * LANES:(blk + 1) * LANES].T for blk in range(n_blocks)] for d in range(2)]
    gc = [gc_all[d][rows(ci), lanes(hg)] for ci, d, hg in chains]
    beta = [beta_all[d][rows(ci), lanes(hg)] for ci, d, hg in chains]
    gc_t = [_head_rows(cum_t[d][ci // per_block], _gate_column(0, d, 0), hg, ci % per_block) for ci, d, hg in chains]

    qk_kk = {}
    for ci, d, hg in chains:
        if d == 0:
            q = q_ref[0, rows(ci), lanes(hg)]
            k = k_ref[0, rows(ci), lanes(hg)]
            k_bd = jnp.concatenate([k] * HEADS_PER_GROUP, axis=0) * bd_b
            qk_kk[ci, hg] = _dot(jnp.concatenate([q, k], axis=0), k_bd, _NT)
    kbeta = [k_ref[0, rows(ci), lanes(hg)].astype(F32) * beta[i] for i, (ci, d, hg) in enumerate(chains)]

    a, t = [], []
    for i, (ci, d, hg) in enumerate(chains):
        decay = jnp.exp(jnp.minimum(gc[i] - gc_t[i], 0.0)) * masks_ref[d, M_INCL]
        outs[d][3][0, rows(ci), lanes(hg)] = (qk_kk[ci, hg][:c] * decay).astype(BF16)
        a.append(qk_kk[ci, hg][c:] * beta[i] * decay * masks_ref[d, M_STRICT])
        t.append(masks_ref[d, M_DIAG] - a[i] * masks_ref[d, M_LEVEL0])

    for p in range(1, LEVELS):
        y = [_mm(t[i], _block_diag(a[i] * masks_ref[d, M_LEVEL0 + p], bd_b)) for i, (_, d, _) in enumerate(chains)]
        t = [t[i] - _mm(y[i], _block_diag(t[i], bd_b)) for i in range(len(chains))]

    for i, (ci, d, hg) in enumerate(chains):
        u_ref, w_ref, qe_ref, _, ke_ref, cd_ref = outs[d]
        last = c - 1 if d == 0 else 0
        g_last = gc[i][last:last + 1, :]
        e_gc = jnp.exp(gc[i])
        v = v_ref[0, rows(ci), lanes(hg)].astype(F32)
        u_ref[0, rows(ci), lanes(hg)] = _mm(t[i], _block_diag(v * beta[i], bd_b))
        w_ref[0, rows(ci), lanes(hg)] = _mm(t[i], _block_diag(kbeta[i] * e_gc, bd_b)).astype(BF16)
        qe_ref[0, rows(ci), lanes(hg)] = (q_ref[0, rows(ci), lanes(hg)].astype(F32) * e_gc).astype(BF16)
        k = k_ref[0, rows(ci), lanes(hg)].astype(F32)
        ke_ref[0, rows(ci), lanes(hg)] = (k * jnp.exp(g_last - gc[i])).astype(BF16)
        cd_ref[0, ci, :, lanes(hg)] = jnp.exp(g_last)


def _gdn_local(q, k, v, gates, cumf, cumb, consts, tl):
    bsz, seq, width = q.shape
    masks, _, bd_b, exp = consts
    tile = lambda wd: pl.BlockSpec((1, tl, wd), lambda b, s: (b, s, 0))
    cd_spec = pl.BlockSpec((1, tl // CHUNK, 1, width), lambda b, s: (b, s, 0, 0))
    big = lambda dt: jax.ShapeDtypeStruct((bsz, seq, width), dt)
    cd_shape = jax.ShapeDtypeStruct((bsz, seq // CHUNK, 1, width), F32)
    per_dir_specs = [tile(width)] * 5 + [cd_spec]
    per_dir_shapes = [big(F32), big(BF16), big(BF16), big(BF16), big(BF16), cd_shape]
    return pl.pallas_call(
        _gdn_local_kernel,
        grid=(bsz, seq // tl),
        in_specs=[tile(width), tile(width), tile(width), tile(LANES), tile(LANES), tile(LANES),
                  _full(masks), _full(bd_b), _full(exp)],
        out_specs=per_dir_specs * 2,
        out_shape=per_dir_shapes * 2,
        compiler_params=_params(("parallel", "parallel")),
        name="gdn_local",
    )(q, k, v, gates, cumf, cumb, masks, bd_b, exp)


def _gdn_state_kernel(uf_ref, wf_ref, qef_ref, inf_ref, kef_ref, cdf_ref,
                      ub_ref, wb_ref, qeb_ref, inb_ref, keb_ref, cdb_ref, bdf_ref, bdb_ref, of_ref, ob_ref, s_ref):
    n = pl.program_id(1)

    @pl.when(n == 0)
    def _():
        s_ref[...] = jnp.zeros_like(s_ref)

    c, w = CHUNK, GROUP_W
    bd_b = bdb_ref[...]
    dirs = ((uf_ref, wf_ref, qef_ref, inf_ref, kef_ref, cdf_ref, of_ref),
            (ub_ref, wb_ref, qeb_ref, inb_ref, keb_ref, cdb_ref, ob_ref))
    nb = uf_ref.shape[0]
    chains = [(bi, d, hg) for bi in range(nb) for d in range(2) for hg in range(N_HEAD_GROUPS)]
    lanes = lambda hg: slice(hg * w, (hg + 1) * w)
    ws_qs = []
    for idx, (bi, d, hg) in enumerate(chains):
        w_ref, qe_ref = dirs[d][1], dirs[d][2]
        lhs = jnp.concatenate([w_ref[bi, :, lanes(hg)], qe_ref[bi, :, lanes(hg)]], axis=0)
        ws_qs.append(_dot(lhs, s_ref[idx].astype(BF16)))
    v_new = [(dirs[d][0][bi, :, lanes(hg)] - ws_qs[idx][:c]).astype(BF16) for idx, (bi, d, hg) in enumerate(chains)]
    for idx, (bi, d, hg) in enumerate(chains):
        in_ref, o_ref = dirs[d][3], dirs[d][6]
        o = ws_qs[idx][c:] + _dot(in_ref[bi, :, lanes(hg)], _block_diag(v_new[idx], bd_b))
        o_ref[bi, :, lanes(hg)] = o.astype(BF16)
    for idx, (bi, d, hg) in enumerate(chains):
        ke_ref, cd_ref = dirs[d][4], dirs[d][5]
        ds = _dot(ke_ref[bi, :, lanes(hg)], v_new[idx], _TN)
        s_ref[idx] = s_ref[idx] * cd_ref[bi, 0, :, lanes(hg)] + ds * bdf_ref[...]


def _gdn_state(local_outs, consts, nb):
    uf = local_outs[0]
    bsz, seq, width = uf.shape
    nc = seq // CHUNK
    _, bd_f, bd_b, _ = consts
    fwd = pl.BlockSpec((nb, CHUNK, width), lambda b, n: (b, n, 0))
    bwd = pl.BlockSpec((nb, CHUNK, width), lambda b, n: (b, nc - 1 - n, 0))
    cd_fwd = pl.BlockSpec((nb, 1, 1, width), lambda b, n: (b, n, 0, 0))
    cd_bwd = pl.BlockSpec((nb, 1, 1, width), lambda b, n: (b, nc - 1 - n, 0, 0))
    return pl.pallas_call(
        _gdn_state_kernel,
        grid=(bsz // nb, nc),
        in_specs=[fwd] * 5 + [cd_fwd] + [bwd] * 5 + [cd_bwd] + [_full(bd_f), _full(bd_b)],
        out_specs=[fwd, bwd],
        out_shape=[jax.ShapeDtypeStruct((bsz, seq, width), BF16)] * 2,
        scratch_shapes=[pltpu.VMEM((nb * 2 * N_HEAD_GROUPS, GROUP_W, GROUP_W), F32)],
        compiler_params=_params(("parallel", "arbitrary")),
        name="gdn_state",
    )(*local_outs, bd_f, bd_b)


def _mlstm_kernel(qf_ref, kf_ref, vf_ref, gf_ref, cf_ref, qb_ref, kb_ref, vb_ref, gb_ref, cb_ref,
                  masks_ref, bdf_ref, bdb_ref, exp_ref, of_ref, ob_ref, c_ref, n_ref, m_ref):
    n = pl.program_id(1)

    @pl.when(n == 0)
    def _():
        c_ref[...] = jnp.zeros_like(c_ref)
        n_ref[...] = jnp.zeros_like(n_ref)
        m_ref[...] = jnp.zeros_like(m_ref)

    c, w = CHUNK, GROUP_W
    bd_f = bdf_ref[...]
    bd_b = bdb_ref[...]
    ones_rows = jnp.ones((SUBLANES, c), BF16)
    row = _iota((c, w), 0)
    dirs = ((qf_ref, kf_ref, vf_ref, gf_ref, cf_ref, of_ref), (qb_ref, kb_ref, vb_ref, gb_ref, cb_ref, ob_ref))
    nb = qf_ref.shape[0]
    chains = [(bi, d, hg) for bi in range(nb) for d in range(2) for hg in range(N_HEAD_GROUPS)]
    lanes = lambda hg: slice(hg * w, (hg + 1) * w)
    gates = [jnp.concatenate([dirs[d][3][bi] for bi in range(nb)], axis=0) for d in range(2)]
    cums = [jnp.concatenate([dirs[d][4][bi] for bi in range(nb)], axis=0) for d in range(2)]
    i_all = [_expand(_split3(gates[d]), exp_ref[1, d, 0]) for d in range(2)]
    bcum_all = [_expand(_split3(cums[d]), exp_ref[1, d, 1]) for d in range(2)]
    gates_t = [gates[d].T for d in range(2)]
    cums_t = [cums[d].T for d in range(2)]
    rows = lambda bi: slice(bi * c, (bi + 1) * c)
    bcum = [bcum_all[d][rows(bi), lanes(hg)] for bi, d, hg in chains]
    r = [i_all[d][rows(bi), lanes(hg)] - bcum[i] for i, (bi, d, hg) in enumerate(chains)]
    r_t = [_head_rows(gates_t[d], _gate_column(1, d, 0), hg, bi) - _head_rows(cums_t[d], _gate_column(1, d, 1), hg, bi)
           for bi, d, hg in chains]
    qk = []
    for bi, d, hg in chains:
        k_bd = jnp.concatenate([dirs[d][1][bi, :, lanes(hg)]] * HEADS_PER_GROUP, axis=0) * bd_b
        qk.append(_dot(dirs[d][0][bi, :, lanes(hg)], k_bd, _NT))

    pm, mx, inter_w = [], [], []
    for i, (bi, d, hg) in enumerate(chains):
        m = r[i]
        for p in range(LEVELS):
            sh = 1 << p
            if d == 0:
                m = jnp.maximum(m, jnp.where(row >= sh, pltpu.roll(m, sh, 0), NEG))
            else:
                m = jnp.maximum(m, jnp.where(row < c - sh, pltpu.roll(m, c - sh, 0), NEG))
        pm.append(m)
        m_prev = m_ref[i, 0:1, :]
        mx.append(jnp.maximum(m_prev, m))
        inter_w.append(jnp.exp(m_prev - mx[i]))
        qk[i] = qk[i] * jnp.exp(jnp.where(masks_ref[d, M_INCL] > 0.0, r_t[i] - mx[i], NEG))

    num = [_mm(qk[i], _block_diag(dirs[d][2][bi, :, lanes(hg)], bd_b)) for i, (bi, d, hg) in enumerate(chains)]
    den = [_mm_lhs2(qk[i], bd_b) for i in range(len(chains))]
    qc = [_mm(dirs[d][0][bi, :, lanes(hg)], c_ref[i]) for i, (bi, d, hg) in enumerate(chains)]
    qn = [_mm(dirs[d][0][bi, :, lanes(hg)].astype(F32) * n_ref[i, 0:1, :], bd_b)
          for i, (bi, d, hg) in enumerate(chains)]
    for i, (bi, d, hg) in enumerate(chains):
        full_num = num[i] + inter_w[i] * qc[i]
        full_den = den[i] + inter_w[i] * qn[i]
        hidden = full_num / jnp.maximum(jnp.abs(full_den), jnp.exp(-(bcum[i] + mx[i])))
        dirs[d][5][bi, :, lanes(hg)] = hidden.astype(BF16)

    for i, (bi, d, hg) in enumerate(chains):
        last = c - 1 if d == 0 else 0
        m_prev = m_ref[i, 0:1, :]
        b_last = bcum[i][last:last + 1, :]
        m_next = b_last + jnp.maximum(m_prev, pm[i][last:last + 1, :])
        scale_prev = jnp.exp(b_last + m_prev - m_next)
        wgt = jnp.exp(b_last + r[i] - m_next)
        kw = (dirs[d][1][bi, :, lanes(hg)].astype(F32) * wgt).astype(BF16)
        c_ref[i] = c_ref[i] * scale_prev + _dot(kw, dirs[d][2][bi, :, lanes(hg)], _TN) * bd_f
        n_ref[i, 0:1, :] = n_ref[i, 0:1, :] * scale_prev + _dot(ones_rows, kw)[0:1, :]
        m_ref[i, 0:1, :] = m_next


def _mlstm_scan(q, k, v, gates, cumf, cumb, consts, nb):
    bsz, seq, width = q.shape
    nc = seq // CHUNK
    masks, bd_f, bd_b, exp = consts
    fwd = lambda wd: pl.BlockSpec((nb, CHUNK, wd), lambda b, n: (b, n, 0))
    bwd = lambda wd: pl.BlockSpec((nb, CHUNK, wd), lambda b, n: (b, nc - 1 - n, 0))
    n_state = nb * 2 * N_HEAD_GROUPS
    state = pltpu.VMEM((n_state, GROUP_W, GROUP_W), F32)
    row_state = pltpu.VMEM((n_state, SUBLANES, GROUP_W), F32)
    return pl.pallas_call(
        _mlstm_kernel,
        grid=(bsz // nb, nc),
        in_specs=[fwd(width), fwd(width), fwd(width), fwd(LANES), fwd(LANES),
                  bwd(width), bwd(width), bwd(width), bwd(LANES), bwd(LANES),
                  _full(masks), _full(bd_f), _full(bd_b), _full(exp)],
        out_specs=[fwd(width), bwd(width)],
        out_shape=[jax.ShapeDtypeStruct((bsz, seq, width), BF16)] * 2,
        scratch_shapes=[state, row_state, row_state],
        compiler_params=_params(("parallel", "arbitrary")),
        name="mlstm_scan",
    )(q, k, v, gates, cumf, q, k, v, gates, cumb, masks, bd_f, bd_b, exp)


def _outproj_kernel(h_ref, gaf_ref, gab_ref, mlf_ref, mlb_ref, az_ref, bo_ref, mg_ref, gt_ref, ng_ref, mlg_ref,
                    wa_ref, wb_ref, wo_ref, sh_ref, sc_ref, fg_ref, wr_ref, hout_ref, hn_ref, comb_ref):
    ones_bd = _head_ones(WIDTH)
    inv = 1.0 / HEAD_DIM
    tm = h_ref.shape[1]
    n_parts = 2 if tm % (2 * BF16_ROWS) == 0 else 1
    parts = [slice(i * tm // n_parts, (i + 1) * tm // n_parts) for i in range(n_parts)]
    f32 = lambda ref, rs: ref[0, rs, :].astype(F32)

    def branch_inputs(rs):
        oa = f32(gaf_ref, rs) + f32(gab_ref, rs)
        ms = _mm(oa * oa, ones_bd) * inv
        oa = oa * lax.rsqrt(ms + EPS) * ng_ref[...] * _silu(f32(az_ref, rs))
        hb = f32(mlf_ref, rs) + f32(mlb_ref, rs)
        mu = _mm_lhs2(hb, ones_bd) * inv
        tc = hb - mu
        var = _mm(tc * tc, ones_bd) * inv
        hb = tc * lax.rsqrt(var + EPS) * mlg_ref[...] * _sigmoid(f32(bo_ref, rs))
        return oa.astype(BF16), hb.astype(BF16)

    def merged(rs, oa, hb):
        gate = _sigmoid(f32(mg_ref, rs))
        return gate[:, :D_MODEL] * _dot(oa, wa_ref[...]) + gate[:, D_MODEL:] * _dot(hb, wb_ref[...])

    def residual(rs, y):
        h_new = h_ref[0, rs, :] + gt_ref[0] * _mm(y, wo_ref[...])
        hout_ref[0, rs, :] = h_new
        ms2 = jnp.mean(h_new * h_new, axis=-1, keepdims=True)
        hn = h_new * lax.rsqrt(ms2 + EPS) * (fg_ref[...] * (1.0 + sc_ref[0])) + sh_ref[0]
        hn_ref[0, rs, :] = hn.astype(BF16)
        return hn

    def routing(rs, logits):
        lane = _iota(logits.shape, 1)
        lane_f = lane.astype(F32)
        big = 1e9
        gl = jnp.where((lane >= N_EXPERTS) & (lane < N_EXPERTS + N_GROUPS), logits, NEG)
        gmax = jnp.max(gl, axis=-1, keepdims=True)
        gidx = jnp.min(jnp.where(gl == gmax, lane_f - N_EXPERTS, big), axis=-1, keepdims=True)
        p_group = 1.0 / jnp.sum(jnp.exp(gl - gmax), axis=-1, keepdims=True)
        el = jnp.where((lane < N_EXPERTS) & ((lane >> 3).astype(F32) == gidx), logits, NEG)
        v1 = jnp.max(el, axis=-1, keepdims=True)
        i1 = jnp.min(jnp.where(el == v1, lane_f, big), axis=-1, keepdims=True)
        el2 = jnp.where(lane_f == i1, NEG, el)
        v2 = jnp.max(el2, axis=-1, keepdims=True)
        i2 = jnp.min(jnp.where(el2 == v2, lane_f, big), axis=-1, keepdims=True)
        e21 = jnp.exp(v2 - v1)
        w1 = p_group / (1.0 + e21)
        route = jnp.where(lane == R_E1, i1, 0.0) + jnp.where(lane == R_E2, i2, 0.0)
        comb_ref[0, rs, :] = route + jnp.where(lane == R_W1, w1, 0.0) + jnp.where(lane == R_W2, w1 * e21, 0.0)

    branches = [branch_inputs(rs) for rs in parts]
    ys = [merged(rs, *branches[i]) for i, rs in enumerate(parts)]
    hns = [residual(rs, ys[i]) for i, rs in enumerate(parts)]
    logits = []
    for hn in hns:
        hi, lo = _split2(hn)
        both = _dot(jnp.concatenate([hi, lo], axis=1), wr_ref[...])
        logits.append(both[:, :LANES] + both[:, LANES:])
    for i, rs in enumerate(parts):
        routing(rs, logits[i])


def _output_projection(h, gaf, gab, mlf, mlb, az, bo, mg, gt, ng, mlg, wa, wb, wo, sh, sc, fg, wr, tm):
    bsz, seq, d = h.shape
    tile = lambda w: pl.BlockSpec((1, tm, w), lambda b, s: (b, s, 0))
    per_b = pl.BlockSpec((1, 1, d), lambda b, s: (b, 0, 0))
    return pl.pallas_call(
        _outproj_kernel,
        grid=(bsz, seq // tm),
        in_specs=[tile(d), tile(WIDTH), tile(WIDTH), tile(WIDTH), tile(WIDTH), tile(WIDTH), tile(WIDTH),
                  tile(2 * D_MODEL), per_b, _full(ng), _full(mlg), _full(wa), _full(wb), _full(wo), per_b, per_b,
                  _full(fg), _full(wr)],
        out_specs=[tile(d), tile(d), tile(LANES)],
        out_shape=[jax.ShapeDtypeStruct((bsz, seq, d), F32), jax.ShapeDtypeStruct((bsz, seq, d), BF16),
                   jax.ShapeDtypeStruct((bsz, seq, LANES), F32)],
        compiler_params=_params(("parallel", "parallel")),
        name="output_projection",
    )(h, gaf, gab, mlf, mlb, az, bo, mg, gt, ng, mlg, wa, wb, wo, sh, sc, fg, wr)


def _moe_kernel(npass_ref, acc_ref, hn_ref, route_ref, gt_ref, wg_ref, wu_ref, wd_ref, fin_ref, tri_ref,
                o_ref, y_ref, *, group, final_norm):
    g = group
    tile = pl.program_id(0) * pl.num_programs(1) + pl.program_id(1)
    tm = hn_ref.shape[1]
    cap = MOE_CAP
    sub = MOE_SUB
    n_sub = tm // sub
    route = route_ref[0]
    lane = _iota(route.shape, 1)
    lane_f = lane.astype(F32)
    column = lambda j: jnp.sum(jnp.where(lane == j, route, 0.0), axis=-1, keepdims=True)
    first = float(g * EXPERTS_PER_GROUP)
    l1 = column(R_E1) - first
    l2 = column(R_E2) - first
    w1 = column(R_W1)
    w2 = column(R_W2)
    in1 = (l1 >= 0.0) & (l1 < EXPERTS_PER_GROUP)
    in2 = (l2 >= 0.0) & (l2 < EXPERTS_PER_GROUP)
    hit1 = in1 & (lane_f == l1)
    hit2 = in2 & (lane_f == l2)
    onehot = (jnp.where(hit1, 1.0, 0.0) + jnp.where(hit2, 1.0, 0.0)).astype(BF16)
    rows = lambda s: slice(s * sub, (s + 1) * sub)
    before = jnp.concatenate([_dot(tri_ref[...], onehot[rows(s)]) for s in range(n_sub)], axis=0)
    rank1 = jnp.sum(jnp.where(hit1, before, 0.0), axis=-1, keepdims=True)
    rank2 = jnp.sum(jnp.where(hit2, before, 0.0), axis=-1, keepdims=True)
    slot_lane = _iota((tm, MOE_SLOTS), 1).astype(F32)
    x = hn_ref[0]
    experts = range(EXPERTS_PER_GROUP)
    subs = range(n_sub)

    def one_pass(p, first):
        lo = p * float(cap) if first else (p * cap).astype(F32)
        s1 = jnp.where(in1 & (rank1 >= lo) & (rank1 < lo + cap), l1 * cap + rank1 - lo, -1.0)
        s2 = jnp.where(in2 & (rank2 >= lo) & (rank2 < lo + cap), l2 * cap + rank2 - lo, -1.0)
        m1 = slot_lane == s1
        m2 = slot_lane == s2
        sel = jnp.where(m1 | m2, 1.0, 0.0).astype(BF16)
        sel_w = (jnp.where(m1, w1, 0.0) + jnp.where(m2, w2, 0.0)).astype(BF16)
        xs = [_dot(sel[rows(s)], x[rows(s)], _TN) for s in subs]
        xe = [jnp.concatenate([xs[s][e * cap:(e + 1) * cap] for s in subs], axis=0).astype(BF16) for e in experts]
        hg = [_dot(xe[e], wg_ref[e]) for e in experts]
        hu = [_dot(xe[e], wu_ref[e]) for e in experts]
        act = [(_silu(hg[e]) * hu[e]).astype(BF16) for e in experts]
        ye = [_dot(act[e], wd_ref[e]) for e in experts]
        pad = jnp.zeros((MOE_SLOTS - EXPERTS_PER_GROUP * cap, x.shape[1]), F32)
        for s in subs:
            ys = jnp.concatenate([ye[e][s * cap:(s + 1) * cap] for e in experts] + [pad], axis=0).astype(BF16)
            if first:
                y_ref[rows(s), :] = _dot(sel_w[rows(s)], ys)
            else:
                y_ref[rows(s), :] += _dot(sel_w[rows(s)], ys)

    one_pass(0, True)

    def extra_pass(p, carry):
        one_pass(p, False)
        return carry

    lax.fori_loop(1, npass_ref[tile], extra_pass, 0)
    out = acc_ref[0] + gt_ref[0] * y_ref[...]
    if final_norm:
        ms = jnp.mean(out * out, axis=-1, keepdims=True)
        out = out * lax.rsqrt(ms + EPS) * fin_ref[...]
    o_ref[0] = out


def _moe(h, hn, route, gt, wg, wu, wd, fin, tm, final_norm):
    bsz, seq, d = h.shape
    nt = seq // tm
    n_sub = tm // MOE_SUB
    ids = route[..., (R_E1, R_E2)].astype(jnp.int32).reshape(bsz * nt, n_sub, MOE_SUB * 2)
    counts = jnp.sum(jax.nn.one_hot(ids, N_EXPERTS, dtype=jnp.int32), axis=2)
    most = jnp.max(counts.reshape(bsz * nt, n_sub, N_GROUPS, EXPERTS_PER_GROUP), axis=(1, 3))
    npass = ((most + MOE_CAP - 1) // MOE_CAP).T.astype(jnp.int32)
    tri = jnp.asarray(np.tril(np.ones((MOE_SUB, MOE_SUB), np.float32), -1)).astype(BF16)

    tile = lambda w: pl.BlockSpec((1, tm, w), lambda b, s, n: (b, s, 0))
    for g in range(N_GROUPS):
        group = lambda shape: pl.BlockSpec(shape, lambda b, s, n, g=g: (g, 0, 0))
        grid_spec = pltpu.PrefetchScalarGridSpec(
            num_scalar_prefetch=1,
            grid=(bsz, nt),
            in_specs=[tile(d), tile(d), tile(LANES), pl.BlockSpec((1, 1, d), lambda b, s, n: (b, 0, 0)),
                      group((EXPERTS_PER_GROUP, d, D_EXPERT)), group((EXPERTS_PER_GROUP, d, D_EXPERT)),
                      group((EXPERTS_PER_GROUP, D_EXPERT, d)),
                      pl.BlockSpec((1, d), lambda b, s, n: (0, 0)), pl.BlockSpec(tri.shape, lambda b, s, n: (0, 0))],
            out_specs=tile(d),
            scratch_shapes=[pltpu.VMEM((tm, d), F32)],
        )
        h = pl.pallas_call(
            functools.partial(_moe_kernel, group=g, final_norm=final_norm and g == N_GROUPS - 1),
            grid_spec=grid_spec,
            out_shape=jax.ShapeDtypeStruct((bsz, seq, d), F32),
            compiler_params=_params(("parallel", "parallel")),
            name="expert_ffn",
        )(npass[g], h, hn, route, gt, wg, wu, wd, fin, tri)
    return h


def _pick_tile(seq, want):
    tm = min(seq, want)
    assert seq % tm == 0 and tm % CHUNK == 0
    return tm


def kernel(x, c, ada_w, ada_b, norm_mix_g, norm_ffn_g, w_in, gdn_conv_w, gdn_a_log, gdn_dt_bias, gdn_norm_g, mlstm_conv_w, mlstm_i_bias, mlstm_f_bias, mlstm_norm_g, w_branch_a, w_branch_b, w_out, router_group, router_expert, w_gate, w_up, w_down, final_norm_g):
    bsz, seq, d = x.shape
    depth = ada_w.shape[0]
    assert d == D_MODEL and seq % CHUNK == 0
    tm_in = _pick_tile(seq, 512)
    tm_out = _pick_tile(seq, 512)
    tm_moe = _pick_tile(seq, 512)
    tl_gdn = _pick_tile(seq, LANES)
    nb_scan = LANES // CHUNK
    assert tl_gdn % LANES == 0 and bsz % nb_scan == 0
    row = lambda a: a.reshape(1, -1).astype(F32)
    consts = _scan_constants()

    mod = _modulation(c.astype(F32), ada_w.astype(F32), ada_b.astype(F32))
    h = x.astype(F32)
    for l in range(depth):
        sh1, sc1, gt1, sh2, sc2, gt2 = [mod[l, :, i * d:(i + 1) * d].reshape(bsz, 1, d) for i in range(N_MOD)]

        w = w_in[l].astype(F32)
        o_z = 3 * WIDTH
        o_ag = o_z + WIDTH
        o_bqk = o_ag + 4 * N_HEADS
        o_bv = o_bqk + 2 * WIDTH
        o_bo = o_bv + WIDTH
        o_bg = o_bo + WIDTH
        o_mg = o_bg + 4 * N_HEADS
        wconv = jnp.concatenate([w[:, :o_z], w[:, o_bqk:o_bv]], axis=1).astype(BF16)
        wg = jnp.concatenate([w[:, o_ag:o_bqk], w[:, o_bg:o_mg], jnp.zeros((d, LANES - 8 * N_HEADS), F32)], axis=1)
        wg_hi = wg.astype(BF16)
        wg_lo = (wg - wg_hi.astype(F32)).astype(BF16)
        wrest = jnp.concatenate([w[:, o_z:o_ag].astype(BF16), w[:, o_bv:o_bo].astype(BF16), w[:, o_bo:o_bg].astype(BF16),
                                 w[:, o_mg:].astype(BF16), wg_hi, wg_lo], axis=1)
        cw = jnp.concatenate([gdn_conv_w[l], mlstm_conv_w[l]], axis=1).astype(F32)
        cw = jnp.concatenate([cw, jnp.zeros((SUBLANES - CONV_K, CONV_W), F32)], axis=0)
        zero8 = jnp.zeros((N_HEADS,), F32)
        bias = jnp.concatenate([gdn_dt_bias[l, 0], zero8, gdn_dt_bias[l, 1], zero8,
                                mlstm_i_bias[l, 0], mlstm_f_bias[l, 0], mlstm_i_bias[l, 1], mlstm_f_bias[l, 1],
                                jnp.zeros((LANES - 8 * N_HEADS,), F32)]).astype(F32)
        alog = jnp.concatenate([gdn_a_log[l, 0], zero8, gdn_a_log[l, 1], zero8,
                                jnp.zeros((LANES - 4 * N_HEADS,), F32)]).astype(F32)
        gp = jnp.concatenate([bias[None], alog[None], jnp.zeros((SUBLANES - 2, LANES), F32)], axis=0)

        aq, ak, av, bq, bk, az, bv, bo, mg, gates, cumf, cumb = _input_projection(
            h, sh1, sc1, row(norm_mix_g[l]), wconv, wrest, cw, gp, tm_in)
        nb_state = 4 if bsz % 4 == 0 else nb_scan
        gaf, gab = _gdn_state(_gdn_local(aq, ak, av, gates, cumf, cumb, consts, tl_gdn), consts, nb_state)
        mlf, mlb = _mlstm_scan(bq, bk, bv, gates, cumf, cumb, consts, nb_scan)

        wr = jnp.concatenate([router_expert[l].astype(F32), router_group[l].astype(F32),
                              jnp.zeros((d, LANES - N_EXPERTS - N_GROUPS), F32)], axis=1)
        wr_hi = wr.astype(BF16)
        wr_lo = (wr - wr_hi.astype(F32)).astype(BF16)
        wr = jnp.concatenate([jnp.concatenate([wr_hi, wr_lo], axis=1),
                              jnp.concatenate([wr_hi, jnp.zeros_like(wr_hi)], axis=1)], axis=0)
        h, hn2, comb = _output_projection(
            h, gaf, gab, mlf, mlb, az, bo, mg, gt1, row(jnp.tile(gdn_norm_g[l], N_HEADS)), row(mlstm_norm_g[l]),
            w_branch_a[l].astype(BF16), w_branch_b[l].astype(BF16), w_out[l].astype(BF16),
            sh2, sc2, row(norm_ffn_g[l]), wr, tm_out)
        h = _moe(h, hn2, comb, gt2, w_gate[l].astype(BF16), w_up[l].astype(BF16), w_down[l].astype(BF16),
                 row(final_norm_g), tm_moe, final_norm=(l == depth - 1))
    return h.astype(x.dtype)
```

```python
import functools

import numpy as np
import jax
import jax.numpy as jnp
from jax import lax
from jax.experimental import pallas as pl
from jax.experimental.pallas import tpu as pltpu

D_MODEL = 1024
HEAD_DIM = 64
N_HEADS = 8
WIDTH = N_HEADS * HEAD_DIM
CONV_K = 5
CONV_W = 3 * WIDTH + 2 * WIDTH
REST_W = 3 * WIDTH + 2 * D_MODEL
N_GROUPS = 4
EXPERTS_PER_GROUP = 8
N_EXPERTS = N_GROUPS * EXPERTS_PER_GROUP
D_EXPERT = D_MODEL // 4
N_MOD = 6
EPS = 1e-6
NEG = -1e30

CHUNK = 64
HEADS_PER_GROUP = 2
N_HEAD_GROUPS = N_HEADS // HEADS_PER_GROUP
GROUP_W = HEADS_PER_GROUP * HEAD_DIM
LEVELS = 6
LANES = 128
SUBLANES = 8
BF16_ROWS = 16
VMEM_LIMIT = 56 * 1024 * 1024

F32 = jnp.float32
BF16 = jnp.bfloat16

_NN = (((1,), (0,)), ((), ()))
_NT = (((1,), (1,)), ((), ()))
_TN = (((0,), (0,)), ((), ()))

R_E1, R_E2, R_W1, R_W2 = 0, 1, 4, 5
MOE_SUB = 128
MOE_CAP = 24
MOE_CHAIN_SUBS = 4
MOE_SLOTS = 256

M_INCL, M_STRICT, M_LEVEL0, M_DIAG = 0, 1, 2, 2 + LEVELS
N_MASKS = M_DIAG + 1


def _dot(a, b, dims=_NN):
    return lax.dot_general(a, b, dims, preferred_element_type=F32)


def _mm(a, b, dims=_NN):
    return _dot(a.astype(BF16), b.astype(BF16), dims)


def _split2(x):
    hi = x.astype(BF16)
    lo = (x - hi.astype(F32)).astype(BF16)
    return hi, lo


def _split3(x):
    hi = x.astype(BF16)
    r = x - hi.astype(F32)
    mid = r.astype(BF16)
    lo = (r - mid.astype(F32)).astype(BF16)
    return hi, mid, lo


def _dot_pieces(pieces, e):
    out = _dot(pieces[0], e)
    for p in pieces[1:]:
        out = out + _dot(p, e)
    return out


def _mm_lhs3(x, e):
    return _dot_pieces(_split3(x), e)


def _mm_rhs3(e, x):
    hi, mid, lo = _split3(x)
    return _dot(e, hi) + _dot(e, mid) + _dot(e, lo)


def _mm_lhs2(x, e):
    return _dot_pieces(_split2(x), e)


def _mm3(a, b):
    a_hi, a_lo = _split2(a)
    b_hi, b_lo = _split2(b)
    return _dot(a_hi, b_hi) + _dot(a_hi, b_lo) + _dot(a_lo, b_hi)


def _iota(shape, dim):
    return lax.broadcasted_iota(jnp.int32, shape, dim)


def _sigmoid(x):
    return 0.5 * jnp.tanh(0.5 * x) + 0.5


def _silu(x):
    return x * _sigmoid(x)


def _softplus(x):
    return jnp.maximum(x, 0.0) + jnp.log1p(jnp.exp(-jnp.abs(x)))


def _head_ones(width):
    r = _iota((width, width), 0) >> 6
    c = _iota((width, width), 1) >> 6
    return jnp.where(r == c, 1.0, 0.0).astype(BF16)


def _params(sem):
    return pltpu.CompilerParams(dimension_semantics=sem, vmem_limit_bytes=VMEM_LIMIT)


def _full(a):
    return pl.BlockSpec(a.shape, lambda *_: (0,) * a.ndim)


def _mod_kernel(c_ref, w_ref, b_ref, o_ref):
    o_ref[0] = _mm3(_silu(c_ref[...]), w_ref[0]) + b_ref[0]


def _modulation(c, ada_w, ada_b):
    depth, d, n = ada_w.shape
    bsz = c.shape[0]
    tn = n // 4
    return pl.pallas_call(
        _mod_kernel,
        grid=(depth, n // tn),
        in_specs=[
            pl.BlockSpec((bsz, d), lambda l, j: (0, 0)),
            pl.BlockSpec((1, d, tn), lambda l, j: (l, 0, j)),
            pl.BlockSpec((1, 1, tn), lambda l, j: (l, 0, j)),
        ],
        out_specs=pl.BlockSpec((1, bsz, tn), lambda l, j: (l, 0, j)),
        out_shape=jax.ShapeDtypeStruct((depth, bsz, n), F32),
        compiler_params=_params(("parallel", "parallel")),
        name="adaln_mod",
    )(c, ada_w, ada_b.reshape(depth, 1, n))


HALO = BF16_ROWS


def _inproj_kernel(h_ref, hprev_ref, hnext_ref, sh_ref, sc_ref, g_ref, wconv_ref, wrest_ref, cw_ref, gp_ref,
                   perm_ref, permt_ref, aq_ref, ak_ref, av_ref, bq_ref, bk_ref, az_ref, bv_ref, bo_ref, mg_ref, gates_ref, cumf_ref,
                   cumb_ref):
    s = pl.program_id(1)
    ns = pl.num_programs(1)
    tm = h_ref.shape[1]
    gain = g_ref[...] * (1.0 + sc_ref[0])
    shift = sh_ref[0]

    def norm(x):
        ms = jnp.mean(x * x, axis=-1, keepdims=True)
        return x * lax.rsqrt(ms + EPS) * gain + shift

    hn = norm(h_ref[0])
    hp = norm(hprev_ref[0]) * jnp.where(s > 0, 1.0, 0.0)
    hx = norm(hnext_ref[0]) * jnp.where(s < ns - 1, 1.0, 0.0)
    hn_hi = hn.astype(BF16)

    nv = tm // SUBLANES
    edge = CONV_K // 2
    hn_perm = _dot(perm_ref[...], hn_hi).astype(BF16)
    hall = jnp.concatenate([hn_perm, hp.astype(BF16), hx.astype(BF16)], axis=0)
    groups = range(CONV_W // WIDTH)
    cols = lambda g: slice(g * WIDTH, (g + 1) * WIDTH)
    pe = [_dot(hall, wconv_ref[:, cols(g)]) for g in groups]
    rest = _dot(hn_hi, wrest_ref[...])
    hn_lo = (hn - hn_hi.astype(F32)).astype(BF16)
    pre = (rest[:, REST_W:REST_W + LANES] + rest[:, REST_W + LANES:]
           + _dot(hn_lo, wrest_ref[:, REST_W:REST_W + LANES]))

    sub = _iota((SUBLANES, WIDTH), 0)
    ones_bd = _head_ones(WIDTH)
    scale = HEAD_DIM ** -0.5
    out_refs = (aq_ref, ak_ref, av_ref, bq_ref, bk_ref)
    permuted = []
    for g in groups:
        prev = pe[g][tm:tm + HALO]
        nxt = pe[g][tm + HALO:]
        block = lambda v: pe[g][v * SUBLANES:(v + 1) * SUBLANES]
        before = [jnp.where(sub == 0, prev[HALO - edge + i:HALO - edge + i + 1], pltpu.roll(block(nv - edge + i), 1, 0))
                  for i in range(edge)]
        after = [jnp.where(sub == SUBLANES - 1, nxt[i:i + 1], pltpu.roll(block(i), SUBLANES - 1, 0))
                 for i in range(edge)]
        ext = jnp.concatenate(before + [pe[g][:tm]] + after, axis=0)

        conv = jnp.zeros((tm, WIDTH), F32)
        for j in range(CONV_K):
            conv = conv + ext[j * SUBLANES:j * SUBLANES + tm] * cw_ref[j:j + 1, cols(g)]
        cv = _silu(conv)
        if g in (0, 1):
            cv = cv * lax.rsqrt(_mm(cv * cv, ones_bd) + EPS)
        if g in (0, 4):
            cv = cv * scale
        permuted.append(cv.astype(BF16))
    for g in groups:
        out_refs[g][0] = _dot(permt_ref[...], permuted[g]).astype(BF16)

    az_ref[0] = rest[:, 0:WIDTH].astype(BF16)
    bv_ref[0] = rest[:, WIDTH:2 * WIDTH].astype(BF16)
    bo_ref[0] = rest[:, 2 * WIDTH:3 * WIDTH].astype(BF16)
    mg_ref[0] = rest[:, 3 * WIDTH:REST_W].astype(BF16)

    pre = pre + gp_ref[0:1, :]
    lane = _iota(pre.shape, 1)
    second = ((lane >> 3) & 1) == 1
    is_gdn = lane < 4 * N_HEADS
    is_ml = (lane >= 4 * N_HEADS) & (lane < 8 * N_HEADS)
    sp = _softplus(pre)
    gdn_val = jnp.where(second, _sigmoid(pre), -jnp.exp(gp_ref[1:2, :]) * sp)
    ml_val = jnp.where(second, -_softplus(-pre), pre)
    gates = jnp.where(is_gdn, gdn_val, jnp.where(is_ml, ml_val, 0.0))
    gates_ref[0] = gates

    ri = _iota((LANES, LANES), 0)
    ci = _iota((LANES, LANES), 1)
    same_chunk = (ri >> 6) == (ci >> 6)
    pieces = _split3(gates)
    for ref, tri in ((cumf_ref, same_chunk & (ri >= ci)), (cumb_ref, same_chunk & (ri <= ci))):
        tri_b = jnp.where(tri, 1.0, 0.0).astype(BF16)
        for blk in range(tm // LANES):
            rs = slice(blk * LANES, (blk + 1) * LANES)
            ref[0, rs, :] = _dot(tri_b, pieces[0][rs]) + _dot(tri_b, pieces[1][rs]) + _dot(tri_b, pieces[2][rs])


def _input_projection(h, sh, sc, gain, wconv, wrest, cw, gp, tm):
    bsz, seq, d = h.shape
    nt = seq // tm
    per = tm // HALO
    nhalo = seq // HALO
    tile = lambda w: pl.BlockSpec((1, tm, w), lambda b, s: (b, s, 0))
    outs = [(WIDTH, BF16)] * 8 + [(2 * D_MODEL, BF16)] + [(LANES, F32)] * 3
    r = np.arange(tm)
    perm_np = np.zeros((tm, tm), np.float32)
    perm_np[r, (r % SUBLANES) * (tm // SUBLANES) + r // SUBLANES] = 1.0
    perm = jnp.asarray(perm_np).astype(BF16)
    perm_t = jnp.asarray(perm_np.T).astype(BF16)
    return pl.pallas_call(
        _inproj_kernel,
        grid=(bsz, nt),
        in_specs=[
            tile(d),
            pl.BlockSpec((1, HALO, d), lambda b, s: (b, jnp.maximum(s * per - 1, 0), 0)),
            pl.BlockSpec((1, HALO, d), lambda b, s: (b, jnp.minimum((s + 1) * per, nhalo - 1), 0)),
            pl.BlockSpec((1, 1, d), lambda b, s: (b, 0, 0)),
            pl.BlockSpec((1, 1, d), lambda b, s: (b, 0, 0)),
            _full(gain), _full(wconv), _full(wrest), _full(cw), _full(gp), _full(perm), _full(perm_t),
        ],
        out_specs=[tile(w) for w, _ in outs],
        out_shape=[jax.ShapeDtypeStruct((bsz, seq, w), dt) for w, dt in outs],
        compiler_params=_params(("parallel", "parallel")),
        name="input_projection",
    )(h, h, h, sh, sc, gain, wconv, wrest, cw, gp, perm, perm_t)


def _gate_column(branch, direction, kind):
    return branch * 4 * N_HEADS + direction * 2 * N_HEADS + kind * N_HEADS


def _expand(pieces, e):
    rows = pieces[0].shape[0]
    out = _dot(jnp.concatenate(pieces, axis=0), e)
    return functools.reduce(lambda a, b: a + b, [out[i * rows:(i + 1) * rows] for i in range(len(pieces))])


def _head_rows(x_t, column, hg, block):
    first = column + hg * HEADS_PER_GROUP
    pieces = [x_t[first + h:first + h + 1, block * CHUNK:(block + 1) * CHUNK] for h in range(HEADS_PER_GROUP)]
    return jnp.concatenate(pieces, axis=1)


def _scan_constants():
    c, w = CHUNK, GROUP_W
    row = np.arange(c)[:, None]
    col = (np.arange(w) % c)[None, :]
    masks = np.zeros((2, N_MASKS, c, w), np.float32)
    for d in range(2):
        masks[d, M_INCL] = (row >= col) if d == 0 else (row <= col)
        masks[d, M_STRICT] = (row > col) if d == 0 else (row < col)
        for p in range(LEVELS):
            same = (row >> (p + 1)) == (col >> (p + 1))
            r_bit = (row >> p) & 1
            c_bit = (col >> p) & 1
            masks[d, M_LEVEL0 + p] = same & ((r_bit == 1) & (c_bit == 0) if d == 0 else (r_bit == 0) & (c_bit == 1))
        masks[d, M_DIAG] = row == col
    bd = (np.arange(w)[:, None] // HEAD_DIM) == (np.arange(w)[None, :] // HEAD_DIM)
    exp = np.zeros((2, 2, 2, LANES, WIDTH), np.float32)
    lane_head = np.arange(WIDTH) // HEAD_DIM
    for br in range(2):
        for d in range(2):
            for kind in range(2):
                exp[br, d, kind, _gate_column(br, d, kind) + lane_head, np.arange(WIDTH)] = 1.0
    return (jnp.asarray(masks), jnp.asarray(bd.astype(np.float32)),
            jnp.asarray(bd.astype(np.float32)).astype(BF16), jnp.asarray(exp).astype(BF16))


def _block_diag(x_cat, bd_bf16):
    xb = x_cat.astype(BF16)
    return jnp.concatenate([xb] * HEADS_PER_GROUP, axis=0) * bd_bf16


def _gdn_local_kernel(q_ref, k_ref, v_ref, g_ref, cumf_ref, cumb_ref, masks_ref, bdb_ref, exp_ref,
                      uf_ref, wf_ref, qef_ref, inf_ref, kef_ref, cdf_ref,
                      ub_ref, wb_ref, qeb_ref, inb_ref, keb_ref, cdb_ref):
    c, w = CHUNK, GROUP_W
    bd_b = bdb_ref[...]
    outs = ((uf_ref, wf_ref, qef_ref, inf_ref, kef_ref, cdf_ref), (ub_ref, wb_ref, qeb_ref, inb_ref, keb_ref, cdb_ref))
    chains = [(ci, d, hg) for ci in range(q_ref.shape[1] // c) for d in range(2) for hg in range(N_HEAD_GROUPS)]
    rows = lambda ci: slice(ci * c, (ci + 1) * c)
    lanes = lambda hg: slice(hg * w, (hg + 1) * w)

    cums = (cumf_ref[0], cumb_ref[0])
    g_pieces = _split2(g_ref[0])
    gc_all = [_expand(_split3(cums[d]), exp_ref[0, d, 0]) for d in range(2)]
    beta_all = [_expand(g_pieces, exp_ref[0, d, 1]) for d in range(2)]
    per_block = LANES // c
    n_blocks = q_ref.shape[1] // LANES
    cum_t = [[cums[d][blk * LANES:(blk + 1) * LANES].T for blk in range(n_blocks)] for d in range(2)]
    gc = [gc_all[d][rows(ci), lanes(hg)] for ci, d, hg in chains]
    beta = [beta_all[d][rows(ci), lanes(hg)] for ci, d, hg in chains]
    gc_t = [_head_rows(cum_t[d][ci // per_block], _gate_column(0, d, 0), hg, ci % per_block) for ci, d, hg in chains]

    qk_kk = {}
    for ci, d, hg in chains:
        if d == 0:
            q = q_ref[0, rows(ci), lanes(hg)]
            k = k_ref[0, rows(ci), lanes(hg)]
            k_bd = jnp.concatenate([k] * HEADS_PER_GROUP, axis=0) * bd_b
            qk_kk[ci, hg] = _dot(jnp.concatenate([q, k], axis=0), k_bd, _NT)
    kbeta = [k_ref[0, rows(ci), lanes(hg)].astype(F32) * beta[i] for i, (ci, d, hg) in enumerate(chains)]

    a, t = [], []
    for i, (ci, d, hg) in enumerate(chains):
        decay = jnp.exp(jnp.minimum(gc[i] - gc_t[i], 0.0)) * masks_ref[d, M_INCL]
        outs[d][3][0, rows(ci), lanes(hg)] = (qk_kk[ci, hg][:c] * decay).astype(BF16)
        a.append(qk_kk[ci, hg][c:] * beta[i] * decay * masks_ref[d, M_STRICT])
        t.append(masks_ref[d, M_DIAG] - a[i] * masks_ref[d, M_LEVEL0])

    for p in range(1, LEVELS):
        y = [_mm(t[i], _block_diag(a[i] * masks_ref[d, M_LEVEL0 + p], bd_b)) for i, (_, d, _) in enumerate(chains)]
        t = [t[i] - _mm(y[i], _block_diag(t[i], bd_b)) for i in range(len(chains))]

    for i, (ci, d, hg) in enumerate(chains):
        u_ref, w_ref, qe_ref, _, ke_ref, cd_ref = outs[d]
        last = c - 1 if d == 0 else 0
        g_last = gc[i][last:last + 1, :]
        e_gc = jnp.exp(gc[i])
        v = v_ref[0, rows(ci), lanes(hg)].astype(F32)
        u_ref[0, rows(ci), lanes(hg)] = _mm(t[i], _block_diag(v * beta[i], bd_b))
        w_ref[0, rows(ci), lanes(hg)] = _mm(t[i], _block_diag(kbeta[i] * e_gc, bd_b)).astype(BF16)
        qe_ref[0, rows(ci), lanes(hg)] = (q_ref[0, rows(ci), lanes(hg)].astype(F32) * e_gc).astype(BF16)
        k = k_ref[0, rows(ci), lanes(hg)].astype(F32)
        ke_ref[0, rows(ci), lanes(hg)] = (k * jnp.exp(g_last - gc[i])).astype(BF16)
        cd_ref[0, ci, :, lanes(hg)] = jnp.exp(g_last)


def _gdn_local(q, k, v, gates, cumf, cumb, consts, tl):
    bsz, seq, width = q.shape
    masks, _, bd_b, exp = consts
    tile = lambda wd: pl.BlockSpec((1, tl, wd), lambda b, s: (b, s, 0))
    cd_spec = pl.BlockSpec((1, tl // CHUNK, 1, width), lambda b, s: (b, s, 0, 0))
    big = lambda dt: jax.ShapeDtypeStruct((bsz, seq, width), dt)
    cd_shape = jax.ShapeDtypeStruct((bsz, seq // CHUNK, 1, width), F32)
    per_dir_specs = [tile(width)] * 5 + [cd_spec]
    per_dir_shapes = [big(F32), big(BF16), big(BF16), big(BF16), big(BF16), cd_shape]
    return pl.pallas_call(
        _gdn_local_kernel,
        grid=(bsz, seq // tl),
        in_specs=[tile(width), tile(width), tile(width), tile(LANES), tile(LANES), tile(LANES),
                  _full(masks), _full(bd_b), _full(exp)],
        out_specs=per_dir_specs * 2,
        out_shape=per_dir_shapes * 2,
        compiler_params=_params(("parallel", "parallel")),
        name="gdn_local",
    )(q, k, v, gates, cumf, cumb, masks, bd_b, exp)


def _gdn_state_kernel(uf_ref, wf_ref, qef_ref, inf_ref, kef_ref, cdf_ref,
                      ub_ref, wb_ref, qeb_ref, inb_ref, keb_ref, cdb_ref, bdf_ref, bdb_ref, of_ref, ob_ref, s_ref):
    n = pl.program_id(1)

    @pl.when(n == 0)
    def _():
        s_ref[...] = jnp.zeros_like(s_ref)

    c, w = CHUNK, GROUP_W
    bd_b = bdb_ref[...]
    dirs = ((uf_ref, wf_ref, qef_ref, inf_ref, kef_ref, cdf_ref, of_ref),
            (ub_ref, wb_ref, qeb_ref, inb_ref, keb_ref, cdb_ref, ob_ref))
    nb = uf_ref.shape[0]
    chains = [(bi, d, hg) for bi in range(nb) for d in range(2) for hg in range(N_HEAD_GROUPS)]
    lanes = lambda hg: slice(hg * w, (hg + 1) * w)
    ws_qs = []
    for idx, (bi, d, hg) in enumerate(chains):
        w_ref, qe_ref = dirs[d][1], dirs[d][2]
        lhs = jnp.concatenate([w_ref[bi, :, lanes(hg)], qe_ref[bi, :, lanes(hg)]], axis=0)
        ws_qs.append(_dot(lhs, s_ref[idx].astype(BF16)))
    v_new = [(dirs[d][0][bi, :, lanes(hg)] - ws_qs[idx][:c]).astype(BF16) for idx, (bi, d, hg) in enumerate(chains)]
    for idx, (bi, d, hg) in enumerate(chains):
        in_ref, o_ref = dirs[d][3], dirs[d][6]
        o = ws_qs[idx][c:] + _dot(in_ref[bi, :, lanes(hg)], _block_diag(v_new[idx], bd_b))
        o_ref[bi, :, lanes(hg)] = o.astype(BF16)
    for idx, (bi, d, hg) in enumerate(chains):
        ke_ref, cd_ref = dirs[d][4], dirs[d][5]
        ds = _dot(ke_ref[bi, :, lanes(hg)], v_new[idx], _TN)
        s_ref[idx] = s_ref[idx] * cd_ref[bi, 0, :, lanes(hg)] + ds * bdf_ref[...]


def _gdn_state(local_outs, consts, nb):
    uf = local_outs[0]
    bsz, seq, width = uf.shape
    nc = seq // CHUNK
    _, bd_f, bd_b, _ = consts
    fwd = pl.BlockSpec((nb, CHUNK, width), lambda b, n: (b, n, 0))
    bwd = pl.BlockSpec((nb, CHUNK, width), lambda b, n: (b, nc - 1 - n, 0))
    cd_fwd = pl.BlockSpec((nb, 1, 1, width), lambda b, n: (b, n, 0, 0))
    cd_bwd = pl.BlockSpec((nb, 1, 1, width), lambda b, n: (b, nc - 1 - n, 0, 0))
    return pl.pallas_call(
        _gdn_state_kernel,
        grid=(bsz // nb, nc),
        in_specs=[fwd] * 5 + [cd_fwd] + [bwd] * 5 + [cd_bwd] + [_full(bd_f), _full(bd_b)],
        out_specs=[fwd, bwd],
        out_shape=[jax.ShapeDtypeStruct((bsz, seq, width), BF16)] * 2,
        scratch_shapes=[pltpu.VMEM((nb * 2 * N_HEAD_GROUPS, GROUP_W, GROUP_W), F32)],
        compiler_params=_params(("parallel", "arbitrary")),
        name="gdn_state",
    )(*local_outs, bd_f, bd_b)


def _mlstm_kernel(qf_ref, kf_ref, vf_ref, gf_ref, cf_ref, qb_ref, kb_ref, vb_ref, gb_ref, cb_ref,
                  masks_ref, bdf_ref, bdb_ref, exp_ref, of_ref, ob_ref, c_ref, n_ref, m_ref):
    n = pl.program_id(1)

    @pl.when(n == 0)
    def _():
        c_ref[...] = jnp.zeros_like(c_ref)
        n_ref[...] = jnp.zeros_like(n_ref)
        m_ref[...] = jnp.zeros_like(m_ref)

    c, w = CHUNK, GROUP_W
    bd_f = bdf_ref[...]
    bd_b = bdb_ref[...]
    ones_rows = jnp.ones((SUBLANES, c), BF16)
    row = _iota((c, w), 0)
    dirs = ((qf_ref, kf_ref, vf_ref, gf_ref, cf_ref, of_ref), (qb_ref, kb_ref, vb_ref, gb_ref, cb_ref, ob_ref))
    nb = qf_ref.shape[0]
    chains = [(bi, d, hg) for bi in range(nb) for d in range(2) for hg in range(N_HEAD_GROUPS)]
    lanes = lambda hg: slice(hg * w, (hg + 1) * w)
    gates = [jnp.concatenate([dirs[d][3][bi] for bi in range(nb)], axis=0) for d in range(2)]
    cums = [jnp.concatenate([dirs[d][4][bi] for bi in range(nb)], axis=0) for d in range(2)]
    i_all = [_expand(_split3(gates[d]), exp_ref[1, d, 0]) for d in range(2)]
    bcum_all = [_expand(_split3(cums[d]), exp_ref[1, d, 1]) for d in range(2)]
    per_block = LANES // c
    blocks = range(nb // per_block)
    block_t = lambda a: [a[blk * LANES:(blk + 1) * LANES].T for blk in blocks]
    gates_t = [block_t(gates[d]) for d in range(2)]
    cums_t = [block_t(cums[d]) for d in range(2)]
    rows = lambda bi: slice(bi * c, (bi + 1) * c)
    bcum = [bcum_all[d][rows(bi), lanes(hg)] for bi, d, hg in chains]
    r = [i_all[d][rows(bi), lanes(hg)] - bcum[i] for i, (bi, d, hg) in enumerate(chains)]
    r_t = [_head_rows(gates_t[d][bi // per_block], _gate_column(1, d, 0), hg, bi % per_block)
           - _head_rows(cums_t[d][bi // per_block], _gate_column(1, d, 1), hg, bi % per_block)
           for bi, d, hg in chains]
    qk = []
    for bi, d, hg in chains:
        k_bd = jnp.concatenate([dirs[d][1][bi, :, lanes(hg)]] * HEADS_PER_GROUP, axis=0) * bd_b
        qk.append(_dot(dirs[d][0][bi, :, lanes(hg)], k_bd, _NT))

    pm, mx, inter_w = [], [], []
    for i, (bi, d, hg) in enumerate(chains):
        m = r[i]
        for p in range(LEVELS):
            sh = 1 << p
            if d == 0:
                m = jnp.maximum(m, jnp.where(row >= sh, pltpu.roll(m, sh, 0), NEG))
            else:
                m = jnp.maximum(m, jnp.where(row < c - sh, pltpu.roll(m, c - sh, 0), NEG))
        pm.append(m)
        m_prev = m_ref[i, 0:1, :]
        mx.append(jnp.maximum(m_prev, m))
        inter_w.append(jnp.exp(m_prev - mx[i]))
        qk[i] = qk[i] * jnp.exp(jnp.where(masks_ref[d, M_INCL] > 0.0, r_t[i] - mx[i], NEG))

    num = [_mm(qk[i], _block_diag(dirs[d][2][bi, :, lanes(hg)], bd_b)) for i, (bi, d, hg) in enumerate(chains)]
    den = [_mm_lhs2(qk[i], bd_b) for i in range(len(chains))]
    qc = [_mm(dirs[d][0][bi, :, lanes(hg)], c_ref[i]) for i, (bi, d, hg) in enumerate(chains)]
    qn = [_mm(dirs[d][0][bi, :, lanes(hg)].astype(F32) * n_ref[i, 0:1, :], bd_b)
          for i, (bi, d, hg) in enumerate(chains)]
    for i, (bi, d, hg) in enumerate(chains):
        full_num = num[i] + inter_w[i] * qc[i]
        full_den = den[i] + inter_w[i] * qn[i]
        hidden = full_num / jnp.maximum(jnp.abs(full_den), jnp.exp(-(bcum[i] + mx[i])))
        dirs[d][5][bi, :, lanes(hg)] = hidden.astype(BF16)

    for i, (bi, d, hg) in enumerate(chains):
        last = c - 1 if d == 0 else 0
        m_prev = m_ref[i, 0:1, :]
        b_last = bcum[i][last:last + 1, :]
        m_next = b_last + jnp.maximum(m_prev, pm[i][last:last + 1, :])
        scale_prev = jnp.exp(b_last + m_prev - m_next)
        wgt = jnp.exp(b_last + r[i] - m_next)
        kw = (dirs[d][1][bi, :, lanes(hg)].astype(F32) * wgt).astype(BF16)
        c_ref[i] = c_ref[i] * scale_prev + _dot(kw, dirs[d][2][bi, :, lanes(hg)], _TN) * bd_f
        n_ref[i, 0:1, :] = n_ref[i, 0:1, :] * scale_prev + _dot(ones_rows, kw)[0:1, :]
        m_ref[i, 0:1, :] = m_next


def _mlstm_scan(q, k, v, gates, cumf, cumb, consts, nb):
    bsz, seq, width = q.shape
    nc = seq // CHUNK
    masks, bd_f, bd_b, exp = consts
    fwd = lambda wd: pl.BlockSpec((nb, CHUNK, wd), lambda b, n: (b, n, 0))
    bwd = lambda wd: pl.BlockSpec((nb, CHUNK, wd), lambda b, n: (b, nc - 1 - n, 0))
    n_state = nb * 2 * N_HEAD_GROUPS
    state = pltpu.VMEM((n_state, GROUP_W, GROUP_W), F32)
    row_state = pltpu.VMEM((n_state, SUBLANES, GROUP_W), F32)
    return pl.pallas_call(
        _mlstm_kernel,
        grid=(bsz // nb, nc),
        in_specs=[fwd(width), fwd(width), fwd(width), fwd(LANES), fwd(LANES),
                  bwd(width), bwd(width), bwd(width), bwd(LANES), bwd(LANES),
                  _full(masks), _full(bd_f), _full(bd_b), _full(exp)],
        out_specs=[fwd(width), bwd(width)],
        out_shape=[jax.ShapeDtypeStruct((bsz, seq, width), BF16)] * 2,
        scratch_shapes=[state, row_state, row_state],
        compiler_params=_params(("parallel", "arbitrary")),
        name="mlstm_scan",
    )(q, k, v, gates, cumf, q, k, v, gates, cumb, masks, bd_f, bd_b, exp)


def _outproj_kernel(h_ref, gaf_ref, gab_ref, mlf_ref, mlb_ref, az_ref, bo_ref, mg_ref, gt_ref, ng_ref, mlg_ref,
                    wa_ref, wb_ref, wo_ref, sh_ref, sc_ref, fg_ref, wr_ref, hout_ref, hn_ref, comb_ref):
    ones_bd = _head_ones(WIDTH)
    inv = 1.0 / HEAD_DIM
    tm = h_ref.shape[1]
    n_parts = 2 if tm % (2 * BF16_ROWS) == 0 else 1
    parts = [slice(i * tm // n_parts, (i + 1) * tm // n_parts) for i in range(n_parts)]
    f32 = lambda ref, rs: ref[0, rs, :].astype(F32)

    def branch_inputs(rs):
        oa = f32(gaf_ref, rs) + f32(gab_ref, rs)
        ms = _mm(oa * oa, ones_bd) * inv
        oa = oa * lax.rsqrt(ms + EPS) * ng_ref[...] * _silu(f32(az_ref, rs))
        hb = f32(mlf_ref, rs) + f32(mlb_ref, rs)
        mu = _mm_lhs2(hb, ones_bd) * inv
        tc = hb - mu
        var = _mm(tc * tc, ones_bd) * inv
        hb = tc * lax.rsqrt(var + EPS) * mlg_ref[...] * _sigmoid(f32(bo_ref, rs))
        return oa.astype(BF16), hb.astype(BF16)

    def merged(rs, oa, hb):
        gate = _sigmoid(f32(mg_ref, rs))
        return gate[:, :D_MODEL] * _dot(oa, wa_ref[...]) + gate[:, D_MODEL:] * _dot(hb, wb_ref[...])

    def residual(rs, y):
        h_new = h_ref[0, rs, :] + gt_ref[0] * _mm(y, wo_ref[...])
        hout_ref[0, rs, :] = h_new
        ms2 = jnp.mean(h_new * h_new, axis=-1, keepdims=True)
        hn = h_new * lax.rsqrt(ms2 + EPS) * (fg_ref[...] * (1.0 + sc_ref[0])) + sh_ref[0]
        hn_ref[0, rs, :] = hn.astype(BF16)
        return hn

    def routing(rs, logits):
        lane = _iota(logits.shape, 1)
        lane_f = lane.astype(F32)
        big = 1e9
        gl = jnp.where((lane >= N_EXPERTS) & (lane < N_EXPERTS + N_GROUPS), logits, NEG)
        gmax = jnp.max(gl, axis=-1, keepdims=True)
        gidx = jnp.min(jnp.where(gl == gmax, lane_f - N_EXPERTS, big), axis=-1, keepdims=True)
        p_group = 1.0 / jnp.sum(jnp.exp(gl - gmax), axis=-1, keepdims=True)
        el = jnp.where((lane < N_EXPERTS) & ((lane >> 3).astype(F32) == gidx), logits, NEG)
        v1 = jnp.max(el, axis=-1, keepdims=True)
        i1 = jnp.min(jnp.where(el == v1, lane_f, big), axis=-1, keepdims=True)
        el2 = jnp.where(lane_f == i1, NEG, el)
        v2 = jnp.max(el2, axis=-1, keepdims=True)
        i2 = jnp.min(jnp.where(el2 == v2, lane_f, big), axis=-1, keepdims=True)
        e21 = jnp.exp(v2 - v1)
        w1 = p_group / (1.0 + e21)
        route = jnp.where(lane == R_E1, i1, 0.0) + jnp.where(lane == R_E2, i2, 0.0)
        comb_ref[0, rs, :] = route + jnp.where(lane == R_W1, w1, 0.0) + jnp.where(lane == R_W2, w1 * e21, 0.0)

    branches = [branch_inputs(rs) for rs in parts]
    ys = [merged(rs, *branches[i]) for i, rs in enumerate(parts)]
    hns = [residual(rs, ys[i]) for i, rs in enumerate(parts)]
    logits = []
    for hn in hns:
        hi, lo = _split2(hn)
        both = _dot(jnp.concatenate([hi, lo], axis=1), wr_ref[...])
        logits.append(both[:, :LANES] + both[:, LANES:])
    for i, rs in enumerate(parts):
        routing(rs, logits[i])


def _output_projection(h, gaf, gab, mlf, mlb, az, bo, mg, gt, ng, mlg, wa, wb, wo, sh, sc, fg, wr, tm):
    bsz, seq, d = h.shape
    tile = lambda w: pl.BlockSpec((1, tm, w), lambda b, s: (b, s, 0))
    per_b = pl.BlockSpec((1, 1, d), lambda b, s: (b, 0, 0))
    return pl.pallas_call(
        _outproj_kernel,
        grid=(bsz, seq // tm),
        in_specs=[tile(d), tile(WIDTH), tile(WIDTH), tile(WIDTH), tile(WIDTH), tile(WIDTH), tile(WIDTH),
                  tile(2 * D_MODEL), per_b, _full(ng), _full(mlg), _full(wa), _full(wb), _full(wo), per_b, per_b,
                  _full(fg), _full(wr)],
        out_specs=[tile(d), tile(d), tile(LANES)],
        out_shape=[jax.ShapeDtypeStruct((bsz, seq, d), F32), jax.ShapeDtypeStruct((bsz, seq, d), BF16),
                   jax.ShapeDtypeStruct((bsz, seq, LANES), F32)],
        compiler_params=_params(("parallel", "parallel")),
        name="output_projection",
    )(h, gaf, gab, mlf, mlb, az, bo, mg, gt, ng, mlg, wa, wb, wo, sh, sc, fg, wr)


def _moe_kernel(npass_ref, acc_ref, hn_ref, route_ref, gt_ref, wg_ref, wu_ref, wd_ref, fin_ref, tri_ref,
                o_ref, y_ref, *, group, final_norm):
    g = group
    tile = pl.program_id(0) * pl.num_programs(1) + pl.program_id(1)
    tm = hn_ref.shape[1]
    cap = MOE_CAP
    sub = MOE_SUB
    n_sub = tm // sub
    route = route_ref[0]
    lane = _iota(route.shape, 1)
    lane_f = lane.astype(F32)
    column = lambda j: jnp.sum(jnp.where(lane == j, route, 0.0), axis=-1, keepdims=True)
    first = float(g * EXPERTS_PER_GROUP)
    l1 = column(R_E1) - first
    l2 = column(R_E2) - first
    w1 = column(R_W1)
    w2 = column(R_W2)
    in1 = (l1 >= 0.0) & (l1 < EXPERTS_PER_GROUP)
    in2 = (l2 >= 0.0) & (l2 < EXPERTS_PER_GROUP)
    hit1 = in1 & (lane_f == l1)
    hit2 = in2 & (lane_f == l2)
    onehot = (jnp.where(hit1, 1.0, 0.0) + jnp.where(hit2, 1.0, 0.0)).astype(BF16)
    rows = lambda s: slice(s * sub, (s + 1) * sub)
    before = jnp.concatenate([_dot(tri_ref[...], onehot[rows(s)]) for s in range(n_sub)], axis=0)
    rank1 = jnp.sum(jnp.where(hit1, before, 0.0), axis=-1, keepdims=True)
    rank2 = jnp.sum(jnp.where(hit2, before, 0.0), axis=-1, keepdims=True)
    slot_lane = _iota((tm, MOE_SLOTS), 1).astype(F32)
    x = hn_ref[0]
    experts = range(EXPERTS_PER_GROUP)
    chains = [range(c0, c0 + MOE_CHAIN_SUBS) for c0 in range(0, n_sub, MOE_CHAIN_SUBS)]

    def one_pass(p, first):
        lo = p * float(cap) if first else (p * cap).astype(F32)
        s1 = jnp.where(in1 & (rank1 >= lo) & (rank1 < lo + cap), l1 * cap + rank1 - lo, -1.0)
        s2 = jnp.where(in2 & (rank2 >= lo) & (rank2 < lo + cap), l2 * cap + rank2 - lo, -1.0)
        m1 = slot_lane == s1
        m2 = slot_lane == s2
        sel_all = jnp.where(m1 | m2, 1.0, 0.0).astype(BF16)
        sel_w_all = (jnp.where(m1, w1, 0.0) + jnp.where(m2, w2, 0.0)).astype(BF16)
        sel = {s: sel_all[rows(s)] for s in range(n_sub)}
        sel_w = {s: sel_w_all[rows(s)] for s in range(n_sub)}
        xs = {s: _dot(sel[s], x[rows(s)], _TN) for subs in chains for s in subs}
        xe = [[jnp.concatenate([xs[s][e * cap:(e + 1) * cap] for s in subs], axis=0).astype(BF16) for e in experts]
              for subs in chains]
        hg = [[_dot(xe[c][e], wg_ref[e]) for e in experts] for c in range(len(chains))]
        hu = [[_dot(xe[c][e], wu_ref[e]) for e in experts] for c in range(len(chains))]
        act = [[(_silu(hg[c][e]) * hu[c][e]).astype(BF16) for e in experts] for c in range(len(chains))]
        ye = [[_dot(act[c][e], wd_ref[e]) for e in experts] for c in range(len(chains))]
        pad = jnp.zeros((MOE_SLOTS - EXPERTS_PER_GROUP * cap, x.shape[1]), F32)
        for c, subs in enumerate(chains):
            for i, s in enumerate(subs):
                ys = jnp.concatenate([ye[c][e][i * cap:(i + 1) * cap] for e in experts] + [pad], axis=0).astype(BF16)
                if first:
                    y_ref[rows(s), :] = _dot(sel_w[s], ys)
                else:
                    y_ref[rows(s), :] += _dot(sel_w[s], ys)

    one_pass(0, True)

    def extra_pass(p, carry):
        one_pass(p, False)
        return carry

    lax.fori_loop(1, npass_ref[tile], extra_pass, 0)
    out = acc_ref[0] + gt_ref[0] * y_ref[...]
    if final_norm:
        ms = jnp.mean(out * out, axis=-1, keepdims=True)
        out = out * lax.rsqrt(ms + EPS) * fin_ref[...]
    o_ref[0] = out


def _moe(h, hn, route, gt, wg, wu, wd, fin, tm, final_norm):
    bsz, seq, d = h.shape
    nt = seq // tm
    n_sub = tm // MOE_SUB
    ids = route[..., (R_E1, R_E2)].astype(jnp.int32).reshape(bsz * nt, n_sub, MOE_SUB * 2)
    counts = jnp.sum(jax.nn.one_hot(ids, N_EXPERTS, dtype=jnp.int32), axis=2)
    most = jnp.max(counts.reshape(bsz * nt, n_sub, N_GROUPS, EXPERTS_PER_GROUP), axis=(1, 3))
    npass = ((most + MOE_CAP - 1) // MOE_CAP).T.astype(jnp.int32)
    tri = jnp.asarray(np.tril(np.ones((MOE_SUB, MOE_SUB), np.float32), -1)).astype(BF16)

    tile = lambda w: pl.BlockSpec((1, tm, w), lambda b, s, n: (b, s, 0))
    for g in range(N_GROUPS):
        group = lambda shape: pl.BlockSpec(shape, lambda b, s, n, g=g: (g, 0, 0))
        grid_spec = pltpu.PrefetchScalarGridSpec(
            num_scalar_prefetch=1,
            grid=(bsz, nt),
            in_specs=[tile(d), tile(d), tile(LANES), pl.BlockSpec((1, 1, d), lambda b, s, n: (b, 0, 0)),
                      group((EXPERTS_PER_GROUP, d, D_EXPERT)), group((EXPERTS_PER_GROUP, d, D_EXPERT)),
                      group((EXPERTS_PER_GROUP, D_EXPERT, d)),
                      pl.BlockSpec((1, d), lambda b, s, n: (0, 0)), pl.BlockSpec(tri.shape, lambda b, s, n: (0, 0))],
            out_specs=tile(d),
            scratch_shapes=[pltpu.VMEM((tm, d), F32)],
        )
        h = pl.pallas_call(
            functools.partial(_moe_kernel, group=g, final_norm=final_norm and g == N_GROUPS - 1),
            grid_spec=grid_spec,
            out_shape=jax.ShapeDtypeStruct((bsz, seq, d), F32),
            compiler_params=_params(("parallel", "parallel")),
            name="expert_ffn",
        )(npass[g], h, hn, route, gt, wg, wu, wd, fin, tri)
    return h


def _pick_tile(seq, want):
    tm = min(seq, want)
    assert seq % tm == 0 and tm % CHUNK == 0
    return tm


def kernel(x, c, ada_w, ada_b, norm_mix_g, norm_ffn_g, w_in, gdn_conv_w, gdn_a_log, gdn_dt_bias, gdn_norm_g, mlstm_conv_w, mlstm_i_bias, mlstm_f_bias, mlstm_norm_g, w_branch_a, w_branch_b, w_out, router_group, router_expert, w_gate, w_up, w_down, final_norm_g):
    bsz, seq, d = x.shape
    depth = ada_w.shape[0]
    assert d == D_MODEL and seq % CHUNK == 0
    tm_in = _pick_tile(seq, 512)
    tm_out = _pick_tile(seq, 512)
    tm_moe = _pick_tile(seq, MOE_CHAIN_SUBS * MOE_SUB)
    tl_gdn = _pick_tile(seq, LANES)
    nb_scan = LANES // CHUNK
    assert tl_gdn % LANES == 0 and bsz % nb_scan == 0
    row = lambda a: a.reshape(1, -1).astype(F32)
    consts = _scan_constants()

    mod = _modulation(c.astype(F32), ada_w.astype(F32), ada_b.astype(F32))
    h = x.astype(F32)
    for l in range(depth):
        sh1, sc1, gt1, sh2, sc2, gt2 = [mod[l, :, i * d:(i + 1) * d].reshape(bsz, 1, d) for i in range(N_MOD)]

        w = w_in[l].astype(F32)
        o_z = 3 * WIDTH
        o_ag = o_z + WIDTH
        o_bqk = o_ag + 4 * N_HEADS
        o_bv = o_bqk + 2 * WIDTH
        o_bo = o_bv + WIDTH
        o_bg = o_bo + WIDTH
        o_mg = o_bg + 4 * N_HEADS
        wconv = jnp.concatenate([w[:, :o_z], w[:, o_bqk:o_bv]], axis=1).astype(BF16)
        wg = jnp.concatenate([w[:, o_ag:o_bqk], w[:, o_bg:o_mg], jnp.zeros((d, LANES - 8 * N_HEADS), F32)], axis=1)
        wg_hi = wg.astype(BF16)
        wg_lo = (wg - wg_hi.astype(F32)).astype(BF16)
        wrest = jnp.concatenate([w[:, o_z:o_ag].astype(BF16), w[:, o_bv:o_bo].astype(BF16), w[:, o_bo:o_bg].astype(BF16),
                                 w[:, o_mg:].astype(BF16), wg_hi, wg_lo], axis=1)
        cw = jnp.concatenate([gdn_conv_w[l], mlstm_conv_w[l]], axis=1).astype(F32)
        cw = jnp.concatenate([cw, jnp.zeros((SUBLANES - CONV_K, CONV_W), F32)], axis=0)
        zero8 = jnp.zeros((N_HEADS,), F32)
        bias = jnp.concatenate([gdn_dt_bias[l, 0], zero8, gdn_dt_bias[l, 1], zero8,
                                mlstm_i_bias[l, 0], mlstm_f_bias[l, 0], mlstm_i_bias[l, 1], mlstm_f_bias[l, 1],
                                jnp.zeros((LANES - 8 * N_HEADS,), F32)]).astype(F32)
        alog = jnp.concatenate([gdn_a_log[l, 0], zero8, gdn_a_log[l, 1], zero8,
                                jnp.zeros((LANES - 4 * N_HEADS,), F32)]).astype(F32)
        gp = jnp.concatenate([bias[None], alog[None], jnp.zeros((SUBLANES - 2, LANES), F32)], axis=0)

        aq, ak, av, bq, bk, az, bv, bo, mg, gates, cumf, cumb = _input_projection(
            h, sh1, sc1, row(norm_mix_g[l]), wconv, wrest, cw, gp, tm_in)
        nb_state = 4 if bsz % 4 == 0 else nb_scan
        gaf, gab = _gdn_state(_gdn_local(aq, ak, av, gates, cumf, cumb, consts, tl_gdn), consts, nb_state)
        mlf, mlb = _mlstm_scan(bq, bk, bv, gates, cumf, cumb, consts, nb_state)

        wr = jnp.concatenate([router_expert[l].astype(F32), router_group[l].astype(F32),
                              jnp.zeros((d, LANES - N_EXPERTS - N_GROUPS), F32)], axis=1)
        wr_hi = wr.astype(BF16)
        wr_lo = (wr - wr_hi.astype(F32)).astype(BF16)
        wr = jnp.concatenate([jnp.concatenate([wr_hi, wr_lo], axis=1),
                              jnp.concatenate([wr_hi, jnp.zeros_like(wr_hi)], axis=1)], axis=0)
        h, hn2, comb = _output_projection(
            h, gaf, gab, mlf, mlb, az, bo, mg, gt1, row(jnp.tile(gdn_norm_g[l], N_HEADS)), row(mlstm_norm_g[l]),
            w_branch_a[l].astype(BF16), w_branch_b[l].astype(BF16), w_out[l].astype(BF16),
            sh2, sc2, row(norm_ffn_g[l]), wr, tm_out)
        h = _moe(h, hn2, comb, gt2, w_gate[l].astype(BF16), w_up[l].astype(BF16), w_down[l].astype(BF16),
                 row(final_norm_g), tm_moe, final_norm=(l == depth - 1))
    return h.astype(x.dtype)
```

```python
import functools

import numpy as np
import jax
import jax.numpy as jnp
from jax import lax
from jax.experimental import pallas as pl
from jax.experimental.pallas import tpu as pltpu

D_MODEL = 1024
HEAD_DIM = 64
N_HEADS = 8
WIDTH = N_HEADS * HEAD_DIM
CONV_K = 5
CONV_W = 3 * WIDTH + 2 * WIDTH
REST_W = 3 * WIDTH + 2 * D_MODEL
N_GROUPS = 4
EXPERTS_PER_GROUP = 8
N_EXPERTS = N_GROUPS * EXPERTS_PER_GROUP
D_EXPERT = D_MODEL // 4
N_MOD = 6
EPS = 1e-6
NEG = -1e30

CHUNK = 64
HEADS_PER_GROUP = 2
N_HEAD_GROUPS = N_HEADS // HEADS_PER_GROUP
GROUP_W = HEADS_PER_GROUP * HEAD_DIM
LEVELS = 6
LANES = 128
SUBLANES = 8
BF16_ROWS = 16
VMEM_LIMIT = 56 * 1024 * 1024

F32 = jnp.float32
BF16 = jnp.bfloat16

_NN = (((1,), (0,)), ((), ()))
_NT = (((1,), (1,)), ((), ()))
_TN = (((0,), (0,)), ((), ()))

R_E1, R_E2, R_W1, R_W2 = 0, 1, 4, 5
MOE_SUB = 128
MOE_CAP = 24
MOE_CHAIN_SUBS = 4
MOE_SLOTS = 256

M_INCL, M_STRICT, M_LEVEL0, M_DIAG = 0, 1, 2, 2 + LEVELS
N_MASKS = M_DIAG + 1


def _dot(a, b, dims=_NN):
    return lax.dot_general(a, b, dims, preferred_element_type=F32)


def _mm(a, b, dims=_NN):
    return _dot(a.astype(BF16), b.astype(BF16), dims)


def _split2(x):
    hi = x.astype(BF16)
    lo = (x - hi.astype(F32)).astype(BF16)
    return hi, lo


def _split3(x):
    hi = x.astype(BF16)
    r = x - hi.astype(F32)
    mid = r.astype(BF16)
    lo = (r - mid.astype(F32)).astype(BF16)
    return hi, mid, lo


def _dot_pieces(pieces, e):
    out = _dot(pieces[0], e)
    for p in pieces[1:]:
        out = out + _dot(p, e)
    return out


def _mm_lhs3(x, e):
    return _dot_pieces(_split3(x), e)


def _mm_rhs3(e, x):
    hi, mid, lo = _split3(x)
    return _dot(e, hi) + _dot(e, mid) + _dot(e, lo)


def _mm_lhs2(x, e):
    return _dot_pieces(_split2(x), e)


def _mm3(a, b):
    a_hi, a_lo = _split2(a)
    b_hi, b_lo = _split2(b)
    return _dot(a_hi, b_hi) + _dot(a_hi, b_lo) + _dot(a_lo, b_hi)


def _iota(shape, dim):
    return lax.broadcasted_iota(jnp.int32, shape, dim)


def _sigmoid(x):
    return 0.5 * jnp.tanh(0.5 * x) + 0.5


def _silu(x):
    return x * _sigmoid(x)


def _softplus(x):
    return jnp.maximum(x, 0.0) + jnp.log1p(jnp.exp(-jnp.abs(x)))


def _head_ones(width):
    r = _iota((width, width), 0) >> 6
    c = _iota((width, width), 1) >> 6
    return jnp.where(r == c, 1.0, 0.0).astype(BF16)


def _params(sem):
    return pltpu.CompilerParams(dimension_semantics=sem, vmem_limit_bytes=VMEM_LIMIT)


def _full(a):
    return pl.BlockSpec(a.shape, lambda *_: (0,) * a.ndim)


def _mod_kernel(c_ref, w_ref, b_ref, o_ref):
    o_ref[0] = _mm3(_silu(c_ref[...]), w_ref[0]) + b_ref[0]


def _modulation(c, ada_w, ada_b):
    depth, d, n = ada_w.shape
    bsz = c.shape[0]
    tn = n // 4
    return pl.pallas_call(
        _mod_kernel,
        grid=(depth, n // tn),
        in_specs=[
            pl.BlockSpec((bsz, d), lambda l, j: (0, 0)),
            pl.BlockSpec((1, d, tn), lambda l, j: (l, 0, j)),
            pl.BlockSpec((1, 1, tn), lambda l, j: (l, 0, j)),
        ],
        out_specs=pl.BlockSpec((1, bsz, tn), lambda l, j: (l, 0, j)),
        out_shape=jax.ShapeDtypeStruct((depth, bsz, n), F32),
        compiler_params=_params(("parallel", "parallel")),
        name="adaln_mod",
    )(c, ada_w, ada_b.reshape(depth, 1, n))


HALO = BF16_ROWS


def _inproj_kernel(h_ref, hprev_ref, hnext_ref, sh_ref, sc_ref, g_ref, wconv_ref, wrest_ref, cw_ref, gp_ref,
                   perm_ref, permt_ref, aq_ref, ak_ref, av_ref, bq_ref, bk_ref, az_ref, bv_ref, bo_ref, mg_ref, gates_ref, cumf_ref,
                   cumb_ref):
    s = pl.program_id(1)
    ns = pl.num_programs(1)
    tm = h_ref.shape[1]
    gain = g_ref[...] * (1.0 + sc_ref[0])
    shift = sh_ref[0]

    def norm(x):
        ms = jnp.mean(x * x, axis=-1, keepdims=True)
        return x * lax.rsqrt(ms + EPS) * gain + shift

    hn = norm(h_ref[0])
    hp = norm(hprev_ref[0]) * jnp.where(s > 0, 1.0, 0.0)
    hx = norm(hnext_ref[0]) * jnp.where(s < ns - 1, 1.0, 0.0)
    hn_hi = hn.astype(BF16)

    nv = tm // SUBLANES
    edge = CONV_K // 2
    hn_perm = _dot(perm_ref[...], hn_hi).astype(BF16)
    hall = jnp.concatenate([hn_perm, hp.astype(BF16), hx.astype(BF16)], axis=0)
    groups = range(CONV_W // WIDTH)
    cols = lambda g: slice(g * WIDTH, (g + 1) * WIDTH)
    pe = [_dot(hall, wconv_ref[:, cols(g)]) for g in groups]
    rest = _dot(hn_hi, wrest_ref[...])
    hn_lo = (hn - hn_hi.astype(F32)).astype(BF16)
    pre = (rest[:, REST_W:REST_W + LANES] + rest[:, REST_W + LANES:]
           + _dot(hn_lo, wrest_ref[:, REST_W:REST_W + LANES]))

    sub = _iota((SUBLANES, WIDTH), 0)
    ones_bd = _head_ones(WIDTH)
    scale = HEAD_DIM ** -0.5
    out_refs = (aq_ref, ak_ref, av_ref, bq_ref, bk_ref)
    permuted = []
    for g in groups:
        prev = pe[g][tm:tm + HALO]
        nxt = pe[g][tm + HALO:]
        block = lambda v: pe[g][v * SUBLANES:(v + 1) * SUBLANES]
        before = [jnp.where(sub == 0, prev[HALO - edge + i:HALO - edge + i + 1], pltpu.roll(block(nv - edge + i), 1, 0))
                  for i in range(edge)]
        after = [jnp.where(sub == SUBLANES - 1, nxt[i:i + 1], pltpu.roll(block(i), SUBLANES - 1, 0))
                 for i in range(edge)]
        ext = jnp.concatenate(before + [pe[g][:tm]] + after, axis=0)

        conv = jnp.zeros((tm, WIDTH), F32)
        for j in range(CONV_K):
            conv = conv + ext[j * SUBLANES:j * SUBLANES + tm] * cw_ref[j:j + 1, cols(g)]
        cv = _silu(conv)
        if g in (0, 1):
            cv = cv * lax.rsqrt(_mm(cv * cv, ones_bd) + EPS)
        if g in (0, 4):
            cv = cv * scale
        permuted.append(cv.astype(BF16))
    for g in groups:
        out_refs[g][0] = _dot(permt_ref[...], permuted[g]).astype(BF16)

    az_ref[0] = rest[:, 0:WIDTH].astype(BF16)
    bv_ref[0] = rest[:, WIDTH:2 * WIDTH].astype(BF16)
    bo_ref[0] = rest[:, 2 * WIDTH:3 * WIDTH].astype(BF16)
    mg_ref[0] = rest[:, 3 * WIDTH:REST_W].astype(BF16)

    pre = pre + gp_ref[0:1, :]
    lane = _iota(pre.shape, 1)
    second = ((lane >> 3) & 1) == 1
    is_gdn = lane < 4 * N_HEADS
    is_ml = (lane >= 4 * N_HEADS) & (lane < 8 * N_HEADS)
    sp = _softplus(pre)
    gdn_val = jnp.where(second, _sigmoid(pre), -jnp.exp(gp_ref[1:2, :]) * sp)
    ml_val = jnp.where(second, -_softplus(-pre), pre)
    gates = jnp.where(is_gdn, gdn_val, jnp.where(is_ml, ml_val, 0.0))
    gates_ref[0] = gates

    ri = _iota((LANES, LANES), 0)
    ci = _iota((LANES, LANES), 1)
    same_chunk = (ri >> 6) == (ci >> 6)
    pieces = _split3(gates)
    for ref, tri in ((cumf_ref, same_chunk & (ri >= ci)), (cumb_ref, same_chunk & (ri <= ci))):
        tri_b = jnp.where(tri, 1.0, 0.0).astype(BF16)
        for blk in range(tm // LANES):
            rs = slice(blk * LANES, (blk + 1) * LANES)
            ref[0, rs, :] = _dot(tri_b, pieces[0][rs]) + _dot(tri_b, pieces[1][rs]) + _dot(tri_b, pieces[2][rs])


def _input_projection(h, sh, sc, gain, wconv, wrest, cw, gp, tm):
    bsz, seq, d = h.shape
    nt = seq // tm
    per = tm // HALO
    nhalo = seq // HALO
    tile = lambda w: pl.BlockSpec((1, tm, w), lambda b, s: (b, s, 0))
    outs = [(WIDTH, BF16)] * 8 + [(2 * D_MODEL, BF16)] + [(LANES, F32)] * 3
    r = np.arange(tm)
    perm_np = np.zeros((tm, tm), np.float32)
    perm_np[r, (r % SUBLANES) * (tm // SUBLANES) + r // SUBLANES] = 1.0
    perm = jnp.asarray(perm_np).astype(BF16)
    perm_t = jnp.asarray(perm_np.T).astype(BF16)
    return pl.pallas_call(
        _inproj_kernel,
        grid=(bsz, nt),
        in_specs=[
            tile(d),
            pl.BlockSpec((1, HALO, d), lambda b, s: (b, jnp.maximum(s * per - 1, 0), 0)),
            pl.BlockSpec((1, HALO, d), lambda b, s: (b, jnp.minimum((s + 1) * per, nhalo - 1), 0)),
            pl.BlockSpec((1, 1, d), lambda b, s: (b, 0, 0)),
            pl.BlockSpec((1, 1, d), lambda b, s: (b, 0, 0)),
            _full(gain), _full(wconv), _full(wrest), _full(cw), _full(gp), _full(perm), _full(perm_t),
        ],
        out_specs=[tile(w) for w, _ in outs],
        out_shape=[jax.ShapeDtypeStruct((bsz, seq, w), dt) for w, dt in outs],
        compiler_params=_params(("parallel", "parallel")),
        name="input_projection",
    )(h, h, h, sh, sc, gain, wconv, wrest, cw, gp, perm, perm_t)


def _gate_column(branch, direction, kind):
    return branch * 4 * N_HEADS + direction * 2 * N_HEADS + kind * N_HEADS


def _expand(pieces, e):
    rows = pieces[0].shape[0]
    out = _dot(jnp.concatenate(pieces, axis=0), e)
    return functools.reduce(lambda a, b: a + b, [out[i * rows:(i + 1) * rows] for i in range(len(pieces))])


def _head_rows(x_t, column, hg, block):
    first = column + hg * HEADS_PER_GROUP
    pieces = [x_t[first + h:first + h + 1, block * CHUNK:(block + 1) * CHUNK] for h in range(HEADS_PER_GROUP)]
    return jnp.concatenate(pieces, axis=1)


def _scan_constants():
    c, w = CHUNK, GROUP_W
    row = np.arange(c)[:, None]
    col = (np.arange(w) % c)[None, :]
    masks = np.zeros((2, N_MASKS, c, w), np.float32)
    for d in range(2):
        masks[d, M_INCL] = (row >= col) if d == 0 else (row <= col)
        masks[d, M_STRICT] = (row > col) if d == 0 else (row < col)
        for p in range(LEVELS):
            same = (row >> (p + 1)) == (col >> (p + 1))
            r_bit = (row >> p) & 1
            c_bit = (col >> p) & 1
            masks[d, M_LEVEL0 + p] = same & ((r_bit == 1) & (c_bit == 0) if d == 0 else (r_bit == 0) & (c_bit == 1))
        masks[d, M_DIAG] = row == col
    bd = (np.arange(w)[:, None] // HEAD_DIM) == (np.arange(w)[None, :] // HEAD_DIM)
    exp = np.zeros((2, 2, 2, LANES, WIDTH), np.float32)
    lane_head = np.arange(WIDTH) // HEAD_DIM
    for br in range(2):
        for d in range(2):
            for kind in range(2):
                exp[br, d, kind, _gate_column(br, d, kind) + lane_head, np.arange(WIDTH)] = 1.0
    return (jnp.asarray(masks), jnp.asarray(bd.astype(np.float32)),
            jnp.asarray(bd.astype(np.float32)).astype(BF16), jnp.asarray(exp).astype(BF16))


def _block_diag(x_cat, bd_bf16):
    xb = x_cat.astype(BF16)
    return jnp.concatenate([xb] * HEADS_PER_GROUP, axis=0) * bd_bf16


def _gdn_local_kernel(q_ref, k_ref, v_ref, g_ref, cumf_ref, cumb_ref, masks_ref, bdb_ref, exp_ref,
                      uf_ref, wf_ref, qef_ref, inf_ref, kef_ref, cdf_ref,
                      ub_ref, wb_ref, qeb_ref, inb_ref, keb_ref, cdb_ref):
    c, w = CHUNK, GROUP_W
    bd_b = bdb_ref[...]
    outs = ((uf_ref, wf_ref, qef_ref, inf_ref, kef_ref, cdf_ref), (ub_ref, wb_ref, qeb_ref, inb_ref, keb_ref, cdb_ref))
    chains = [(ci, d, hg) for ci in range(q_ref.shape[1] // c) for d in range(2) for hg in range(N_HEAD_GROUPS)]
    rows = lambda ci: slice(ci * c, (ci + 1) * c)
    lanes = lambda hg: slice(hg * w, (hg + 1) * w)

    cums = (cumf_ref[0], cumb_ref[0])
    g_pieces = _split2(g_ref[0])
    gc_all = [_expand(_split3(cums[d]), exp_ref[0, d, 0]) for d in range(2)]
    beta_all = [_expand(g_pieces, exp_ref[0, d, 1]) for d in range(2)]
    per_block = LANES // c
    n_blocks = q_ref.shape[1] // LANES
    cum_t = [[cums[d][blk * LANES:(blk + 1) * LANES].T for blk in range(n_blocks)] for d in range(2)]
    gc = [gc_all[d][rows(ci), lanes(hg)] for ci, d, hg in chains]
    beta = [beta_all[d][rows(ci), lanes(hg)] for ci, d, hg in chains]
    gc_t = [_head_rows(cum_t[d][ci // per_block], _gate_column(0, d, 0), hg, ci % per_block) for ci, d, hg in chains]

    qk_kk = {}
    for ci, d, hg in chains:
        if d == 0:
            q = q_ref[0, rows(ci), lanes(hg)]
            k = k_ref[0, rows(ci), lanes(hg)]
            k_bd = jnp.concatenate([k] * HEADS_PER_GROUP, axis=0) * bd_b
            qk_kk[ci, hg] = _dot(jnp.concatenate([q, k], axis=0), k_bd, _NT)
    kbeta = [k_ref[0, rows(ci), lanes(hg)].astype(F32) * beta[i] for i, (ci, d, hg) in enumerate(chains)]

    a, t = [], []
    for i, (ci, d, hg) in enumerate(chains):
        decay = jnp.exp(jnp.minimum(gc[i] - gc_t[i], 0.0)) * masks_ref[d, M_INCL]
        outs[d][3][0, rows(ci), lanes(hg)] = (qk_kk[ci, hg][:c] * decay).astype(BF16)
        a.append(qk_kk[ci, hg][c:] * beta[i] * decay * masks_ref[d, M_STRICT])
        t.append(masks_ref[d, M_DIAG] - a[i] * masks_ref[d, M_LEVEL0])

    for p in range(1, LEVELS):
        y = [_mm(t[i], _block_diag(a[i] * masks_ref[d, M_LEVEL0 + p], bd_b)) for i, (_, d, _) in enumerate(chains)]
        t = [t[i] - _mm(y[i], _block_diag(t[i], bd_b)) for i in range(len(chains))]

    for i, (ci, d, hg) in enumerate(chains):
        u_ref, w_ref, qe_ref, _, ke_ref, cd_ref = outs[d]
        last = c - 1 if d == 0 else 0
        g_last = gc[i][last:last + 1, :]
        e_gc = jnp.exp(gc[i])
        v = v_ref[0, rows(ci), lanes(hg)].astype(F32)
        u_ref[0, rows(ci), lanes(hg)] = _mm(t[i], _block_diag(v * beta[i], bd_b))
        w_ref[0, rows(ci), lanes(hg)] = _mm(t[i], _block_diag(kbeta[i] * e_gc, bd_b)).astype(BF16)
        qe_ref[0, rows(ci), lanes(hg)] = (q_ref[0, rows(ci), lanes(hg)].astype(F32) * e_gc).astype(BF16)
        k = k_ref[0, rows(ci), lanes(hg)].astype(F32)
        ke_ref[0, rows(ci), lanes(hg)] = (k * jnp.exp(g_last - gc[i])).astype(BF16)
        cd_ref[0, ci, :, lanes(hg)] = jnp.exp(g_last)


def _gdn_local(q, k, v, gates, cumf, cumb, consts, tl):
    bsz, seq, width = q.shape
    masks, _, bd_b, exp = consts
    tile = lambda wd: pl.BlockSpec((1, tl, wd), lambda b, s: (b, s, 0))
    cd_spec = pl.BlockSpec((1, tl // CHUNK, 1, width), lambda b, s: (b, s, 0, 0))
    big = lambda dt: jax.ShapeDtypeStruct((bsz, seq, width), dt)
    cd_shape = jax.ShapeDtypeStruct((bsz, seq // CHUNK, 1, width), F32)
    per_dir_specs = [tile(width)] * 5 + [cd_spec]
    per_dir_shapes = [big(F32), big(BF16), big(BF16), big(BF16), big(BF16), cd_shape]
    return pl.pallas_call(
        _gdn_local_kernel,
        grid=(bsz, seq // tl),
        in_specs=[tile(width), tile(width), tile(width), tile(LANES), tile(LANES), tile(LANES),
                  _full(masks), _full(bd_b), _full(exp)],
        out_specs=per_dir_specs * 2,
        out_shape=per_dir_shapes * 2,
        compiler_params=_params(("parallel", "parallel")),
        name="gdn_local",
    )(q, k, v, gates, cumf, cumb, masks, bd_b, exp)


def _gdn_state_kernel(uf_ref, wf_ref, qef_ref, inf_ref, kef_ref, cdf_ref,
                      ub_ref, wb_ref, qeb_ref, inb_ref, keb_ref, cdb_ref, bdf_ref, bdb_ref, of_ref, ob_ref, s_ref):
    n = pl.program_id(1)

    @pl.when(n == 0)
    def _():
        s_ref[...] = jnp.zeros_like(s_ref)

    c, w = CHUNK, GROUP_W
    bd_b = bdb_ref[...]
    dirs = ((uf_ref, wf_ref, qef_ref, inf_ref, kef_ref, cdf_ref, of_ref),
            (ub_ref, wb_ref, qeb_ref, inb_ref, keb_ref, cdb_ref, ob_ref))
    nb = uf_ref.shape[0]
    chains = [(bi, d, hg) for bi in range(nb) for d in range(2) for hg in range(N_HEAD_GROUPS)]
    lanes = lambda hg: slice(hg * w, (hg + 1) * w)
    ws_qs = []
    for idx, (bi, d, hg) in enumerate(chains):
        w_ref, qe_ref = dirs[d][1], dirs[d][2]
        lhs = jnp.concatenate([w_ref[bi, :, lanes(hg)], qe_ref[bi, :, lanes(hg)]], axis=0)
        ws_qs.append(_dot(lhs, s_ref[idx].astype(BF16)))
    v_new = [(dirs[d][0][bi, :, lanes(hg)] - ws_qs[idx][:c]).astype(BF16) for idx, (bi, d, hg) in enumerate(chains)]
    for idx, (bi, d, hg) in enumerate(chains):
        in_ref, o_ref = dirs[d][3], dirs[d][6]
        o = ws_qs[idx][c:] + _dot(in_ref[bi, :, lanes(hg)], _block_diag(v_new[idx], bd_b))
        o_ref[bi, :, lanes(hg)] = o.astype(BF16)
    for idx, (bi, d, hg) in enumerate(chains):
        ke_ref, cd_ref = dirs[d][4], dirs[d][5]
        ds = _dot(ke_ref[bi, :, lanes(hg)], v_new[idx], _TN)
        s_ref[idx] = s_ref[idx] * cd_ref[bi, 0, :, lanes(hg)] + ds * bdf_ref[...]


def _gdn_state(local_outs, consts, nb):
    uf = local_outs[0]
    bsz, seq, width = uf.shape
    nc = seq // CHUNK
    _, bd_f, bd_b, _ = consts
    fwd = pl.BlockSpec((nb, CHUNK, width), lambda b, n: (b, n, 0))
    bwd = pl.BlockSpec((nb, CHUNK, width), lambda b, n: (b, nc - 1 - n, 0))
    cd_fwd = pl.BlockSpec((nb, 1, 1, width), lambda b, n: (b, n, 0, 0))
    cd_bwd = pl.BlockSpec((nb, 1, 1, width), lambda b, n: (b, nc - 1 - n, 0, 0))
    return pl.pallas_call(
        _gdn_state_kernel,
        grid=(bsz // nb, nc),
        in_specs=[fwd] * 5 + [cd_fwd] + [bwd] * 5 + [cd_bwd] + [_full(bd_f), _full(bd_b)],
        out_specs=[fwd, bwd],
        out_shape=[jax.ShapeDtypeStruct((bsz, seq, width), BF16)] * 2,
        scratch_shapes=[pltpu.VMEM((nb * 2 * N_HEAD_GROUPS, GROUP_W, GROUP_W), F32)],
        compiler_params=_params(("parallel", "arbitrary")),
        name="gdn_state",
    )(*local_outs, bd_f, bd_b)


def _mlstm_kernel(qf_ref, kf_ref, vf_ref, gf_ref, cf_ref, qb_ref, kb_ref, vb_ref, gb_ref, cb_ref,
                  masks_ref, bdf_ref, bdb_ref, exp_ref, of_ref, ob_ref, c_ref, n_ref, m_ref):
    n = pl.program_id(1)

    @pl.when(n == 0)
    def _():
        c_ref[...] = jnp.zeros_like(c_ref)
        n_ref[...] = jnp.zeros_like(n_ref)
        m_ref[...] = jnp.zeros_like(m_ref)

    c, w = CHUNK, GROUP_W
    bd_f = bdf_ref[...]
    bd_b = bdb_ref[...]
    ones_rows = jnp.ones((SUBLANES, c), BF16)
    row = _iota((c, w), 0)
    dirs = ((qf_ref, kf_ref, vf_ref, gf_ref, cf_ref, of_ref), (qb_ref, kb_ref, vb_ref, gb_ref, cb_ref, ob_ref))
    nb = qf_ref.shape[0]
    chains = [(bi, d, hg) for bi in range(nb) for d in range(2) for hg in range(N_HEAD_GROUPS)]
    lanes = lambda hg: slice(hg * w, (hg + 1) * w)
    gates = [jnp.concatenate([dirs[d][3][bi] for bi in range(nb)], axis=0) for d in range(2)]
    cums = [jnp.concatenate([dirs[d][4][bi] for bi in range(nb)], axis=0) for d in range(2)]
    i_all = [_expand(_split3(gates[d]), exp_ref[1, d, 0]) for d in range(2)]
    bcum_all = [_expand(_split3(cums[d]), exp_ref[1, d, 1]) for d in range(2)]
    per_block = LANES // c
    blocks = range(nb // per_block)
    block_t = lambda a: [a[blk * LANES:(blk + 1) * LANES].T for blk in blocks]
    gates_t = [block_t(gates[d]) for d in range(2)]
    cums_t = [block_t(cums[d]) for d in range(2)]
    rows = lambda bi: slice(bi * c, (bi + 1) * c)
    bcum = [bcum_all[d][rows(bi), lanes(hg)] for bi, d, hg in chains]
    r = [i_all[d][rows(bi), lanes(hg)] - bcum[i] for i, (bi, d, hg) in enumerate(chains)]
    r_t = [_head_rows(gates_t[d][bi // per_block], _gate_column(1, d, 0), hg, bi % per_block)
           - _head_rows(cums_t[d][bi // per_block], _gate_column(1, d, 1), hg, bi % per_block)
           for bi, d, hg in chains]
    qk = []
    for bi, d, hg in chains:
        k_bd = jnp.concatenate([dirs[d][1][bi, :, lanes(hg)]] * HEADS_PER_GROUP, axis=0) * bd_b
        qk.append(_dot(dirs[d][0][bi, :, lanes(hg)], k_bd, _NT))

    pm, mx, inter_w = [], [], []
    for i, (bi, d, hg) in enumerate(chains):
        m = r[i]
        for p in range(LEVELS):
            sh = 1 << p
            if d == 0:
                m = jnp.maximum(m, jnp.where(row >= sh, pltpu.roll(m, sh, 0), NEG))
            else:
                m = jnp.maximum(m, jnp.where(row < c - sh, pltpu.roll(m, c - sh, 0), NEG))
        pm.append(m)
        m_prev = m_ref[i, 0:1, :]
        mx.append(jnp.maximum(m_prev, m))
        inter_w.append(jnp.exp(m_prev - mx[i]))
        qk[i] = qk[i] * jnp.exp(jnp.where(masks_ref[d, M_INCL] > 0.0, r_t[i] - mx[i], NEG))

    num = [_mm(qk[i], _block_diag(dirs[d][2][bi, :, lanes(hg)], bd_b)) for i, (bi, d, hg) in enumerate(chains)]
    den = [_mm_lhs2(qk[i], bd_b) for i in range(len(chains))]
    qc = [_mm(dirs[d][0][bi, :, lanes(hg)], c_ref[i]) for i, (bi, d, hg) in enumerate(chains)]
    qn = [_mm(dirs[d][0][bi, :, lanes(hg)].astype(F32) * n_ref[i, 0:1, :], bd_b)
          for i, (bi, d, hg) in enumerate(chains)]
    for i, (bi, d, hg) in enumerate(chains):
        full_num = num[i] + inter_w[i] * qc[i]
        full_den = den[i] + inter_w[i] * qn[i]
        hidden = full_num / jnp.maximum(jnp.abs(full_den), jnp.exp(-(bcum[i] + mx[i])))
        dirs[d][5][bi, :, lanes(hg)] = hidden.astype(BF16)

    for i, (bi, d, hg) in enumerate(chains):
        last = c - 1 if d == 0 else 0
        m_prev = m_ref[i, 0:1, :]
        b_last = bcum[i][last:last + 1, :]
        m_next = b_last + jnp.maximum(m_prev, pm[i][last:last + 1, :])
        scale_prev = jnp.exp(b_last + m_prev - m_next)
        wgt = jnp.exp(b_last + r[i] - m_next)
        kw = (dirs[d][1][bi, :, lanes(hg)].astype(F32) * wgt).astype(BF16)
        c_ref[i] = c_ref[i] * scale_prev + _dot(kw, dirs[d][2][bi, :, lanes(hg)], _TN) * bd_f
        n_ref[i, 0:1, :] = n_ref[i, 0:1, :] * scale_prev + _dot(ones_rows, kw)[0:1, :]
        m_ref[i, 0:1, :] = m_next


def _mlstm_scan(q, k, v, gates, cumf, cumb, consts, nb):
    bsz, seq, width = q.shape
    nc = seq // CHUNK
    masks, bd_f, bd_b, exp = consts
    fwd = lambda wd: pl.BlockSpec((nb, CHUNK, wd), lambda b, n: (b, n, 0))
    bwd = lambda wd: pl.BlockSpec((nb, CHUNK, wd), lambda b, n: (b, nc - 1 - n, 0))
    n_state = nb * 2 * N_HEAD_GROUPS
    state = pltpu.VMEM((n_state, GROUP_W, GROUP_W), F32)
    row_state = pltpu.VMEM((n_state, SUBLANES, GROUP_W), F32)
    return pl.pallas_call(
        _mlstm_kernel,
        grid=(bsz // nb, nc),
        in_specs=[fwd(width), fwd(width), fwd(width), fwd(LANES), fwd(LANES),
                  bwd(width), bwd(width), bwd(width), bwd(LANES), bwd(LANES),
                  _full(masks), _full(bd_f), _full(bd_b), _full(exp)],
        out_specs=[fwd(width), bwd(width)],
        out_shape=[jax.ShapeDtypeStruct((bsz, seq, width), BF16)] * 2,
        scratch_shapes=[state, row_state, row_state],
        compiler_params=_params(("parallel", "arbitrary")),
        name="mlstm_scan",
    )(q, k, v, gates, cumf, q, k, v, gates, cumb, masks, bd_f, bd_b, exp)


def _outproj_kernel(h_ref, gaf_ref, gab_ref, mlf_ref, mlb_ref, az_ref, bo_ref, mg_ref, gt_ref, ng_ref, mlg_ref,
                    wa_ref, wb_ref, wo_ref, sh_ref, sc_ref, fg_ref, wr_ref, hout_ref, hn_ref, comb_ref):
    ones_bd = _head_ones(WIDTH)
    inv = 1.0 / HEAD_DIM
    tm = h_ref.shape[1]
    n_parts = 2 if tm % (2 * BF16_ROWS) == 0 else 1
    parts = [slice(i * tm // n_parts, (i + 1) * tm // n_parts) for i in range(n_parts)]
    f32 = lambda ref, rs: ref[0, rs, :].astype(F32)

    def branch_inputs(rs):
        oa = f32(gaf_ref, rs) + f32(gab_ref, rs)
        ms = _mm(oa * oa, ones_bd) * inv
        oa = oa * lax.rsqrt(ms + EPS) * ng_ref[...] * _silu(f32(az_ref, rs))
        hb = f32(mlf_ref, rs) + f32(mlb_ref, rs)
        mu = _mm_lhs2(hb, ones_bd) * inv
        tc = hb - mu
        var = _mm(tc * tc, ones_bd) * inv
        hb = tc * lax.rsqrt(var + EPS) * mlg_ref[...] * _sigmoid(f32(bo_ref, rs))
        return oa.astype(BF16), hb.astype(BF16)

    def merged(rs, oa, hb):
        gate = _sigmoid(f32(mg_ref, rs))
        return gate[:, :D_MODEL] * _dot(oa, wa_ref[...]) + gate[:, D_MODEL:] * _dot(hb, wb_ref[...])

    def residual(rs, y):
        h_new = h_ref[0, rs, :] + gt_ref[0] * _mm(y, wo_ref[...])
        hout_ref[0, rs, :] = h_new
        ms2 = jnp.mean(h_new * h_new, axis=-1, keepdims=True)
        hn = h_new * lax.rsqrt(ms2 + EPS) * (fg_ref[...] * (1.0 + sc_ref[0])) + sh_ref[0]
        hn_ref[0, rs, :] = hn.astype(BF16)
        return hn

    def routing(rs, logits):
        lane = _iota(logits.shape, 1)
        lane_f = lane.astype(F32)
        big = 1e9
        gl = jnp.where((lane >= N_EXPERTS) & (lane < N_EXPERTS + N_GROUPS), logits, NEG)
        gmax = jnp.max(gl, axis=-1, keepdims=True)
        gidx = jnp.min(jnp.where(gl == gmax, lane_f - N_EXPERTS, big), axis=-1, keepdims=True)
        p_group = 1.0 / jnp.sum(jnp.exp(gl - gmax), axis=-1, keepdims=True)
        el = jnp.where((lane < N_EXPERTS) & ((lane >> 3).astype(F32) == gidx), logits, NEG)
        v1 = jnp.max(el, axis=-1, keepdims=True)
        i1 = jnp.min(jnp.where(el == v1, lane_f, big), axis=-1, keepdims=True)
        el2 = jnp.where(lane_f == i1, NEG, el)
        v2 = jnp.max(el2, axis=-1, keepdims=True)
        i2 = jnp.min(jnp.where(el2 == v2, lane_f, big), axis=-1, keepdims=True)
        e21 = jnp.exp(v2 - v1)
        w1 = p_group / (1.0 + e21)
        route = jnp.where(lane == R_E1, i1, 0.0) + jnp.where(lane == R_E2, i2, 0.0)
        comb_ref[0, rs, :] = route + jnp.where(lane == R_W1, w1, 0.0) + jnp.where(lane == R_W2, w1 * e21, 0.0)

    branches = [branch_inputs(rs) for rs in parts]
    ys = [merged(rs, *branches[i]) for i, rs in enumerate(parts)]
    hns = [residual(rs, ys[i]) for i, rs in enumerate(parts)]
    logits = []
    for hn in hns:
        hi, lo = _split2(hn)
        both = _dot(jnp.concatenate([hi, lo], axis=1), wr_ref[...])
        logits.append(both[:, :LANES] + both[:, LANES:])
    for i, rs in enumerate(parts):
        routing(rs, logits[i])


def _output_projection(h, gaf, gab, mlf, mlb, az, bo, mg, gt, ng, mlg, wa, wb, wo, sh, sc, fg, wr, tm):
    bsz, seq, d = h.shape
    tile = lambda w: pl.BlockSpec((1, tm, w), lambda b, s: (b, s, 0))
    per_b = pl.BlockSpec((1, 1, d), lambda b, s: (b, 0, 0))
    return pl.pallas_call(
        _outproj_kernel,
        grid=(bsz, seq // tm),
        in_specs=[tile(d), tile(WIDTH), tile(WIDTH), tile(WIDTH), tile(WIDTH), tile(WIDTH), tile(WIDTH),
                  tile(2 * D_MODEL), per_b, _full(ng), _full(mlg), _full(wa), _full(wb), _full(wo), per_b, per_b,
                  _full(fg), _full(wr)],
        out_specs=[tile(d), tile(d), tile(LANES)],
        out_shape=[jax.ShapeDtypeStruct((bsz, seq, d), F32), jax.ShapeDtypeStruct((bsz, seq, d), BF16),
                   jax.ShapeDtypeStruct((bsz, seq, LANES), F32)],
        compiler_params=_params(("parallel", "parallel")),
        name="output_projection",
    )(h, gaf, gab, mlf, mlb, az, bo, mg, gt, ng, mlg, wa, wb, wo, sh, sc, fg, wr)


def _moe_kernel(npass_ref, acc_ref, hn_ref, route_ref, gt_ref, wg_ref, wu_ref, wd_ref, fin_ref, tri_ref,
                o_ref, y_ref, *, group, final_norm):
    g = group
    tile = pl.program_id(0) * pl.num_programs(1) + pl.program_id(1)
    tm = hn_ref.shape[1]
    cap = MOE_CAP
    sub = MOE_SUB
    n_sub = tm // sub
    route = route_ref[0]
    lane = _iota(route.shape, 1)
    lane_f = lane.astype(F32)
    column = lambda j: jnp.sum(jnp.where(lane == j, route, 0.0), axis=-1, keepdims=True)
    first = float(g * EXPERTS_PER_GROUP)
    l1 = column(R_E1) - first
    l2 = column(R_E2) - first
    w1 = column(R_W1)
    w2 = column(R_W2)
    in1 = (l1 >= 0.0) & (l1 < EXPERTS_PER_GROUP)
    in2 = (l2 >= 0.0) & (l2 < EXPERTS_PER_GROUP)
    hit1 = in1 & (lane_f == l1)
    hit2 = in2 & (lane_f == l2)
    onehot = (jnp.where(hit1, 1.0, 0.0) + jnp.where(hit2, 1.0, 0.0)).astype(BF16)
    rows = lambda s: slice(s * sub, (s + 1) * sub)
    before = jnp.concatenate([_dot(tri_ref[...], onehot[rows(s)]) for s in range(n_sub)], axis=0)
    rank1 = jnp.sum(jnp.where(hit1, before, 0.0), axis=-1, keepdims=True)
    rank2 = jnp.sum(jnp.where(hit2, before, 0.0), axis=-1, keepdims=True)
    slot_lane = _iota((tm, MOE_SLOTS), 1).astype(F32)
    x = hn_ref[0]
    experts = range(EXPERTS_PER_GROUP)
    chains = [range(c0, c0 + MOE_CHAIN_SUBS) for c0 in range(0, n_sub, MOE_CHAIN_SUBS)]

    def one_pass(p, first):
        lo = p * float(cap) if first else (p * cap).astype(F32)
        s1 = jnp.where(in1 & (rank1 >= lo) & (rank1 < lo + cap), l1 * cap + rank1 - lo, -1.0)
        s2 = jnp.where(in2 & (rank2 >= lo) & (rank2 < lo + cap), l2 * cap + rank2 - lo, -1.0)
        m1 = slot_lane == s1
        m2 = slot_lane == s2
        sel_all = jnp.where(m1 | m2, 1.0, 0.0).astype(BF16)
        sel_w_all = (jnp.where(m1, w1, 0.0) + jnp.where(m2, w2, 0.0)).astype(BF16)
        sel = {s: sel_all[rows(s)] for s in range(n_sub)}
        sel_w = {s: sel_w_all[rows(s)] for s in range(n_sub)}
        xs = {s: _dot(sel[s], x[rows(s)], _TN) for subs in chains for s in subs}
        xe = [[jnp.concatenate([xs[s][e * cap:(e + 1) * cap] for s in subs], axis=0).astype(BF16) for e in experts]
              for subs in chains]
        hg = [[_dot(xe[c][e], wg_ref[e]) for e in experts] for c in range(len(chains))]
        hu = [[_dot(xe[c][e], wu_ref[e]) for e in experts] for c in range(len(chains))]
        act = [[(_silu(hg[c][e]) * hu[c][e]).astype(BF16) for e in experts] for c in range(len(chains))]
        ye = [[_dot(act[c][e], wd_ref[e]) for e in experts] for c in range(len(chains))]
        pad = jnp.zeros((MOE_SLOTS - EXPERTS_PER_GROUP * cap, x.shape[1]), F32)
        for c, subs in enumerate(chains):
            for i, s in enumerate(subs):
                ys = jnp.concatenate([ye[c][e][i * cap:(i + 1) * cap] for e in experts] + [pad], axis=0).astype(BF16)
                if first:
                    y_ref[rows(s), :] = _dot(sel_w[s], ys)
                else:
                    y_ref[rows(s), :] += _dot(sel_w[s], ys)

    one_pass(0, True)

    def extra_pass(p, carry):
        one_pass(p, False)
        return carry

    lax.fori_loop(1, npass_ref[tile], extra_pass, 0)
    out = acc_ref[0] + gt_ref[0] * y_ref[...]
    if final_norm:
        ms = jnp.mean(out * out, axis=-1, keepdims=True)
        out = out * lax.rsqrt(ms + EPS) * fin_ref[...]
    o_ref[0] = out


def _moe(h, hn, route, gt, wg, wu, wd, fin, tm, final_norm):
    bsz, seq, d = h.shape
    nt = seq // tm
    n_sub = tm // MOE_SUB
    ids = route[..., (R_E1, R_E2)].astype(jnp.int32).reshape(bsz * nt, n_sub, MOE_SUB * 2)
    counts = jnp.sum(jax.nn.one_hot(ids, N_EXPERTS, dtype=jnp.int32), axis=2)
    most = jnp.max(counts.reshape(bsz * nt, n_sub, N_GROUPS, EXPERTS_PER_GROUP), axis=(1, 3))
    npass = ((most + MOE_CAP - 1) // MOE_CAP).T.astype(jnp.int32)
    tri = jnp.asarray(np.tril(np.ones((MOE_SUB, MOE_SUB), np.float32), -1)).astype(BF16)

    tile = lambda w: pl.BlockSpec((1, tm, w), lambda b, s, n: (b, s, 0))
    for g in range(N_GROUPS):
        group = lambda shape: pl.BlockSpec(shape, lambda b, s, n, g=g: (g, 0, 0))
        grid_spec = pltpu.PrefetchScalarGridSpec(
            num_scalar_prefetch=1,
            grid=(bsz, nt),
            in_specs=[tile(d), tile(d), tile(LANES), pl.BlockSpec((1, 1, d), lambda b, s, n: (b, 0, 0)),
                      group((EXPERTS_PER_GROUP, d, D_EXPERT)), group((EXPERTS_PER_GROUP, d, D_EXPERT)),
                      group((EXPERTS_PER_GROUP, D_EXPERT, d)),
                      pl.BlockSpec((1, d), lambda b, s, n: (0, 0)), pl.BlockSpec(tri.shape, lambda b, s, n: (0, 0))],
            out_specs=tile(d),
            scratch_shapes=[pltpu.VMEM((tm, d), F32)],
        )
        h = pl.pallas_call(
            functools.partial(_moe_kernel, group=g, final_norm=final_norm and g == N_GROUPS - 1),
            grid_spec=grid_spec,
            out_shape=jax.ShapeDtypeStruct((bsz, seq, d), F32),
            compiler_params=_params(("parallel", "parallel")),
            name="expert_ffn",
        )(npass[g], h, hn, route, gt, wg, wu, wd, fin, tri)
    return h


def _pick_tile(seq, want):
    tm = min(seq, want)
    assert seq % tm == 0 and tm % CHUNK == 0
    return tm


def kernel(x, c, ada_w, ada_b, norm_mix_g, norm_ffn_g, w_in, gdn_conv_w, gdn_a_log, gdn_dt_bias, gdn_norm_g, mlstm_conv_w, mlstm_i_bias, mlstm_f_bias, mlstm_norm_g, w_branch_a, w_branch_b, w_out, router_group, router_expert, w_gate, w_up, w_down, final_norm_g):
    bsz, seq, d = x.shape
    depth = ada_w.shape[0]
    assert d == D_MODEL and seq % CHUNK == 0
    tm_in = _pick_tile(seq, 512)
    tm_out = _pick_tile(seq, 512)
    tm_moe = _pick_tile(seq, MOE_CHAIN_SUBS * MOE_SUB)
    tl_gdn = _pick_tile(seq, 2 * LANES)
    nb_scan = LANES // CHUNK
    assert tl_gdn % LANES == 0 and bsz % nb_scan == 0
    row = lambda a: a.reshape(1, -1).astype(F32)
    consts = _scan_constants()

    mod = _modulation(c.astype(F32), ada_w.astype(F32), ada_b.astype(F32))
    h = x.astype(F32)
    for l in range(depth):
        sh1, sc1, gt1, sh2, sc2, gt2 = [mod[l, :, i * d:(i + 1) * d].reshape(bsz, 1, d) for i in range(N_MOD)]

        w = w_in[l].astype(F32)
        o_z = 3 * WIDTH
        o_ag = o_z + WIDTH
        o_bqk = o_ag + 4 * N_HEADS
        o_bv = o_bqk + 2 * WIDTH
        o_bo = o_bv + WIDTH
        o_bg = o_bo + WIDTH
        o_mg = o_bg + 4 * N_HEADS
        wconv = jnp.concatenate([w[:, :o_z], w[:, o_bqk:o_bv]], axis=1).astype(BF16)
        wg = jnp.concatenate([w[:, o_ag:o_bqk], w[:, o_bg:o_mg], jnp.zeros((d, LANES - 8 * N_HEADS), F32)], axis=1)
        wg_hi = wg.astype(BF16)
        wg_lo = (wg - wg_hi.astype(F32)).astype(BF16)
        wrest = jnp.concatenate([w[:, o_z:o_ag].astype(BF16), w[:, o_bv:o_bo].astype(BF16), w[:, o_bo:o_bg].astype(BF16),
                                 w[:, o_mg:].astype(BF16), wg_hi, wg_lo], axis=1)
        cw = jnp.concatenate([gdn_conv_w[l], mlstm_conv_w[l]], axis=1).astype(F32)
        cw = jnp.concatenate([cw, jnp.zeros((SUBLANES - CONV_K, CONV_W), F32)], axis=0)
        zero8 = jnp.zeros((N_HEADS,), F32)
        bias = jnp.concatenate([gdn_dt_bias[l, 0], zero8, gdn_dt_bias[l, 1], zero8,
                                mlstm_i_bias[l, 0], mlstm_f_bias[l, 0], mlstm_i_bias[l, 1], mlstm_f_bias[l, 1],
                                jnp.zeros((LANES - 8 * N_HEADS,), F32)]).astype(F32)
        alog = jnp.concatenate([gdn_a_log[l, 0], zero8, gdn_a_log[l, 1], zero8,
                                jnp.zeros((LANES - 4 * N_HEADS,), F32)]).astype(F32)
        gp = jnp.concatenate([bias[None], alog[None], jnp.zeros((SUBLANES - 2, LANES), F32)], axis=0)

        aq, ak, av, bq, bk, az, bv, bo, mg, gates, cumf, cumb = _input_projection(
            h, sh1, sc1, row(norm_mix_g[l]), wconv, wrest, cw, gp, tm_in)
        nb_state = 4 if bsz % 4 == 0 else nb_scan
        nb_gdn = 8 if bsz % 8 == 0 else nb_state
        gaf, gab = _gdn_state(_gdn_local(aq, ak, av, gates, cumf, cumb, consts, tl_gdn), consts, nb_gdn)
        mlf, mlb = _mlstm_scan(bq, bk, bv, gates, cumf, cumb, consts, nb_state)

        wr = jnp.concatenate([router_expert[l].astype(F32), router_group[l].astype(F32),
                              jnp.zeros((d, LANES - N_EXPERTS - N_GROUPS), F32)], axis=1)
        wr_hi = wr.astype(BF16)
        wr_lo = (wr - wr_hi.astype(F32)).astype(BF16)
        wr = jnp.concatenate([jnp.concatenate([wr_hi, wr_lo], axis=1),
                              jnp.concatenate([wr_hi, jnp.zeros_like(wr_hi)], axis=1)], axis=0)
        h, hn2, comb = _output_projection(
            h, gaf, gab, mlf, mlb, az, bo, mg, gt1, row(jnp.tile(gdn_norm_g[l], N_HEADS)), row(mlstm_norm_g[l]),
            w_branch_a[l].astype(BF16), w_branch_b[l].astype(BF16), w_out[l].astype(BF16),
            sh2, sc2, row(norm_ffn_g[l]), wr, tm_out)
        h = _moe(h, hn2, comb, gt2, w_gate[l].astype(BF16), w_up[l].astype(BF16), w_down[l].astype(BF16),
                 row(final_norm_g), tm_moe, final_norm=(l == depth - 1))
    return h.astype(x.dtype)
```

```python
import functools

import numpy as np
import jax
import jax.numpy as jnp
from jax import lax
from jax.experimental import pallas as pl
from jax.experimental.pallas import tpu as pltpu

D_MODEL = 1024
HEAD_DIM = 64
N_HEADS = 8
WIDTH = N_HEADS * HEAD_DIM
CONV_K = 5
CONV_W = 3 * WIDTH + 2 * WIDTH
REST_W = 3 * WIDTH + 2 * D_MODEL
N_GROUPS = 4
EXPERTS_PER_GROUP = 8
N_EXPERTS = N_GROUPS * EXPERTS_PER_GROUP
D_EXPERT = D_MODEL // 4
N_MOD = 6
EPS = 1e-6
NEG = -1e30

CHUNK = 64
HEADS_PER_GROUP = 2
N_HEAD_GROUPS = N_HEADS // HEADS_PER_GROUP
GROUP_W = HEADS_PER_GROUP * HEAD_DIM
LEVELS = 6
LANES = 128
SUBLANES = 8
BF16_ROWS = 16
VMEM_LIMIT = 56 * 1024 * 1024

F32 = jnp.float32
BF16 = jnp.bfloat16

_NN = (((1,), (0,)), ((), ()))
_NT = (((1,), (1,)), ((), ()))
_TN = (((0,), (0,)), ((), ()))

R_E1, R_E2, R_W1, R_W2 = 0, 1, 4, 5
MOE_SUB = 128
MOE_CAP = 24
MOE_CHAIN_SUBS = 4
MOE_SLOTS = 256

M_INCL, M_STRICT, M_LEVEL0, M_DIAG = 0, 1, 2, 2 + LEVELS
N_MASKS = M_DIAG + 1


def _dot(a, b, dims=_NN):
    return lax.dot_general(a, b, dims, preferred_element_type=F32)


def _mm(a, b, dims=_NN):
    return _dot(a.astype(BF16), b.astype(BF16), dims)


def _split2(x):
    hi = x.astype(BF16)
    lo = (x - hi.astype(F32)).astype(BF16)
    return hi, lo


def _split3(x):
    hi = x.astype(BF16)
    r = x - hi.astype(F32)
    mid = r.astype(BF16)
    lo = (r - mid.astype(F32)).astype(BF16)
    return hi, mid, lo


def _dot_pieces(pieces, e):
    out = _dot(pieces[0], e)
    for p in pieces[1:]:
        out = out + _dot(p, e)
    return out


def _mm_lhs3(x, e):
    return _dot_pieces(_split3(x), e)


def _mm_rhs3(e, x):
    hi, mid, lo = _split3(x)
    return _dot(e, hi) + _dot(e, mid) + _dot(e, lo)


def _mm_lhs2(x, e):
    return _dot_pieces(_split2(x), e)


def _mm3(a, b):
    a_hi, a_lo = _split2(a)
    b_hi, b_lo = _split2(b)
    return _dot(a_hi, b_hi) + _dot(a_hi, b_lo) + _dot(a_lo, b_hi)


def _iota(shape, dim):
    return lax.broadcasted_iota(jnp.int32, shape, dim)


def _sigmoid(x):
    return 0.5 * jnp.tanh(0.5 * x) + 0.5


def _silu(x):
    return x * _sigmoid(x)


def _softplus(x):
    return jnp.maximum(x, 0.0) + jnp.log1p(jnp.exp(-jnp.abs(x)))


def _head_ones(width):
    r = _iota((width, width), 0) >> 6
    c = _iota((width, width), 1) >> 6
    return jnp.where(r == c, 1.0, 0.0).astype(BF16)


def _params(sem):
    return pltpu.CompilerParams(dimension_semantics=sem, vmem_limit_bytes=VMEM_LIMIT)


def _full(a):
    return pl.BlockSpec(a.shape, lambda *_: (0,) * a.ndim)


def _mod_kernel(c_ref, w_ref, b_ref, o_ref):
    o_ref[0] = _mm3(_silu(c_ref[...]), w_ref[0]) + b_ref[0]


def _modulation(c, ada_w, ada_b):
    depth, d, n = ada_w.shape
    bsz = c.shape[0]
    tn = n // 4
    return pl.pallas_call(
        _mod_kernel,
        grid=(depth, n // tn),
        in_specs=[
            pl.BlockSpec((bsz, d), lambda l, j: (0, 0)),
            pl.BlockSpec((1, d, tn), lambda l, j: (l, 0, j)),
            pl.BlockSpec((1, 1, tn), lambda l, j: (l, 0, j)),
        ],
        out_specs=pl.BlockSpec((1, bsz, tn), lambda l, j: (l, 0, j)),
        out_shape=jax.ShapeDtypeStruct((depth, bsz, n), F32),
        compiler_params=_params(("parallel", "parallel")),
        name="adaln_mod",
    )(c, ada_w, ada_b.reshape(depth, 1, n))


HALO = BF16_ROWS


def _inproj_kernel(h_ref, hprev_ref, hnext_ref, sh_ref, sc_ref, g_ref, wconv_ref, wrest_ref, cw_ref, gp_ref,
                   perm_ref, permt_ref, aq_ref, ak_ref, av_ref, bq_ref, bk_ref, az_ref, bv_ref, bo_ref, mg_ref, gates_ref, cumf_ref,
                   cumb_ref):
    s = pl.program_id(1)
    ns = pl.num_programs(1)
    tm = h_ref.shape[1]
    gain = g_ref[...] * (1.0 + sc_ref[0])
    shift = sh_ref[0]

    def norm(x):
        ms = jnp.mean(x * x, axis=-1, keepdims=True)
        return x * lax.rsqrt(ms + EPS) * gain + shift

    hn = norm(h_ref[0])
    hp = norm(hprev_ref[0]) * jnp.where(s > 0, 1.0, 0.0)
    hx = norm(hnext_ref[0]) * jnp.where(s < ns - 1, 1.0, 0.0)
    hn_hi = hn.astype(BF16)

    nv = tm // SUBLANES
    edge = CONV_K // 2
    hn_perm = _dot(perm_ref[...], hn_hi).astype(BF16)
    hall = jnp.concatenate([hn_perm, hp.astype(BF16), hx.astype(BF16)], axis=0)
    groups = range(CONV_W // WIDTH)
    cols = lambda g: slice(g * WIDTH, (g + 1) * WIDTH)
    pe = [_dot(hall, wconv_ref[:, cols(g)]) for g in groups]
    rest = _dot(hn_hi, wrest_ref[...])
    hn_lo = (hn - hn_hi.astype(F32)).astype(BF16)
    pre = (rest[:, REST_W:REST_W + LANES] + rest[:, REST_W + LANES:]
           + _dot(hn_lo, wrest_ref[:, REST_W:REST_W + LANES]))

    sub = _iota((SUBLANES, WIDTH), 0)
    ones_bd = _head_ones(WIDTH)
    scale = HEAD_DIM ** -0.5
    out_refs = (aq_ref, ak_ref, av_ref, bq_ref, bk_ref)
    permuted = []
    for g in groups:
        prev = pe[g][tm:tm + HALO]
        nxt = pe[g][tm + HALO:]
        block = lambda v: pe[g][v * SUBLANES:(v + 1) * SUBLANES]
        before = [jnp.where(sub == 0, prev[HALO - edge + i:HALO - edge + i + 1], pltpu.roll(block(nv - edge + i), 1, 0))
                  for i in range(edge)]
        after = [jnp.where(sub == SUBLANES - 1, nxt[i:i + 1], pltpu.roll(block(i), SUBLANES - 1, 0))
                 for i in range(edge)]
        ext = jnp.concatenate(before + [pe[g][:tm]] + after, axis=0)

        conv = jnp.zeros((tm, WIDTH), F32)
        for j in range(CONV_K):
            conv = conv + ext[j * SUBLANES:j * SUBLANES + tm] * cw_ref[j:j + 1, cols(g)]
        cv = _silu(conv)
        if g in (0, 1):
            cv = cv * lax.rsqrt(_mm(cv * cv, ones_bd) + EPS)
        if g in (0, 4):
            cv = cv * scale
        permuted.append(cv.astype(BF16))
    for g in groups:
        out_refs[g][0] = _dot(permt_ref[...], permuted[g]).astype(BF16)

    az_ref[0] = rest[:, 0:WIDTH].astype(BF16)
    bv_ref[0] = rest[:, WIDTH:2 * WIDTH].astype(BF16)
    bo_ref[0] = rest[:, 2 * WIDTH:3 * WIDTH].astype(BF16)
    mg_ref[0] = rest[:, 3 * WIDTH:REST_W].astype(BF16)

    pre = pre + gp_ref[0:1, :]
    lane = _iota(pre.shape, 1)
    second = ((lane >> 3) & 1) == 1
    is_gdn = lane < 4 * N_HEADS
    is_ml = (lane >= 4 * N_HEADS) & (lane < 8 * N_HEADS)
    sp = _softplus(pre)
    gdn_val = jnp.where(second, _sigmoid(pre), -jnp.exp(gp_ref[1:2, :]) * sp)
    ml_val = jnp.where(second, -_softplus(-pre), pre)
    gates = jnp.where(is_gdn, gdn_val, jnp.where(is_ml, ml_val, 0.0))
    gates_ref[0] = gates

    ri = _iota((LANES, LANES), 0)
    ci = _iota((LANES, LANES), 1)
    same_chunk = (ri >> 6) == (ci >> 6)
    pieces = _split3(gates)
    for ref, tri in ((cumf_ref, same_chunk & (ri >= ci)), (cumb_ref, same_chunk & (ri <= ci))):
        tri_b = jnp.where(tri, 1.0, 0.0).astype(BF16)
        for blk in range(tm // LANES):
            rs = slice(blk * LANES, (blk + 1) * LANES)
            ref[0, rs, :] = _dot(tri_b, pieces[0][rs]) + _dot(tri_b, pieces[1][rs]) + _dot(tri_b, pieces[2][rs])


def _input_projection(h, sh, sc, gain, wconv, wrest, cw, gp, tm):
    bsz, seq, d = h.shape
    nt = seq // tm
    per = tm // HALO
    nhalo = seq // HALO
    tile = lambda w: pl.BlockSpec((1, tm, w), lambda b, s: (b, s, 0))
    outs = [(WIDTH, BF16)] * 8 + [(2 * D_MODEL, BF16)] + [(LANES, F32)] * 3
    r = np.arange(tm)
    perm_np = np.zeros((tm, tm), np.float32)
    perm_np[r, (r % SUBLANES) * (tm // SUBLANES) + r // SUBLANES] = 1.0
    perm = jnp.asarray(perm_np).astype(BF16)
    perm_t = jnp.asarray(perm_np.T).astype(BF16)
    return pl.pallas_call(
        _inproj_kernel,
        grid=(bsz, nt),
        in_specs=[
            tile(d),
            pl.BlockSpec((1, HALO, d), lambda b, s: (b, jnp.maximum(s * per - 1, 0), 0)),
            pl.BlockSpec((1, HALO, d), lambda b, s: (b, jnp.minimum((s + 1) * per, nhalo - 1), 0)),
            pl.BlockSpec((1, 1, d), lambda b, s: (b, 0, 0)),
            pl.BlockSpec((1, 1, d), lambda b, s: (b, 0, 0)),
            _full(gain), _full(wconv), _full(wrest), _full(cw), _full(gp), _full(perm), _full(perm_t),
        ],
        out_specs=[tile(w) for w, _ in outs],
        out_shape=[jax.ShapeDtypeStruct((bsz, seq, w), dt) for w, dt in outs],
        compiler_params=_params(("parallel", "parallel")),
        name="input_projection",
    )(h, h, h, sh, sc, gain, wconv, wrest, cw, gp, perm, perm_t)


def _gate_column(branch, direction, kind):
    return branch * 4 * N_HEADS + direction * 2 * N_HEADS + kind * N_HEADS


def _expand(pieces, e):
    rows = pieces[0].shape[0]
    out = _dot(jnp.concatenate(pieces, axis=0), e)
    return functools.reduce(lambda a, b: a + b, [out[i * rows:(i + 1) * rows] for i in range(len(pieces))])


def _head_rows(x_t, column, hg, block):
    first = column + hg * HEADS_PER_GROUP
    pieces = [x_t[first + h:first + h + 1, block * CHUNK:(block + 1) * CHUNK] for h in range(HEADS_PER_GROUP)]
    return jnp.concatenate(pieces, axis=1)


def _scan_constants():
    c, w = CHUNK, GROUP_W
    row = np.arange(c)[:, None]
    col = (np.arange(w) % c)[None, :]
    masks = np.zeros((2, N_MASKS, c, w), np.float32)
    for d in range(2):
        masks[d, M_INCL] = (row >= col) if d == 0 else (row <= col)
        masks[d, M_STRICT] = (row > col) if d == 0 else (row < col)
        for p in range(LEVELS):
            same = (row >> (p + 1)) == (col >> (p + 1))
            r_bit = (row >> p) & 1
            c_bit = (col >> p) & 1
            masks[d, M_LEVEL0 + p] = same & ((r_bit == 1) & (c_bit == 0) if d == 0 else (r_bit == 0) & (c_bit == 1))
        masks[d, M_DIAG] = row == col
    bd = (np.arange(w)[:, None] // HEAD_DIM) == (np.arange(w)[None, :] // HEAD_DIM)
    exp = np.zeros((2, 2, 2, LANES, WIDTH), np.float32)
    lane_head = np.arange(WIDTH) // HEAD_DIM
    for br in range(2):
        for d in range(2):
            for kind in range(2):
                exp[br, d, kind, _gate_column(br, d, kind) + lane_head, np.arange(WIDTH)] = 1.0
    return (jnp.asarray(masks), jnp.asarray(bd.astype(np.float32)),
            jnp.asarray(bd.astype(np.float32)).astype(BF16), jnp.asarray(exp).astype(BF16))


def _block_diag(x_cat, bd_bf16):
    xb = x_cat.astype(BF16)
    return jnp.concatenate([xb] * HEADS_PER_GROUP, axis=0) * bd_bf16


def _gdn_local_kernel(q_ref, k_ref, v_ref, g_ref, cumf_ref, cumb_ref, masks_ref, bdb_ref, exp_ref,
                      uf_ref, wf_ref, qef_ref, inf_ref, kef_ref, cdf_ref,
                      ub_ref, wb_ref, qeb_ref, inb_ref, keb_ref, cdb_ref):
    c, w = CHUNK, GROUP_W
    bd_b = bdb_ref[...]
    outs = ((uf_ref, wf_ref, qef_ref, inf_ref, kef_ref, cdf_ref), (ub_ref, wb_ref, qeb_ref, inb_ref, keb_ref, cdb_ref))
    chains = [(ci, d, hg) for ci in range(q_ref.shape[1] // c) for d in range(2) for hg in range(N_HEAD_GROUPS)]
    rows = lambda ci: slice(ci * c, (ci + 1) * c)
    lanes = lambda hg: slice(hg * w, (hg + 1) * w)

    cums = (cumf_ref[0], cumb_ref[0])
    g_pieces = _split2(g_ref[0])
    gc_all = [_expand(_split3(cums[d]), exp_ref[0, d, 0]) for d in range(2)]
    beta_all = [_expand(g_pieces, exp_ref[0, d, 1]) for d in range(2)]
    per_block = LANES // c
    n_blocks = q_ref.shape[1] // LANES
    cum_t = [[cums[d][blk * LANES:(blk + 1) * LANES].T for blk in range(n_blocks)] for d in range(2)]
    gc = [gc_all[d][rows(ci), lanes(hg)] for ci, d, hg in chains]
    beta = [beta_all[d][rows(ci), lanes(hg)] for ci, d, hg in chains]
    gc_t = [_head_rows(cum_t[d][ci // per_block], _gate_column(0, d, 0), hg, ci % per_block) for ci, d, hg in chains]

    qk_kk = {}
    for ci, d, hg in chains:
        if d == 0:
            q = q_ref[0, rows(ci), lanes(hg)]
            k = k_ref[0, rows(ci), lanes(hg)]
            k_bd = jnp.concatenate([k] * HEADS_PER_GROUP, axis=0) * bd_b
            qk_kk[ci, hg] = _dot(jnp.concatenate([q, k], axis=0), k_bd, _NT)
    kbeta = [k_ref[0, rows(ci), lanes(hg)].astype(F32) * beta[i] for i, (ci, d, hg) in enumerate(chains)]

    a, t = [], []
    for i, (ci, d, hg) in enumerate(chains):
        decay = jnp.exp(jnp.minimum(gc[i] - gc_t[i], 0.0)) * masks_ref[d, M_INCL]
        outs[d][3][0, rows(ci), lanes(hg)] = (qk_kk[ci, hg][:c] * decay).astype(BF16)
        a.append(qk_kk[ci, hg][c:] * beta[i] * decay * masks_ref[d, M_STRICT])
        t.append(masks_ref[d, M_DIAG] - a[i] * masks_ref[d, M_LEVEL0])

    for p in range(1, LEVELS):
        y = [_mm(t[i], _block_diag(a[i] * masks_ref[d, M_LEVEL0 + p], bd_b)) for i, (_, d, _) in enumerate(chains)]
        t = [t[i] - _mm(y[i], _block_diag(t[i], bd_b)) for i in range(len(chains))]

    for i, (ci, d, hg) in enumerate(chains):
        u_ref, w_ref, qe_ref, _, ke_ref, cd_ref = outs[d]
        last = c - 1 if d == 0 else 0
        g_last = gc[i][last:last + 1, :]
        e_gc = jnp.exp(gc[i])
        v = v_ref[0, rows(ci), lanes(hg)].astype(F32)
        u_ref[0, rows(ci), lanes(hg)] = _mm(t[i], _block_diag(v * beta[i], bd_b))
        w_ref[0, rows(ci), lanes(hg)] = _mm(t[i], _block_diag(kbeta[i] * e_gc, bd_b)).astype(BF16)
        qe_ref[0, rows(ci), lanes(hg)] = (q_ref[0, rows(ci), lanes(hg)].astype(F32) * e_gc).astype(BF16)
        k = k_ref[0, rows(ci), lanes(hg)].astype(F32)
        ke_ref[0, rows(ci), lanes(hg)] = (k * jnp.exp(g_last - gc[i])).astype(BF16)
        cd_ref[0, ci, :, lanes(hg)] = jnp.exp(g_last)


def _gdn_local(q, k, v, gates, cumf, cumb, consts, tl):
    bsz, seq, width = q.shape
    masks, _, bd_b, exp = consts
    tile = lambda wd: pl.BlockSpec((1, tl, wd), lambda b, s: (b, s, 0))
    cd_spec = pl.BlockSpec((1, tl // CHUNK, 1, width), lambda b, s: (b, s, 0, 0))
    big = lambda dt: jax.ShapeDtypeStruct((bsz, seq, width), dt)
    cd_shape = jax.ShapeDtypeStruct((bsz, seq // CHUNK, 1, width), F32)
    per_dir_specs = [tile(width)] * 5 + [cd_spec]
    per_dir_shapes = [big(F32), big(BF16), big(BF16), big(BF16), big(BF16), cd_shape]
    return pl.pallas_call(
        _gdn_local_kernel,
        grid=(bsz, seq // tl),
        in_specs=[tile(width), tile(width), tile(width), tile(LANES), tile(LANES), tile(LANES),
                  _full(masks), _full(bd_b), _full(exp)],
        out_specs=per_dir_specs * 2,
        out_shape=per_dir_shapes * 2,
        compiler_params=_params(("parallel", "parallel")),
        name="gdn_local",
    )(q, k, v, gates, cumf, cumb, masks, bd_b, exp)


def _gdn_state_kernel(uf_ref, wf_ref, qef_ref, inf_ref, kef_ref, cdf_ref,
                      ub_ref, wb_ref, qeb_ref, inb_ref, keb_ref, cdb_ref, bdf_ref, bdb_ref, of_ref, ob_ref, s_ref):
    n = pl.program_id(1)

    @pl.when(n == 0)
    def _():
        s_ref[...] = jnp.zeros_like(s_ref)

    c, w = CHUNK, GROUP_W
    bd_b = bdb_ref[...]
    dirs = ((uf_ref, wf_ref, qef_ref, inf_ref, kef_ref, cdf_ref, of_ref),
            (ub_ref, wb_ref, qeb_ref, inb_ref, keb_ref, cdb_ref, ob_ref))
    nb = uf_ref.shape[0]
    chains = [(bi, d, hg) for bi in range(nb) for d in range(2) for hg in range(N_HEAD_GROUPS)]
    lanes = lambda hg: slice(hg * w, (hg + 1) * w)
    ws_qs = []
    for idx, (bi, d, hg) in enumerate(chains):
        w_ref, qe_ref = dirs[d][1], dirs[d][2]
        lhs = jnp.concatenate([w_ref[bi, :, lanes(hg)], qe_ref[bi, :, lanes(hg)]], axis=0)
        ws_qs.append(_dot(lhs, s_ref[idx].astype(BF16)))
    v_new = [(dirs[d][0][bi, :, lanes(hg)] - ws_qs[idx][:c]).astype(BF16) for idx, (bi, d, hg) in enumerate(chains)]
    for idx, (bi, d, hg) in enumerate(chains):
        in_ref, o_ref = dirs[d][3], dirs[d][6]
        o = ws_qs[idx][c:] + _dot(in_ref[bi, :, lanes(hg)], _block_diag(v_new[idx], bd_b))
        o_ref[bi, :, lanes(hg)] = o.astype(BF16)
    for idx, (bi, d, hg) in enumerate(chains):
        ke_ref, cd_ref = dirs[d][4], dirs[d][5]
        ds = _dot(ke_ref[bi, :, lanes(hg)], v_new[idx], _TN)
        s_ref[idx] = s_ref[idx] * cd_ref[bi, 0, :, lanes(hg)] + ds * bdf_ref[...]


def _gdn_state(local_outs, consts, nb):
    uf = local_outs[0]
    bsz, seq, width = uf.shape
    nc = seq // CHUNK
    _, bd_f, bd_b, _ = consts
    fwd = pl.BlockSpec((nb, CHUNK, width), lambda b, n: (b, n, 0))
    bwd = pl.BlockSpec((nb, CHUNK, width), lambda b, n: (b, nc - 1 - n, 0))
    cd_fwd = pl.BlockSpec((nb, 1, 1, width), lambda b, n: (b, n, 0, 0))
    cd_bwd = pl.BlockSpec((nb, 1, 1, width), lambda b, n: (b, nc - 1 - n, 0, 0))
    return pl.pallas_call(
        _gdn_state_kernel,
        grid=(bsz // nb, nc),
        in_specs=[fwd] * 5 + [cd_fwd] + [bwd] * 5 + [cd_bwd] + [_full(bd_f), _full(bd_b)],
        out_specs=[fwd, bwd],
        out_shape=[jax.ShapeDtypeStruct((bsz, seq, width), BF16)] * 2,
        scratch_shapes=[pltpu.VMEM((nb * 2 * N_HEAD_GROUPS, GROUP_W, GROUP_W), F32)],
        compiler_params=_params(("parallel", "arbitrary")),
        name="gdn_state",
    )(*local_outs, bd_f, bd_b)


def _mlstm_kernel(qf_ref, kf_ref, vf_ref, gf_ref, cf_ref, qb_ref, kb_ref, vb_ref, gb_ref, cb_ref,
                  masks_ref, bdf_ref, bdb_ref, exp_ref, of_ref, ob_ref, c_ref, n_ref, m_ref):
    n = pl.program_id(1)

    @pl.when(n == 0)
    def _():
        c_ref[...] = jnp.zeros_like(c_ref)
        n_ref[...] = jnp.zeros_like(n_ref)
        m_ref[...] = jnp.zeros_like(m_ref)

    c, w = CHUNK, GROUP_W
    bd_f = bdf_ref[...]
    bd_b = bdb_ref[...]
    ones_rows = jnp.ones((SUBLANES, c), BF16)
    row = _iota((c, w), 0)
    dirs = ((qf_ref, kf_ref, vf_ref, gf_ref, cf_ref, of_ref), (qb_ref, kb_ref, vb_ref, gb_ref, cb_ref, ob_ref))
    nb = qf_ref.shape[0]
    chains = [(bi, d, hg) for bi in range(nb) for d in range(2) for hg in range(N_HEAD_GROUPS)]
    lanes = lambda hg: slice(hg * w, (hg + 1) * w)
    gates = [jnp.concatenate([dirs[d][3][bi] for bi in range(nb)], axis=0) for d in range(2)]
    cums = [jnp.concatenate([dirs[d][4][bi] for bi in range(nb)], axis=0) for d in range(2)]
    i_all = [_expand(_split3(gates[d]), exp_ref[1, d, 0]) for d in range(2)]
    bcum_all = [_expand(_split3(cums[d]), exp_ref[1, d, 1]) for d in range(2)]
    per_block = LANES // c
    blocks = range(nb // per_block)
    block_t = lambda a: [a[blk * LANES:(blk + 1) * LANES].T for blk in blocks]
    gates_t = [block_t(gates[d]) for d in range(2)]
    cums_t = [block_t(cums[d]) for d in range(2)]
    rows = lambda bi: slice(bi * c, (bi + 1) * c)
    bcum = [bcum_all[d][rows(bi), lanes(hg)] for bi, d, hg in chains]
    r = [i_all[d][rows(bi), lanes(hg)] - bcum[i] for i, (bi, d, hg) in enumerate(chains)]
    r_t = [_head_rows(gates_t[d][bi // per_block], _gate_column(1, d, 0), hg, bi % per_block)
           - _head_rows(cums_t[d][bi // per_block], _gate_column(1, d, 1), hg, bi % per_block)
           for bi, d, hg in chains]
    qk = []
    for bi, d, hg in chains:
        k_bd = jnp.concatenate([dirs[d][1][bi, :, lanes(hg)]] * HEADS_PER_GROUP, axis=0) * bd_b
        qk.append(_dot(dirs[d][0][bi, :, lanes(hg)], k_bd, _NT))

    pm, mx, inter_w = [], [], []
    for i, (bi, d, hg) in enumerate(chains):
        m = r[i]
        for p in range(LEVELS):
            sh = 1 << p
            if d == 0:
                m = jnp.maximum(m, jnp.where(row >= sh, pltpu.roll(m, sh, 0), NEG))
            else:
                m = jnp.maximum(m, jnp.where(row < c - sh, pltpu.roll(m, c - sh, 0), NEG))
        pm.append(m)
        m_prev = m_ref[i, 0:1, :]
        mx.append(jnp.maximum(m_prev, m))
        inter_w.append(jnp.exp(m_prev - mx[i]))
        qk[i] = qk[i] * jnp.exp(jnp.where(masks_ref[d, M_INCL] > 0.0, r_t[i] - mx[i], NEG))

    num = [_mm(qk[i], _block_diag(dirs[d][2][bi, :, lanes(hg)], bd_b)) for i, (bi, d, hg) in enumerate(chains)]
    den = [_mm_lhs2(qk[i], bd_b) for i in range(len(chains))]
    qc = [_mm(dirs[d][0][bi, :, lanes(hg)], c_ref[i]) for i, (bi, d, hg) in enumerate(chains)]
    qn = [_mm(dirs[d][0][bi, :, lanes(hg)].astype(F32) * n_ref[i, 0:1, :], bd_b)
          for i, (bi, d, hg) in enumerate(chains)]
    for i, (bi, d, hg) in enumerate(chains):
        full_num = num[i] + inter_w[i] * qc[i]
        full_den = den[i] + inter_w[i] * qn[i]
        hidden = full_num / jnp.maximum(jnp.abs(full_den), jnp.exp(-(bcum[i] + mx[i])))
        dirs[d][5][bi, :, lanes(hg)] = hidden.astype(BF16)

    for i, (bi, d, hg) in enumerate(chains):
        last = c - 1 if d == 0 else 0
        m_prev = m_ref[i, 0:1, :]
        b_last = bcum[i][last:last + 1, :]
        m_next = b_last + jnp.maximum(m_prev, pm[i][last:last + 1, :])
        scale_prev = jnp.exp(b_last + m_prev - m_next)
        wgt = jnp.exp(b_last + r[i] - m_next)
        kw = (dirs[d][1][bi, :, lanes(hg)].astype(F32) * wgt).astype(BF16)
        c_ref[i] = c_ref[i] * scale_prev + _dot(kw, dirs[d][2][bi, :, lanes(hg)], _TN) * bd_f
        n_ref[i, 0:1, :] = n_ref[i, 0:1, :] * scale_prev + _dot(ones_rows, kw)[0:1, :]
        m_ref[i, 0:1, :] = m_next


def _mlstm_scan(q, k, v, gates, cumf, cumb, consts, nb):
    bsz, seq, width = q.shape
    nc = seq // CHUNK
    masks, bd_f, bd_b, exp = consts
    fwd = lambda wd: pl.BlockSpec((nb, CHUNK, wd), lambda b, n: (b, n, 0))
    bwd = lambda wd: pl.BlockSpec((nb, CHUNK, wd), lambda b, n: (b, nc - 1 - n, 0))
    n_state = nb * 2 * N_HEAD_GROUPS
    state = pltpu.VMEM((n_state, GROUP_W, GROUP_W), F32)
    row_state = pltpu.VMEM((n_state, SUBLANES, GROUP_W), F32)
    return pl.pallas_call(
        _mlstm_kernel,
        grid=(bsz // nb, nc),
        in_specs=[fwd(width), fwd(width), fwd(width), fwd(LANES), fwd(LANES),
                  bwd(width), bwd(width), bwd(width), bwd(LANES), bwd(LANES),
                  _full(masks), _full(bd_f), _full(bd_b), _full(exp)],
        out_specs=[fwd(width), bwd(width)],
        out_shape=[jax.ShapeDtypeStruct((bsz, seq, width), BF16)] * 2,
        scratch_shapes=[state, row_state, row_state],
        compiler_params=_params(("parallel", "arbitrary")),
        name="mlstm_scan",
    )(q, k, v, gates, cumf, q, k, v, gates, cumb, masks, bd_f, bd_b, exp)


def _outproj_kernel(h_ref, gaf_ref, gab_ref, mlf_ref, mlb_ref, az_ref, bo_ref, mg_ref, gt_ref, ng_ref, mlg_ref,
                    wa_ref, wb_ref, wo_ref, sh_ref, sc_ref, fg_ref, wr_ref, hout_ref, hn_ref, comb_ref):
    ones_bd = _head_ones(WIDTH)
    inv = 1.0 / HEAD_DIM
    tm = h_ref.shape[1]
    n_parts = 2 if tm % (2 * BF16_ROWS) == 0 else 1
    parts = [slice(i * tm // n_parts, (i + 1) * tm // n_parts) for i in range(n_parts)]
    f32 = lambda ref, rs: ref[0, rs, :].astype(F32)

    def branch_inputs(rs):
        oa = f32(gaf_ref, rs) + f32(gab_ref, rs)
        ms = _mm(oa * oa, ones_bd) * inv
        oa = oa * lax.rsqrt(ms + EPS) * ng_ref[...] * _silu(f32(az_ref, rs))
        hb = f32(mlf_ref, rs) + f32(mlb_ref, rs)
        mu = _mm_lhs2(hb, ones_bd) * inv
        tc = hb - mu
        var = _mm(tc * tc, ones_bd) * inv
        hb = tc * lax.rsqrt(var + EPS) * mlg_ref[...] * _sigmoid(f32(bo_ref, rs))
        return oa.astype(BF16), hb.astype(BF16)

    def merged(rs, oa, hb):
        gate = _sigmoid(f32(mg_ref, rs))
        return gate[:, :D_MODEL] * _dot(oa, wa_ref[...]) + gate[:, D_MODEL:] * _dot(hb, wb_ref[...])

    def residual(rs, y):
        h_new = h_ref[0, rs, :] + gt_ref[0] * _mm(y, wo_ref[...])
        hout_ref[0, rs, :] = h_new
        ms2 = jnp.mean(h_new * h_new, axis=-1, keepdims=True)
        hn = h_new * lax.rsqrt(ms2 + EPS) * (fg_ref[...] * (1.0 + sc_ref[0])) + sh_ref[0]
        hn_ref[0, rs, :] = hn.astype(BF16)
        return hn

    def routing(rs, logits):
        lane = _iota(logits.shape, 1)
        lane_f = lane.astype(F32)
        big = 1e9
        gl = jnp.where((lane >= N_EXPERTS) & (lane < N_EXPERTS + N_GROUPS), logits, NEG)
        gmax = jnp.max(gl, axis=-1, keepdims=True)
        gidx = jnp.min(jnp.where(gl == gmax, lane_f - N_EXPERTS, big), axis=-1, keepdims=True)
        p_group = 1.0 / jnp.sum(jnp.exp(gl - gmax), axis=-1, keepdims=True)
        el = jnp.where((lane < N_EXPERTS) & ((lane >> 3).astype(F32) == gidx), logits, NEG)
        v1 = jnp.max(el, axis=-1, keepdims=True)
        i1 = jnp.min(jnp.where(el == v1, lane_f, big), axis=-1, keepdims=True)
        el2 = jnp.where(lane_f == i1, NEG, el)
        v2 = jnp.max(el2, axis=-1, keepdims=True)
        i2 = jnp.min(jnp.where(el2 == v2, lane_f, big), axis=-1, keepdims=True)
        e21 = jnp.exp(v2 - v1)
        w1 = p_group / (1.0 + e21)
        route = jnp.where(lane == R_E1, i1, 0.0) + jnp.where(lane == R_E2, i2, 0.0)
        comb_ref[0, rs, :] = route + jnp.where(lane == R_W1, w1, 0.0) + jnp.where(lane == R_W2, w1 * e21, 0.0)

    branches = [branch_inputs(rs) for rs in parts]
    ys = [merged(rs, *branches[i]) for i, rs in enumerate(parts)]
    hns = [residual(rs, ys[i]) for i, rs in enumerate(parts)]
    logits = []
    for hn in hns:
        hi, lo = _split2(hn)
        both = _dot(jnp.concatenate([hi, lo], axis=1), wr_ref[...])
        logits.append(both[:, :LANES] + both[:, LANES:])
    for i, rs in enumerate(parts):
        routing(rs, logits[i])


def _output_projection(h, gaf, gab, mlf, mlb, az, bo, mg, gt, ng, mlg, wa, wb, wo, sh, sc, fg, wr, tm):
    bsz, seq, d = h.shape
    tile = lambda w: pl.BlockSpec((1, tm, w), lambda b, s: (b, s, 0))
    per_b = pl.BlockSpec((1, 1, d), lambda b, s: (b, 0, 0))
    return pl.pallas_call(
        _outproj_kernel,
        grid=(bsz, seq // tm),
        in_specs=[tile(d), tile(WIDTH), tile(WIDTH), tile(WIDTH), tile(WIDTH), tile(WIDTH), tile(WIDTH),
                  tile(2 * D_MODEL), per_b, _full(ng), _full(mlg), _full(wa), _full(wb), _full(wo), per_b, per_b,
                  _full(fg), _full(wr)],
        out_specs=[tile(d), tile(d), tile(LANES)],
        out_shape=[jax.ShapeDtypeStruct((bsz, seq, d), F32), jax.ShapeDtypeStruct((bsz, seq, d), BF16),
                   jax.ShapeDtypeStruct((bsz, seq, LANES), F32)],
        compiler_params=_params(("parallel", "parallel")),
        name="output_projection",
    )(h, gaf, gab, mlf, mlb, az, bo, mg, gt, ng, mlg, wa, wb, wo, sh, sc, fg, wr)


def _moe_kernel(npass_ref, acc_ref, hn_ref, route_ref, gt_ref, wg_ref, wu_ref, wd_ref, fin_ref, tri_ref,
                o_ref, y_ref, *, group, final_norm):
    g = group
    tile = pl.program_id(0) * pl.num_programs(1) + pl.program_id(1)
    tm = hn_ref.shape[1]
    cap = MOE_CAP
    sub = MOE_SUB
    n_sub = tm // sub
    route = route_ref[0]
    lane = _iota(route.shape, 1)
    lane_f = lane.astype(F32)
    column = lambda j: jnp.sum(jnp.where(lane == j, route, 0.0), axis=-1, keepdims=True)
    first = float(g * EXPERTS_PER_GROUP)
    l1 = column(R_E1) - first
    l2 = column(R_E2) - first
    w1 = column(R_W1)
    w2 = column(R_W2)
    in1 = (l1 >= 0.0) & (l1 < EXPERTS_PER_GROUP)
    in2 = (l2 >= 0.0) & (l2 < EXPERTS_PER_GROUP)
    hit1 = in1 & (lane_f == l1)
    hit2 = in2 & (lane_f == l2)
    onehot = (jnp.where(hit1, 1.0, 0.0) + jnp.where(hit2, 1.0, 0.0)).astype(BF16)
    rows = lambda s: slice(s * sub, (s + 1) * sub)
    before = jnp.concatenate([_dot(tri_ref[...], onehot[rows(s)]) for s in range(n_sub)], axis=0)
    rank1 = jnp.sum(jnp.where(hit1, before, 0.0), axis=-1, keepdims=True)
    rank2 = jnp.sum(jnp.where(hit2, before, 0.0), axis=-1, keepdims=True)
    slot_lane = _iota((tm, MOE_SLOTS), 1).astype(F32)
    x = hn_ref[0]
    experts = range(EXPERTS_PER_GROUP)
    chains = [range(c0, c0 + MOE_CHAIN_SUBS) for c0 in range(0, n_sub, MOE_CHAIN_SUBS)]

    def one_pass(p, first):
        lo = p * float(cap) if first else (p * cap).astype(F32)
        s1 = jnp.where(in1 & (rank1 >= lo) & (rank1 < lo + cap), l1 * cap + rank1 - lo, -1.0)
        s2 = jnp.where(in2 & (rank2 >= lo) & (rank2 < lo + cap), l2 * cap + rank2 - lo, -1.0)
        m1 = slot_lane == s1
        m2 = slot_lane == s2
        sel_all = jnp.where(m1 | m2, 1.0, 0.0).astype(BF16)
        sel_w_all = (jnp.where(m1, w1, 0.0) + jnp.where(m2, w2, 0.0)).astype(BF16)
        sel = {s: sel_all[rows(s)] for s in range(n_sub)}
        sel_w = {s: sel_w_all[rows(s)] for s in range(n_sub)}
        xs = {s: _dot(sel[s], x[rows(s)], _TN) for subs in chains for s in subs}
        xe = [[jnp.concatenate([xs[s][e * cap:(e + 1) * cap] for s in subs], axis=0).astype(BF16) for e in experts]
              for subs in chains]
        hg = [[_dot(xe[c][e], wg_ref[e]) for e in experts] for c in range(len(chains))]
        hu = [[_dot(xe[c][e], wu_ref[e]) for e in experts] for c in range(len(chains))]
        act = [[(_silu(hg[c][e]) * hu[c][e]).astype(BF16) for e in experts] for c in range(len(chains))]
        ye = [[_dot(act[c][e], wd_ref[e]) for e in experts] for c in range(len(chains))]
        pad = jnp.zeros((MOE_SLOTS - EXPERTS_PER_GROUP * cap, x.shape[1]), F32)
        for c, subs in enumerate(chains):
            for i, s in enumerate(subs):
                ys = jnp.concatenate([ye[c][e][i * cap:(i + 1) * cap] for e in experts] + [pad], axis=0).astype(BF16)
                if first:
                    y_ref[rows(s), :] = _dot(sel_w[s], ys)
                else:
                    y_ref[rows(s), :] += _dot(sel_w[s], ys)

    one_pass(0, True)

    def extra_pass(p, carry):
        one_pass(p, False)
        return carry

    lax.fori_loop(1, npass_ref[tile], extra_pass, 0)
    out = acc_ref[0] + gt_ref[0] * y_ref[...]
    if final_norm:
        ms = jnp.mean(out * out, axis=-1, keepdims=True)
        out = out * lax.rsqrt(ms + EPS) * fin_ref[...]
    o_ref[0] = out


def _moe(h, hn, route, gt, wg, wu, wd, fin, tm, final_norm):
    bsz, seq, d = h.shape
    nt = seq // tm
    n_sub = tm // MOE_SUB
    ids = route[..., (R_E1, R_E2)].astype(jnp.int32).reshape(bsz * nt, n_sub, MOE_SUB * 2)
    counts = jnp.sum(jax.nn.one_hot(ids, N_EXPERTS, dtype=jnp.int32), axis=2)
    most = jnp.max(counts.reshape(bsz * nt, n_sub, N_GROUPS, EXPERTS_PER_GROUP), axis=(1, 3))
    npass = ((most + MOE_CAP - 1) // MOE_CAP).T.astype(jnp.int32)
    tri = jnp.asarray(np.tril(np.ones((MOE_SUB, MOE_SUB), np.float32), -1)).astype(BF16)

    tile = lambda w: pl.BlockSpec((1, tm, w), lambda b, s, n: (b, s, 0))
    for g in range(N_GROUPS):
        group = lambda shape: pl.BlockSpec(shape, lambda b, s, n, g=g: (g, 0, 0))
        grid_spec = pltpu.PrefetchScalarGridSpec(
            num_scalar_prefetch=1,
            grid=(bsz, nt),
            in_specs=[tile(d), tile(d), tile(LANES), pl.BlockSpec((1, 1, d), lambda b, s, n: (b, 0, 0)),
                      group((EXPERTS_PER_GROUP, d, D_EXPERT)), group((EXPERTS_PER_GROUP, d, D_EXPERT)),
                      group((EXPERTS_PER_GROUP, D_EXPERT, d)),
                      pl.BlockSpec((1, d), lambda b, s, n: (0, 0)), pl.BlockSpec(tri.shape, lambda b, s, n: (0, 0))],
            out_specs=tile(d),
            scratch_shapes=[pltpu.VMEM((tm, d), F32)],
        )
        h = pl.pallas_call(
            functools.partial(_moe_kernel, group=g, final_norm=final_norm and g == N_GROUPS - 1),
            grid_spec=grid_spec,
            out_shape=jax.ShapeDtypeStruct((bsz, seq, d), F32),
            compiler_params=_params(("parallel", "parallel")),
            name="expert_ffn",
        )(npass[g], h, hn, route, gt, wg, wu, wd, fin, tri)
    return h


def _pick_tile(seq, want):
    tm = min(seq, want)
    assert seq % tm == 0 and tm % CHUNK == 0
    return tm


def kernel(x, c, ada_w, ada_b, norm_mix_g, norm_ffn_g, w_in, gdn_conv_w, gdn_a_log, gdn_dt_bias, gdn_norm_g, mlstm_conv_w, mlstm_i_bias, mlstm_f_bias, mlstm_norm_g, w_branch_a, w_branch_b, w_out, router_group, router_expert, w_gate, w_up, w_down, final_norm_g):
    bsz, seq, d = x.shape
    depth = ada_w.shape[0]
    assert d == D_MODEL and seq % CHUNK == 0
    tm_in = _pick_tile(seq, 512)
    tm_out = _pick_tile(seq, 512)
    tm_moe = _pick_tile(seq, MOE_CHAIN_SUBS * MOE_SUB)
    tl_gdn = _pick_tile(seq, 2 * LANES)
    nb_scan = LANES // CHUNK
    assert tl_gdn % LANES == 0 and bsz % nb_scan == 0
    row = lambda a: a.reshape(1, -1).astype(F32)
    consts = _scan_constants()

    mod = _modulation(c.astype(F32), ada_w.astype(F32), ada_b.astype(F32))
    h = x.astype(F32)
    for l in range(depth):
        sh1, sc1, gt1, sh2, sc2, gt2 = [mod[l, :, i * d:(i + 1) * d].reshape(bsz, 1, d) for i in range(N_MOD)]

        w = w_in[l].astype(F32)
        o_z = 3 * WIDTH
        o_ag = o_z + WIDTH
        o_bqk = o_ag + 4 * N_HEADS
        o_bv = o_bqk + 2 * WIDTH
        o_bo = o_bv + WIDTH
        o_bg = o_bo + WIDTH
        o_mg = o_bg + 4 * N_HEADS
        wconv = jnp.concatenate([w[:, :o_z], w[:, o_bqk:o_bv]], axis=1).astype(BF16)
        wg = jnp.concatenate([w[:, o_ag:o_bqk], w[:, o_bg:o_mg], jnp.zeros((d, LANES - 8 * N_HEADS), F32)], axis=1)
        wg_hi = wg.astype(BF16)
        wg_lo = (wg - wg_hi.astype(F32)).astype(BF16)
        wrest = jnp.concatenate([w[:, o_z:o_ag].astype(BF16), w[:, o_bv:o_bo].astype(BF16), w[:, o_bo:o_bg].astype(BF16),
                                 w[:, o_mg:].astype(BF16), wg_hi, wg_lo], axis=1)
        cw = jnp.concatenate([gdn_conv_w[l], mlstm_conv_w[l]], axis=1).astype(F32)
        cw = jnp.concatenate([cw, jnp.zeros((SUBLANES - CONV_K, CONV_W), F32)], axis=0)
        zero8 = jnp.zeros((N_HEADS,), F32)
        bias = jnp.concatenate([gdn_dt_bias[l, 0], zero8, gdn_dt_bias[l, 1], zero8,
                                mlstm_i_bias[l, 0], mlstm_f_bias[l, 0], mlstm_i_bias[l, 1], mlstm_f_bias[l, 1],
                                jnp.zeros((LANES - 8 * N_HEADS,), F32)]).astype(F32)
        alog = jnp.concatenate([gdn_a_log[l, 0], zero8, gdn_a_log[l, 1], zero8,
                                jnp.zeros((LANES - 4 * N_HEADS,), F32)]).astype(F32)
        gp = jnp.concatenate([bias[None], alog[None], jnp.zeros((SUBLANES - 2, LANES), F32)], axis=0)

        aq, ak, av, bq, bk, az, bv, bo, mg, gates, cumf, cumb = _input_projection(
            h, sh1, sc1, row(norm_mix_g[l]), wconv, wrest, cw, gp, tm_in)
        nb_state = 4 if bsz % 4 == 0 else nb_scan
        nb_gdn = 8 if bsz % 8 == 0 else nb_state
        gaf, gab = _gdn_state(_gdn_local(aq, ak, av, gates, cumf, cumb, consts, tl_gdn), consts, nb_gdn)
        mlf, mlb = _mlstm_scan(bq, bk, bv, gates, cumf, cumb, consts, nb_gdn)

        wr = jnp.concatenate([router_expert[l].astype(F32), router_group[l].astype(F32),
                              jnp.zeros((d, LANES - N_EXPERTS - N_GROUPS), F32)], axis=1)
        wr_hi = wr.astype(BF16)
        wr_lo = (wr - wr_hi.astype(F32)).astype(BF16)
        wr = jnp.concatenate([jnp.concatenate([wr_hi, wr_lo], axis=1),
                              jnp.concatenate([wr_hi, jnp.zeros_like(wr_hi)], axis=1)], axis=0)
        h, hn2, comb = _output_projection(
            h, gaf, gab, mlf, mlb, az, bo, mg, gt1, row(jnp.tile(gdn_norm_g[l], N_HEADS)), row(mlstm_norm_g[l]),
            w_branch_a[l].astype(BF16), w_branch_b[l].astype(BF16), w_out[l].astype(BF16),
            sh2, sc2, row(norm_ffn_g[l]), wr, tm_out)
        h = _moe(h, hn2, comb, gt2, w_gate[l].astype(BF16), w_up[l].astype(BF16), w_down[l].astype(BF16),
                 row(final_norm_g), tm_moe, final_norm=(l == depth - 1))
    return h.astype(x.dtype)
```

```python
import functools

import numpy as np
import jax
import jax.numpy as jnp
from jax import lax
from jax.experimental import pallas as pl
from jax.experimental.pallas import tpu as pltpu

D_MODEL = 1024
HEAD_DIM = 64
N_HEADS = 8
WIDTH = N_HEADS * HEAD_DIM
CONV_K = 5
CONV_W = 3 * WIDTH + 2 * WIDTH
REST_W = 3 * WIDTH + 2 * D_MODEL
N_GROUPS = 4
EXPERTS_PER_GROUP = 8
N_EXPERTS = N_GROUPS * EXPERTS_PER_GROUP
D_EXPERT = D_MODEL // 4
N_MOD = 6
EPS = 1e-6
NEG = -1e30

CHUNK = 64
HEADS_PER_GROUP = 2
N_HEAD_GROUPS = N_HEADS // HEADS_PER_GROUP
GROUP_W = HEADS_PER_GROUP * HEAD_DIM
LEVELS = 6
LANES = 128
SUBLANES = 8
BF16_ROWS = 16
VMEM_LIMIT = 56 * 1024 * 1024

F32 = jnp.float32
BF16 = jnp.bfloat16

_NN = (((1,), (0,)), ((), ()))
_NT = (((1,), (1,)), ((), ()))
_TN = (((0,), (0,)), ((), ()))

R_E1, R_E2, R_W1, R_W2 = 0, 1, 4, 5
MOE_SUB = 128
MOE_CAP = 24
MOE_GROUPS_PER_CALL = 2
MOE_SLOTS = 256

M_INCL, M_STRICT, M_LEVEL0, M_DIAG = 0, 1, 2, 2 + LEVELS
N_MASKS = M_DIAG + 1


def _dot(a, b, dims=_NN):
    return lax.dot_general(a, b, dims, preferred_element_type=F32)


def _mm(a, b, dims=_NN):
    return _dot(a.astype(BF16), b.astype(BF16), dims)


def _split2(x):
    hi = x.astype(BF16)
    lo = (x - hi.astype(F32)).astype(BF16)
    return hi, lo


def _split3(x):
    hi = x.astype(BF16)
    r = x - hi.astype(F32)
    mid = r.astype(BF16)
    lo = (r - mid.astype(F32)).astype(BF16)
    return hi, mid, lo


def _dot_pieces(pieces, e):
    out = _dot(pieces[0], e)
    for p in pieces[1:]:
        out = out + _dot(p, e)
    return out


def _mm_lhs3(x, e):
    return _dot_pieces(_split3(x), e)


def _mm_rhs3(e, x):
    hi, mid, lo = _split3(x)
    return _dot(e, hi) + _dot(e, mid) + _dot(e, lo)


def _mm_lhs2(x, e):
    return _dot_pieces(_split2(x), e)


def _mm3(a, b):
    a_hi, a_lo = _split2(a)
    b_hi, b_lo = _split2(b)
    return _dot(a_hi, b_hi) + _dot(a_hi, b_lo) + _dot(a_lo, b_hi)


def _iota(shape, dim):
    return lax.broadcasted_iota(jnp.int32, shape, dim)


def _sigmoid(x):
    return 0.5 * jnp.tanh(0.5 * x) + 0.5


def _silu(x):
    return x * _sigmoid(x)


def _softplus(x):
    return jnp.maximum(x, 0.0) + jnp.log1p(jnp.exp(-jnp.abs(x)))


def _head_ones(width):
    r = _iota((width, width), 0) >> 6
    c = _iota((width, width), 1) >> 6
    return jnp.where(r == c, 1.0, 0.0).astype(BF16)


def _params(sem):
    return pltpu.CompilerParams(dimension_semantics=sem, vmem_limit_bytes=VMEM_LIMIT)


def _full(a):
    return pl.BlockSpec(a.shape, lambda *_: (0,) * a.ndim)


def _mod_kernel(c_ref, w_ref, b_ref, o_ref):
    o_ref[0] = _mm3(_silu(c_ref[...]), w_ref[0]) + b_ref[0]


def _modulation(c, ada_w, ada_b):
    depth, d, n = ada_w.shape
    bsz = c.shape[0]
    tn = n // 4
    return pl.pallas_call(
        _mod_kernel,
        grid=(depth, n // tn),
        in_specs=[
            pl.BlockSpec((bsz, d), lambda l, j: (0, 0)),
            pl.BlockSpec((1, d, tn), lambda l, j: (l, 0, j)),
            pl.BlockSpec((1, 1, tn), lambda l, j: (l, 0, j)),
        ],
        out_specs=pl.BlockSpec((1, bsz, tn), lambda l, j: (l, 0, j)),
        out_shape=jax.ShapeDtypeStruct((depth, bsz, n), F32),
        compiler_params=_params(("parallel", "parallel")),
        name="adaln_mod",
    )(c, ada_w, ada_b.reshape(depth, 1, n))


HALO = BF16_ROWS


def _inproj_kernel(h_ref, hprev_ref, hnext_ref, sh_ref, sc_ref, g_ref, wconv_ref, wrest_ref, cw_ref, gp_ref,
                   perm_ref, permt_ref, aq_ref, ak_ref, av_ref, bq_ref, bk_ref, az_ref, bv_ref, bo_ref, mg_ref, gates_ref, cumf_ref,
                   cumb_ref):
    s = pl.program_id(1)
    ns = pl.num_programs(1)
    tm = h_ref.shape[1]
    gain = g_ref[...] * (1.0 + sc_ref[0])
    shift = sh_ref[0]

    def norm(x):
        ms = jnp.mean(x * x, axis=-1, keepdims=True)
        return x * lax.rsqrt(ms + EPS) * gain + shift

    hn = norm(h_ref[0])
    hp = norm(hprev_ref[0]) * jnp.where(s > 0, 1.0, 0.0)
    hx = norm(hnext_ref[0]) * jnp.where(s < ns - 1, 1.0, 0.0)
    hn_hi = hn.astype(BF16)

    nv = tm // SUBLANES
    edge = CONV_K // 2
    hn_perm = _dot(perm_ref[...], hn_hi).astype(BF16)
    hall = jnp.concatenate([hn_perm, hp.astype(BF16), hx.astype(BF16)], axis=0)
    groups = range(CONV_W // WIDTH)
    cols = lambda g: slice(g * WIDTH, (g + 1) * WIDTH)
    pe = [_dot(hall, wconv_ref[:, cols(g)]) for g in groups]
    rest = _dot(hn_hi, wrest_ref[...])
    hn_lo = (hn - hn_hi.astype(F32)).astype(BF16)
    pre = (rest[:, REST_W:REST_W + LANES] + rest[:, REST_W + LANES:]
           + _dot(hn_lo, wrest_ref[:, REST_W:REST_W + LANES]))

    sub = _iota((SUBLANES, WIDTH), 0)
    ones_bd = _head_ones(WIDTH)
    scale = HEAD_DIM ** -0.5
    out_refs = (aq_ref, ak_ref, av_ref, bq_ref, bk_ref)
    permuted = []
    for g in groups:
        prev = pe[g][tm:tm + HALO]
        nxt = pe[g][tm + HALO:]
        block = lambda v: pe[g][v * SUBLANES:(v + 1) * SUBLANES]
        before = [jnp.where(sub == 0, prev[HALO - edge + i:HALO - edge + i + 1], pltpu.roll(block(nv - edge + i), 1, 0))
                  for i in range(edge)]
        after = [jnp.where(sub == SUBLANES - 1, nxt[i:i + 1], pltpu.roll(block(i), SUBLANES - 1, 0))
                 for i in range(edge)]
        ext = jnp.concatenate(before + [pe[g][:tm]] + after, axis=0)

        conv = jnp.zeros((tm, WIDTH), F32)
        for j in range(CONV_K):
            conv = conv + ext[j * SUBLANES:j * SUBLANES + tm] * cw_ref[j:j + 1, cols(g)]
        cv = _silu(conv)
        if g in (0, 1):
            cv = cv * lax.rsqrt(_mm(cv * cv, ones_bd) + EPS)
        if g in (0, 4):
            cv = cv * scale
        permuted.append(cv.astype(BF16))
    for g in groups:
        out_refs[g][0] = _dot(permt_ref[...], permuted[g]).astype(BF16)

    az_ref[0] = rest[:, 0:WIDTH].astype(BF16)
    bv_ref[0] = rest[:, WIDTH:2 * WIDTH].astype(BF16)
    bo_ref[0] = rest[:, 2 * WIDTH:3 * WIDTH].astype(BF16)
    mg_ref[0] = rest[:, 3 * WIDTH:REST_W].astype(BF16)

    pre = pre + gp_ref[0:1, :]
    lane = _iota(pre.shape, 1)
    second = ((lane >> 3) & 1) == 1
    is_gdn = lane < 4 * N_HEADS
    is_ml = (lane >= 4 * N_HEADS) & (lane < 8 * N_HEADS)
    sp = _softplus(pre)
    gdn_val = jnp.where(second, _sigmoid(pre), -jnp.exp(gp_ref[1:2, :]) * sp)
    ml_val = jnp.where(second, -_softplus(-pre), pre)
    gates = jnp.where(is_gdn, gdn_val, jnp.where(is_ml, ml_val, 0.0))
    gates_ref[0] = gates

    ri = _iota((LANES, LANES), 0)
    ci = _iota((LANES, LANES), 1)
    same_chunk = (ri >> 6) == (ci >> 6)
    pieces = _split3(gates)
    for ref, tri in ((cumf_ref, same_chunk & (ri >= ci)), (cumb_ref, same_chunk & (ri <= ci))):
        tri_b = jnp.where(tri, 1.0, 0.0).astype(BF16)
        for blk in range(tm // LANES):
            rs = slice(blk * LANES, (blk + 1) * LANES)
            ref[0, rs, :] = _dot(tri_b, pieces[0][rs]) + _dot(tri_b, pieces[1][rs]) + _dot(tri_b, pieces[2][rs])


def _input_projection(h, sh, sc, gain, wconv, wrest, cw, gp, tm):
    bsz, seq, d = h.shape
    nt = seq // tm
    per = tm // HALO
    nhalo = seq // HALO
    tile = lambda w: pl.BlockSpec((1, tm, w), lambda b, s: (b, s, 0))
    outs = [(WIDTH, BF16)] * 8 + [(2 * D_MODEL, BF16)] + [(LANES, F32)] * 3
    r = np.arange(tm)
    perm_np = np.zeros((tm, tm), np.float32)
    perm_np[r, (r % SUBLANES) * (tm // SUBLANES) + r // SUBLANES] = 1.0
    perm = jnp.asarray(perm_np).astype(BF16)
    perm_t = jnp.asarray(perm_np.T).astype(BF16)
    return pl.pallas_call(
        _inproj_kernel,
        grid=(bsz, nt),
        in_specs=[
            tile(d),
            pl.BlockSpec((1, HALO, d), lambda b, s: (b, jnp.maximum(s * per - 1, 0), 0)),
            pl.BlockSpec((1, HALO, d), lambda b, s: (b, jnp.minimum((s + 1) * per, nhalo - 1), 0)),
            pl.BlockSpec((1, 1, d), lambda b, s: (b, 0, 0)),
            pl.BlockSpec((1, 1, d), lambda b, s: (b, 0, 0)),
            _full(gain), _full(wconv), _full(wrest), _full(cw), _full(gp), _full(perm), _full(perm_t),
        ],
        out_specs=[tile(w) for w, _ in outs],
        out_shape=[jax.ShapeDtypeStruct((bsz, seq, w), dt) for w, dt in outs],
        compiler_params=_params(("parallel", "parallel")),
        name="input_projection",
    )(h, h, h, sh, sc, gain, wconv, wrest, cw, gp, perm, perm_t)


def _gate_column(branch, direction, kind):
    return branch * 4 * N_HEADS + direction * 2 * N_HEADS + kind * N_HEADS


def _expand(pieces, e):
    rows = pieces[0].shape[0]
    out = _dot(jnp.concatenate(pieces, axis=0), e)
    return functools.reduce(lambda a, b: a + b, [out[i * rows:(i + 1) * rows] for i in range(len(pieces))])


def _head_rows(x_t, column, hg, block):
    first = column + hg * HEADS_PER_GROUP
    pieces = [x_t[first + h:first + h + 1, block * CHUNK:(block + 1) * CHUNK] for h in range(HEADS_PER_GROUP)]
    return jnp.concatenate(pieces, axis=1)


def _scan_constants():
    c, w = CHUNK, GROUP_W
    row = np.arange(c)[:, None]
    col = (np.arange(w) % c)[None, :]
    masks = np.zeros((2, N_MASKS, c, w), np.float32)
    for d in range(2):
        masks[d, M_INCL] = (row >= col) if d == 0 else (row <= col)
        masks[d, M_STRICT] = (row > col) if d == 0 else (row < col)
        for p in range(LEVELS):
            same = (row >> (p + 1)) == (col >> (p + 1))
            r_bit = (row >> p) & 1
            c_bit = (col >> p) & 1
            masks[d, M_LEVEL0 + p] = same & ((r_bit == 1) & (c_bit == 0) if d == 0 else (r_bit == 0) & (c_bit == 1))
        masks[d, M_DIAG] = row == col
    bd = (np.arange(w)[:, None] // HEAD_DIM) == (np.arange(w)[None, :] // HEAD_DIM)
    exp = np.zeros((2, 2, 2, LANES, WIDTH), np.float32)
    lane_head = np.arange(WIDTH) // HEAD_DIM
    for br in range(2):
        for d in range(2):
            for kind in range(2):
                exp[br, d, kind, _gate_column(br, d, kind) + lane_head, np.arange(WIDTH)] = 1.0
    return (jnp.asarray(masks), jnp.asarray(bd.astype(np.float32)),
            jnp.asarray(bd.astype(np.float32)).astype(BF16), jnp.asarray(exp).astype(BF16))


def _block_diag(x_cat, bd_bf16):
    xb = x_cat.astype(BF16)
    return jnp.concatenate([xb] * HEADS_PER_GROUP, axis=0) * bd_bf16


def _gdn_local_kernel(q_ref, k_ref, v_ref, g_ref, cumf_ref, cumb_ref, masks_ref, bdb_ref, exp_ref,
                      uf_ref, wf_ref, qef_ref, inf_ref, kef_ref, cdf_ref,
                      ub_ref, wb_ref, qeb_ref, inb_ref, keb_ref, cdb_ref):
    c, w = CHUNK, GROUP_W
    bd_b = bdb_ref[...]
    outs = ((uf_ref, wf_ref, qef_ref, inf_ref, kef_ref, cdf_ref), (ub_ref, wb_ref, qeb_ref, inb_ref, keb_ref, cdb_ref))
    chains = [(ci, d, hg) for ci in range(q_ref.shape[1] // c) for d in range(2) for hg in range(N_HEAD_GROUPS)]
    rows = lambda ci: slice(ci * c, (ci + 1) * c)
    lanes = lambda hg: slice(hg * w, (hg + 1) * w)

    cums = (cumf_ref[0], cumb_ref[0])
    g_pieces = _split2(g_ref[0])
    gc_all = [_expand(_split3(cums[d]), exp_ref[0, d, 0]) for d in range(2)]
    beta_all = [_expand(g_pieces, exp_ref[0, d, 1]) for d in range(2)]
    per_block = LANES // c
    n_blocks = q_ref.shape[1] // LANES
    cum_t = [[cums[d][blk * LANES:(blk + 1) * LANES].T for blk in range(n_blocks)] for d in range(2)]
    gc = [gc_all[d][rows(ci), lanes(hg)] for ci, d, hg in chains]
    beta = [beta_all[d][rows(ci), lanes(hg)] for ci, d, hg in chains]
    gc_t = [_head_rows(cum_t[d][ci // per_block], _gate_column(0, d, 0), hg, ci % per_block) for ci, d, hg in chains]

    qk_kk = {}
    for ci, d, hg in chains:
        if d == 0:
            q = q_ref[0, rows(ci), lanes(hg)]
            k = k_ref[0, rows(ci), lanes(hg)]
            k_bd = jnp.concatenate([k] * HEADS_PER_GROUP, axis=0) * bd_b
            qk_kk[ci, hg] = _dot(jnp.concatenate([q, k], axis=0), k_bd, _NT)
    kbeta = [k_ref[0, rows(ci), lanes(hg)].astype(F32) * beta[i] for i, (ci, d, hg) in enumerate(chains)]

    a, t = [], []
    for i, (ci, d, hg) in enumerate(chains):
        decay = jnp.exp(jnp.minimum(gc[i] - gc_t[i], 0.0)) * masks_ref[d, M_INCL]
        outs[d][3][0, rows(ci), lanes(hg)] = (qk_kk[ci, hg][:c] * decay).astype(BF16)
        a.append(qk_kk[ci, hg][c:] * beta[i] * decay * masks_ref[d, M_STRICT])
        t.append(masks_ref[d, M_DIAG] - a[i] * masks_ref[d, M_LEVEL0])

    for p in range(1, LEVELS):
        y = [_mm(t[i], _block_diag(a[i] * masks_ref[d, M_LEVEL0 + p], bd_b)) for i, (_, d, _) in enumerate(chains)]
        t = [t[i] - _mm(y[i], _block_diag(t[i], bd_b)) for i in range(len(chains))]

    for i, (ci, d, hg) in enumerate(chains):
        u_ref, w_ref, qe_ref, _, ke_ref, cd_ref = outs[d]
        last = c - 1 if d == 0 else 0
        g_last = gc[i][last:last + 1, :]
        e_gc = jnp.exp(gc[i])
        v = v_ref[0, rows(ci), lanes(hg)].astype(F32)
        u_ref[0, rows(ci), lanes(hg)] = _mm(t[i], _block_diag(v * beta[i], bd_b))
        w_ref[0, rows(ci), lanes(hg)] = _mm(t[i], _block_diag(kbeta[i] * e_gc, bd_b)).astype(BF16)
        qe_ref[0, rows(ci), lanes(hg)] = (q_ref[0, rows(ci), lanes(hg)].astype(F32) * e_gc).astype(BF16)
        k = k_ref[0, rows(ci), lanes(hg)].astype(F32)
        ke_ref[0, rows(ci), lanes(hg)] = (k * jnp.exp(g_last - gc[i])).astype(BF16)
        cd_ref[0, ci, :, lanes(hg)] = jnp.exp(g_last)


def _gdn_local(q, k, v, gates, cumf, cumb, consts, tl):
    bsz, seq, width = q.shape
    masks, _, bd_b, exp = consts
    tile = lambda wd: pl.BlockSpec((1, tl, wd), lambda b, s: (b, s, 0))
    cd_spec = pl.BlockSpec((1, tl // CHUNK, 1, width), lambda b, s: (b, s, 0, 0))
    big = lambda dt: jax.ShapeDtypeStruct((bsz, seq, width), dt)
    cd_shape = jax.ShapeDtypeStruct((bsz, seq // CHUNK, 1, width), F32)
    per_dir_specs = [tile(width)] * 5 + [cd_spec]
    per_dir_shapes = [big(F32), big(BF16), big(BF16), big(BF16), big(BF16), cd_shape]
    return pl.pallas_call(
        _gdn_local_kernel,
        grid=(bsz, seq // tl),
        in_specs=[tile(width), tile(width), tile(width), tile(LANES), tile(LANES), tile(LANES),
                  _full(masks), _full(bd_b), _full(exp)],
        out_specs=per_dir_specs * 2,
        out_shape=per_dir_shapes * 2,
        compiler_params=_params(("parallel", "parallel")),
        name="gdn_local",
    )(q, k, v, gates, cumf, cumb, masks, bd_b, exp)


def _gdn_state_kernel(uf_ref, wf_ref, qef_ref, inf_ref, kef_ref, cdf_ref,
                      ub_ref, wb_ref, qeb_ref, inb_ref, keb_ref, cdb_ref, bdf_ref, bdb_ref, of_ref, ob_ref, s_ref):
    n = pl.program_id(1)

    @pl.when(n == 0)
    def _():
        s_ref[...] = jnp.zeros_like(s_ref)

    c, w = CHUNK, GROUP_W
    bd_b = bdb_ref[...]
    dirs = ((uf_ref, wf_ref, qef_ref, inf_ref, kef_ref, cdf_ref, of_ref),
            (ub_ref, wb_ref, qeb_ref, inb_ref, keb_ref, cdb_ref, ob_ref))
    nb = uf_ref.shape[0]
    chains = [(bi, d, hg) for bi in range(nb) for d in range(2) for hg in range(N_HEAD_GROUPS)]
    lanes = lambda hg: slice(hg * w, (hg + 1) * w)
    ws_qs = []
    for idx, (bi, d, hg) in enumerate(chains):
        w_ref, qe_ref = dirs[d][1], dirs[d][2]
        lhs = jnp.concatenate([w_ref[bi, :, lanes(hg)], qe_ref[bi, :, lanes(hg)]], axis=0)
        ws_qs.append(_dot(lhs, s_ref[idx].astype(BF16)))
    v_new = [(dirs[d][0][bi, :, lanes(hg)] - ws_qs[idx][:c]).astype(BF16) for idx, (bi, d, hg) in enumerate(chains)]
    for idx, (bi, d, hg) in enumerate(chains):
        in_ref, o_ref = dirs[d][3], dirs[d][6]
        o = ws_qs[idx][c:] + _dot(in_ref[bi, :, lanes(hg)], _block_diag(v_new[idx], bd_b))
        o_ref[bi, :, lanes(hg)] = o.astype(BF16)
    for idx, (bi, d, hg) in enumerate(chains):
        ke_ref, cd_ref = dirs[d][4], dirs[d][5]
        ds = _dot(ke_ref[bi, :, lanes(hg)], v_new[idx], _TN)
        s_ref[idx] = s_ref[idx] * cd_ref[bi, 0, :, lanes(hg)] + ds * bdf_ref[...]


def _gdn_state(local_outs, consts, nb):
    uf = local_outs[0]
    bsz, seq, width = uf.shape
    nc = seq // CHUNK
    _, bd_f, bd_b, _ = consts
    fwd = pl.BlockSpec((nb, CHUNK, width), lambda b, n: (b, n, 0))
    bwd = pl.BlockSpec((nb, CHUNK, width), lambda b, n: (b, nc - 1 - n, 0))
    cd_fwd = pl.BlockSpec((nb, 1, 1, width), lambda b, n: (b, n, 0, 0))
    cd_bwd = pl.BlockSpec((nb, 1, 1, width), lambda b, n: (b, nc - 1 - n, 0, 0))
    return pl.pallas_call(
        _gdn_state_kernel,
        grid=(bsz // nb, nc),
        in_specs=[fwd] * 5 + [cd_fwd] + [bwd] * 5 + [cd_bwd] + [_full(bd_f), _full(bd_b)],
        out_specs=[fwd, bwd],
        out_shape=[jax.ShapeDtypeStruct((bsz, seq, width), BF16)] * 2,
        scratch_shapes=[pltpu.VMEM((nb * 2 * N_HEAD_GROUPS, GROUP_W, GROUP_W), F32)],
        compiler_params=_params(("parallel", "arbitrary")),
        name="gdn_state",
    )(*local_outs, bd_f, bd_b)


def _mlstm_kernel(qf_ref, kf_ref, vf_ref, gf_ref, cf_ref, qb_ref, kb_ref, vb_ref, gb_ref, cb_ref,
                  masks_ref, bdf_ref, bdb_ref, exp_ref, of_ref, ob_ref, c_ref, n_ref, m_ref):
    n = pl.program_id(1)

    @pl.when(n == 0)
    def _():
        c_ref[...] = jnp.zeros_like(c_ref)
        n_ref[...] = jnp.zeros_like(n_ref)
        m_ref[...] = jnp.zeros_like(m_ref)

    c, w = CHUNK, GROUP_W
    bd_f = bdf_ref[...]
    bd_b = bdb_ref[...]
    ones_rows = jnp.ones((SUBLANES, c), BF16)
    row = _iota((c, w), 0)
    dirs = ((qf_ref, kf_ref, vf_ref, gf_ref, cf_ref, of_ref), (qb_ref, kb_ref, vb_ref, gb_ref, cb_ref, ob_ref))
    nb = qf_ref.shape[0]
    chains = [(bi, d, hg) for bi in range(nb) for d in range(2) for hg in range(N_HEAD_GROUPS)]
    lanes = lambda hg: slice(hg * w, (hg + 1) * w)
    gates = [jnp.concatenate([dirs[d][3][bi] for bi in range(nb)], axis=0) for d in range(2)]
    cums = [jnp.concatenate([dirs[d][4][bi] for bi in range(nb)], axis=0) for d in range(2)]
    i_all = [_expand(_split3(gates[d]), exp_ref[1, d, 0]) for d in range(2)]
    bcum_all = [_expand(_split3(cums[d]), exp_ref[1, d, 1]) for d in range(2)]
    per_block = LANES // c
    blocks = range(nb // per_block)
    block_t = lambda a: [a[blk * LANES:(blk + 1) * LANES].T for blk in blocks]
    gates_t = [block_t(gates[d]) for d in range(2)]
    cums_t = [block_t(cums[d]) for d in range(2)]
    rows = lambda bi: slice(bi * c, (bi + 1) * c)
    bcum = [bcum_all[d][rows(bi), lanes(hg)] for bi, d, hg in chains]
    r = [i_all[d][rows(bi), lanes(hg)] - bcum[i] for i, (bi, d, hg) in enumerate(chains)]
    r_t = [_head_rows(gates_t[d][bi // per_block], _gate_column(1, d, 0), hg, bi % per_block)
           - _head_rows(cums_t[d][bi // per_block], _gate_column(1, d, 1), hg, bi % per_block)
           for bi, d, hg in chains]
    qk = []
    for bi, d, hg in chains:
        k_bd = jnp.concatenate([dirs[d][1][bi, :, lanes(hg)]] * HEADS_PER_GROUP, axis=0) * bd_b
        qk.append(_dot(dirs[d][0][bi, :, lanes(hg)], k_bd, _NT))

    pm, mx, inter_w = [], [], []
    for i, (bi, d, hg) in enumerate(chains):
        m = r[i]
        for p in range(LEVELS):
            sh = 1 << p
            if d == 0:
                m = jnp.maximum(m, jnp.where(row >= sh, pltpu.roll(m, sh, 0), NEG))
            else:
                m = jnp.maximum(m, jnp.where(row < c - sh, pltpu.roll(m, c - sh, 0), NEG))
        pm.append(m)
        m_prev = m_ref[i, 0:1, :]
        mx.append(jnp.maximum(m_prev, m))
        inter_w.append(jnp.exp(m_prev - mx[i]))
        qk[i] = qk[i] * jnp.exp(jnp.where(masks_ref[d, M_INCL] > 0.0, r_t[i] - mx[i], NEG))

    num = [_mm(qk[i], _block_diag(dirs[d][2][bi, :, lanes(hg)], bd_b)) for i, (bi, d, hg) in enumerate(chains)]
    den = [_mm_lhs2(qk[i], bd_b) for i in range(len(chains))]
    qc = [_mm(dirs[d][0][bi, :, lanes(hg)], c_ref[i]) for i, (bi, d, hg) in enumerate(chains)]
    qn = [_mm(dirs[d][0][bi, :, lanes(hg)].astype(F32) * n_ref[i, 0:1, :], bd_b)
          for i, (bi, d, hg) in enumerate(chains)]
    for i, (bi, d, hg) in enumerate(chains):
        full_num = num[i] + inter_w[i] * qc[i]
        full_den = den[i] + inter_w[i] * qn[i]
        hidden = full_num / jnp.maximum(jnp.abs(full_den), jnp.exp(-(bcum[i] + mx[i])))
        dirs[d][5][bi, :, lanes(hg)] = hidden.astype(BF16)

    for i, (bi, d, hg) in enumerate(chains):
        last = c - 1 if d == 0 else 0
        m_prev = m_ref[i, 0:1, :]
        b_last = bcum[i][last:last + 1, :]
        m_next = b_last + jnp.maximum(m_prev, pm[i][last:last + 1, :])
        scale_prev = jnp.exp(b_last + m_prev - m_next)
        wgt = jnp.exp(b_last + r[i] - m_next)
        kw = (dirs[d][1][bi, :, lanes(hg)].astype(F32) * wgt).astype(BF16)
        c_ref[i] = c_ref[i] * scale_prev + _dot(kw, dirs[d][2][bi, :, lanes(hg)], _TN) * bd_f
        n_ref[i, 0:1, :] = n_ref[i, 0:1, :] * scale_prev + _dot(ones_rows, kw)[0:1, :]
        m_ref[i, 0:1, :] = m_next


def _mlstm_scan(q, k, v, gates, cumf, cumb, consts, nb):
    bsz, seq, width = q.shape
    nc = seq // CHUNK
    masks, bd_f, bd_b, exp = consts
    fwd = lambda wd: pl.BlockSpec((nb, CHUNK, wd), lambda b, n: (b, n, 0))
    bwd = lambda wd: pl.BlockSpec((nb, CHUNK, wd), lambda b, n: (b, nc - 1 - n, 0))
    n_state = nb * 2 * N_HEAD_GROUPS
    state = pltpu.VMEM((n_state, GROUP_W, GROUP_W), F32)
    row_state = pltpu.VMEM((n_state, SUBLANES, GROUP_W), F32)
    return pl.pallas_call(
        _mlstm_kernel,
        grid=(bsz // nb, nc),
        in_specs=[fwd(width), fwd(width), fwd(width), fwd(LANES), fwd(LANES),
                  bwd(width), bwd(width), bwd(width), bwd(LANES), bwd(LANES),
                  _full(masks), _full(bd_f), _full(bd_b), _full(exp)],
        out_specs=[fwd(width), bwd(width)],
        out_shape=[jax.ShapeDtypeStruct((bsz, seq, width), BF16)] * 2,
        scratch_shapes=[state, row_state, row_state],
        compiler_params=_params(("parallel", "arbitrary")),
        name="mlstm_scan",
    )(q, k, v, gates, cumf, q, k, v, gates, cumb, masks, bd_f, bd_b, exp)


def _outproj_kernel(h_ref, gaf_ref, gab_ref, mlf_ref, mlb_ref, az_ref, bo_ref, mg_ref, gt_ref, ng_ref, mlg_ref,
                    wa_ref, wb_ref, wo_ref, sh_ref, sc_ref, fg_ref, wr_ref, hout_ref, hn_ref, comb_ref):
    ones_bd = _head_ones(WIDTH)
    inv = 1.0 / HEAD_DIM
    tm = h_ref.shape[1]
    n_parts = 2 if tm % (2 * BF16_ROWS) == 0 else 1
    parts = [slice(i * tm // n_parts, (i + 1) * tm // n_parts) for i in range(n_parts)]
    f32 = lambda ref, rs: ref[0, rs, :].astype(F32)

    def branch_inputs(rs):
        oa = f32(gaf_ref, rs) + f32(gab_ref, rs)
        ms = _mm(oa * oa, ones_bd) * inv
        oa = oa * lax.rsqrt(ms + EPS) * ng_ref[...] * _silu(f32(az_ref, rs))
        hb = f32(mlf_ref, rs) + f32(mlb_ref, rs)
        mu = _mm_lhs2(hb, ones_bd) * inv
        tc = hb - mu
        var = _mm(tc * tc, ones_bd) * inv
        hb = tc * lax.rsqrt(var + EPS) * mlg_ref[...] * _sigmoid(f32(bo_ref, rs))
        return oa.astype(BF16), hb.astype(BF16)

    def merged(rs, oa, hb):
        gate = _sigmoid(f32(mg_ref, rs))
        return gate[:, :D_MODEL] * _dot(oa, wa_ref[...]) + gate[:, D_MODEL:] * _dot(hb, wb_ref[...])

    def residual(rs, y):
        h_new = h_ref[0, rs, :] + gt_ref[0] * _mm(y, wo_ref[...])
        hout_ref[0, rs, :] = h_new
        ms2 = jnp.mean(h_new * h_new, axis=-1, keepdims=True)
        hn = h_new * lax.rsqrt(ms2 + EPS) * (fg_ref[...] * (1.0 + sc_ref[0])) + sh_ref[0]
        hn_ref[0, rs, :] = hn.astype(BF16)
        return hn

    def routing(rs, logits):
        lane = _iota(logits.shape, 1)
        lane_f = lane.astype(F32)
        big = 1e9
        gl = jnp.where((lane >= N_EXPERTS) & (lane < N_EXPERTS + N_GROUPS), logits, NEG)
        gmax = jnp.max(gl, axis=-1, keepdims=True)
        gidx = jnp.min(jnp.where(gl == gmax, lane_f - N_EXPERTS, big), axis=-1, keepdims=True)
        p_group = 1.0 / jnp.sum(jnp.exp(gl - gmax), axis=-1, keepdims=True)
        el = jnp.where((lane < N_EXPERTS) & ((lane >> 3).astype(F32) == gidx), logits, NEG)
        v1 = jnp.max(el, axis=-1, keepdims=True)
        i1 = jnp.min(jnp.where(el == v1, lane_f, big), axis=-1, keepdims=True)
        el2 = jnp.where(lane_f == i1, NEG, el)
        v2 = jnp.max(el2, axis=-1, keepdims=True)
        i2 = jnp.min(jnp.where(el2 == v2, lane_f, big), axis=-1, keepdims=True)
        e21 = jnp.exp(v2 - v1)
        w1 = p_group / (1.0 + e21)
        route = jnp.where(lane == R_E1, i1, 0.0) + jnp.where(lane == R_E2, i2, 0.0)
        comb_ref[0, rs, :] = route + jnp.where(lane == R_W1, w1, 0.0) + jnp.where(lane == R_W2, w1 * e21, 0.0)

    branches = [branch_inputs(rs) for rs in parts]
    ys = [merged(rs, *branches[i]) for i, rs in enumerate(parts)]
    hns = [residual(rs, ys[i]) for i, rs in enumerate(parts)]
    logits = []
    for hn in hns:
        hi, lo = _split2(hn)
        both = _dot(jnp.concatenate([hi, lo], axis=1), wr_ref[...])
        logits.append(both[:, :LANES] + both[:, LANES:])
    for i, rs in enumerate(parts):
        routing(rs, logits[i])


def _output_projection(h, gaf, gab, mlf, mlb, az, bo, mg, gt, ng, mlg, wa, wb, wo, sh, sc, fg, wr, tm):
    bsz, seq, d = h.shape
    tile = lambda w: pl.BlockSpec((1, tm, w), lambda b, s: (b, s, 0))
    per_b = pl.BlockSpec((1, 1, d), lambda b, s: (b, 0, 0))
    return pl.pallas_call(
        _outproj_kernel,
        grid=(bsz, seq // tm),
        in_specs=[tile(d), tile(WIDTH), tile(WIDTH), tile(WIDTH), tile(WIDTH), tile(WIDTH), tile(WIDTH),
                  tile(2 * D_MODEL), per_b, _full(ng), _full(mlg), _full(wa), _full(wb), _full(wo), per_b, per_b,
                  _full(fg), _full(wr)],
        out_specs=[tile(d), tile(d), tile(LANES)],
        out_shape=[jax.ShapeDtypeStruct((bsz, seq, d), F32), jax.ShapeDtypeStruct((bsz, seq, d), BF16),
                   jax.ShapeDtypeStruct((bsz, seq, LANES), F32)],
        compiler_params=_params(("parallel", "parallel")),
        name="output_projection",
    )(h, gaf, gab, mlf, mlb, az, bo, mg, gt, ng, mlg, wa, wb, wo, sh, sc, fg, wr)


def _moe_kernel(npass_ref, acc_ref, hn_ref, route_ref, gt_ref, wg_ref, wu_ref, wd_ref, fin_ref, tri_ref,
                o_ref, y_ref, *, groups, final_norm):
    tile = pl.program_id(0) * pl.num_programs(1) + pl.program_id(1)
    tm = hn_ref.shape[1]
    cap = MOE_CAP
    sub = MOE_SUB
    subs = range(tm // sub)
    rows = lambda s: slice(s * sub, (s + 1) * sub)
    experts = range(EXPERTS_PER_GROUP)
    route = route_ref[0]
    lane = _iota(route.shape, 1)
    lane_f = lane.astype(F32)
    column = lambda j: jnp.sum(jnp.where(lane == j, route, 0.0), axis=-1, keepdims=True)
    e1 = column(R_E1)
    e2 = column(R_E2)
    w1 = column(R_W1)
    w2 = column(R_W2)
    slot_lane = _iota((tm, MOE_SLOTS), 1).astype(F32)
    x = hn_ref[0]

    chains = []
    for g in groups:
        first = float(g * EXPERTS_PER_GROUP)
        l1 = e1 - first
        l2 = e2 - first
        in1 = (l1 >= 0.0) & (l1 < EXPERTS_PER_GROUP)
        in2 = (l2 >= 0.0) & (l2 < EXPERTS_PER_GROUP)
        hit1 = in1 & (lane_f == l1)
        hit2 = in2 & (lane_f == l2)
        onehot = (jnp.where(hit1, 1.0, 0.0) + jnp.where(hit2, 1.0, 0.0)).astype(BF16)
        before = jnp.concatenate([_dot(tri_ref[...], onehot[rows(s)]) for s in subs], axis=0)
        rank1 = jnp.sum(jnp.where(hit1, before, 0.0), axis=-1, keepdims=True)
        rank2 = jnp.sum(jnp.where(hit2, before, 0.0), axis=-1, keepdims=True)
        chains.append((l1, l2, in1, in2, rank1, rank2))
    n_chain = len(chains)
    weights = lambda ref, c, e: ref[c * EXPERTS_PER_GROUP + e]

    def one_pass(p, first):
        lo = p * float(cap) if first else (p * cap).astype(F32)
        sel, sel_w = [], []
        for l1, l2, in1, in2, rank1, rank2 in chains:
            s1 = jnp.where(in1 & (rank1 >= lo) & (rank1 < lo + cap), l1 * cap + rank1 - lo, -1.0)
            s2 = jnp.where(in2 & (rank2 >= lo) & (rank2 < lo + cap), l2 * cap + rank2 - lo, -1.0)
            m1 = slot_lane == s1
            m2 = slot_lane == s2
            sel.append(jnp.where(m1 | m2, 1.0, 0.0).astype(BF16))
            sel_w.append((jnp.where(m1, w1, 0.0) + jnp.where(m2, w2, 0.0)).astype(BF16))
        xs = [[_dot(sel[c][rows(s)], x[rows(s)], _TN) for s in subs] for c in range(n_chain)]
        xe = [[jnp.concatenate([xs[c][s][e * cap:(e + 1) * cap] for s in subs], axis=0).astype(BF16) for e in experts]
              for c in range(n_chain)]
        hg = [[_dot(xe[c][e], weights(wg_ref, c, e)) for e in experts] for c in range(n_chain)]
        hu = [[_dot(xe[c][e], weights(wu_ref, c, e)) for e in experts] for c in range(n_chain)]
        act = [[(_silu(hg[c][e]) * hu[c][e]).astype(BF16) for e in experts] for c in range(n_chain)]
        ye = [[_dot(act[c][e], weights(wd_ref, c, e)) for e in experts] for c in range(n_chain)]
        pad = jnp.zeros((MOE_SLOTS - EXPERTS_PER_GROUP * cap, x.shape[1]), F32)
        for s in subs:
            back = None
            for c in range(n_chain):
                ys = jnp.concatenate([ye[c][e][s * cap:(s + 1) * cap] for e in experts] + [pad], axis=0).astype(BF16)
                part = _dot(sel_w[c][rows(s)], ys)
                back = part if back is None else back + part
            if first:
                y_ref[rows(s), :] = back
            else:
                y_ref[rows(s), :] += back

    one_pass(0, True)

    def extra_pass(p, carry):
        one_pass(p, False)
        return carry

    lax.fori_loop(1, npass_ref[tile], extra_pass, 0)
    out = acc_ref[0] + gt_ref[0] * y_ref[...]
    if final_norm:
        ms = jnp.mean(out * out, axis=-1, keepdims=True)
        out = out * lax.rsqrt(ms + EPS) * fin_ref[...]
    o_ref[0] = out


def _moe(h, hn, route, gt, wg, wu, wd, fin, tm, final_norm):
    bsz, seq, d = h.shape
    nt = seq // tm
    per_call = MOE_GROUPS_PER_CALL
    n_calls = N_GROUPS // per_call
    n_sub = tm // MOE_SUB
    ids = route[..., (R_E1, R_E2)].astype(jnp.int32).reshape(bsz * nt, n_sub, MOE_SUB * 2)
    counts = jnp.sum(jax.nn.one_hot(ids, N_EXPERTS, dtype=jnp.int32), axis=2)
    most = jnp.max(counts.reshape(bsz * nt, n_sub, n_calls, per_call * EXPERTS_PER_GROUP), axis=(1, 3))
    npass = ((most + MOE_CAP - 1) // MOE_CAP).T.astype(jnp.int32)
    tri = jnp.asarray(np.tril(np.ones((MOE_SUB, MOE_SUB), np.float32), -1)).astype(BF16)

    tile = lambda w: pl.BlockSpec((1, tm, w), lambda b, s, n: (b, s, 0))
    n_exp = per_call * EXPERTS_PER_GROUP
    for call in range(n_calls):
        resident = lambda shape: pl.BlockSpec(shape, lambda b, s, n, call=call: (call, 0, 0),
                                              pipeline_mode=pl.Buffered(1))
        grid_spec = pltpu.PrefetchScalarGridSpec(
            num_scalar_prefetch=1,
            grid=(bsz, nt),
            in_specs=[tile(d), tile(d), tile(LANES), pl.BlockSpec((1, 1, d), lambda b, s, n: (b, 0, 0)),
                      resident((n_exp, d, D_EXPERT)), resident((n_exp, d, D_EXPERT)), resident((n_exp, D_EXPERT, d)),
                      pl.BlockSpec((1, d), lambda b, s, n: (0, 0)), pl.BlockSpec(tri.shape, lambda b, s, n: (0, 0))],
            out_specs=tile(d),
            scratch_shapes=[pltpu.VMEM((tm, d), F32)],
        )
        groups = tuple(range(call * per_call, (call + 1) * per_call))
        h = pl.pallas_call(
            functools.partial(_moe_kernel, groups=groups, final_norm=final_norm and call == n_calls - 1),
            grid_spec=grid_spec,
            out_shape=jax.ShapeDtypeStruct((bsz, seq, d), F32),
            compiler_params=_params(("parallel", "parallel")),
            name="expert_ffn",
        )(npass[call], h, hn, route, gt, wg, wu, wd, fin, tri)
    return h


def _pick_tile(seq, want):
    tm = min(seq, want)
    assert seq % tm == 0 and tm % CHUNK == 0
    return tm


def kernel(x, c, ada_w, ada_b, norm_mix_g, norm_ffn_g, w_in, gdn_conv_w, gdn_a_log, gdn_dt_bias, gdn_norm_g, mlstm_conv_w, mlstm_i_bias, mlstm_f_bias, mlstm_norm_g, w_branch_a, w_branch_b, w_out, router_group, router_expert, w_gate, w_up, w_down, final_norm_g):
    bsz, seq, d = x.shape
    depth = ada_w.shape[0]
    assert d == D_MODEL and seq % CHUNK == 0
    tm_in = _pick_tile(seq, 512)
    tm_out = _pick_tile(seq, 512)
    tm_moe = _pick_tile(seq, 4 * MOE_SUB)
    tl_gdn = _pick_tile(seq, 4 * LANES)
    nb_scan = LANES // CHUNK
    assert tl_gdn % LANES == 0 and bsz % nb_scan == 0
    row = lambda a: a.reshape(1, -1).astype(F32)
    consts = _scan_constants()

    mod = _modulation(c.astype(F32), ada_w.astype(F32), ada_b.astype(F32))
    h = x.astype(F32)
    for l in range(depth):
        sh1, sc1, gt1, sh2, sc2, gt2 = [mod[l, :, i * d:(i + 1) * d].reshape(bsz, 1, d) for i in range(N_MOD)]

        w = w_in[l].astype(F32)
        o_z = 3 * WIDTH
        o_ag = o_z + WIDTH
        o_bqk = o_ag + 4 * N_HEADS
        o_bv = o_bqk + 2 * WIDTH
        o_bo = o_bv + WIDTH
        o_bg = o_bo + WIDTH
        o_mg = o_bg + 4 * N_HEADS
        wconv = jnp.concatenate([w[:, :o_z], w[:, o_bqk:o_bv]], axis=1).astype(BF16)
        wg = jnp.concatenate([w[:, o_ag:o_bqk], w[:, o_bg:o_mg], jnp.zeros((d, LANES - 8 * N_HEADS), F32)], axis=1)
        wg_hi = wg.astype(BF16)
        wg_lo = (wg - wg_hi.astype(F32)).astype(BF16)
        wrest = jnp.concatenate([w[:, o_z:o_ag].astype(BF16), w[:, o_bv:o_bo].astype(BF16), w[:, o_bo:o_bg].astype(BF16),
                                 w[:, o_mg:].astype(BF16), wg_hi, wg_lo], axis=1)
        cw = jnp.concatenate([gdn_conv_w[l], mlstm_conv_w[l]], axis=1).astype(F32)
        cw = jnp.concatenate([cw, jnp.zeros((SUBLANES - CONV_K, CONV_W), F32)], axis=0)
        zero8 = jnp.zeros((N_HEADS,), F32)
        bias = jnp.concatenate([gdn_dt_bias[l, 0], zero8, gdn_dt_bias[l, 1], zero8,
                                mlstm_i_bias[l, 0], mlstm_f_bias[l, 0], mlstm_i_bias[l, 1], mlstm_f_bias[l, 1],
                                jnp.zeros((LANES - 8 * N_HEADS,), F32)]).astype(F32)
        alog = jnp.concatenate([gdn_a_log[l, 0], zero8, gdn_a_log[l, 1], zero8,
                                jnp.zeros((LANES - 4 * N_HEADS,), F32)]).astype(F32)
        gp = jnp.concatenate([bias[None], alog[None], jnp.zeros((SUBLANES - 2, LANES), F32)], axis=0)

        aq, ak, av, bq, bk, az, bv, bo, mg, gates, cumf, cumb = _input_projection(
            h, sh1, sc1, row(norm_mix_g[l]), wconv, wrest, cw, gp, tm_in)
        nb_state = 4 if bsz % 4 == 0 else nb_scan
        nb_gdn = 8 if bsz % 8 == 0 else nb_state
        gaf, gab = _gdn_state(_gdn_local(aq, ak, av, gates, cumf, cumb, consts, tl_gdn), consts, nb_gdn)
        mlf, mlb = _mlstm_scan(bq, bk, bv, gates, cumf, cumb, consts, nb_gdn)

        wr = jnp.concatenate([router_expert[l].astype(F32), router_group[l].astype(F32),
                              jnp.zeros((d, LANES - N_EXPERTS - N_GROUPS), F32)], axis=1)
        wr_hi = wr.astype(BF16)
        wr_lo = (wr - wr_hi.astype(F32)).astype(BF16)
        wr = jnp.concatenate([jnp.concatenate([wr_hi, wr_lo], axis=1),
                              jnp.concatenate([wr_hi, jnp.zeros_like(wr_hi)], axis=1)], axis=0)
        h, hn2, comb = _output_projection(
            h, gaf, gab, mlf, mlb, az, bo, mg, gt1, row(jnp.tile(gdn_norm_g[l], N_HEADS)), row(mlstm_norm_g[l]),
            w_branch_a[l].astype(BF16), w_branch_b[l].astype(BF16), w_out[l].astype(BF16),
            sh2, sc2, row(norm_ffn_g[l]), wr, tm_out)
        h = _moe(h, hn2, comb, gt2, w_gate[l].astype(BF16), w_up[l].astype(BF16), w_down[l].astype(BF16),
                 row(final_norm_g), tm_moe, final_norm=(l == depth - 1))
    return h.astype(x.dtype)
```

```python
import functools

import numpy as np
import jax
import jax.numpy as jnp
from jax import lax
from jax.experimental import pallas as pl
from jax.experimental.pallas import tpu as pltpu

D_MODEL = 1024
HEAD_DIM = 64
N_HEADS = 8
WIDTH = N_HEADS * HEAD_DIM
CONV_K = 5
CONV_W = 3 * WIDTH + 2 * WIDTH
REST_W = 3 * WIDTH + 2 * D_MODEL
N_GROUPS = 4
EXPERTS_PER_GROUP = 8
N_EXPERTS = N_GROUPS * EXPERTS_PER_GROUP
D_EXPERT = D_MODEL // 4
N_MOD = 6
EPS = 1e-6
NEG = -1e30

CHUNK = 64
HEADS_PER_GROUP = 2
N_HEAD_GROUPS = N_HEADS // HEADS_PER_GROUP
GROUP_W = HEADS_PER_GROUP * HEAD_DIM
LEVELS = 6
LANES = 128
SUBLANES = 8
BF16_ROWS = 16
VMEM_LIMIT = 56 * 1024 * 1024

F32 = jnp.float32
BF16 = jnp.bfloat16

_NN = (((1,), (0,)), ((), ()))
_NT = (((1,), (1,)), ((), ()))
_TN = (((0,), (0,)), ((), ()))

R_E1, R_E2, R_W1, R_W2 = 0, 1, 4, 5
MOE_SUB = 128
MOE_CAP = 24
MOE_GROUPS_PER_CALL = 2
MOE_SLOTS = 256

M_INCL, M_STRICT, M_LEVEL0, M_DIAG = 0, 1, 2, 2 + LEVELS
N_MASKS = M_DIAG + 1


def _dot(a, b, dims=_NN):
    return lax.dot_general(a, b, dims, preferred_element_type=F32)


def _mm(a, b, dims=_NN):
    return _dot(a.astype(BF16), b.astype(BF16), dims)


def _split2(x):
    hi = x.astype(BF16)
    lo = (x - hi.astype(F32)).astype(BF16)
    return hi, lo


def _split3(x):
    hi = x.astype(BF16)
    r = x - hi.astype(F32)
    mid = r.astype(BF16)
    lo = (r - mid.astype(F32)).astype(BF16)
    return hi, mid, lo


def _dot_pieces(pieces, e):
    out = _dot(pieces[0], e)
    for p in pieces[1:]:
        out = out + _dot(p, e)
    return out


def _mm_lhs2(x, e):
    return _dot_pieces(_split2(x), e)


def _mm3(a, b):
    a_hi, a_lo = _split2(a)
    b_hi, b_lo = _split2(b)
    return _dot(a_hi, b_hi) + _dot(a_hi, b_lo) + _dot(a_lo, b_hi)


def _iota(shape, dim):
    return lax.broadcasted_iota(jnp.int32, shape, dim)


def _sigmoid(x):
    return 0.5 * jnp.tanh(0.5 * x) + 0.5


def _silu(x):
    return x * _sigmoid(x)


def _softplus(x):
    return jnp.maximum(x, 0.0) + jnp.log1p(jnp.exp(-jnp.abs(x)))


def _head_ones(width):
    r = _iota((width, width), 0) >> 6
    c = _iota((width, width), 1) >> 6
    return jnp.where(r == c, 1.0, 0.0).astype(BF16)


def _params(sem):
    return pltpu.CompilerParams(dimension_semantics=sem, vmem_limit_bytes=VMEM_LIMIT)


def _full(a):
    return pl.BlockSpec(a.shape, lambda *_: (0,) * a.ndim)


def _mod_kernel(c_ref, w_ref, b_ref, o_ref):
    o_ref[0] = _mm3(_silu(c_ref[...]), w_ref[0]) + b_ref[0]


def _modulation(c, ada_w, ada_b):
    depth, d, n = ada_w.shape
    bsz = c.shape[0]
    tn = n // 4
    return pl.pallas_call(
        _mod_kernel,
        grid=(depth, n // tn),
        in_specs=[
            pl.BlockSpec((bsz, d), lambda l, j: (0, 0)),
            pl.BlockSpec((1, d, tn), lambda l, j: (l, 0, j)),
            pl.BlockSpec((1, 1, tn), lambda l, j: (l, 0, j)),
        ],
        out_specs=pl.BlockSpec((1, bsz, tn), lambda l, j: (l, 0, j)),
        out_shape=jax.ShapeDtypeStruct((depth, bsz, n), F32),
        compiler_params=_params(("parallel", "parallel")),
        name="adaln_mod",
    )(c, ada_w, ada_b.reshape(depth, 1, n))


HALO = BF16_ROWS


def _inproj_kernel(h_ref, hprev_ref, hnext_ref, sh_ref, sc_ref, g_ref, wconv_ref, wrest_ref, cw_ref, gp_ref,
                   perm_ref, permt_ref, aq_ref, ak_ref, av_ref, bq_ref, bk_ref, az_ref, bv_ref, bo_ref, mg_ref, gates_ref, cumf_ref,
                   cumb_ref):
    s = pl.program_id(1)
    ns = pl.num_programs(1)
    tm = h_ref.shape[1]
    gain = g_ref[...] * (1.0 + sc_ref[0])
    shift = sh_ref[0]

    def norm(x):
        ms = jnp.mean(x * x, axis=-1, keepdims=True)
        return x * lax.rsqrt(ms + EPS) * gain + shift

    hn = norm(h_ref[0])
    hp = norm(hprev_ref[0]) * jnp.where(s > 0, 1.0, 0.0)
    hx = norm(hnext_ref[0]) * jnp.where(s < ns - 1, 1.0, 0.0)
    hn_hi = hn.astype(BF16)

    nv = tm // SUBLANES
    edge = CONV_K // 2
    hn_perm = _dot(perm_ref[...], hn_hi).astype(BF16)
    hall = jnp.concatenate([hn_perm, hp.astype(BF16), hx.astype(BF16)], axis=0)
    groups = range(CONV_W // WIDTH)
    cols = lambda g: slice(g * WIDTH, (g + 1) * WIDTH)
    pe = [_dot(hall, wconv_ref[:, cols(g)]) for g in groups]
    rest = _dot(hn_hi, wrest_ref[...])
    hn_lo = (hn - hn_hi.astype(F32)).astype(BF16)
    pre = (rest[:, REST_W:REST_W + LANES] + rest[:, REST_W + LANES:]
           + _dot(hn_lo, wrest_ref[:, REST_W:REST_W + LANES]))

    sub = _iota((SUBLANES, WIDTH), 0)
    ones_bd = _head_ones(WIDTH)
    scale = HEAD_DIM ** -0.5
    out_refs = (aq_ref, ak_ref, av_ref, bq_ref, bk_ref)
    permuted = []
    for g in groups:
        prev = pe[g][tm:tm + HALO]
        nxt = pe[g][tm + HALO:]
        block = lambda v: pe[g][v * SUBLANES:(v + 1) * SUBLANES]
        before = [jnp.where(sub == 0, prev[HALO - edge + i:HALO - edge + i + 1], pltpu.roll(block(nv - edge + i), 1, 0))
                  for i in range(edge)]
        after = [jnp.where(sub == SUBLANES - 1, nxt[i:i + 1], pltpu.roll(block(i), SUBLANES - 1, 0))
                 for i in range(edge)]
        ext = jnp.concatenate(before + [pe[g][:tm]] + after, axis=0)

        conv = jnp.zeros((tm, WIDTH), F32)
        for j in range(CONV_K):
            conv = conv + ext[j * SUBLANES:j * SUBLANES + tm] * cw_ref[j:j + 1, cols(g)]
        cv = _silu(conv)
        if g in (0, 1):
            cv = cv * lax.rsqrt(_mm(cv * cv, ones_bd) + EPS)
        if g in (0, 4):
            cv = cv * scale
        permuted.append(cv.astype(BF16))
    for g in groups:
        out_refs[g][0] = _dot(permt_ref[...], permuted[g]).astype(BF16)

    az_ref[0] = rest[:, 0:WIDTH].astype(BF16)
    bv_ref[0] = rest[:, WIDTH:2 * WIDTH].astype(BF16)
    bo_ref[0] = rest[:, 2 * WIDTH:3 * WIDTH].astype(BF16)
    mg_ref[0] = rest[:, 3 * WIDTH:REST_W].astype(BF16)

    pre = pre + gp_ref[0:1, :]
    lane = _iota(pre.shape, 1)
    second = ((lane >> 3) & 1) == 1
    is_gdn = lane < 4 * N_HEADS
    is_ml = (lane >= 4 * N_HEADS) & (lane < 8 * N_HEADS)
    sp = _softplus(pre)
    gdn_val = jnp.where(second, _sigmoid(pre), -jnp.exp(gp_ref[1:2, :]) * sp)
    ml_val = jnp.where(second, -_softplus(-pre), pre)
    gates = jnp.where(is_gdn, gdn_val, jnp.where(is_ml, ml_val, 0.0))
    gates_ref[0] = gates

    ri = _iota((LANES, LANES), 0)
    ci = _iota((LANES, LANES), 1)
    same_chunk = (ri >> 6) == (ci >> 6)
    pieces = _split3(gates)
    for ref, tri in ((cumf_ref, same_chunk & (ri >= ci)), (cumb_ref, same_chunk & (ri <= ci))):
        tri_b = jnp.where(tri, 1.0, 0.0).astype(BF16)
        for blk in range(tm // LANES):
            rs = slice(blk * LANES, (blk + 1) * LANES)
            ref[0, rs, :] = _dot(tri_b, pieces[0][rs]) + _dot(tri_b, pieces[1][rs]) + _dot(tri_b, pieces[2][rs])


def _input_projection(h, sh, sc, gain, wconv, wrest, cw, gp, tm):
    bsz, seq, d = h.shape
    nt = seq // tm
    per = tm // HALO
    nhalo = seq // HALO
    tile = lambda w: pl.BlockSpec((1, tm, w), lambda b, s: (b, s, 0))
    outs = [(WIDTH, BF16)] * 8 + [(2 * D_MODEL, BF16)] + [(LANES, F32)] * 3
    r = np.arange(tm)
    perm_np = np.zeros((tm, tm), np.float32)
    perm_np[r, (r % SUBLANES) * (tm // SUBLANES) + r // SUBLANES] = 1.0
    perm = jnp.asarray(perm_np).astype(BF16)
    perm_t = jnp.asarray(perm_np.T).astype(BF16)
    return pl.pallas_call(
        _inproj_kernel,
        grid=(bsz, nt),
        in_specs=[
            tile(d),
            pl.BlockSpec((1, HALO, d), lambda b, s: (b, jnp.maximum(s * per - 1, 0), 0)),
            pl.BlockSpec((1, HALO, d), lambda b, s: (b, jnp.minimum((s + 1) * per, nhalo - 1), 0)),
            pl.BlockSpec((1, 1, d), lambda b, s: (b, 0, 0)),
            pl.BlockSpec((1, 1, d), lambda b, s: (b, 0, 0)),
            _full(gain), _full(wconv), _full(wrest), _full(cw), _full(gp), _full(perm), _full(perm_t),
        ],
        out_specs=[tile(w) for w, _ in outs],
        out_shape=[jax.ShapeDtypeStruct((bsz, seq, w), dt) for w, dt in outs],
        compiler_params=_params(("parallel", "parallel")),
        name="input_projection",
    )(h, h, h, sh, sc, gain, wconv, wrest, cw, gp, perm, perm_t)


def _gate_column(branch, direction, kind):
    return branch * 4 * N_HEADS + direction * 2 * N_HEADS + kind * N_HEADS


def _expand(pieces, e):
    rows = pieces[0].shape[0]
    out = _dot(jnp.concatenate(pieces, axis=0), e)
    return functools.reduce(lambda a, b: a + b, [out[i * rows:(i + 1) * rows] for i in range(len(pieces))])


def _head_rows(x_t, column, hg, block):
    first = column + hg * HEADS_PER_GROUP
    pieces = [x_t[first + h:first + h + 1, block * CHUNK:(block + 1) * CHUNK] for h in range(HEADS_PER_GROUP)]
    return jnp.concatenate(pieces, axis=1)


def _scan_constants():
    c, w = CHUNK, GROUP_W
    row = np.arange(c)[:, None]
    col = (np.arange(w) % c)[None, :]
    masks = np.zeros((2, N_MASKS, c, w), np.float32)
    for d in range(2):
        masks[d, M_INCL] = (row >= col) if d == 0 else (row <= col)
        masks[d, M_STRICT] = (row > col) if d == 0 else (row < col)
        for p in range(LEVELS):
            same = (row >> (p + 1)) == (col >> (p + 1))
            r_bit = (row >> p) & 1
            c_bit = (col >> p) & 1
            masks[d, M_LEVEL0 + p] = same & ((r_bit == 1) & (c_bit == 0) if d == 0 else (r_bit == 0) & (c_bit == 1))
        masks[d, M_DIAG] = row == col
    bd = (np.arange(w)[:, None] // HEAD_DIM) == (np.arange(w)[None, :] // HEAD_DIM)
    exp = np.zeros((2, 2, 2, LANES, WIDTH), np.float32)
    lane_head = np.arange(WIDTH) // HEAD_DIM
    for br in range(2):
        for d in range(2):
            for kind in range(2):
                exp[br, d, kind, _gate_column(br, d, kind) + lane_head, np.arange(WIDTH)] = 1.0
    return (jnp.asarray(masks), jnp.asarray(bd.astype(np.float32)),
            jnp.asarray(bd.astype(np.float32)).astype(BF16), jnp.asarray(exp).astype(BF16))


def _block_diag(x_cat, bd_bf16):
    xb = x_cat.astype(BF16)
    return jnp.concatenate([xb] * HEADS_PER_GROUP, axis=0) * bd_bf16


def _gdn_local_kernel(q_ref, k_ref, v_ref, g_ref, cumf_ref, cumb_ref, masks_ref, bdb_ref, exp_ref,
                      uf_ref, wf_ref, qef_ref, inf_ref, kef_ref, cdf_ref,
                      ub_ref, wb_ref, qeb_ref, inb_ref, keb_ref, cdb_ref):
    c, w = CHUNK, GROUP_W
    bd_b = bdb_ref[...]
    outs = ((uf_ref, wf_ref, qef_ref, inf_ref, kef_ref, cdf_ref), (ub_ref, wb_ref, qeb_ref, inb_ref, keb_ref, cdb_ref))
    chains = [(ci, d, hg) for ci in range(q_ref.shape[1] // c) for d in range(2) for hg in range(N_HEAD_GROUPS)]
    rows = lambda ci: slice(ci * c, (ci + 1) * c)
    lanes = lambda hg: slice(hg * w, (hg + 1) * w)

    cums = (cumf_ref[0], cumb_ref[0])
    g_pieces = _split2(g_ref[0])
    gc_all = [_expand(_split3(cums[d]), exp_ref[0, d, 0]) for d in range(2)]
    beta_all = [_expand(g_pieces, exp_ref[0, d, 1]) for d in range(2)]
    per_block = LANES // c
    n_blocks = q_ref.shape[1] // LANES
    cum_t = [[cums[d][blk * LANES:(blk + 1) * LANES].T for blk in range(n_blocks)] for d in range(2)]
    gc = [gc_all[d][rows(ci), lanes(hg)] for ci, d, hg in chains]
    beta = [beta_all[d][rows(ci), lanes(hg)] for ci, d, hg in chains]
    gc_t = [_head_rows(cum_t[d][ci // per_block], _gate_column(0, d, 0), hg, ci % per_block) for ci, d, hg in chains]

    qk_kk = {}
    for ci, d, hg in chains:
        if d == 0:
            q = q_ref[0, rows(ci), lanes(hg)]
            k = k_ref[0, rows(ci), lanes(hg)]
            k_bd = jnp.concatenate([k] * HEADS_PER_GROUP, axis=0) * bd_b
            qk_kk[ci, hg] = _dot(jnp.concatenate([q, k], axis=0), k_bd, _NT)
    kbeta = [k_ref[0, rows(ci), lanes(hg)].astype(F32) * beta[i] for i, (ci, d, hg) in enumerate(chains)]

    a, t = [], []
    for i, (ci, d, hg) in enumerate(chains):
        decay = jnp.exp(jnp.minimum(gc[i] - gc_t[i], 0.0)) * masks_ref[d, M_INCL]
        outs[d][3][0, rows(ci), lanes(hg)] = (qk_kk[ci, hg][:c] * decay).astype(BF16)
        a.append(qk_kk[ci, hg][c:] * beta[i] * decay * masks_ref[d, M_STRICT])
        t.append(masks_ref[d, M_DIAG] - a[i] * masks_ref[d, M_LEVEL0])

    for p in range(1, LEVELS):
        y = [_mm(t[i], _block_diag(a[i] * masks_ref[d, M_LEVEL0 + p], bd_b)) for i, (_, d, _) in enumerate(chains)]
        t = [t[i] - _mm(y[i], _block_diag(t[i], bd_b)) for i in range(len(chains))]

    for i, (ci, d, hg) in enumerate(chains):
        u_ref, w_ref, qe_ref, _, ke_ref, cd_ref = outs[d]
        last = c - 1 if d == 0 else 0
        g_last = gc[i][last:last + 1, :]
        e_gc = jnp.exp(gc[i])
        v = v_ref[0, rows(ci), lanes(hg)].astype(F32)
        u_ref[0, rows(ci), lanes(hg)] = _mm(t[i], _block_diag(v * beta[i], bd_b))
        w_ref[0, rows(ci), lanes(hg)] = _mm(t[i], _block_diag(kbeta[i] * e_gc, bd_b)).astype(BF16)
        qe_ref[0, rows(ci), lanes(hg)] = (q_ref[0, rows(ci), lanes(hg)].astype(F32) * e_gc).astype(BF16)
        k = k_ref[0, rows(ci), lanes(hg)].astype(F32)
        ke_ref[0, rows(ci), lanes(hg)] = (k * jnp.exp(g_last - gc[i])).astype(BF16)
        cd_ref[0, ci, :, lanes(hg)] = jnp.exp(g_last)


def _gdn_local(q, k, v, gates, cumf, cumb, consts, tl):
    bsz, seq, width = q.shape
    masks, _, bd_b, exp = consts
    tile = lambda wd: pl.BlockSpec((1, tl, wd), lambda b, s: (b, s, 0))
    cd_spec = pl.BlockSpec((1, tl // CHUNK, 1, width), lambda b, s: (b, s, 0, 0))
    big = lambda dt: jax.ShapeDtypeStruct((bsz, seq, width), dt)
    cd_shape = jax.ShapeDtypeStruct((bsz, seq // CHUNK, 1, width), F32)
    per_dir_specs = [tile(width)] * 5 + [cd_spec]
    per_dir_shapes = [big(F32), big(BF16), big(BF16), big(BF16), big(BF16), cd_shape]
    return pl.pallas_call(
        _gdn_local_kernel,
        grid=(bsz, seq // tl),
        in_specs=[tile(width), tile(width), tile(width), tile(LANES), tile(LANES), tile(LANES),
                  _full(masks), _full(bd_b), _full(exp)],
        out_specs=per_dir_specs * 2,
        out_shape=per_dir_shapes * 2,
        compiler_params=_params(("parallel", "parallel")),
        name="gdn_local",
    )(q, k, v, gates, cumf, cumb, masks, bd_b, exp)


def _gdn_state_kernel(uf_ref, wf_ref, qef_ref, inf_ref, kef_ref, cdf_ref,
                      ub_ref, wb_ref, qeb_ref, inb_ref, keb_ref, cdb_ref, bdf_ref, bdb_ref, of_ref, ob_ref, s_ref):
    n = pl.program_id(1)

    @pl.when(n == 0)
    def _():
        s_ref[...] = jnp.zeros_like(s_ref)

    c, w = CHUNK, GROUP_W
    bd_b = bdb_ref[...]
    dirs = ((uf_ref, wf_ref, qef_ref, inf_ref, kef_ref, cdf_ref, of_ref),
            (ub_ref, wb_ref, qeb_ref, inb_ref, keb_ref, cdb_ref, ob_ref))
    nb = uf_ref.shape[0]
    chains = [(bi, d, hg) for bi in range(nb) for d in range(2) for hg in range(N_HEAD_GROUPS)]
    lanes = lambda hg: slice(hg * w, (hg + 1) * w)
    ws_qs = []
    for idx, (bi, d, hg) in enumerate(chains):
        w_ref, qe_ref = dirs[d][1], dirs[d][2]
        lhs = jnp.concatenate([w_ref[bi, :, lanes(hg)], qe_ref[bi, :, lanes(hg)]], axis=0)
        ws_qs.append(_dot(lhs, s_ref[idx].astype(BF16)))
    v_new = [(dirs[d][0][bi, :, lanes(hg)] - ws_qs[idx][:c]).astype(BF16) for idx, (bi, d, hg) in enumerate(chains)]
    for idx, (bi, d, hg) in enumerate(chains):
        in_ref, o_ref = dirs[d][3], dirs[d][6]
        o = ws_qs[idx][c:] + _dot(in_ref[bi, :, lanes(hg)], _block_diag(v_new[idx], bd_b))
        o_ref[bi, :, lanes(hg)] = o.astype(BF16)
    for idx, (bi, d, hg) in enumerate(chains):
        ke_ref, cd_ref = dirs[d][4], dirs[d][5]
        ds = _dot(ke_ref[bi, :, lanes(hg)], v_new[idx], _TN)
        s_ref[idx] = s_ref[idx] * cd_ref[bi, 0, :, lanes(hg)] + ds * bdf_ref[...]


def _gdn_state(local_outs, consts, nb):
    uf = local_outs[0]
    bsz, seq, width = uf.shape
    nc = seq // CHUNK
    _, bd_f, bd_b, _ = consts
    fwd = pl.BlockSpec((nb, CHUNK, width), lambda b, n: (b, n, 0))
    bwd = pl.BlockSpec((nb, CHUNK, width), lambda b, n: (b, nc - 1 - n, 0))
    cd_fwd = pl.BlockSpec((nb, 1, 1, width), lambda b, n: (b, n, 0, 0))
    cd_bwd = pl.BlockSpec((nb, 1, 1, width), lambda b, n: (b, nc - 1 - n, 0, 0))
    return pl.pallas_call(
        _gdn_state_kernel,
        grid=(bsz // nb, nc),
        in_specs=[fwd] * 5 + [cd_fwd] + [bwd] * 5 + [cd_bwd] + [_full(bd_f), _full(bd_b)],
        out_specs=[fwd, bwd],
        out_shape=[jax.ShapeDtypeStruct((bsz, seq, width), BF16)] * 2,
        scratch_shapes=[pltpu.VMEM((nb * 2 * N_HEAD_GROUPS, GROUP_W, GROUP_W), F32)],
        compiler_params=_params(("parallel", "arbitrary")),
        name="gdn_state",
    )(*local_outs, bd_f, bd_b)


def _mlstm_kernel(qf_ref, kf_ref, vf_ref, gf_ref, cf_ref, qb_ref, kb_ref, vb_ref, gb_ref, cb_ref,
                  masks_ref, bdf_ref, bdb_ref, exp_ref, of_ref, ob_ref, c_ref, n_ref, m_ref):
    n = pl.program_id(1)

    @pl.when(n == 0)
    def _():
        c_ref[...] = jnp.zeros_like(c_ref)
        n_ref[...] = jnp.zeros_like(n_ref)
        m_ref[...] = jnp.zeros_like(m_ref)

    c, w = CHUNK, GROUP_W
    bd_f = bdf_ref[...]
    bd_b = bdb_ref[...]
    ones_rows = jnp.ones((SUBLANES, c), BF16)
    row = _iota((c, w), 0)
    dirs = ((qf_ref, kf_ref, vf_ref, gf_ref, cf_ref, of_ref), (qb_ref, kb_ref, vb_ref, gb_ref, cb_ref, ob_ref))
    nb = qf_ref.shape[0]
    chains = [(bi, d, hg) for bi in range(nb) for d in range(2) for hg in range(N_HEAD_GROUPS)]
    lanes = lambda hg: slice(hg * w, (hg + 1) * w)
    gates = [jnp.concatenate([dirs[d][3][bi] for bi in range(nb)], axis=0) for d in range(2)]
    cums = [jnp.concatenate([dirs[d][4][bi] for bi in range(nb)], axis=0) for d in range(2)]
    i_all = [_expand(_split3(gates[d]), exp_ref[1, d, 0]) for d in range(2)]
    bcum_all = [_expand(_split3(cums[d]), exp_ref[1, d, 1]) for d in range(2)]
    per_block = LANES // c
    blocks = range(nb // per_block)
    block_t = lambda a: [a[blk * LANES:(blk + 1) * LANES].T for blk in blocks]
    gates_t = [block_t(gates[d]) for d in range(2)]
    cums_t = [block_t(cums[d]) for d in range(2)]
    rows = lambda bi: slice(bi * c, (bi + 1) * c)
    bcum = [bcum_all[d][rows(bi), lanes(hg)] for bi, d, hg in chains]
    r = [i_all[d][rows(bi), lanes(hg)] - bcum[i] for i, (bi, d, hg) in enumerate(chains)]
    r_t = [_head_rows(gates_t[d][bi // per_block], _gate_column(1, d, 0), hg, bi % per_block)
           - _head_rows(cums_t[d][bi // per_block], _gate_column(1, d, 1), hg, bi % per_block)
           for bi, d, hg in chains]
    qk = []
    for bi, d, hg in chains:
        k_bd = jnp.concatenate([dirs[d][1][bi, :, lanes(hg)]] * HEADS_PER_GROUP, axis=0) * bd_b
        qk.append(_dot(dirs[d][0][bi, :, lanes(hg)], k_bd, _NT))

    pm, mx, inter_w = [], [], []
    for i, (bi, d, hg) in enumerate(chains):
        m = r[i]
        for p in range(LEVELS):
            sh = 1 << p
            if d == 0:
                m = jnp.maximum(m, jnp.where(row >= sh, pltpu.roll(m, sh, 0), NEG))
            else:
                m = jnp.maximum(m, jnp.where(row < c - sh, pltpu.roll(m, c - sh, 0), NEG))
        pm.append(m)
        m_prev = m_ref[i, 0:1, :]
        mx.append(jnp.maximum(m_prev, m))
        inter_w.append(jnp.exp(m_prev - mx[i]))
        qk[i] = qk[i] * jnp.exp(jnp.where(masks_ref[d, M_INCL] > 0.0, r_t[i] - mx[i], NEG))

    num = [_mm(qk[i], _block_diag(dirs[d][2][bi, :, lanes(hg)], bd_b)) for i, (bi, d, hg) in enumerate(chains)]
    den = [_mm_lhs2(qk[i], bd_b) for i in range(len(chains))]
    qc = [_mm(dirs[d][0][bi, :, lanes(hg)], c_ref[i]) for i, (bi, d, hg) in enumerate(chains)]
    qn = [_mm(dirs[d][0][bi, :, lanes(hg)].astype(F32) * n_ref[i, 0:1, :], bd_b)
          for i, (bi, d, hg) in enumerate(chains)]
    for i, (bi, d, hg) in enumerate(chains):
        full_num = num[i] + inter_w[i] * qc[i]
        full_den = den[i] + inter_w[i] * qn[i]
        hidden = full_num / jnp.maximum(jnp.abs(full_den), jnp.exp(-(bcum[i] + mx[i])))
        dirs[d][5][bi, :, lanes(hg)] = hidden.astype(BF16)

    for i, (bi, d, hg) in enumerate(chains):
        last = c - 1 if d == 0 else 0
        m_prev = m_ref[i, 0:1, :]
        b_last = bcum[i][last:last + 1, :]
        m_next = b_last + jnp.maximum(m_prev, pm[i][last:last + 1, :])
        scale_prev = jnp.exp(b_last + m_prev - m_next)
        wgt = jnp.exp(b_last + r[i] - m_next)
        kw = (dirs[d][1][bi, :, lanes(hg)].astype(F32) * wgt).astype(BF16)
        c_ref[i] = c_ref[i] * scale_prev + _dot(kw, dirs[d][2][bi, :, lanes(hg)], _TN) * bd_f
        n_ref[i, 0:1, :] = n_ref[i, 0:1, :] * scale_prev + _dot(ones_rows, kw)[0:1, :]
        m_ref[i, 0:1, :] = m_next


def _mlstm_scan(q, k, v, gates, cumf, cumb, consts, nb):
    bsz, seq, width = q.shape
    nc = seq // CHUNK
    masks, bd_f, bd_b, exp = consts
    fwd = lambda wd: pl.BlockSpec((nb, CHUNK, wd), lambda b, n: (b, n, 0))
    bwd = lambda wd: pl.BlockSpec((nb, CHUNK, wd), lambda b, n: (b, nc - 1 - n, 0))
    n_state = nb * 2 * N_HEAD_GROUPS
    state = pltpu.VMEM((n_state, GROUP_W, GROUP_W), F32)
    row_state = pltpu.VMEM((n_state, SUBLANES, GROUP_W), F32)
    return pl.pallas_call(
        _mlstm_kernel,
        grid=(bsz // nb, nc),
        in_specs=[fwd(width), fwd(width), fwd(width), fwd(LANES), fwd(LANES),
                  bwd(width), bwd(width), bwd(width), bwd(LANES), bwd(LANES),
                  _full(masks), _full(bd_f), _full(bd_b), _full(exp)],
        out_specs=[fwd(width), bwd(width)],
        out_shape=[jax.ShapeDtypeStruct((bsz, seq, width), BF16)] * 2,
        scratch_shapes=[state, row_state, row_state],
        compiler_params=_params(("parallel", "arbitrary")),
        name="mlstm_scan",
    )(q, k, v, gates, cumf, q, k, v, gates, cumb, masks, bd_f, bd_b, exp)


def _outproj_kernel(h_ref, gaf_ref, gab_ref, mlf_ref, mlb_ref, az_ref, bo_ref, mg_ref, gt_ref, ng_ref, mlg_ref,
                    wa_ref, wb_ref, wo_ref, sh_ref, sc_ref, fg_ref, wr_ref, hout_ref, hn_ref, comb_ref):
    ones_bd = _head_ones(WIDTH)
    inv = 1.0 / HEAD_DIM
    tm = h_ref.shape[1]
    n_parts = 2 if tm % (2 * BF16_ROWS) == 0 else 1
    parts = [slice(i * tm // n_parts, (i + 1) * tm // n_parts) for i in range(n_parts)]
    f32 = lambda ref, rs: ref[0, rs, :].astype(F32)

    def branch_inputs(rs):
        oa = f32(gaf_ref, rs) + f32(gab_ref, rs)
        ms = _mm(oa * oa, ones_bd) * inv
        oa = oa * lax.rsqrt(ms + EPS) * ng_ref[...] * _silu(f32(az_ref, rs))
        hb = f32(mlf_ref, rs) + f32(mlb_ref, rs)
        mu = _mm_lhs2(hb, ones_bd) * inv
        tc = hb - mu
        var = _mm(tc * tc, ones_bd) * inv
        hb = tc * lax.rsqrt(var + EPS) * mlg_ref[...] * _sigmoid(f32(bo_ref, rs))
        return oa.astype(BF16), hb.astype(BF16)

    def merged(rs, oa, hb):
        gate = _sigmoid(f32(mg_ref, rs))
        return gate[:, :D_MODEL] * _dot(oa, wa_ref[...]) + gate[:, D_MODEL:] * _dot(hb, wb_ref[...])

    def residual(rs, y):
        h_new = h_ref[0, rs, :] + gt_ref[0] * _mm(y, wo_ref[...])
        hout_ref[0, rs, :] = h_new
        ms2 = jnp.mean(h_new * h_new, axis=-1, keepdims=True)
        hn = h_new * lax.rsqrt(ms2 + EPS) * (fg_ref[...] * (1.0 + sc_ref[0])) + sh_ref[0]
        hn_ref[0, rs, :] = hn.astype(BF16)
        return hn

    def routing(rs, logits):
        lane = _iota(logits.shape, 1)
        lane_f = lane.astype(F32)
        big = 1e9
        gl = jnp.where((lane >= N_EXPERTS) & (lane < N_EXPERTS + N_GROUPS), logits, NEG)
        gmax = jnp.max(gl, axis=-1, keepdims=True)
        gidx = jnp.min(jnp.where(gl == gmax, lane_f - N_EXPERTS, big), axis=-1, keepdims=True)
        p_group = 1.0 / jnp.sum(jnp.exp(gl - gmax), axis=-1, keepdims=True)
        el = jnp.where((lane < N_EXPERTS) & ((lane >> 3).astype(F32) == gidx), logits, NEG)
        v1 = jnp.max(el, axis=-1, keepdims=True)
        i1 = jnp.min(jnp.where(el == v1, lane_f, big), axis=-1, keepdims=True)
        el2 = jnp.where(lane_f == i1, NEG, el)
        v2 = jnp.max(el2, axis=-1, keepdims=True)
        i2 = jnp.min(jnp.where(el2 == v2, lane_f, big), axis=-1, keepdims=True)
        e21 = jnp.exp(v2 - v1)
        w1 = p_group / (1.0 + e21)
        route = jnp.where(lane == R_E1, i1, 0.0) + jnp.where(lane == R_E2, i2, 0.0)
        comb_ref[0, rs, :] = route + jnp.where(lane == R_W1, w1, 0.0) + jnp.where(lane == R_W2, w1 * e21, 0.0)

    branches = [branch_inputs(rs) for rs in parts]
    ys = [merged(rs, *branches[i]) for i, rs in enumerate(parts)]
    hns = [residual(rs, ys[i]) for i, rs in enumerate(parts)]
    logits = []
    for hn in hns:
        hi, lo = _split2(hn)
        both = _dot(jnp.concatenate([hi, lo], axis=1), wr_ref[...])
        logits.append(both[:, :LANES] + both[:, LANES:])
    for i, rs in enumerate(parts):
        routing(rs, logits[i])


def _output_projection(h, gaf, gab, mlf, mlb, az, bo, mg, gt, ng, mlg, wa, wb, wo, sh, sc, fg, wr, tm):
    bsz, seq, d = h.shape
    tile = lambda w: pl.BlockSpec((1, tm, w), lambda b, s: (b, s, 0))
    per_b = pl.BlockSpec((1, 1, d), lambda b, s: (b, 0, 0))
    return pl.pallas_call(
        _outproj_kernel,
        grid=(bsz, seq // tm),
        in_specs=[tile(d), tile(WIDTH), tile(WIDTH), tile(WIDTH), tile(WIDTH), tile(WIDTH), tile(WIDTH),
                  tile(2 * D_MODEL), per_b, _full(ng), _full(mlg), _full(wa), _full(wb), _full(wo), per_b, per_b,
                  _full(fg), _full(wr)],
        out_specs=[tile(d), tile(d), tile(LANES)],
        out_shape=[jax.ShapeDtypeStruct((bsz, seq, d), F32), jax.ShapeDtypeStruct((bsz, seq, d), BF16),
                   jax.ShapeDtypeStruct((bsz, seq, LANES), F32)],
        compiler_params=_params(("parallel", "parallel")),
        name="output_projection",
    )(h, gaf, gab, mlf, mlb, az, bo, mg, gt, ng, mlg, wa, wb, wo, sh, sc, fg, wr)


def _moe_kernel(npass_ref, acc_ref, hn_ref, route_ref, gt_ref, wg_ref, wu_ref, wd_ref, fin_ref, tri_ref,
                o_ref, y_ref, *, groups, final_norm):
    tile = pl.program_id(0) * pl.num_programs(1) + pl.program_id(1)
    tm = hn_ref.shape[1]
    cap = MOE_CAP
    sub = MOE_SUB
    subs = range(tm // sub)
    rows = lambda s: slice(s * sub, (s + 1) * sub)
    experts = range(EXPERTS_PER_GROUP)
    route = route_ref[0]
    lane = _iota(route.shape, 1)
    lane_f = lane.astype(F32)
    column = lambda j: jnp.sum(jnp.where(lane == j, route, 0.0), axis=-1, keepdims=True)
    e1 = column(R_E1)
    e2 = column(R_E2)
    w1 = column(R_W1)
    w2 = column(R_W2)
    slot_lane = _iota((tm, MOE_SLOTS), 1).astype(F32)
    x = hn_ref[0]

    chains = []
    for g in groups:
        first = float(g * EXPERTS_PER_GROUP)
        l1 = e1 - first
        l2 = e2 - first
        in1 = (l1 >= 0.0) & (l1 < EXPERTS_PER_GROUP)
        in2 = (l2 >= 0.0) & (l2 < EXPERTS_PER_GROUP)
        hit1 = in1 & (lane_f == l1)
        hit2 = in2 & (lane_f == l2)
        onehot = (jnp.where(hit1, 1.0, 0.0) + jnp.where(hit2, 1.0, 0.0)).astype(BF16)
        before = jnp.concatenate([_dot(tri_ref[...], onehot[rows(s)]) for s in subs], axis=0)
        rank1 = jnp.sum(jnp.where(hit1, before, 0.0), axis=-1, keepdims=True)
        rank2 = jnp.sum(jnp.where(hit2, before, 0.0), axis=-1, keepdims=True)
        chains.append((l1, l2, in1, in2, rank1, rank2))
    n_chain = len(chains)
    weights = lambda ref, c, e: ref[c * EXPERTS_PER_GROUP + e]

    def one_pass(p, first):
        lo = p * float(cap) if first else (p * cap).astype(F32)
        sel, sel_w = [], []
        for l1, l2, in1, in2, rank1, rank2 in chains:
            s1 = jnp.where(in1 & (rank1 >= lo) & (rank1 < lo + cap), l1 * cap + rank1 - lo, -1.0)
            s2 = jnp.where(in2 & (rank2 >= lo) & (rank2 < lo + cap), l2 * cap + rank2 - lo, -1.0)
            m1 = slot_lane == s1
            m2 = slot_lane == s2
            sel.append(jnp.where(m1 | m2, 1.0, 0.0).astype(BF16))
            sel_w.append((jnp.where(m1, w1, 0.0) + jnp.where(m2, w2, 0.0)).astype(BF16))
        xs = [[_dot(sel[c][rows(s)], x[rows(s)], _TN) for s in subs] for c in range(n_chain)]
        xe = [[jnp.concatenate([xs[c][s][e * cap:(e + 1) * cap] for s in subs], axis=0).astype(BF16) for e in experts]
              for c in range(n_chain)]
        hg = [[_dot(xe[c][e], weights(wg_ref, c, e)) for e in experts] for c in range(n_chain)]
        hu = [[_dot(xe[c][e], weights(wu_ref, c, e)) for e in experts] for c in range(n_chain)]
        act = [[(_silu(hg[c][e]) * hu[c][e]).astype(BF16) for e in experts] for c in range(n_chain)]
        ye = [[_dot(act[c][e], weights(wd_ref, c, e)) for e in experts] for c in range(n_chain)]
        pad = jnp.zeros((MOE_SLOTS - EXPERTS_PER_GROUP * cap, x.shape[1]), F32)
        for s in subs:
            back = None
            for c in range(n_chain):
                ys = jnp.concatenate([ye[c][e][s * cap:(s + 1) * cap] for e in experts] + [pad], axis=0).astype(BF16)
                part = _dot(sel_w[c][rows(s)], ys)
                back = part if back is None else back + part
            if first:
                y_ref[rows(s), :] = back
            else:
                y_ref[rows(s), :] += back

    one_pass(0, True)

    def extra_pass(p, carry):
        one_pass(p, False)
        return carry

    lax.fori_loop(1, npass_ref[tile], extra_pass, 0)
    out = acc_ref[0] + gt_ref[0] * y_ref[...]
    if final_norm:
        ms = jnp.mean(out * out, axis=-1, keepdims=True)
        out = out * lax.rsqrt(ms + EPS) * fin_ref[...]
    o_ref[0] = out


def _moe(h, hn, route, gt, wg, wu, wd, fin, tm, final_norm):
    bsz, seq, d = h.shape
    nt = seq // tm
    per_call = MOE_GROUPS_PER_CALL
    n_calls = N_GROUPS // per_call
    n_sub = tm // MOE_SUB
    ids = route[..., (R_E1, R_E2)].astype(jnp.int32).reshape(bsz * nt, n_sub, MOE_SUB * 2)
    counts = jnp.sum(jax.nn.one_hot(ids, N_EXPERTS, dtype=jnp.int32), axis=2)
    most = jnp.max(counts.reshape(bsz * nt, n_sub, n_calls, per_call * EXPERTS_PER_GROUP), axis=(1, 3))
    npass = ((most + MOE_CAP - 1) // MOE_CAP).T.astype(jnp.int32)
    tri = jnp.asarray(np.tril(np.ones((MOE_SUB, MOE_SUB), np.float32), -1)).astype(BF16)

    tile = lambda w: pl.BlockSpec((1, tm, w), lambda b, s, n: (b, s, 0))
    n_exp = per_call * EXPERTS_PER_GROUP
    for call in range(n_calls):
        resident = lambda shape: pl.BlockSpec(shape, lambda b, s, n, call=call: (call, 0, 0),
                                              pipeline_mode=pl.Buffered(1))
        grid_spec = pltpu.PrefetchScalarGridSpec(
            num_scalar_prefetch=1,
            grid=(bsz, nt),
            in_specs=[tile(d), tile(d), tile(LANES), pl.BlockSpec((1, 1, d), lambda b, s, n: (b, 0, 0)),
                      resident((n_exp, d, D_EXPERT)), resident((n_exp, d, D_EXPERT)), resident((n_exp, D_EXPERT, d)),
                      pl.BlockSpec((1, d), lambda b, s, n: (0, 0)), pl.BlockSpec(tri.shape, lambda b, s, n: (0, 0))],
            out_specs=tile(d),
            scratch_shapes=[pltpu.VMEM((tm, d), F32)],
        )
        groups = tuple(range(call * per_call, (call + 1) * per_call))
        h = pl.pallas_call(
            functools.partial(_moe_kernel, groups=groups, final_norm=final_norm and call == n_calls - 1),
            grid_spec=grid_spec,
            out_shape=jax.ShapeDtypeStruct((bsz, seq, d), F32),
            compiler_params=_params(("parallel", "parallel")),
            name="expert_ffn",
        )(npass[call], h, hn, route, gt, wg, wu, wd, fin, tri)
    return h


def _pick_tile(seq, want):
    tm = min(seq, want)
    assert seq % tm == 0 and tm % CHUNK == 0
    return tm


def kernel(x, c, ada_w, ada_b, norm_mix_g, norm_ffn_g, w_in, gdn_conv_w, gdn_a_log, gdn_dt_bias, gdn_norm_g, mlstm_conv_w, mlstm_i_bias, mlstm_f_bias, mlstm_norm_g, w_branch_a, w_branch_b, w_out, router_group, router_expert, w_gate, w_up, w_down, final_norm_g):
    bsz, seq, d = x.shape
    depth = ada_w.shape[0]
    assert d == D_MODEL and seq % CHUNK == 0
    tm_in = _pick_tile(seq, 512)
    tm_out = _pick_tile(seq, 512)
    tm_moe = _pick_tile(seq, 4 * MOE_SUB)
    tl_gdn = _pick_tile(seq, 4 * LANES)
    nb_scan = LANES // CHUNK
    assert tl_gdn % LANES == 0 and bsz % nb_scan == 0
    row = lambda a: a.reshape(1, -1).astype(F32)
    consts = _scan_constants()

    mod = _modulation(c.astype(F32), ada_w.astype(F32), ada_b.astype(F32))
    h = x.astype(F32)
    for l in range(depth):
        sh1, sc1, gt1, sh2, sc2, gt2 = [mod[l, :, i * d:(i + 1) * d].reshape(bsz, 1, d) for i in range(N_MOD)]

        w = w_in[l].astype(F32)
        o_z = 3 * WIDTH
        o_ag = o_z + WIDTH
        o_bqk = o_ag + 4 * N_HEADS
        o_bv = o_bqk + 2 * WIDTH
        o_bo = o_bv + WIDTH
        o_bg = o_bo + WIDTH
        o_mg = o_bg + 4 * N_HEADS
        wconv = jnp.concatenate([w[:, :o_z], w[:, o_bqk:o_bv]], axis=1).astype(BF16)
        wg = jnp.concatenate([w[:, o_ag:o_bqk], w[:, o_bg:o_mg], jnp.zeros((d, LANES - 8 * N_HEADS), F32)], axis=1)
        wg_hi = wg.astype(BF16)
        wg_lo = (wg - wg_hi.astype(F32)).astype(BF16)
        wrest = jnp.concatenate([w[:, o_z:o_ag].astype(BF16), w[:, o_bv:o_bo].astype(BF16), w[:, o_bo:o_bg].astype(BF16),
                                 w[:, o_mg:].astype(BF16), wg_hi, wg_lo], axis=1)
        cw = jnp.concatenate([gdn_conv_w[l], mlstm_conv_w[l]], axis=1).astype(F32)
        cw = jnp.concatenate([cw, jnp.zeros((SUBLANES - CONV_K, CONV_W), F32)], axis=0)
        zero8 = jnp.zeros((N_HEADS,), F32)
        bias = jnp.concatenate([gdn_dt_bias[l, 0], zero8, gdn_dt_bias[l, 1], zero8,
                                mlstm_i_bias[l, 0], mlstm_f_bias[l, 0], mlstm_i_bias[l, 1], mlstm_f_bias[l, 1],
                                jnp.zeros((LANES - 8 * N_HEADS,), F32)]).astype(F32)
        alog = jnp.concatenate([gdn_a_log[l, 0], zero8, gdn_a_log[l, 1], zero8,
                                jnp.zeros((LANES - 4 * N_HEADS,), F32)]).astype(F32)
        gp = jnp.concatenate([bias[None], alog[None], jnp.zeros((SUBLANES - 2, LANES), F32)], axis=0)

        aq, ak, av, bq, bk, az, bv, bo, mg, gates, cumf, cumb = _input_projection(
            h, sh1, sc1, row(norm_mix_g[l]), wconv, wrest, cw, gp, tm_in)
        nb_state = 4 if bsz % 4 == 0 else nb_scan
        nb_gdn = 8 if bsz % 8 == 0 else nb_state
        gaf, gab = _gdn_state(_gdn_local(aq, ak, av, gates, cumf, cumb, consts, tl_gdn), consts, nb_gdn)
        mlf, mlb = _mlstm_scan(bq, bk, bv, gates, cumf, cumb, consts, nb_gdn)

        wr = jnp.concatenate([router_expert[l].astype(F32), router_group[l].astype(F32),
                              jnp.zeros((d, LANES - N_EXPERTS - N_GROUPS), F32)], axis=1)
        wr_hi = wr.astype(BF16)
        wr_lo = (wr - wr_hi.astype(F32)).astype(BF16)
        wr = jnp.concatenate([jnp.concatenate([wr_hi, wr_lo], axis=1),
                              jnp.concatenate([wr_hi, jnp.zeros_like(wr_hi)], axis=1)], axis=0)
        h, hn2, comb = _output_projection(
            h, gaf, gab, mlf, mlb, az, bo, mg, gt1, row(jnp.tile(gdn_norm_g[l], N_HEADS)), row(mlstm_norm_g[l]),
            w_branch_a[l].astype(BF16), w_branch_b[l].astype(BF16), w_out[l].astype(BF16),
            sh2, sc2, row(norm_ffn_g[l]), wr, tm_out)
        h = _moe(h, hn2, comb, gt2, w_gate[l].astype(BF16), w_up[l].astype(BF16), w_down[l].astype(BF16),
                 row(final_norm_g), tm_moe, final_norm=(l == depth - 1))
    return h.astype(x.dtype)
```

```python
import functools

import numpy as np
import jax
import jax.numpy as jnp
from jax import lax
from jax.experimental import pallas as pl
from jax.experimental.pallas import tpu as pltpu

D_MODEL = 1024
HEAD_DIM = 64
N_HEADS = 8
WIDTH = N_HEADS * HEAD_DIM
CONV_K = 5
CONV_W = 3 * WIDTH + 2 * WIDTH
REST_W = 3 * WIDTH + 2 * D_MODEL
N_GROUPS = 4
EXPERTS_PER_GROUP = 8
N_EXPERTS = N_GROUPS * EXPERTS_PER_GROUP
D_EXPERT = D_MODEL // 4
N_MOD = 6
EPS = 1e-6
NEG = -1e30

CHUNK = 64
HEADS_PER_GROUP = 2
N_HEAD_GROUPS = N_HEADS // HEADS_PER_GROUP
GROUP_W = HEADS_PER_GROUP * HEAD_DIM
LEVELS = 6
LANES = 128
SUBLANES = 8
BF16_ROWS = 16
VMEM_LIMIT = 56 * 1024 * 1024

F32 = jnp.float32
BF16 = jnp.bfloat16

_NN = (((1,), (0,)), ((), ()))
_NT = (((1,), (1,)), ((), ()))
_TN = (((0,), (0,)), ((), ()))

R_E1, R_E2, R_W1, R_W2 = 0, 1, 4, 5
MOE_SUB = 128
MOE_CAP = 24
MOE_GROUPS_PER_CALL = 2
MOE_SLOTS = 256

M_INCL, M_STRICT, M_LEVEL0, M_DIAG = 0, 1, 2, 2 + LEVELS
N_MASKS = M_DIAG + 1


def _dot(a, b, dims=_NN):
    return lax.dot_general(a, b, dims, preferred_element_type=F32)


def _mm(a, b, dims=_NN):
    return _dot(a.astype(BF16), b.astype(BF16), dims)


def _split2(x):
    hi = x.astype(BF16)
    lo = (x - hi.astype(F32)).astype(BF16)
    return hi, lo


def _split3(x):
    hi = x.astype(BF16)
    r = x - hi.astype(F32)
    mid = r.astype(BF16)
    lo = (r - mid.astype(F32)).astype(BF16)
    return hi, mid, lo


def _dot_pieces(pieces, e):
    out = _dot(pieces[0], e)
    for p in pieces[1:]:
        out = out + _dot(p, e)
    return out


def _mm_lhs2(x, e):
    return _dot_pieces(_split2(x), e)


def _mm3(a, b):
    a_hi, a_lo = _split2(a)
    b_hi, b_lo = _split2(b)
    return _dot(a_hi, b_hi) + _dot(a_hi, b_lo) + _dot(a_lo, b_hi)


def _iota(shape, dim):
    return lax.broadcasted_iota(jnp.int32, shape, dim)


def _sigmoid(x):
    return 0.5 * jnp.tanh(0.5 * x) + 0.5


def _silu(x):
    return x * _sigmoid(x)


def _softplus(x):
    return jnp.maximum(x, 0.0) + jnp.log1p(jnp.exp(-jnp.abs(x)))


def _head_ones(width):
    r = _iota((width, width), 0) >> 6
    c = _iota((width, width), 1) >> 6
    return jnp.where(r == c, 1.0, 0.0).astype(BF16)


def _params(sem):
    return pltpu.CompilerParams(dimension_semantics=sem, vmem_limit_bytes=VMEM_LIMIT)


def _full(a):
    return pl.BlockSpec(a.shape, lambda *_: (0,) * a.ndim)


def _mod_kernel(c_ref, w_ref, b_ref, o_ref):
    o_ref[0] = _mm3(_silu(c_ref[...]), w_ref[0]) + b_ref[0]


def _modulation(c, ada_w, ada_b):
    depth, d, n = ada_w.shape
    bsz = c.shape[0]
    tn = n // 4
    return pl.pallas_call(
        _mod_kernel,
        grid=(depth, n // tn),
        in_specs=[
            pl.BlockSpec((bsz, d), lambda l, j: (0, 0)),
            pl.BlockSpec((1, d, tn), lambda l, j: (l, 0, j)),
            pl.BlockSpec((1, 1, tn), lambda l, j: (l, 0, j)),
        ],
        out_specs=pl.BlockSpec((1, bsz, tn), lambda l, j: (l, 0, j)),
        out_shape=jax.ShapeDtypeStruct((depth, bsz, n), F32),
        compiler_params=_params(("parallel", "parallel")),
        name="adaln_mod",
    )(c, ada_w, ada_b.reshape(depth, 1, n))


HALO = BF16_ROWS


def _inproj_kernel(h_ref, hprev_ref, hnext_ref, sh_ref, sc_ref, g_ref, wconv_ref, wrest_ref, cw_ref, gp_ref,
                   perm_ref, permt_ref, aq_ref, ak_ref, av_ref, bq_ref, bk_ref, az_ref, bv_ref, bo_ref, mg_ref, gates_ref, cumf_ref,
                   cumb_ref):
    s = pl.program_id(1)
    ns = pl.num_programs(1)
    tm = h_ref.shape[1]
    gain = g_ref[...] * (1.0 + sc_ref[0])
    shift = sh_ref[0]

    def norm(x):
        ms = jnp.mean(x * x, axis=-1, keepdims=True)
        return x * lax.rsqrt(ms + EPS) * gain + shift

    hn = norm(h_ref[0])
    hp = norm(hprev_ref[0]) * jnp.where(s > 0, 1.0, 0.0)
    hx = norm(hnext_ref[0]) * jnp.where(s < ns - 1, 1.0, 0.0)
    hn_hi = hn.astype(BF16)

    nv = tm // SUBLANES
    edge = CONV_K // 2
    hn_perm = _dot(perm_ref[...], hn_hi).astype(BF16)
    hall = jnp.concatenate([hn_perm, hp.astype(BF16), hx.astype(BF16)], axis=0)
    groups = range(CONV_W // WIDTH)
    cols = lambda g: slice(g * WIDTH, (g + 1) * WIDTH)
    pe = [_dot(hall, wconv_ref[:, cols(g)]) for g in groups]
    rest = _dot(hn_hi, wrest_ref[...])
    hn_lo = (hn - hn_hi.astype(F32)).astype(BF16)
    pre = (rest[:, REST_W:REST_W + LANES] + rest[:, REST_W + LANES:]
           + _dot(hn_lo, wrest_ref[:, REST_W:REST_W + LANES]))

    sub = _iota((SUBLANES, WIDTH), 0)
    ones_bd = _head_ones(WIDTH)
    scale = HEAD_DIM ** -0.5
    out_refs = (aq_ref, ak_ref, av_ref, bq_ref, bk_ref)
    permuted = []
    for g in groups:
        prev = pe[g][tm:tm + HALO]
        nxt = pe[g][tm + HALO:]
        block = lambda v: pe[g][v * SUBLANES:(v + 1) * SUBLANES]
        before = [jnp.where(sub == 0, prev[HALO - edge + i:HALO - edge + i + 1], pltpu.roll(block(nv - edge + i), 1, 0))
                  for i in range(edge)]
        after = [jnp.where(sub == SUBLANES - 1, nxt[i:i + 1], pltpu.roll(block(i), SUBLANES - 1, 0))
                 for i in range(edge)]
        ext = jnp.concatenate(before + [pe[g][:tm]] + after, axis=0)

        conv = jnp.zeros((tm, WIDTH), F32)
        for j in range(CONV_K):
            conv = conv + ext[j * SUBLANES:j * SUBLANES + tm] * cw_ref[j:j + 1, cols(g)]
        cv = _silu(conv)
        if g in (0, 1):
            cv = cv * lax.rsqrt(_mm(cv * cv, ones_bd) + EPS)
        if g in (0, 4):
            cv = cv * scale
        permuted.append(cv.astype(BF16))
    for g in groups:
        out_refs[g][0] = _dot(permt_ref[...], permuted[g]).astype(BF16)

    az_ref[0] = rest[:, 0:WIDTH].astype(BF16)
    bv_ref[0] = rest[:, WIDTH:2 * WIDTH].astype(BF16)
    bo_ref[0] = rest[:, 2 * WIDTH:3 * WIDTH].astype(BF16)
    mg_ref[0] = rest[:, 3 * WIDTH:REST_W].astype(BF16)

    pre = pre + gp_ref[0:1, :]
    lane = _iota(pre.shape, 1)
    second = ((lane >> 3) & 1) == 1
    is_gdn = lane < 4 * N_HEADS
    is_ml = (lane >= 4 * N_HEADS) & (lane < 8 * N_HEADS)
    sp = _softplus(pre)
    gdn_val = jnp.where(second, _sigmoid(pre), -jnp.exp(gp_ref[1:2, :]) * sp)
    ml_val = jnp.where(second, -_softplus(-pre), pre)
    gates = jnp.where(is_gdn, gdn_val, jnp.where(is_ml, ml_val, 0.0))
    gates_ref[0] = gates

    ri = _iota((LANES, LANES), 0)
    ci = _iota((LANES, LANES), 1)
    same_chunk = (ri >> 6) == (ci >> 6)
    pieces = _split3(gates)
    for ref, tri in ((cumf_ref, same_chunk & (ri >= ci)), (cumb_ref, same_chunk & (ri <= ci))):
        tri_b = jnp.where(tri, 1.0, 0.0).astype(BF16)
        for blk in range(tm // LANES):
            rs = slice(blk * LANES, (blk + 1) * LANES)
            ref[0, rs, :] = _dot(tri_b, pieces[0][rs]) + _dot(tri_b, pieces[1][rs]) + _dot(tri_b, pieces[2][rs])


def _input_projection(h, sh, sc, gain, wconv, wrest, cw, gp, tm):
    bsz, seq, d = h.shape
    nt = seq // tm
    per = tm // HALO
    nhalo = seq // HALO
    tile = lambda w: pl.BlockSpec((1, tm, w), lambda b, s: (b, s, 0))
    outs = [(WIDTH, BF16)] * 8 + [(2 * D_MODEL, BF16)] + [(LANES, F32)] * 3
    r = np.arange(tm)
    perm_np = np.zeros((tm, tm), np.float32)
    perm_np[r, (r % SUBLANES) * (tm // SUBLANES) + r // SUBLANES] = 1.0
    perm = jnp.asarray(perm_np).astype(BF16)
    perm_t = jnp.asarray(perm_np.T).astype(BF16)
    return pl.pallas_call(
        _inproj_kernel,
        grid=(bsz, nt),
        in_specs=[
            tile(d),
            pl.BlockSpec((1, HALO, d), lambda b, s: (b, jnp.maximum(s * per - 1, 0), 0)),
            pl.BlockSpec((1, HALO, d), lambda b, s: (b, jnp.minimum((s + 1) * per, nhalo - 1), 0)),
            pl.BlockSpec((1, 1, d), lambda b, s: (b, 0, 0)),
            pl.BlockSpec((1, 1, d), lambda b, s: (b, 0, 0)),
            _full(gain), _full(wconv), _full(wrest), _full(cw), _full(gp), _full(perm), _full(perm_t),
        ],
        out_specs=[tile(w) for w, _ in outs],
        out_shape=[jax.ShapeDtypeStruct((bsz, seq, w), dt) for w, dt in outs],
        compiler_params=_params(("parallel", "parallel")),
        name="input_projection",
    )(h, h, h, sh, sc, gain, wconv, wrest, cw, gp, perm, perm_t)


def _gate_column(branch, direction, kind):
    return branch * 4 * N_HEADS + direction * 2 * N_HEADS + kind * N_HEADS


def _expand(pieces, e):
    rows = pieces[0].shape[0]
    out = _dot(jnp.concatenate(pieces, axis=0), e)
    return functools.reduce(lambda a, b: a + b, [out[i * rows:(i + 1) * rows] for i in range(len(pieces))])


def _head_rows(x_t, column, hg, block):
    first = column + hg * HEADS_PER_GROUP
    pieces = [x_t[first + h:first + h + 1, block * CHUNK:(block + 1) * CHUNK] for h in range(HEADS_PER_GROUP)]
    return jnp.concatenate(pieces, axis=1)


def _scan_constants():
    c, w = CHUNK, GROUP_W
    row = np.arange(c)[:, None]
    col = (np.arange(w) % c)[None, :]
    masks = np.zeros((2, N_MASKS, c, w), np.float32)
    for d in range(2):
        masks[d, M_INCL] = (row >= col) if d == 0 else (row <= col)
        masks[d, M_STRICT] = (row > col) if d == 0 else (row < col)
        for p in range(LEVELS):
            same = (row >> (p + 1)) == (col >> (p + 1))
            r_bit = (row >> p) & 1
            c_bit = (col >> p) & 1
            masks[d, M_LEVEL0 + p] = same & ((r_bit == 1) & (c_bit == 0) if d == 0 else (r_bit == 0) & (c_bit == 1))
        masks[d, M_DIAG] = row == col
    bd = (np.arange(w)[:, None] // HEAD_DIM) == (np.arange(w)[None, :] // HEAD_DIM)
    exp = np.zeros((2, 2, 2, LANES, WIDTH), np.float32)
    lane_head = np.arange(WIDTH) // HEAD_DIM
    for br in range(2):
        for d in range(2):
            for kind in range(2):
                exp[br, d, kind, _gate_column(br, d, kind) + lane_head, np.arange(WIDTH)] = 1.0
    return (jnp.asarray(masks), jnp.asarray(bd.astype(np.float32)),
            jnp.asarray(bd.astype(np.float32)).astype(BF16), jnp.asarray(exp).astype(BF16))


def _block_diag(x_cat, bd_bf16):
    xb = x_cat.astype(BF16)
    return jnp.concatenate([xb] * HEADS_PER_GROUP, axis=0) * bd_bf16


def _gdn_local_kernel(q_ref, k_ref, v_ref, g_ref, cumf_ref, cumb_ref, masks_ref, bdb_ref, exp_ref,
                      uf_ref, wf_ref, qef_ref, inf_ref, kef_ref, cdf_ref,
                      ub_ref, wb_ref, qeb_ref, inb_ref, keb_ref, cdb_ref):
    c, w = CHUNK, GROUP_W
    bd_b = bdb_ref[...]
    outs = ((uf_ref, wf_ref, qef_ref, inf_ref, kef_ref, cdf_ref), (ub_ref, wb_ref, qeb_ref, inb_ref, keb_ref, cdb_ref))
    chains = [(ci, d, hg) for ci in range(q_ref.shape[1] // c) for d in range(2) for hg in range(N_HEAD_GROUPS)]
    rows = lambda ci: slice(ci * c, (ci + 1) * c)
    lanes = lambda hg: slice(hg * w, (hg + 1) * w)

    cums = (cumf_ref[0], cumb_ref[0])
    g_pieces = _split2(g_ref[0])
    gc_all = [_expand(_split3(cums[d]), exp_ref[0, d, 0]) for d in range(2)]
    beta_all = [_expand(g_pieces, exp_ref[0, d, 1]) for d in range(2)]
    per_block = LANES // c
    n_blocks = q_ref.shape[1] // LANES
    cum_t = [[cums[d][blk * LANES:(blk + 1) * LANES].T for blk in range(n_blocks)] for d in range(2)]
    gc = [gc_all[d][rows(ci), lanes(hg)] for ci, d, hg in chains]
    beta = [beta_all[d][rows(ci), lanes(hg)] for ci, d, hg in chains]
    gc_t = [_head_rows(cum_t[d][ci // per_block], _gate_column(0, d, 0), hg, ci % per_block) for ci, d, hg in chains]

    qk_kk = {}
    for ci, d, hg in chains:
        if d == 0:
            q = q_ref[0, rows(ci), lanes(hg)]
            k = k_ref[0, rows(ci), lanes(hg)]
            k_bd = jnp.concatenate([k] * HEADS_PER_GROUP, axis=0) * bd_b
            qk_kk[ci, hg] = _dot(jnp.concatenate([q, k], axis=0), k_bd, _NT)
    kbeta = [k_ref[0, rows(ci), lanes(hg)].astype(F32) * beta[i] for i, (ci, d, hg) in enumerate(chains)]

    a, t = [], []
    for i, (ci, d, hg) in enumerate(chains):
        decay = jnp.exp(jnp.minimum(gc[i] - gc_t[i], 0.0)) * masks_ref[d, M_INCL]
        outs[d][3][0, rows(ci), lanes(hg)] = (qk_kk[ci, hg][:c] * decay).astype(BF16)
        a.append(qk_kk[ci, hg][c:] * beta[i] * decay * masks_ref[d, M_STRICT])
        t.append(masks_ref[d, M_DIAG] - a[i] * masks_ref[d, M_LEVEL0])

    for p in range(1, LEVELS):
        y = [_mm(t[i], _block_diag(a[i] * masks_ref[d, M_LEVEL0 + p], bd_b)) for i, (_, d, _) in enumerate(chains)]
        t = [t[i] - _mm(y[i], _block_diag(t[i], bd_b)) for i in range(len(chains))]

    for i, (ci, d, hg) in enumerate(chains):
        u_ref, w_ref, qe_ref, _, ke_ref, cd_ref = outs[d]
        last = c - 1 if d == 0 else 0
        g_last = gc[i][last:last + 1, :]
        e_gc = jnp.exp(gc[i])
        v = v_ref[0, rows(ci), lanes(hg)].astype(F32)
        u_ref[0, rows(ci), lanes(hg)] = _mm(t[i], _block_diag(v * beta[i], bd_b))
        w_ref[0, rows(ci), lanes(hg)] = _mm(t[i], _block_diag(kbeta[i] * e_gc, bd_b)).astype(BF16)
        qe_ref[0, rows(ci), lanes(hg)] = (q_ref[0, rows(ci), lanes(hg)].astype(F32) * e_gc).astype(BF16)
        k = k_ref[0, rows(ci), lanes(hg)].astype(F32)
        ke_ref[0, rows(ci), lanes(hg)] = (k * jnp.exp(g_last - gc[i])).astype(BF16)
        cd_ref[0, ci, :, lanes(hg)] = jnp.exp(g_last)


def _gdn_local(q, k, v, gates, cumf, cumb, consts, tl):
    bsz, seq, width = q.shape
    masks, _, bd_b, exp = consts
    tile = lambda wd: pl.BlockSpec((1, tl, wd), lambda b, s: (b, s, 0))
    cd_spec = pl.BlockSpec((1, tl // CHUNK, 1, width), lambda b, s: (b, s, 0, 0))
    big = lambda dt: jax.ShapeDtypeStruct((bsz, seq, width), dt)
    cd_shape = jax.ShapeDtypeStruct((bsz, seq // CHUNK, 1, width), F32)
    per_dir_specs = [tile(width)] * 5 + [cd_spec]
    per_dir_shapes = [big(F32), big(BF16), big(BF16), big(BF16), big(BF16), cd_shape]
    return pl.pallas_call(
        _gdn_local_kernel,
        grid=(bsz, seq // tl),
        in_specs=[tile(width), tile(width), tile(width), tile(LANES), tile(LANES), tile(LANES),
                  _full(masks), _full(bd_b), _full(exp)],
        out_specs=per_dir_specs * 2,
        out_shape=per_dir_shapes * 2,
        compiler_params=_params(("parallel", "parallel")),
        name="gdn_local",
    )(q, k, v, gates, cumf, cumb, masks, bd_b, exp)


def _gdn_state_kernel(uf_ref, wf_ref, qef_ref, inf_ref, kef_ref, cdf_ref,
                      ub_ref, wb_ref, qeb_ref, inb_ref, keb_ref, cdb_ref, bdf_ref, bdb_ref, of_ref, ob_ref, s_ref):
    n = pl.program_id(1)

    @pl.when(n == 0)
    def _():
        s_ref[...] = jnp.zeros_like(s_ref)

    c, w = CHUNK, GROUP_W
    bd_b = bdb_ref[...]
    dirs = ((uf_ref, wf_ref, qef_ref, inf_ref, kef_ref, cdf_ref, of_ref),
            (ub_ref, wb_ref, qeb_ref, inb_ref, keb_ref, cdb_ref, ob_ref))
    nb = uf_ref.shape[0]
    chains = [(bi, d, hg) for bi in range(nb) for d in range(2) for hg in range(N_HEAD_GROUPS)]
    lanes = lambda hg: slice(hg * w, (hg + 1) * w)
    ws_qs = []
    for idx, (bi, d, hg) in enumerate(chains):
        w_ref, qe_ref = dirs[d][1], dirs[d][2]
        lhs = jnp.concatenate([w_ref[bi, :, lanes(hg)], qe_ref[bi, :, lanes(hg)]], axis=0)
        ws_qs.append(_dot(lhs, s_ref[idx].astype(BF16)))
    v_new = [(dirs[d][0][bi, :, lanes(hg)] - ws_qs[idx][:c]).astype(BF16) for idx, (bi, d, hg) in enumerate(chains)]
    for idx, (bi, d, hg) in enumerate(chains):
        in_ref, o_ref = dirs[d][3], dirs[d][6]
        o = ws_qs[idx][c:] + _dot(in_ref[bi, :, lanes(hg)], _block_diag(v_new[idx], bd_b))
        o_ref[bi, :, lanes(hg)] = o.astype(BF16)
    for idx, (bi, d, hg) in enumerate(chains):
        ke_ref, cd_ref = dirs[d][4], dirs[d][5]
        ds = _dot(ke_ref[bi, :, lanes(hg)], v_new[idx], _TN)
        s_ref[idx] = s_ref[idx] * cd_ref[bi, 0, :, lanes(hg)] + ds * bdf_ref[...]


def _gdn_state(local_outs, consts, nb):
    uf = local_outs[0]
    bsz, seq, width = uf.shape
    nc = seq // CHUNK
    _, bd_f, bd_b, _ = consts
    fwd = pl.BlockSpec((nb, CHUNK, width), lambda b, n: (b, n, 0))
    bwd = pl.BlockSpec((nb, CHUNK, width), lambda b, n: (b, nc - 1 - n, 0))
    cd_fwd = pl.BlockSpec((nb, 1, 1, width), lambda b, n: (b, n, 0, 0))
    cd_bwd = pl.BlockSpec((nb, 1, 1, width), lambda b, n: (b, nc - 1 - n, 0, 0))
    return pl.pallas_call(
        _gdn_state_kernel,
        grid=(bsz // nb, nc),
        in_specs=[fwd] * 5 + [cd_fwd] + [bwd] * 5 + [cd_bwd] + [_full(bd_f), _full(bd_b)],
        out_specs=[fwd, bwd],
        out_shape=[jax.ShapeDtypeStruct((bsz, seq, width), BF16)] * 2,
        scratch_shapes=[pltpu.VMEM((nb * 2 * N_HEAD_GROUPS, GROUP_W, GROUP_W), F32)],
        compiler_params=_params(("parallel", "arbitrary")),
        name="gdn_state",
    )(*local_outs, bd_f, bd_b)


def _mlstm_kernel(qf_ref, kf_ref, vf_ref, gf_ref, cf_ref, qb_ref, kb_ref, vb_ref, gb_ref, cb_ref,
                  masks_ref, bdf_ref, bdb_ref, exp_ref, of_ref, ob_ref, c_ref, n_ref, m_ref):
    n = pl.program_id(1)

    @pl.when(n == 0)
    def _():
        c_ref[...] = jnp.zeros_like(c_ref)
        n_ref[...] = jnp.zeros_like(n_ref)
        m_ref[...] = jnp.zeros_like(m_ref)

    c, w = CHUNK, GROUP_W
    bd_f = bdf_ref[...]
    bd_b = bdb_ref[...]
    ones_rows = jnp.ones((SUBLANES, c), BF16)
    row = _iota((c, w), 0)
    dirs = ((qf_ref, kf_ref, vf_ref, gf_ref, cf_ref, of_ref), (qb_ref, kb_ref, vb_ref, gb_ref, cb_ref, ob_ref))
    nb = qf_ref.shape[0]
    chains = [(bi, d, hg) for bi in range(nb) for d in range(2) for hg in range(N_HEAD_GROUPS)]
    lanes = lambda hg: slice(hg * w, (hg + 1) * w)
    gates = [jnp.concatenate([dirs[d][3][bi] for bi in range(nb)], axis=0) for d in range(2)]
    cums = [jnp.concatenate([dirs[d][4][bi] for bi in range(nb)], axis=0) for d in range(2)]
    i_all = [_expand(_split3(gates[d]), exp_ref[1, d, 0]) for d in range(2)]
    bcum_all = [_expand(_split3(cums[d]), exp_ref[1, d, 1]) for d in range(2)]
    per_block = LANES // c
    blocks = range(nb // per_block)
    block_t = lambda a: [a[blk * LANES:(blk + 1) * LANES].T for blk in blocks]
    gates_t = [block_t(gates[d]) for d in range(2)]
    cums_t = [block_t(cums[d]) for d in range(2)]
    rows = lambda bi: slice(bi * c, (bi + 1) * c)
    bcum = [bcum_all[d][rows(bi), lanes(hg)] for bi, d, hg in chains]
    r = [i_all[d][rows(bi), lanes(hg)] - bcum[i] for i, (bi, d, hg) in enumerate(chains)]
    r_t = [_head_rows(gates_t[d][bi // per_block], _gate_column(1, d, 0), hg, bi % per_block)
           - _head_rows(cums_t[d][bi // per_block], _gate_column(1, d, 1), hg, bi % per_block)
           for bi, d, hg in chains]
    qk = []
    for bi, d, hg in chains:
        k_bd = jnp.concatenate([dirs[d][1][bi, :, lanes(hg)]] * HEADS_PER_GROUP, axis=0) * bd_b
        qk.append(_dot(dirs[d][0][bi, :, lanes(hg)], k_bd, _NT))

    pm, mx, inter_w = [], [], []
    for i, (bi, d, hg) in enumerate(chains):
        m = r[i]
        for p in range(LEVELS):
            sh = 1 << p
            if d == 0:
                m = jnp.maximum(m, jnp.where(row >= sh, pltpu.roll(m, sh, 0), NEG))
            else:
                m = jnp.maximum(m, jnp.where(row < c - sh, pltpu.roll(m, c - sh, 0), NEG))
        pm.append(m)
        m_prev = m_ref[i, 0:1, :]
        mx.append(jnp.maximum(m_prev, m))
        inter_w.append(jnp.exp(m_prev - mx[i]))
        qk[i] = qk[i] * jnp.exp(jnp.where(masks_ref[d, M_INCL] > 0.0, r_t[i] - mx[i], NEG))

    num = [_mm(qk[i], _block_diag(dirs[d][2][bi, :, lanes(hg)], bd_b)) for i, (bi, d, hg) in enumerate(chains)]
    den = [_mm(qk[i], bd_b) for i in range(len(chains))]
    qc = [_mm(dirs[d][0][bi, :, lanes(hg)], c_ref[i]) for i, (bi, d, hg) in enumerate(chains)]
    qn = [_mm(dirs[d][0][bi, :, lanes(hg)].astype(F32) * n_ref[i, 0:1, :], bd_b)
          for i, (bi, d, hg) in enumerate(chains)]
    for i, (bi, d, hg) in enumerate(chains):
        full_num = num[i] + inter_w[i] * qc[i]
        full_den = den[i] + inter_w[i] * qn[i]
        hidden = full_num / jnp.maximum(jnp.abs(full_den), jnp.exp(-(bcum[i] + mx[i])))
        dirs[d][5][bi, :, lanes(hg)] = hidden.astype(BF16)

    for i, (bi, d, hg) in enumerate(chains):
        last = c - 1 if d == 0 else 0
        m_prev = m_ref[i, 0:1, :]
        b_last = bcum[i][last:last + 1, :]
        m_next = b_last + jnp.maximum(m_prev, pm[i][last:last + 1, :])
        scale_prev = jnp.exp(b_last + m_prev - m_next)
        wgt = jnp.exp(b_last + r[i] - m_next)
        kw = (dirs[d][1][bi, :, lanes(hg)].astype(F32) * wgt).astype(BF16)
        c_ref[i] = c_ref[i] * scale_prev + _dot(kw, dirs[d][2][bi, :, lanes(hg)], _TN) * bd_f
        n_ref[i, 0:1, :] = n_ref[i, 0:1, :] * scale_prev + _dot(ones_rows, kw)[0:1, :]
        m_ref[i, 0:1, :] = m_next


def _mlstm_scan(q, k, v, gates, cumf, cumb, consts, nb):
    bsz, seq, width = q.shape
    nc = seq // CHUNK
    masks, bd_f, bd_b, exp = consts
    fwd = lambda wd: pl.BlockSpec((nb, CHUNK, wd), lambda b, n: (b, n, 0))
    bwd = lambda wd: pl.BlockSpec((nb, CHUNK, wd), lambda b, n: (b, nc - 1 - n, 0))
    n_state = nb * 2 * N_HEAD_GROUPS
    state = pltpu.VMEM((n_state, GROUP_W, GROUP_W), F32)
    row_state = pltpu.VMEM((n_state, SUBLANES, GROUP_W), F32)
    return pl.pallas_call(
        _mlstm_kernel,
        grid=(bsz // nb, nc),
        in_specs=[fwd(width), fwd(width), fwd(width), fwd(LANES), fwd(LANES),
                  bwd(width), bwd(width), bwd(width), bwd(LANES), bwd(LANES),
                  _full(masks), _full(bd_f), _full(bd_b), _full(exp)],
        out_specs=[fwd(width), bwd(width)],
        out_shape=[jax.ShapeDtypeStruct((bsz, seq, width), BF16)] * 2,
        scratch_shapes=[state, row_state, row_state],
        compiler_params=_params(("parallel", "arbitrary")),
        name="mlstm_scan",
    )(q, k, v, gates, cumf, q, k, v, gates, cumb, masks, bd_f, bd_b, exp)


def _outproj_kernel(h_ref, gaf_ref, gab_ref, mlf_ref, mlb_ref, az_ref, bo_ref, mg_ref, gt_ref, ng_ref, mlg_ref,
                    wa_ref, wb_ref, wo_ref, sh_ref, sc_ref, fg_ref, wr_ref, hout_ref, hn_ref, comb_ref):
    ones_bd = _head_ones(WIDTH)
    inv = 1.0 / HEAD_DIM
    tm = h_ref.shape[1]
    n_parts = 2 if tm % (2 * BF16_ROWS) == 0 else 1
    parts = [slice(i * tm // n_parts, (i + 1) * tm // n_parts) for i in range(n_parts)]
    f32 = lambda ref, rs: ref[0, rs, :].astype(F32)

    def branch_inputs(rs):
        oa = f32(gaf_ref, rs) + f32(gab_ref, rs)
        ms = _mm(oa * oa, ones_bd) * inv
        oa = oa * lax.rsqrt(ms + EPS) * ng_ref[...] * _silu(f32(az_ref, rs))
        hb = f32(mlf_ref, rs) + f32(mlb_ref, rs)
        mu = _mm_lhs2(hb, ones_bd) * inv
        tc = hb - mu
        var = _mm(tc * tc, ones_bd) * inv
        hb = tc * lax.rsqrt(var + EPS) * mlg_ref[...] * _sigmoid(f32(bo_ref, rs))
        return oa.astype(BF16), hb.astype(BF16)

    def merged(rs, oa, hb):
        gate = _sigmoid(f32(mg_ref, rs))
        return gate[:, :D_MODEL] * _dot(oa, wa_ref[...]) + gate[:, D_MODEL:] * _dot(hb, wb_ref[...])

    def residual(rs, y):
        h_new = h_ref[0, rs, :] + gt_ref[0] * _mm(y, wo_ref[...])
        hout_ref[0, rs, :] = h_new
        ms2 = jnp.mean(h_new * h_new, axis=-1, keepdims=True)
        hn = h_new * lax.rsqrt(ms2 + EPS) * (fg_ref[...] * (1.0 + sc_ref[0])) + sh_ref[0]
        hn_ref[0, rs, :] = hn.astype(BF16)
        return hn

    def routing(rs, logits):
        lane = _iota(logits.shape, 1)
        lane_f = lane.astype(F32)
        big = 1e9
        gl = jnp.where((lane >= N_EXPERTS) & (lane < N_EXPERTS + N_GROUPS), logits, NEG)
        gmax = jnp.max(gl, axis=-1, keepdims=True)
        gidx = jnp.min(jnp.where(gl == gmax, lane_f - N_EXPERTS, big), axis=-1, keepdims=True)
        p_group = 1.0 / jnp.sum(jnp.exp(gl - gmax), axis=-1, keepdims=True)
        el = jnp.where((lane < N_EXPERTS) & ((lane >> 3).astype(F32) == gidx), logits, NEG)
        v1 = jnp.max(el, axis=-1, keepdims=True)
        i1 = jnp.min(jnp.where(el == v1, lane_f, big), axis=-1, keepdims=True)
        el2 = jnp.where(lane_f == i1, NEG, el)
        v2 = jnp.max(el2, axis=-1, keepdims=True)
        i2 = jnp.min(jnp.where(el2 == v2, lane_f, big), axis=-1, keepdims=True)
        e21 = jnp.exp(v2 - v1)
        w1 = p_group / (1.0 + e21)
        route = jnp.where(lane == R_E1, i1, 0.0) + jnp.where(lane == R_E2, i2, 0.0)
        comb_ref[0, rs, :] = route + jnp.where(lane == R_W1, w1, 0.0) + jnp.where(lane == R_W2, w1 * e21, 0.0)

    branches = [branch_inputs(rs) for rs in parts]
    ys = [merged(rs, *branches[i]) for i, rs in enumerate(parts)]
    hns = [residual(rs, ys[i]) for i, rs in enumerate(parts)]
    logits = []
    for hn in hns:
        hi, lo = _split2(hn)
        both = _dot(jnp.concatenate([hi, lo], axis=1), wr_ref[...])
        logits.append(both[:, :LANES] + both[:, LANES:])
    for i, rs in enumerate(parts):
        routing(rs, logits[i])


def _output_projection(h, gaf, gab, mlf, mlb, az, bo, mg, gt, ng, mlg, wa, wb, wo, sh, sc, fg, wr, tm):
    bsz, seq, d = h.shape
    tile = lambda w: pl.BlockSpec((1, tm, w), lambda b, s: (b, s, 0))
    per_b = pl.BlockSpec((1, 1, d), lambda b, s: (b, 0, 0))
    return pl.pallas_call(
        _outproj_kernel,
        grid=(bsz, seq // tm),
        in_specs=[tile(d), tile(WIDTH), tile(WIDTH), tile(WIDTH), tile(WIDTH), tile(WIDTH), tile(WIDTH),
                  tile(2 * D_MODEL), per_b, _full(ng), _full(mlg), _full(wa), _full(wb), _full(wo), per_b, per_b,
                  _full(fg), _full(wr)],
        out_specs=[tile(d), tile(d), tile(LANES)],
        out_shape=[jax.ShapeDtypeStruct((bsz, seq, d), F32), jax.ShapeDtypeStruct((bsz, seq, d), BF16),
                   jax.ShapeDtypeStruct((bsz, seq, LANES), F32)],
        compiler_params=_params(("parallel", "parallel")),
        name="output_projection",
    )(h, gaf, gab, mlf, mlb, az, bo, mg, gt, ng, mlg, wa, wb, wo, sh, sc, fg, wr)


def _moe_kernel(npass_ref, acc_ref, hn_ref, route_ref, gt_ref, wg_ref, wu_ref, wd_ref, fin_ref, tri_ref,
                o_ref, y_ref, *, groups, final_norm):
    tile = pl.program_id(0) * pl.num_programs(1) + pl.program_id(1)
    tm = hn_ref.shape[1]
    cap = MOE_CAP
    sub = MOE_SUB
    subs = range(tm // sub)
    rows = lambda s: slice(s * sub, (s + 1) * sub)
    experts = range(EXPERTS_PER_GROUP)
    route = route_ref[0]
    lane = _iota(route.shape, 1)
    lane_f = lane.astype(F32)
    column = lambda j: jnp.sum(jnp.where(lane == j, route, 0.0), axis=-1, keepdims=True)
    e1 = column(R_E1)
    e2 = column(R_E2)
    w1 = column(R_W1)
    w2 = column(R_W2)
    slot_lane = _iota((tm, MOE_SLOTS), 1).astype(F32)
    x = hn_ref[0]

    chains = []
    for g in groups:
        first = float(g * EXPERTS_PER_GROUP)
        l1 = e1 - first
        l2 = e2 - first
        in1 = (l1 >= 0.0) & (l1 < EXPERTS_PER_GROUP)
        in2 = (l2 >= 0.0) & (l2 < EXPERTS_PER_GROUP)
        hit1 = in1 & (lane_f == l1)
        hit2 = in2 & (lane_f == l2)
        onehot = (jnp.where(hit1, 1.0, 0.0) + jnp.where(hit2, 1.0, 0.0)).astype(BF16)
        before = jnp.concatenate([_dot(tri_ref[...], onehot[rows(s)]) for s in subs], axis=0)
        rank1 = jnp.sum(jnp.where(hit1, before, 0.0), axis=-1, keepdims=True)
        rank2 = jnp.sum(jnp.where(hit2, before, 0.0), axis=-1, keepdims=True)
        chains.append((l1, l2, in1, in2, rank1, rank2))
    n_chain = len(chains)
    weights = lambda ref, c, e: ref[c * EXPERTS_PER_GROUP + e]

    def one_pass(p, first):
        lo = p * float(cap) if first else (p * cap).astype(F32)
        sel, sel_w = [], []
        for l1, l2, in1, in2, rank1, rank2 in chains:
            s1 = jnp.where(in1 & (rank1 >= lo) & (rank1 < lo + cap), l1 * cap + rank1 - lo, -1.0)
            s2 = jnp.where(in2 & (rank2 >= lo) & (rank2 < lo + cap), l2 * cap + rank2 - lo, -1.0)
            m1 = slot_lane == s1
            m2 = slot_lane == s2
            sel.append(jnp.where(m1 | m2, 1.0, 0.0).astype(BF16))
            sel_w.append((jnp.where(m1, w1, 0.0) + jnp.where(m2, w2, 0.0)).astype(BF16))
        xs = [[_dot(sel[c][rows(s)], x[rows(s)], _TN) for s in subs] for c in range(n_chain)]
        xe = [[jnp.concatenate([xs[c][s][e * cap:(e + 1) * cap] for s in subs], axis=0).astype(BF16) for e in experts]
              for c in range(n_chain)]
        hg = [[_dot(xe[c][e], weights(wg_ref, c, e)) for e in experts] for c in range(n_chain)]
        hu = [[_dot(xe[c][e], weights(wu_ref, c, e)) for e in experts] for c in range(n_chain)]
        act = [[(_silu(hg[c][e]) * hu[c][e]).astype(BF16) for e in experts] for c in range(n_chain)]
        ye = [[_dot(act[c][e], weights(wd_ref, c, e)) for e in experts] for c in range(n_chain)]
        pad = jnp.zeros((MOE_SLOTS - EXPERTS_PER_GROUP * cap, x.shape[1]), F32)
        for s in subs:
            back = None
            for c in range(n_chain):
                ys = jnp.concatenate([ye[c][e][s * cap:(s + 1) * cap] for e in experts] + [pad], axis=0).astype(BF16)
                part = _dot(sel_w[c][rows(s)], ys)
                back = part if back is None else back + part
            if first:
                y_ref[rows(s), :] = back
            else:
                y_ref[rows(s), :] += back

    one_pass(0, True)

    def extra_pass(p, carry):
        one_pass(p, False)
        return carry

    lax.fori_loop(1, npass_ref[tile], extra_pass, 0)
    out = acc_ref[0] + gt_ref[0] * y_ref[...]
    if final_norm:
        ms = jnp.mean(out * out, axis=-1, keepdims=True)
        out = out * lax.rsqrt(ms + EPS) * fin_ref[...]
    o_ref[0] = out


def _moe(h, hn, route, gt, wg, wu, wd, fin, tm, final_norm):
    bsz, seq, d = h.shape
    nt = seq // tm
    per_call = MOE_GROUPS_PER_CALL
    n_calls = N_GROUPS // per_call
    n_sub = tm // MOE_SUB
    ids = route[..., (R_E1, R_E2)].astype(jnp.int32).reshape(bsz * nt, n_sub, MOE_SUB * 2)
    counts = jnp.sum(jax.nn.one_hot(ids, N_EXPERTS, dtype=jnp.int32), axis=2)
    most = jnp.max(counts.reshape(bsz * nt, n_sub, n_calls, per_call * EXPERTS_PER_GROUP), axis=(1, 3))
    npass = ((most + MOE_CAP - 1) // MOE_CAP).T.astype(jnp.int32)
    tri = jnp.asarray(np.tril(np.ones((MOE_SUB, MOE_SUB), np.float32), -1)).astype(BF16)

    tile = lambda w: pl.BlockSpec((1, tm, w), lambda b, s, n: (b, s, 0))
    n_exp = per_call * EXPERTS_PER_GROUP
    for call in range(n_calls):
        resident = lambda shape: pl.BlockSpec(shape, lambda b, s, n, call=call: (call, 0, 0),
                                              pipeline_mode=pl.Buffered(1))
        grid_spec = pltpu.PrefetchScalarGridSpec(
            num_scalar_prefetch=1,
            grid=(bsz, nt),
            in_specs=[tile(d), tile(d), tile(LANES), pl.BlockSpec((1, 1, d), lambda b, s, n: (b, 0, 0)),
                      resident((n_exp, d, D_EXPERT)), resident((n_exp, d, D_EXPERT)), resident((n_exp, D_EXPERT, d)),
                      pl.BlockSpec((1, d), lambda b, s, n: (0, 0)), pl.BlockSpec(tri.shape, lambda b, s, n: (0, 0))],
            out_specs=tile(d),
            scratch_shapes=[pltpu.VMEM((tm, d), F32)],
        )
        groups = tuple(range(call * per_call, (call + 1) * per_call))
        h = pl.pallas_call(
            functools.partial(_moe_kernel, groups=groups, final_norm=final_norm and call == n_calls - 1),
            grid_spec=grid_spec,
            out_shape=jax.ShapeDtypeStruct((bsz, seq, d), F32),
            compiler_params=_params(("parallel", "parallel")),
            name="expert_ffn",
        )(npass[call], h, hn, route, gt, wg, wu, wd, fin, tri)
    return h


def _pick_tile(seq, want):
    tm = min(seq, want)
    assert seq % tm == 0 and tm % CHUNK == 0
    return tm


def kernel(x, c, ada_w, ada_b, norm_mix_g, norm_ffn_g, w_in, gdn_conv_w, gdn_a_log, gdn_dt_bias, gdn_norm_g, mlstm_conv_w, mlstm_i_bias, mlstm_f_bias, mlstm_norm_g, w_branch_a, w_branch_b, w_out, router_group, router_expert, w_gate, w_up, w_down, final_norm_g):
    bsz, seq, d = x.shape
    depth = ada_w.shape[0]
    assert d == D_MODEL and seq % CHUNK == 0
    tm_in = _pick_tile(seq, 512)
    tm_out = _pick_tile(seq, 512)
    tm_moe = _pick_tile(seq, 4 * MOE_SUB)
    tl_gdn = _pick_tile(seq, 4 * LANES)
    nb_scan = LANES // CHUNK
    assert tl_gdn % LANES == 0 and bsz % nb_scan == 0
    row = lambda a: a.reshape(1, -1).astype(F32)
    consts = _scan_constants()

    mod = _modulation(c.astype(F32), ada_w.astype(F32), ada_b.astype(F32))
    h = x.astype(F32)
    for l in range(depth):
        sh1, sc1, gt1, sh2, sc2, gt2 = [mod[l, :, i * d:(i + 1) * d].reshape(bsz, 1, d) for i in range(N_MOD)]

        w = w_in[l].astype(F32)
        o_z = 3 * WIDTH
        o_ag = o_z + WIDTH
        o_bqk = o_ag + 4 * N_HEADS
        o_bv = o_bqk + 2 * WIDTH
        o_bo = o_bv + WIDTH
        o_bg = o_bo + WIDTH
        o_mg = o_bg + 4 * N_HEADS
        wconv = jnp.concatenate([w[:, :o_z], w[:, o_bqk:o_bv]], axis=1).astype(BF16)
        wg = jnp.concatenate([w[:, o_ag:o_bqk], w[:, o_bg:o_mg], jnp.zeros((d, LANES - 8 * N_HEADS), F32)], axis=1)
        wg_hi = wg.astype(BF16)
        wg_lo = (wg - wg_hi.astype(F32)).astype(BF16)
        wrest = jnp.concatenate([w[:, o_z:o_ag].astype(BF16), w[:, o_bv:o_bo].astype(BF16), w[:, o_bo:o_bg].astype(BF16),
                                 w[:, o_mg:].astype(BF16), wg_hi, wg_lo], axis=1)
        cw = jnp.concatenate([gdn_conv_w[l], mlstm_conv_w[l]], axis=1).astype(F32)
        cw = jnp.concatenate([cw, jnp.zeros((SUBLANES - CONV_K, CONV_W), F32)], axis=0)
        zero8 = jnp.zeros((N_HEADS,), F32)
        bias = jnp.concatenate([gdn_dt_bias[l, 0], zero8, gdn_dt_bias[l, 1], zero8,
                                mlstm_i_bias[l, 0], mlstm_f_bias[l, 0], mlstm_i_bias[l, 1], mlstm_f_bias[l, 1],
                                jnp.zeros((LANES - 8 * N_HEADS,), F32)]).astype(F32)
        alog = jnp.concatenate([gdn_a_log[l, 0], zero8, gdn_a_log[l, 1], zero8,
                                jnp.zeros((LANES - 4 * N_HEADS,), F32)]).astype(F32)
        gp = jnp.concatenate([bias[None], alog[None], jnp.zeros((SUBLANES - 2, LANES), F32)], axis=0)

        aq, ak, av, bq, bk, az, bv, bo, mg, gates, cumf, cumb = _input_projection(
            h, sh1, sc1, row(norm_mix_g[l]), wconv, wrest, cw, gp, tm_in)
        nb_state = 4 if bsz % 4 == 0 else nb_scan
        nb_gdn = 8 if bsz % 8 == 0 else nb_state
        gaf, gab = _gdn_state(_gdn_local(aq, ak, av, gates, cumf, cumb, consts, tl_gdn), consts, nb_gdn)
        mlf, mlb = _mlstm_scan(bq, bk, bv, gates, cumf, cumb, consts, nb_gdn)

        wr = jnp.concatenate([router_expert[l].astype(F32), router_group[l].astype(F32),
                              jnp.zeros((d, LANES - N_EXPERTS - N_GROUPS), F32)], axis=1)
        wr_hi = wr.astype(BF16)
        wr_lo = (wr - wr_hi.astype(F32)).astype(BF16)
        wr = jnp.concatenate([jnp.concatenate([wr_hi, wr_lo], axis=1),
                              jnp.concatenate([wr_hi, jnp.zeros_like(wr_hi)], axis=1)], axis=0)
        h, hn2, comb = _output_projection(
            h, gaf, gab, mlf, mlb, az, bo, mg, gt1, row(jnp.tile(gdn_norm_g[l], N_HEADS)), row(mlstm_norm_g[l]),
            w_branch_a[l].astype(BF16), w_branch_b[l].astype(BF16), w_out[l].astype(BF16),
            sh2, sc2, row(norm_ffn_g[l]), wr, tm_out)
        h = _moe(h, hn2, comb, gt2, w_gate[l].astype(BF16), w_up[l].astype(BF16), w_down[l].astype(BF16),
                 row(final_norm_g), tm_moe, final_norm=(l == depth - 1))
    return h.astype(x.dtype)
```
